```python
import math
import jax
import jax.numpy as jnp
from jax import lax
import numpy as np

D_MODEL = 1024
BATCH = 32
SEQ = 256
DEPTH = 4
DEC_BATCH = 4
DEC_SEQ = 1024
PAST_LEN = 512

GRID_W = 64
N_MIXERS = 4
N_LAYERS_FOURIER = (DEPTH + 3) // 4
N_LAYERS_NA = (DEPTH + 2) // 4
N_LAYERS_GMLP = (DEPTH + 1) // 4
N_LAYERS_SSD = DEPTH // 4
N_MOD = 6
EPS = 1e-6
NEG_INF = -1e30

FNET_GROUPS = 8
FNET_GROUP_DIM = D_MODEL // FNET_GROUPS

NA_HEADS = 16
NA_HEAD_DIM = D_MODEL // NA_HEADS
NA_WIN_ROWS_MAX = 8
NA_WIN_COLS = 16
NA_COL_BLOCK = 16
NA_COL_BAND = NA_COL_BLOCK + NA_WIN_COLS
CTX_Q_BLOCK = 128

GMLP_CHUNK = 128
GMLP_DFF = 2 * D_MODEL
GMLP_GROUPS = 8
GMLP_GROUP_DIM = GMLP_DFF // GMLP_GROUPS

SSD_D_INNER = 2 * D_MODEL
SSD_HEAD_DIM = 64
SSD_HEADS = SSD_D_INNER // SSD_HEAD_DIM
SSD_GROUPS = 8
SSD_STATE = 128
SSD_CONV = 4
SSD_CHUNK = 128
SSD_CONV_DIM = SSD_D_INNER + 2 * SSD_GROUPS * SSD_STATE
SSD_PROJ = SSD_D_INNER + SSD_CONV_DIM + 2 * SSD_HEADS

MOE_GROUPS = 4
MOE_EXPERTS_PER_GROUP = 4
MOE_EXPERTS = MOE_GROUPS * MOE_EXPERTS_PER_GROUP
MOE_TOP_K = 2
MOE_D_FF = 512

kernel_name = 'hybrid_dit_context_prefix_step'


def rmsnorm(x, g):
    x32 = x.astype(jnp.float32)
    y = x32 * lax.rsqrt(jnp.mean(x32 * x32, axis=-1, keepdims=True) + EPS)
    return (y * g.astype(jnp.float32)).astype(x.dtype)


def adaln(cond, w, b):
    return (jax.nn.silu(cond) @ w + b).reshape(cond.shape[0], N_MOD, D_MODEL)


def modulate(x, mod, i):
    return x * (1 + mod[:, i + 1][:, None, :]) + mod[:, i][:, None, :]


def fourier_mixer(h, w_o):
    b, s, _ = h.shape
    hg = h.astype(jnp.float32).reshape(b, s, FNET_GROUPS, FNET_GROUP_DIM)
    f = jnp.fft.fft2(hg, axes=(1, 3), norm='ortho').real
    return f.reshape(b, s, D_MODEL).astype(h.dtype) @ w_o


def _split_heads(h, w_qkv):
    b, s, _ = h.shape
    qkv = (h @ w_qkv).reshape(b, s, 3, NA_HEADS, NA_HEAD_DIM).transpose(2, 0, 3, 1, 4)
    return qkv[0] * (NA_HEAD_DIM ** -0.5), qkv[1], qkv[2]


def na_context(h, w_qkv, w_o):
    b, s, _ = h.shape
    q, k, v = _split_heads(h, w_qkv)
    nblk = s // CTX_Q_BLOCK
    q_blocks = q.reshape(b, NA_HEADS, nblk, CTX_Q_BLOCK, NA_HEAD_DIM).transpose(2, 0, 1, 3, 4)

    def block(q_i):
        sc = jnp.einsum('bhqd,bhkd->bhqk', q_i, k).astype(jnp.float32)
        p = jax.nn.softmax(sc, axis=-1).astype(v.dtype)
        return jnp.einsum('bhqk,bhkd->bhqd', p, v)

    o = lax.map(block, q_blocks)
    o = o.transpose(1, 0, 3, 2, 4).reshape(b, s, D_MODEL)
    return o @ w_o, k, v


def na_latent(h, k_ctx, v_ctx, w_qkv, w_o, rpb):
    b, s, _ = h.shape
    R = s // GRID_W
    kr = min(NA_WIN_ROWS_MAX, R)
    ncb = GRID_W // NA_COL_BLOCK
    q, k, v = _split_heads(h, w_qkv)
    q = q.reshape(b, NA_HEADS, R, GRID_W, NA_HEAD_DIM)
    k = k.reshape(b, NA_HEADS, R, GRID_W, NA_HEAD_DIM)
    v = v.reshape(b, NA_HEADS, R, GRID_W, NA_HEAD_DIM)
    cols = jnp.arange(GRID_W)
    win0 = jnp.clip(cols - NA_WIN_COLS // 2, 0, GRID_W - NA_WIN_COLS).reshape(ncb, NA_COL_BLOCK)
    band0 = jnp.clip(jnp.arange(ncb) * NA_COL_BLOCK - NA_WIN_COLS // 2, 0, GRID_W - NA_COL_BAND)
    band_cols = band0[:, None] + jnp.arange(NA_COL_BAND)[None, :]
    qcols = cols.reshape(ncb, NA_COL_BLOCK)
    kc = band_cols[:, None, :]
    col_ok = (kc >= win0[:, :, None]) & (kc < win0[:, :, None] + NA_WIN_COLS)
    dc_idx = jnp.clip(kc - qcols[:, :, None], 1 - NA_WIN_COLS, NA_WIN_COLS - 1) + NA_WIN_COLS - 1
    k_band = k[:, :, :, band_cols]
    v_band = v[:, :, :, band_cols]
    q_blk = q.reshape(b, NA_HEADS, R, ncb, NA_COL_BLOCK, NA_HEAD_DIM).transpose(2, 0, 1, 3, 4, 5)

    def row(args):
        r, q_r = args
        rs = jnp.clip(r - kr // 2, 0, R - kr)
        kb = lax.dynamic_slice_in_dim(k_band, rs, kr, axis=2)
        vb = lax.dynamic_slice_in_dim(v_band, rs, kr, axis=2)
        dr_idx = rs + jnp.arange(kr) - r + NA_WIN_ROWS_MAX - 1
        bias = rpb[:, dr_idx][:, :, dc_idx].transpose(0, 2, 3, 1, 4)
        s_loc = jnp.einsum('bhnqd,bhinkd->bhnqik', q_r, kb).astype(jnp.float32) + bias.astype(jnp.float32)[None]
        s_loc = jnp.where(col_ok[None, None, :, :, None, :], s_loc, NEG_INF)
        s_ctx = jnp.einsum('bhnqd,bhpd->bhnqp', q_r, k_ctx).astype(jnp.float32)
        sc = jnp.concatenate([s_loc.reshape(b, NA_HEADS, ncb, NA_COL_BLOCK, kr * NA_COL_BAND), s_ctx], axis=-1)
        p = jax.nn.softmax(sc, axis=-1).astype(v.dtype)
        p_loc = p[..., :kr * NA_COL_BAND].reshape(b, NA_HEADS, ncb, NA_COL_BLOCK, kr, NA_COL_BAND)
        p_ctx = p[..., kr * NA_COL_BAND:]
        return (jnp.einsum('bhnqik,bhinkd->bhnqd', p_loc, vb)
                + jnp.einsum('bhnqp,bhpd->bhnqd', p_ctx, v_ctx))

    o = lax.map(row, (jnp.arange(R), q_blk))
    o = o.transpose(1, 0, 3, 4, 2, 5).reshape(b, s, D_MODEL)
    return o @ w_o


def gmlp_mixer(h, w_in, g_v, w_s, b_s, w_out):
    b, s, _ = h.shape
    uv = jax.nn.gelu(h @ w_in)
    u, v = uv[..., :GMLP_DFF], uv[..., GMLP_DFF:]
    v = rmsnorm(v, g_v).reshape(b, s // GMLP_CHUNK, GMLP_CHUNK, GMLP_GROUPS, GMLP_GROUP_DIM)
    v = jnp.einsum('gpq,bnqgc->bnpgc', w_s, v) + b_s.T[:, :, None]
    return (u * v.reshape(b, s, GMLP_DFF)) @ w_out


def dwconv(x, w, bias):
    width = w.shape[0]
    left = width // 2
    y = lax.conv_general_dilated(x, w[:, None, :], window_strides=(1,),
                                 padding=[(left, width - 1 - left)],
                                 dimension_numbers=('NWC', 'WIO', 'NWC'),
                                 feature_group_count=x.shape[-1])
    return y + bias


def ssd_scan(x, dt, a, bm, cm, h0):
    b, s, _, _ = x.shape
    hpg = SSD_HEADS // SSD_GROUPS
    L = SSD_CHUNK
    nc = s // L
    f32 = jnp.float32
    x = x.astype(f32).reshape(b, nc, L, SSD_GROUPS, hpg, SSD_HEAD_DIM)
    dt = dt.reshape(b, nc, L, SSD_GROUPS, hpg)
    bm = bm.astype(f32).reshape(b, nc, L, SSD_GROUPS, SSD_STATE)
    cm = cm.astype(f32).reshape(b, nc, L, SSD_GROUPS, SSD_STATE)
    a_cs = jnp.cumsum(dt * a.reshape(SSD_GROUPS, hpg), axis=2)
    dtx = x * dt[..., None]
    lower = jnp.tril(jnp.ones((L, L), bool))[None, None, :, :, None, None]
    seg = a_cs[:, :, :, None] - a_cs[:, :, None, :]
    decay = jnp.exp(jnp.where(lower, seg, -jnp.inf))
    cb = jnp.einsum('bclgn,bcsgn->bclsg', cm, bm)
    y_diag = jnp.einsum('bclsg,bclsgh,bcsghp->bclghp', cb, decay, dtx)
    decay_to_end = jnp.exp(a_cs[:, :, -1:] - a_cs)
    chunk_states = jnp.einsum('bclgn,bclgh,bclghp->bcghpn', bm, decay_to_end, dtx)
    chunk_decay = jnp.exp(a_cs[:, :, -1])

    def step(state, inp):
        st, dec = inp
        return state * dec[..., None, None] + st, state

    h_init = h0.astype(f32).reshape(b, SSD_GROUPS, hpg, SSD_HEAD_DIM, SSD_STATE)
    h_last, h_prev = lax.scan(step, h_init, (jnp.moveaxis(chunk_states, 1, 0), jnp.moveaxis(chunk_decay, 1, 0)))
    h_prev = jnp.moveaxis(h_prev, 0, 1)
    y_off = jnp.einsum('bclgn,bcghpn,bclgh->bclghp', cm, h_prev, jnp.exp(a_cs))
    y = (y_diag + y_off).reshape(b, s, SSD_HEADS, SSD_HEAD_DIM)
    return y, h_last.reshape(b, SSD_HEADS, SSD_HEAD_DIM, SSD_STATE)


def _flip(t):
    return jnp.flip(t, axis=1)


def ssd_mixer(h, h0_f, h0_b, w_in, conv_w, conv_b, a_log, dt_bias, d_skip, g_norm, w_out):
    b, s, _ = h.shape
    proj = h @ w_in
    z = proj[..., :SSD_D_INNER]
    xbc = jax.nn.silu(dwconv(proj[..., SSD_D_INNER:SSD_D_INNER + SSD_CONV_DIM], conv_w, conv_b))
    dt_raw = proj[..., SSD_D_INNER + SSD_CONV_DIM:].astype(jnp.float32).reshape(b, s, 2, SSD_HEADS)
    gn = SSD_GROUPS * SSD_STATE
    x = xbc[..., :SSD_D_INNER].reshape(b, s, SSD_HEADS, SSD_HEAD_DIM)
    bm = xbc[..., SSD_D_INNER:SSD_D_INNER + gn].reshape(b, s, SSD_GROUPS, SSD_STATE)
    cm = xbc[..., SSD_D_INNER + gn:].reshape(b, s, SSD_GROUPS, SSD_STATE)
    dt = jax.nn.softplus(dt_raw + dt_bias.astype(jnp.float32))
    a = -jnp.exp(a_log.astype(jnp.float32))
    y_f, hf = ssd_scan(x, dt[:, :, 0], a[0], bm, cm, h0_f)
    y_b, hb = ssd_scan(_flip(x), _flip(dt[:, :, 1]), a[1], _flip(bm), _flip(cm), h0_b)
    y = y_f + _flip(y_b) + d_skip.astype(jnp.float32)[:, None] * x.astype(jnp.float32)
    y = y.reshape(b, s, SSD_D_INNER) * jax.nn.silu(z.astype(jnp.float32))
    y = rmsnorm(y, g_norm).astype(h.dtype)
    return y @ w_out, hf, hb


def hier_moe(h, w_gr, b_gr, w_er, b_er, w_gate, w_up, w_down):
    t = h.reshape(-1, D_MODEL)
    g_prob = jax.nn.softmax((t @ w_gr).astype(jnp.float32) + b_gr.astype(jnp.float32), axis=-1)
    g_p, g_idx = lax.top_k(g_prob, 1)
    e_logits = ((t @ w_er).astype(jnp.float32) + b_er.astype(jnp.float32)).reshape(-1, MOE_GROUPS, MOE_EXPERTS_PER_GROUP)
    e_logits = jnp.take_along_axis(e_logits, g_idx[:, :, None], axis=1)[:, 0]
    e_p, e_idx = lax.top_k(jax.nn.softmax(e_logits, axis=-1), MOE_TOP_K)
    w = g_p * e_p / jnp.sum(e_p, axis=-1, keepdims=True)
    sel = jax.nn.one_hot(g_idx * MOE_EXPERTS_PER_GROUP + e_idx, MOE_EXPERTS, dtype=jnp.float32)
    combine = jnp.einsum('tk,tke->te', w, sel)
    hg = jnp.einsum('td,edf->tef', t, w_gate)
    hu = jnp.einsum('td,edf->tef', t, w_up)
    act = jax.nn.silu(hg) * hu * combine[:, :, None].astype(t.dtype)
    return jnp.einsum('tef,efd->td', act, w_down).reshape(h.shape)


def setup_inputs(seed: int = 0) -> dict:
    key = jax.random.key(seed)
    k = jax.random.split(key, 40)
    f32 = jnp.float32

    def nrm(i, shape, scale):
        return jax.random.normal(k[i], shape, f32) * scale

    dt0 = jnp.exp(jax.random.uniform(k[26], (N_LAYERS_SSD, 2, SSD_HEADS), f32, math.log(1e-3), math.log(1e-1)))
    return {
        'x_prompt': nrm(0, (BATCH, SEQ, D_MODEL), 1.0),
        'x_sample': nrm(1, (DEC_BATCH, DEC_SEQ, D_MODEL), 1.0),
        'cache_k': nrm(2, (DEC_BATCH, N_LAYERS_NA, NA_HEADS, PAST_LEN, NA_HEAD_DIM), 1.0),
        'cache_v': nrm(3, (DEC_BATCH, N_LAYERS_NA, NA_HEADS, PAST_LEN, NA_HEAD_DIM), 1.0),
        'state_ssm_fwd': nrm(4, (DEC_BATCH, N_LAYERS_SSD, SSD_HEADS, SSD_HEAD_DIM, SSD_STATE), 0.1),
        'state_ssm_bwd': nrm(5, (DEC_BATCH, N_LAYERS_SSD, SSD_HEADS, SSD_HEAD_DIM, SSD_STATE), 0.1),
        'c': nrm(6, (DEC_BATCH, D_MODEL), 1.0),
        'c_ctx': nrm(7, (D_MODEL,), 1.0),
        'ada_w': nrm(8, (DEPTH, D_MODEL, N_MOD * D_MODEL), 0.5 * D_MODEL ** -0.5),
        'ada_b': nrm(9, (DEPTH, N_MOD * D_MODEL), 0.02),
        'norm1_g': 1.0 + nrm(10, (DEPTH, D_MODEL), 0.02),
        'norm2_g': 1.0 + nrm(11, (DEPTH, D_MODEL), 0.02),
        'final_g': 1.0 + nrm(12, (D_MODEL,), 0.02),
        'fnet_w_o': nrm(13, (N_LAYERS_FOURIER, D_MODEL, D_MODEL), D_MODEL ** -0.5),
        'na_w_qkv': nrm(14, (N_LAYERS_NA, D_MODEL, 3 * D_MODEL), D_MODEL ** -0.5),
        'na_w_o': nrm(15, (N_LAYERS_NA, D_MODEL, D_MODEL), D_MODEL ** -0.5),
        'na_rpb': nrm(16, (N_LAYERS_NA, NA_HEADS, 2 * NA_WIN_ROWS_MAX - 1, 2 * NA_WIN_COLS - 1), 0.1),
        'gmlp_w_in': nrm(17, (N_LAYERS_GMLP, D_MODEL, 2 * GMLP_DFF), D_MODEL ** -0.5),
        'gmlp_g_v': 1.0 + nrm(18, (N_LAYERS_GMLP, GMLP_DFF), 0.02),
        'gmlp_w_s': nrm(19, (N_LAYERS_GMLP, GMLP_GROUPS, GMLP_CHUNK, GMLP_CHUNK), GMLP_CHUNK ** -0.5),
        'gmlp_b_s': 1.0 + nrm(20, (N_LAYERS_GMLP, GMLP_GROUPS, GMLP_CHUNK), 0.1),
        'gmlp_w_out': nrm(21, (N_LAYERS_GMLP, GMLP_DFF, D_MODEL), GMLP_DFF ** -0.5),
        'ssd_w_in': nrm(22, (N_LAYERS_SSD, D_MODEL, SSD_PROJ), D_MODEL ** -0.5),
        'ssd_conv_w': nrm(23, (N_LAYERS_SSD, SSD_CONV, SSD_CONV_DIM), SSD_CONV ** -0.5),
        'ssd_conv_b': nrm(24, (N_LAYERS_SSD, SSD_CONV_DIM), 0.02),
        'ssd_a_log': jnp.log(jax.random.uniform(k[25], (N_LAYERS_SSD, 2, SSD_HEADS), f32, 1.0, 16.0)),
        'ssd_dt_bias': dt0 + jnp.log(-jnp.expm1(-dt0)),
        'ssd_d_skip': 1.0 + nrm(27, (N_LAYERS_SSD, SSD_HEADS), 0.02),
        'ssd_g_norm': 1.0 + nrm(28, (N_LAYERS_SSD, SSD_D_INNER), 0.02),
        'ssd_w_out': nrm(29, (N_LAYERS_SSD, SSD_D_INNER, D_MODEL), SSD_D_INNER ** -0.5),
        'moe_w_gr': nrm(30, (DEPTH, D_MODEL, MOE_GROUPS), D_MODEL ** -0.5),
        'moe_b_gr': nrm(31, (DEPTH, MOE_GROUPS), 0.01),
        'moe_w_er': nrm(32, (DEPTH, D_MODEL, MOE_EXPERTS), D_MODEL ** -0.5),
        'moe_b_er': nrm(33, (DEPTH, MOE_EXPERTS), 0.01),
        'moe_w_gate': nrm(34, (DEPTH, MOE_EXPERTS, D_MODEL, MOE_D_FF), D_MODEL ** -0.5),
        'moe_w_up': nrm(35, (DEPTH, MOE_EXPERTS, D_MODEL, MOE_D_FF), D_MODEL ** -0.5),
        'moe_w_down': nrm(36, (DEPTH, MOE_EXPERTS, MOE_D_FF, D_MODEL), MOE_D_FF ** -0.5),
    }


def reference(x_prompt, x_sample, cache_k, cache_v, state_ssm_fwd, state_ssm_bwd, c, c_ctx,
              ada_w, ada_b, norm1_g, norm2_g, final_g, fnet_w_o,
              na_w_qkv, na_w_o, na_rpb,
              gmlp_w_in, gmlp_g_v, gmlp_w_s, gmlp_b_s, gmlp_w_out,
              ssd_w_in, ssd_conv_w, ssd_conv_b, ssd_a_log, ssd_dt_bias, ssd_d_skip, ssd_g_norm, ssd_w_out,
              moe_w_gr, moe_b_gr, moe_w_er, moe_b_er, moe_w_gate, moe_w_up, moe_w_down):
    hp, hs = x_prompt, x_sample
    bsz = x_prompt.shape[0]
    new_k, new_v, new_sf, new_sb = [], [], [], []
    for l in range(DEPTH):
        kind, j = l % N_MIXERS, l // N_MIXERS
        mp = adaln(c_ctx[None, :], ada_w[l], ada_b[l])
        ms = adaln(c, ada_w[l], ada_b[l])
        ap = modulate(rmsnorm(hp, norm1_g[l]), mp, 0)
        a_s = modulate(rmsnorm(hs, norm1_g[l]), ms, 0)
        if kind == 0:
            op = fourier_mixer(ap, fnet_w_o[j])
            os_ = fourier_mixer(a_s, fnet_w_o[j])
        elif kind == 1:
            op, kp, vp = na_context(ap, na_w_qkv[j], na_w_o[j])
            os_ = na_latent(a_s, cache_k[:, j], cache_v[:, j], na_w_qkv[j], na_w_o[j], na_rpb[j])
            new_k.append(kp)
            new_v.append(vp)
        elif kind == 2:
            op = gmlp_mixer(ap, gmlp_w_in[j], gmlp_g_v[j], gmlp_w_s[j], gmlp_b_s[j], gmlp_w_out[j])
            os_ = gmlp_mixer(a_s, gmlp_w_in[j], gmlp_g_v[j], gmlp_w_s[j], gmlp_b_s[j], gmlp_w_out[j])
        else:
            zero = jnp.zeros((bsz, SSD_HEADS, SSD_HEAD_DIM, SSD_STATE), jnp.float32)
            op, sf, sb = ssd_mixer(ap, zero, zero, ssd_w_in[j], ssd_conv_w[j], ssd_conv_b[j], ssd_a_log[j],
                                   ssd_dt_bias[j], ssd_d_skip[j], ssd_g_norm[j], ssd_w_out[j])
            os_ = ssd_mixer(a_s, state_ssm_fwd[:, j], state_ssm_bwd[:, j], ssd_w_in[j], ssd_conv_w[j], ssd_conv_b[j],
                            ssd_a_log[j], ssd_dt_bias[j], ssd_d_skip[j], ssd_g_norm[j], ssd_w_out[j])[0]
            new_sf.append(sf)
            new_sb.append(sb)
        hp = hp + mp[:, 2][:, None, :] * op
        hs = hs + ms[:, 2][:, None, :] * os_
        hp = hp + mp[:, 5][:, None, :] * hier_moe(modulate(rmsnorm(hp, norm2_g[l]), mp, 3), moe_w_gr[l], moe_b_gr[l],
                                                  moe_w_er[l], moe_b_er[l], moe_w_gate[l], moe_w_up[l], moe_w_down[l])
        hs = hs + ms[:, 5][:, None, :] * hier_moe(modulate(rmsnorm(hs, norm2_g[l]), ms, 3), moe_w_gr[l], moe_b_gr[l],
                                                  moe_w_er[l], moe_b_er[l], moe_w_gate[l], moe_w_up[l], moe_w_down[l])
    y_prompt = rmsnorm(hp, final_g)
    y_sample = rmsnorm(hs, final_g)
    new_cache_k = jnp.stack(new_k, axis=1)
    new_cache_v = jnp.stack(new_v, axis=1)
    new_state_ssm_fwd = jnp.stack(new_sf, axis=1)
    new_state_ssm_bwd = jnp.stack(new_sb, axis=1)
    return (y_prompt, y_sample, new_cache_k, new_cache_v, new_state_ssm_fwd, new_state_ssm_bwd)
```

```python
import functools
import math

import jax
import jax.numpy as jnp
from jax import lax
from jax.experimental import pallas as pl
from jax.experimental.pallas import tpu as pltpu

F32 = jnp.float32
BF16 = jnp.bfloat16

D = 1024
EPS = 1e-6
NEG = -1e30
N_MOD = 6
MOD_ROWS = 8
GRID_W = 64
FNET_GROUPS = 8
NA_HEADS = 16
NA_HD = 64
NA_WIN_ROWS = 8
NA_WIN_COLS = 16
GMLP_CHUNK = 128
GMLP_DFF = 2048
GMLP_GROUPS = 8
SSD_INNER = 2048
SSD_HD = 64
SSD_HEADS = 32
SSD_GROUPS = 8
SSD_STATE = 128
SSD_CHUNK = 128
SSD_HPG = SSD_HEADS // SSD_GROUPS
MOE_GROUPS = 4
MOE_EPG = 4
MOE_EXPERTS = 16
MOE_DFF = 512
ROUTE_W = 128
ROUTE_E0 = 4

TM = 512
VMEM_LIMIT = 56 * 1024 * 1024


def _dot(a, b):
    return jnp.dot(a, b, preferred_element_type=F32)


def _dot_nt(a, b):
    return lax.dot_general(a, b, (((1,), (1,)), ((), ())), preferred_element_type=F32)


def _silu(x):
    return x * (1.0 / (1.0 + jnp.exp(-x)))


def _rms(x, g):
    return x * lax.rsqrt(jnp.mean(x * x, axis=-1, keepdims=True) + EPS) * g


def _normmod(x, g, shift, scale):
    return _rms(x, g) * (1.0 + scale) + shift


def _split3(x):
    hi = x.astype(BF16)
    r = x - hi.astype(F32)
    mid = r.astype(BF16)
    lo = (r - mid.astype(F32)).astype(BF16)
    return hi, mid, lo


def _params(n_axes):
    return pltpu.CompilerParams(dimension_semantics=("arbitrary",) * n_axes,
                                vmem_limit_bytes=VMEM_LIMIT)


def _full(shape):
    nd = len(shape)
    return pl.BlockSpec(shape, lambda *_: (0,) * nd)


def _mod_spec(layer, k, row_fn):
    return pl.BlockSpec((1, 1, D), lambda *idx: ((layer * MOD_ROWS + row_fn(*idx)) * N_MOD + k, 0, 0))


def _ada_kernel(c_ref, w_ref, b_ref, o_ref):
    c = c_ref[...]
    o_ref[0] = _dot(_silu(c).astype(BF16), w_ref[0].astype(BF16)) + b_ref[0]


def _ada_table(cond, ada_w, ada_b):
    depth = ada_w.shape[0]
    n = N_MOD * D
    tn = 1536
    out = pl.pallas_call(
        _ada_kernel,
        grid=(depth, n // tn),
        in_specs=[_full((MOD_ROWS, D)),
                  pl.BlockSpec((1, D, tn), lambda l, j: (l, 0, j)),
                  pl.BlockSpec((1, 1, tn), lambda l, j: (l, 0, j))],
        out_specs=pl.BlockSpec((1, MOD_ROWS, tn), lambda l, j: (l, 0, j)),
        out_shape=jax.ShapeDtypeStruct((depth, MOD_ROWS, n), F32),
        compiler_params=_params(2),
        name="ada_table",
    )(cond, ada_w, ada_b.reshape(depth, 1, n))
    return out.reshape(depth * MOD_ROWS * N_MOD, 1, D)


def _route(h1, g2, sh2, sc2, wr_hi, wr_lo, br):
    xn = _normmod(h1, g2, sh2, sc2)
    xh = xn.astype(BF16)
    xl = (xn - xh.astype(F32)).astype(BF16)
    logits = _dot(xh, wr_hi) + _dot(xh, wr_lo) + _dot(xl, wr_hi) + br
    lane = lax.broadcasted_iota(jnp.int32, logits.shape, 1).astype(F32)
    far = float(ROUTE_W)
    gl = jnp.where(lane < MOE_GROUPS, logits, NEG)
    gmax = jnp.max(gl, axis=-1, keepdims=True)
    g_p = 1.0 / jnp.sum(jnp.exp(gl - gmax), axis=-1, keepdims=True)
    gidx = jnp.min(jnp.where(gl == gmax, lane, far), axis=-1, keepdims=True)
    lo = ROUTE_E0 + MOE_EPG * gidx
    el = jnp.where((lane >= lo) & (lane < lo + MOE_EPG), logits, NEG)
    m1 = jnp.max(el, axis=-1, keepdims=True)
    i1 = jnp.min(jnp.where(el == m1, lane, far), axis=-1, keepdims=True)
    el2 = jnp.where(lane == i1, NEG, el)
    m2 = jnp.max(el2, axis=-1, keepdims=True)
    i2 = jnp.min(jnp.where(el2 == m2, lane, far), axis=-1, keepdims=True)
    e2 = jnp.exp(m2 - m1)
    w1 = g_p / (1.0 + e2)
    w2 = w1 * e2
    route = jnp.where(lane == i1, w1, 0.0) + jnp.where(lane == i2, w2, 0.0)
    route = jnp.where(lane == 0.0, gidx, route)
    return xh, route


def _route_specs(layer, row_fn):
    return [_full((1, D)), _mod_spec(layer, 3, row_fn), _mod_spec(layer, 4, row_fn),
            _full((D, ROUTE_W)), _full((D, ROUTE_W)), _full((1, ROUTE_W))]


def _route_args(lw):
    return [lw["g2"], lw["mod"], lw["mod"], lw["wr_hi"], lw["wr_lo"], lw["br"]]


def _finish(h, o, gate, rt_refs, h1_ref, xn_ref, route_ref):
    g2, sh2, sc2, wr_hi, wr_lo, br = rt_refs
    h1 = h + gate * o
    h1_ref[...] = h1
    xn, route = _route(h1, g2[...], sh2[0], sc2[0], wr_hi[...], wr_lo[...], br[...])
    xn_ref[...] = xn
    route_ref[...] = route


def _stream_outs(t):
    return [jax.ShapeDtypeStruct((t, D), F32), jax.ShapeDtypeStruct((t, D), BF16),
            jax.ShapeDtypeStruct((t, ROUTE_W), F32)]


def _two_stream_call(make_call, t_p, seq_p, seq_s, nb_p, nb_s, args, n_out):
    outs = make_call(seq_p, nb_p, 0, None)(*args)
    outs = make_call(seq_s, nb_s, t_p // seq_s, n_out)(*args, *outs)
    return outs


def _fnet_kernel(h_ref, g1, sh1, sc1, gt1, csc_ref, fs_ref, wo_ref, g2, sh2, sc2, wr_hi, wr_lo, br,
                 *rest, seq):
    h1_ref, xn_ref, route_ref, ab_ref = rest[-4:]
    h = h_ref[...]
    a = _normmod(h, g1[...], sh1[0], sc1[0]).astype(BF16)
    gd = D // FNET_GROUPS
    for g in range(FNET_GROUPS):
        ab = _dot(a[:, g * gd:(g + 1) * gd], csc_ref[...])
        ab_ref[0:seq, g * gd:(g + 1) * gd] = ab[:, :gd].astype(BF16)
        ab_ref[seq:2 * seq, g * gd:(g + 1) * gd] = ab[:, gd:].astype(BF16)
    f = _dot(fs_ref[...], ab_ref[...])
    o = _dot(f.astype(BF16), wo_ref[...])
    _finish(h, o, gt1[0], (g2, sh2, sc2, wr_hi, wr_lo, br), h1_ref, xn_ref, route_ref)


def _dft_tables(n):
    k = jnp.arange(n, dtype=jnp.int32)
    ang = ((k[:, None] * k[None, :]) % n).astype(F32) * (2.0 * math.pi / n)
    s = 1.0 / math.sqrt(n)
    return jnp.cos(ang) * s, jnp.sin(ang) * s


def _fnet_layer(h, lw, layer, dims):
    t, t_p = dims["t"], dims["t_p"]
    gd = D // FNET_GROUPS
    cc, sc = _dft_tables(gd)
    csc = jnp.concatenate([cc, sc], axis=1).astype(BF16)

    def make_call(seq, nb, off, n_alias):
        cs, ss = _dft_tables(seq)
        fs = jnp.concatenate([cs, -ss], axis=1).astype(BF16)
        row_fn = (lambda b: 0) if off == 0 else (lambda b: 1 + b)
        tile = lambda: pl.BlockSpec((seq, D), lambda b: (off + b, 0))
        in_specs = ([tile(), _full((1, D))] + [_mod_spec(layer, k, row_fn) for k in (0, 1, 2)]
                    + [_full((gd, 2 * gd)), _full((seq, 2 * seq)), _full((D, D))]
                    + _route_specs(layer, row_fn))
        aliases = {}
        if n_alias:
            base = len(in_specs)
            in_specs = in_specs + [pl.BlockSpec(memory_space=pl.ANY)] * n_alias
            aliases = {base + i: i for i in range(n_alias)}
        call = pl.pallas_call(
            functools.partial(_fnet_kernel, seq=seq),
            grid=(nb,),
            in_specs=in_specs,
            out_specs=[tile(), tile(), pl.BlockSpec((seq, ROUTE_W), lambda b: (off + b, 0))],
            out_shape=_stream_outs(t),
            scratch_shapes=[pltpu.VMEM((2 * seq, D), BF16)],
            input_output_aliases=aliases,
            compiler_params=_params(1),
            name="fnet_seq%d" % seq,
        )
        return lambda *a: call(*a[:2], *a[2:5], a[5], fs, *a[6:])

    args = [h, lw["g1"], lw["mod"], lw["mod"], lw["mod"], csc, lw["w_o"]] + _route_args(lw)
    return _two_stream_call(make_call, t_p, dims["seq_p"], dims["seq_s"], dims["nb_p"], dims["nb_s"], args, 3)


def _pre_kernel(h_ref, g1, sh1, sc1, w_ref, o_ref, a_ref):
    @pl.when(pl.program_id(1) == 0)
    def _():
        a_ref[...] = _normmod(h_ref[...], g1[...], sh1[0], sc1[0]).astype(BF16)

    o_ref[...] = _dot(a_ref[...], w_ref[...]).astype(o_ref.dtype)


def _tile_row_fn(dims):
    npt = dims["t_p"] // TM
    tps = dims["seq_s"] // TM
    return lambda i, *_: jnp.where(i < npt, 0, 1 + (i - npt) // tps)


def _pre_proj(h, lw, layer, w, tn, out_dtype, dims):
    t = dims["t"]
    n = w.shape[1]
    row_fn = _tile_row_fn(dims)
    return pl.pallas_call(
        _pre_kernel,
        grid=(t // TM, n // tn),
        in_specs=[pl.BlockSpec((TM, D), lambda i, j: (i, 0)), _full((1, D)),
                  _mod_spec(layer, 0, row_fn), _mod_spec(layer, 1, row_fn),
                  pl.BlockSpec((D, tn), lambda i, j: (0, j))],
        out_specs=pl.BlockSpec((TM, tn), lambda i, j: (i, j)),
        out_shape=jax.ShapeDtypeStruct((t, n), out_dtype),
        scratch_shapes=[pltpu.VMEM((TM, D), BF16)],
        compiler_params=_params(2),
        name="pre_proj_l%d_n%d" % (layer, n),
    )(h, lw["g1"], lw["mod"], lw["mod"], w)


def _out_kernel(o_ref, w_ref, h_ref, gt1, g2, sh2, sc2, wr_hi, wr_lo, br, h1_ref, xn_ref, route_ref):
    o = _dot(o_ref[...], w_ref[...])
    _finish(h_ref[...], o, gt1[0], (g2, sh2, sc2, wr_hi, wr_lo, br), h1_ref, xn_ref, route_ref)


def _out_proj(o, w, h, lw, layer, dims):
    t = dims["t"]
    k = o.shape[1]
    row_fn = _tile_row_fn(dims)
    tile = lambda: pl.BlockSpec((TM, D), lambda i: (i, 0))
    return pl.pallas_call(
        _out_kernel,
        grid=(t // TM,),
        in_specs=[pl.BlockSpec((TM, k), lambda i: (i, 0)), _full((k, D)), tile(),
                  _mod_spec(layer, 2, row_fn)] + _route_specs(layer, row_fn),
        out_specs=[tile(), tile(), pl.BlockSpec((TM, ROUTE_W), lambda i: (i, 0))],
        out_shape=_stream_outs(t),
        compiler_params=_params(1),
        name="out_proj_l%d" % layer,
    )(o, w, h, lw["mod"], *_route_args(lw))


def _na_ctx_kernel(q_ref, k_ref, v_ref, o_ref, kc_ref, vc_ref):
    for hd in range(NA_HEADS):
        sl = slice(hd * NA_HD, (hd + 1) * NA_HD)
        q = q_ref[:, sl]
        k = k_ref[:, sl]
        v = v_ref[:, sl]
        s = _dot_nt(q, k) * (NA_HD ** -0.5)
        p = jnp.exp(s - jnp.max(s, axis=-1, keepdims=True))
        l = jnp.sum(p, axis=-1, keepdims=True)
        o = _dot(p.astype(BF16), v) / l
        o_ref[:, sl] = o.astype(BF16)
        kc_ref[0, 0, hd] = k.astype(F32)
        vc_ref[0, 0, hd] = v.astype(F32)


def _na_lat_kernel(q_ref, k_ref, v_ref, bias_ref, kc_ref, vc_ref, o_in, o_ref, *, seq, qb):
    del o_in
    for hh in range(2):
        sl = slice(hh * NA_HD, (hh + 1) * NA_HD)
        k = k_ref[:, sl]
        v = v_ref[:, sl]
        kc = kc_ref[0, 0, hh].astype(BF16)
        vc = vc_ref[0, 0, hh].astype(BF16)
        for b0 in range(0, seq, qb):
            q = q_ref[b0:b0 + qb, sl]
            s1 = _dot_nt(q, k) * (NA_HD ** -0.5) + bias_ref[hh, b0:b0 + qb, :]
            s2 = _dot_nt(q, kc) * (NA_HD ** -0.5)
            m = jnp.maximum(jnp.max(s1, axis=-1, keepdims=True), jnp.max(s2, axis=-1, keepdims=True))
            p1 = jnp.exp(s1 - m)
            p2 = jnp.exp(s2 - m)
            l = jnp.sum(p1, axis=-1, keepdims=True) + jnp.sum(p2, axis=-1, keepdims=True)
            o = (_dot(p1.astype(BF16), v) + _dot(p2.astype(BF16), vc)) / l
            o_ref[b0:b0 + qb, sl] = o.astype(BF16)


def _na_bias(rpb, seq):
    rows = seq // GRID_W
    kr = min(NA_WIN_ROWS, rows)
    r = jnp.arange(rows)
    c = jnp.arange(GRID_W)
    rs = jnp.clip(r - kr // 2, 0, rows - kr)
    ok_r = (r[None, :] >= rs[:, None]) & (r[None, :] < rs[:, None] + kr)
    dr = jnp.clip(r[None, :] - r[:, None] + NA_WIN_ROWS - 1, 0, 2 * NA_WIN_ROWS - 2)
    win0 = jnp.clip(c - NA_WIN_COLS // 2, 0, GRID_W - NA_WIN_COLS)
    ok_c = (c[None, :] >= win0[:, None]) & (c[None, :] < win0[:, None] + NA_WIN_COLS)
    dc = jnp.clip(c[None, :] - c[:, None], 1 - NA_WIN_COLS, NA_WIN_COLS - 1) + NA_WIN_COLS - 1
    b = rpb[:, dr][:, :, :, dc]
    ok = ok_r[:, :, None, None] & ok_c[None, None, :, :]
    b = jnp.where(ok[None], b, NEG).transpose(0, 1, 3, 2, 4)
    return b.reshape(rpb.shape[0], seq, seq).astype(F32)


def _na_layer(h, cache_k, cache_v, j, lw, layer, dims):
    t, t_p, seq_p, seq_s = dims["t"], dims["t_p"], dims["seq_p"], dims["seq_s"]
    nb_p, nb_s = dims["nb_p"], dims["nb_s"]
    qkv = _pre_proj(h, lw, layer, lw["w_qkv"], 1536, BF16, dims)
    cshape = (nb_p, 1, NA_HEADS, seq_p, NA_HD)
    cspec = lambda: pl.BlockSpec((1, 1, NA_HEADS, seq_p, NA_HD), lambda b: (b, 0, 0, 0, 0))
    o, kc, vc = pl.pallas_call(
        _na_ctx_kernel,
        grid=(nb_p,),
        in_specs=[pl.BlockSpec((seq_p, D), lambda b: (b, 0)), pl.BlockSpec((seq_p, D), lambda b: (b, 1)),
                  pl.BlockSpec((seq_p, D), lambda b: (b, 2))],
        out_specs=[pl.BlockSpec((seq_p, D), lambda b: (b, 0)), cspec(), cspec()],
        out_shape=[jax.ShapeDtypeStruct((t, D), BF16), jax.ShapeDtypeStruct(cshape, F32),
                   jax.ShapeDtypeStruct(cshape, F32)],
        compiler_params=_params(1),
        name="na_context",
    )(qkv, qkv, qkv)

    bias = _na_bias(lw["rpb"], seq_s)
    past = cache_k.shape[3]
    off = t_p // seq_s
    npair = NA_HEADS // 2
    pw = 2 * NA_HD
    pspec = lambda: pl.BlockSpec((1, 1, 2, past, NA_HD), lambda hp, b: (b, j, hp, 0, 0))
    o = pl.pallas_call(
        functools.partial(_na_lat_kernel, seq=seq_s, qb=256),
        grid=(npair, nb_s),
        in_specs=[pl.BlockSpec((seq_s, pw), lambda hp, b: (off + b, hp)),
                  pl.BlockSpec((seq_s, pw), lambda hp, b: (off + b, npair + hp)),
                  pl.BlockSpec((seq_s, pw), lambda hp, b: (off + b, 2 * npair + hp)),
                  pl.BlockSpec((2, seq_s, seq_s), lambda hp, b: (hp, 0, 0)),
                  pspec(), pspec(), pl.BlockSpec(memory_space=pl.ANY)],
        out_specs=pl.BlockSpec((seq_s, pw), lambda hp, b: (off + b, hp)),
        out_shape=jax.ShapeDtypeStruct((t, D), BF16),
        input_output_aliases={6: 0},
        compiler_params=_params(2),
        name="na_latent",
    )(qkv, qkv, qkv, bias, cache_k, cache_v, o)
    outs = _out_proj(o, lw["w_o"], h, lw, layer, dims)
    return outs, kc, vc


def _gelu_tanh(x):
    return 0.5 * x * (1.0 + jnp.tanh(math.sqrt(2.0 / math.pi) * (x + 0.044715 * (x * x * x))))


def _gmlp_kernel(h_ref, g1, sh1, sc1, gt1, win_ref, gv_ref, ws_ref, bs_ref, wout_ref,
                 g2, sh2, sc2, wr_hi, wr_lo, br, h1_ref, xn_ref, route_ref, m_ref):
    h = h_ref[...]
    a = _normmod(h, g1[...], sh1[0], sc1[0]).astype(BF16)
    u = _gelu_tanh(_dot(a, win_ref[:, :GMLP_DFF]))
    v = _gelu_tanh(_dot(a, win_ref[:, GMLP_DFF:]))
    v = _rms(v, gv_ref[...]).astype(BF16)
    gw = GMLP_DFF // GMLP_GROUPS
    for c in range(TM // GMLP_CHUNK):
        rows = slice(c * GMLP_CHUNK, (c + 1) * GMLP_CHUNK)
        for g in range(GMLP_GROUPS):
            cols = slice(g * gw, (g + 1) * gw)
            vs = _dot(ws_ref[g], v[rows, cols]) + bs_ref[g]
            m_ref[rows, cols] = (u[rows, cols] * vs).astype(BF16)
    o = _dot(m_ref[...], wout_ref[...])
    _finish(h, o, gt1[0], (g2, sh2, sc2, wr_hi, wr_lo, br), h1_ref, xn_ref, route_ref)


def _gmlp_layer(h, lw, layer, dims):
    t = dims["t"]
    row_fn = _tile_row_fn(dims)
    gw = GMLP_DFF // GMLP_GROUPS
    tile = lambda: pl.BlockSpec((TM, D), lambda i: (i, 0))
    one = pl.Buffered(1)
    return pl.pallas_call(
        _gmlp_kernel,
        grid=(t // TM,),
        in_specs=[tile(), _full((1, D))] + [_mod_spec(layer, k, row_fn) for k in (0, 1, 2)]
                 + [pl.BlockSpec((D, 2 * GMLP_DFF), lambda i: (0, 0), pipeline_mode=one),
                    _full((1, GMLP_DFF)), _full((GMLP_GROUPS, GMLP_CHUNK, GMLP_CHUNK)),
                    _full((GMLP_GROUPS, GMLP_CHUNK, gw)),
                    pl.BlockSpec((GMLP_DFF, D), lambda i: (0, 0), pipeline_mode=one)]
                 + _route_specs(layer, row_fn),
        out_specs=[tile(), tile(), pl.BlockSpec((TM, ROUTE_W), lambda i: (i, 0))],
        out_shape=_stream_outs(t),
        scratch_shapes=[pltpu.VMEM((TM, GMLP_DFF), BF16)],
        compiler_params=_params(1),
        name="gmlp",
    )(h, lw["g1"], lw["mod"], lw["mod"], lw["mod"], lw["w_in"], lw["g_v"], lw["w_s"], lw["b_s"],
      lw["w_out"], *_route_args(lw))


HALO = 16


def _ssd_scan_kernel(*refs, seq, rev, has_h0, want_state, add_skip):
    (x_ref, bc_ref, dt_ref, cw_ref, cb_ref, dtb_ref, alog_ref, dsk_ref, tri_ref, rep_ref) = refs[:10]
    pos = 10
    h0_ref = None
    if has_h0:
        h0_ref = refs[pos]
        pos += 1
    n_alias = len(refs) - pos - (2 if want_state else 1) - 2
    pos += n_alias
    y_ref = refs[pos]
    st_ref = refs[pos + 1] if want_state else None
    state, cat = refs[-2:]

    L = SSD_CHUNK
    nc = seq // L
    c = pl.program_id(1)
    cc = (nc - 1 - c) if rev else c

    @pl.when(c == 0)
    def _():
        if has_h0:
            for i in range(SSD_INNER // L):
                hpb = L // SSD_HD
                blk = h0_ref[0, 0, i * hpb:(i + 1) * hpb].reshape(L, SSD_STATE)
                state[:, i * L:(i + 1) * L] = blk.T
        else:
            state[...] = jnp.zeros_like(state)

    r0 = pl.multiple_of(cc * L, L)
    rp = pl.multiple_of(jnp.maximum(r0 - HALO, 0), HALO)
    rn = pl.multiple_of(jnp.minimum(r0 + L, seq - HALO), HALO)
    has_prev = (cc > 0).astype(F32)
    has_next = (cc < nc - 1).astype(F32)
    for src, lo in ((x_ref, 0), (bc_ref, SSD_INNER)):
        cols = slice(lo, lo + SSD_INNER)
        cat[0:HALO, cols] = src[pl.ds(rp, HALO), :].astype(F32) * has_prev
        cat[HALO:HALO + L, cols] = src[pl.ds(r0, L), :].astype(F32)
        cat[HALO + L:2 * HALO + L, cols] = src[pl.ds(rn, HALO), :].astype(F32) * has_next
    conv = cb_ref[...] + sum(cat[HALO - 2 + k:HALO - 2 + k + L, :] * cw_ref[k:k + 1, :] for k in range(4))
    xbc = _silu(conv)
    xc = xbc[:, :SSD_INNER]
    bm = xbc[:, SSD_INNER:SSD_INNER + SSD_GROUPS * SSD_STATE]
    cm = xbc[:, SSD_INNER + SSD_GROUPS * SSD_STATE:]

    dtr = dt_ref[pl.ds(r0, L), :] + dtb_ref[...]
    dt = jnp.maximum(dtr, 0.0) + jnp.log(1.0 + jnp.exp(-jnp.abs(dtr)))
    dta = dt * (-jnp.exp(alog_ref[...]))
    tri = tri_ref[...]
    p = sum(_dot(tri, part) for part in _split3(dta))
    pt = p.T
    edge = 0 if rev else L - 1
    p_edge = p[edge:edge + 1, :]
    rep = rep_ref[...]
    dt_x = _dot(dt.astype(BF16), rep)
    ep_x = _dot(jnp.exp(p).astype(BF16), rep)
    dte_x = _dot(jnp.exp(p_edge - p).astype(BF16), rep)
    cdec_x = _dot(jnp.broadcast_to(jnp.exp(p_edge), (8, p.shape[1])).astype(BF16), rep)[0:1, :]

    dtx = xc * dt_x
    dtxb = dtx.astype(BF16)
    xdte = (dtx * dte_x).astype(BF16)
    li = lax.broadcasted_iota(jnp.int32, (L, L), 0)
    si = lax.broadcasted_iota(jnp.int32, (L, L), 1)
    keep = (li <= si) if rev else (li >= si)
    lane0 = SSD_HEADS if rev else 0
    gw = SSD_HPG * SSD_HD
    for g in range(SSD_GROUPS):
        gcols = slice(g * gw, (g + 1) * gw)
        b_g = bm[:, g * SSD_STATE:(g + 1) * SSD_STATE]
        c_g = cm[:, g * SSD_STATE:(g + 1) * SSD_STATE].astype(BF16)
        cb = _dot_nt(c_g, b_g.astype(BF16))
        st_prev = state[:, gcols]
        y_g = _dot(c_g, st_prev.astype(BF16)) * ep_x[:, gcols]
        yd = []
        for hh in range(SSD_HPG):
            hl = lane0 + g * SSD_HPG + hh
            seg = p[:, hl:hl + 1] - pt[hl:hl + 1, :]
            mat = cb * jnp.exp(jnp.where(keep, seg, NEG))
            hc = slice((g * SSD_HPG + hh) * SSD_HD, (g * SSD_HPG + hh + 1) * SSD_HD)
            yd.append(_dot(mat.astype(BF16), dtxb[:, hc]))
        y_g = y_g + jnp.concatenate(yd, axis=1)
        if add_skip:
            y_g = y_g + dsk_ref[:, gcols] * xc[:, gcols]
        y_ref[:, gcols] = y_g
        state[:, gcols] = st_prev * cdec_x[:, gcols] + _dot(b_g.T.astype(BF16), xdte[:, gcols])

    if want_state:
        @pl.when(c == nc - 1)
        def _():
            for i in range(SSD_INNER // L):
                blk = state[:, i * L:(i + 1) * L].T
                st_ref[0, 0, i * (L // SSD_HD):(i + 1) * (L // SSD_HD)] = blk.reshape(L // SSD_HD, SSD_HD, SSD_STATE)


def _ssd_scan(zxbc, dt_raw, lw, h0, j, rev, dims):
    t, t_p, seq_p, seq_s = dims["t"], dims["t_p"], dims["seq_p"], dims["seq_s"]
    nb_p, nb_s = dims["nb_p"], dims["nb_s"]
    L = SSD_CHUNK
    d = 1 if rev else 0
    li = jnp.arange(L)
    tri = ((li[:, None] <= li[None, :]) if rev else (li[:, None] >= li[None, :])).astype(BF16)
    lane = jnp.arange(128)
    col_head = jnp.arange(SSD_INNER) // SSD_HD
    rep = (lane[:, None] == (d * SSD_HEADS + col_head)[None, :]).astype(BF16)
    dsk = jnp.repeat(lw["d_skip"], SSD_HD)[None, :].astype(F32)
    st_shape = (nb_p, 1, SSD_HEADS, SSD_HD, SSD_STATE)

    def make_call(seq, nb, off, has_h0, want_state, n_alias):
        nc = seq // L
        chunk = (lambda b, c: (off * nc + b * nc + (nc - 1 - c), 0)) if rev else (lambda b, c: (off * nc + b * nc + c, 0))
        in_specs = [pl.BlockSpec((seq, SSD_INNER), lambda b, c: (off + b, 1)),
                    pl.BlockSpec((seq, SSD_INNER), lambda b, c: (off + b, 2)),
                    pl.BlockSpec((seq, 128), lambda b, c: (off + b, 0)),
                    _full((4, 2 * SSD_INNER)), _full((1, 2 * SSD_INNER)), _full((1, 128)), _full((1, 128)),
                    _full((1, SSD_INNER)), _full((L, L)), _full((128, SSD_INNER))]
        if has_h0:
            in_specs.append(pl.BlockSpec((1, 1, SSD_HEADS, SSD_HD, SSD_STATE), lambda b, c: (b, j, 0, 0, 0)))
        aliases = {}
        if n_alias:
            aliases = {len(in_specs): 0}
            in_specs.append(pl.BlockSpec(memory_space=pl.ANY))
        out_specs = [pl.BlockSpec((L, SSD_INNER), chunk)]
        out_shape = [jax.ShapeDtypeStruct((t, SSD_INNER), F32)]
        if want_state:
            out_specs.append(pl.BlockSpec((1, 1, SSD_HEADS, SSD_HD, SSD_STATE), lambda b, c: (b, 0, 0, 0, 0)))
            out_shape.append(jax.ShapeDtypeStruct(st_shape, F32))
        return pl.pallas_call(
            functools.partial(_ssd_scan_kernel, seq=seq, rev=rev, has_h0=has_h0, want_state=want_state,
                              add_skip=not rev),
            grid=(nb, nc),
            in_specs=in_specs,
            out_specs=out_specs,
            out_shape=out_shape,
            scratch_shapes=[pltpu.VMEM((SSD_STATE, SSD_INNER), F32),
                            pltpu.VMEM((L + 2 * HALO, 2 * SSD_INNER), F32)],
            input_output_aliases=aliases,
            compiler_params=_params(2),
            name="ssd_scan_%s_seq%d" % ("bwd" if rev else "fwd", seq),
        )

    common = [zxbc, zxbc, dt_raw, lw["conv_w"], lw["conv_b"], lw["dt_bias"], lw["a_log"], dsk, tri, rep]
    y, st = make_call(seq_p, nb_p, 0, False, True, 0)(*common)
    (y,) = make_call(seq_s, nb_s, t_p // seq_s, True, False, 1)(*common, h0, y)
    return y, st


def _ssd_out_kernel(yf_ref, yb_ref, z_ref, gn_ref, w_ref, h_ref, gt1, g2, sh2, sc2, wr_hi, wr_lo, br,
                    h1_ref, xn_ref, route_ref):
    y = (yf_ref[...] + yb_ref[...]) * _silu(z_ref[...].astype(F32))
    o = _dot(_rms(y, gn_ref[...]).astype(BF16), w_ref[...])
    _finish(h_ref[...], o, gt1[0], (g2, sh2, sc2, wr_hi, wr_lo, br), h1_ref, xn_ref, route_ref)


def _ssd_layer(h, state_f, state_b, j, lw, layer, dims):
    t = dims["t"]
    zxbc = _pre_proj(h, lw, layer, lw["w_in"], 1536, BF16, dims)
    dt_raw = _pre_proj(h, lw, layer, lw["w_dt"], 128, F32, dims)
    y_f, st_f = _ssd_scan(zxbc, dt_raw, lw, state_f, j, False, dims)
    y_b, st_b = _ssd_scan(zxbc, dt_raw, lw, state_b, j, True, dims)
    row_fn = _tile_row_fn(dims)
    tile = lambda: pl.BlockSpec((TM, D), lambda i: (i, 0))
    wide = lambda: pl.BlockSpec((TM, SSD_INNER), lambda i: (i, 0))
    outs = pl.pallas_call(
        _ssd_out_kernel,
        grid=(t // TM,),
        in_specs=[wide(), wide(), wide(), _full((1, SSD_INNER)), _full((SSD_INNER, D)), tile(),
                  _mod_spec(layer, 2, row_fn)] + _route_specs(layer, row_fn),
        out_specs=[tile(), tile(), pl.BlockSpec((TM, ROUTE_W), lambda i: (i, 0))],
        out_shape=_stream_outs(t),
        compiler_params=_params(1),
        name="ssd_out",
    )(y_f, y_b, zxbc, lw["g_norm"], lw["w_out"], h, lw["mod"], *_route_args(lw))
    return outs, st_f, st_b


def _moe_kernel(grp_ref, xn_ref, route_ref, wg_ref, wu_ref, wd_ref, h1_ref, gt2, fg_ref, out_ref, acc_ref,
                *, ept, final):
    i = pl.program_id(0)
    j = pl.program_id(1)

    @pl.when(j == 0)
    def _():
        acc_ref[...] = jnp.zeros_like(acc_ref)

    x = xn_ref[...]
    hg = _dot(x, wg_ref[0].astype(BF16))
    hu = _dot(x, wu_ref[0].astype(BF16))
    route = route_ref[...]
    lane = lax.broadcasted_iota(jnp.int32, route.shape, 1)
    cw = jnp.sum(jnp.where(lane == ROUTE_E0 + grp_ref[i] * ept + j, route, 0.0), axis=-1, keepdims=True)
    act = _silu(hg) * hu * cw
    acc_ref[...] += _dot(act.astype(BF16), wd_ref[0].astype(BF16))

    @pl.when(j == ept - 1)
    def _():
        h2 = h1_ref[...] + gt2[0] * acc_ref[...]
        out_ref[...] = _rms(h2, fg_ref[...]) if final else h2


def _moe_dense(h1, xn, route, lw, layer, final_g, final, dims):
    t = dims["t"]
    row_fn = _tile_row_fn(dims)
    ept = MOE_EXPERTS
    grp = jnp.zeros((t // TM,), jnp.int32)
    ex = lambda i, j, g: (g[i] * ept + j, 0, 0)
    tile = lambda: pl.BlockSpec((TM, D), lambda i, j, g: (i, 0))
    grid_spec = pltpu.PrefetchScalarGridSpec(
        num_scalar_prefetch=1,
        grid=(t // TM, ept),
        in_specs=[tile(), pl.BlockSpec((TM, ROUTE_W), lambda i, j, g: (i, 0)),
                  pl.BlockSpec((1, D, MOE_DFF), ex), pl.BlockSpec((1, D, MOE_DFF), ex),
                  pl.BlockSpec((1, MOE_DFF, D), ex), tile(),
                  pl.BlockSpec((1, 1, D), lambda i, j, g: ((layer * MOD_ROWS + row_fn(i)) * N_MOD + 5, 0, 0)),
                  pl.BlockSpec((1, D), lambda i, j, g: (0, 0))],
        out_specs=tile(),
        scratch_shapes=[pltpu.VMEM((TM, D), F32)],
    )
    return pl.pallas_call(
        functools.partial(_moe_kernel, ept=ept, final=final),
        grid_spec=grid_spec,
        out_shape=jax.ShapeDtypeStruct((t, D), F32),
        compiler_params=_params(2),
        name="moe_l%d" % layer,
    )(grp, xn, route, lw["w_gate"], lw["w_up"], lw["w_down"], h1, lw["mod"], final_g)


def kernel(x_prompt, x_sample, cache_k, cache_v, state_ssm_fwd, state_ssm_bwd, c, c_ctx, ada_w, ada_b, norm1_g, norm2_g, final_g, fnet_w_o, na_w_qkv, na_w_o, na_rpb, gmlp_w_in, gmlp_g_v, gmlp_w_s, gmlp_b_s, gmlp_w_out, ssd_w_in, ssd_conv_w, ssd_conv_b, ssd_a_log, ssd_dt_bias, ssd_d_skip, ssd_g_norm, ssd_w_out, moe_w_gr, moe_b_gr, moe_w_er, moe_b_er, moe_w_gate, moe_w_up, moe_w_down):
    nb_p, seq_p, _ = x_prompt.shape
    nb_s, seq_s, _ = x_sample.shape
    depth = ada_w.shape[0]
    t_p, t_s = nb_p * seq_p, nb_s * seq_s
    dims = dict(t=t_p + t_s, t_p=t_p, seq_p=seq_p, seq_s=seq_s, nb_p=nb_p, nb_s=nb_s)
    assert 1 + nb_s <= MOD_ROWS and t_p % seq_s == 0 and t_p % TM == 0 and seq_s % TM == 0

    cond = jnp.zeros((MOD_ROWS, D), F32).at[0].set(c_ctx).at[1:1 + nb_s].set(c)
    mod = _ada_table(cond, ada_w, ada_b)
    h = jnp.concatenate([x_prompt.reshape(t_p, D), x_sample.reshape(t_s, D)], axis=0)
    fg = final_g.reshape(1, D)

    new_k, new_v, new_sf, new_sb = [], [], [], []
    for l in range(depth):
        kind, j = l % 4, l // 4
        w_r = jnp.concatenate([moe_w_gr[l], moe_w_er[l]], axis=1)
        w_r = jnp.pad(w_r, ((0, 0), (0, ROUTE_W - MOE_GROUPS - MOE_EXPERTS)))
        b_r = jnp.pad(jnp.concatenate([moe_b_gr[l], moe_b_er[l]]), (0, ROUTE_W - MOE_GROUPS - MOE_EXPERTS))
        wr_hi = w_r.astype(BF16)
        lw = dict(mod=mod, g1=norm1_g[l].reshape(1, D), g2=norm2_g[l].reshape(1, D),
                  wr_hi=wr_hi, wr_lo=(w_r - wr_hi.astype(F32)).astype(BF16), br=b_r.reshape(1, ROUTE_W),
                  w_gate=moe_w_gate[l], w_up=moe_w_up[l], w_down=moe_w_down[l])
        if kind == 0:
            lw.update(w_o=fnet_w_o[j].astype(BF16))
            h1, xn, route = _fnet_layer(h, lw, l, dims)
        elif kind == 1:
            lw.update(w_qkv=na_w_qkv[j].astype(BF16), w_o=na_w_o[j].astype(BF16), rpb=na_rpb[j])
            (h1, xn, route), kc, vc = _na_layer(h, cache_k, cache_v, j, lw, l, dims)
            new_k.append(kc)
            new_v.append(vc)
        elif kind == 2:
            gw = GMLP_DFF // GMLP_GROUPS
            lw.update(w_in=gmlp_w_in[j].astype(BF16), g_v=gmlp_g_v[j].reshape(1, GMLP_DFF),
                      w_s=gmlp_w_s[j].astype(BF16),
                      b_s=jnp.broadcast_to(gmlp_b_s[j][:, :, None], (GMLP_GROUPS, GMLP_CHUNK, gw)),
                      w_out=gmlp_w_out[j].astype(BF16))
            h1, xn, route = _gmlp_layer(h, lw, l, dims)
        else:
            n_main = 3 * SSD_INNER
            w_in = ssd_w_in[j]
            pad = lambda v: jnp.pad(v, ((0, 0), (0, 128 - 2 * SSD_HEADS)))
            lw.update(w_in=w_in[:, :n_main].astype(BF16), w_dt=pad(w_in[:, n_main:]).astype(BF16),
                      conv_w=ssd_conv_w[j], conv_b=ssd_conv_b[j].reshape(1, -1),
                      dt_bias=pad(ssd_dt_bias[j].reshape(1, -1)), a_log=pad(ssd_a_log[j].reshape(1, -1)),
                      d_skip=ssd_d_skip[j], g_norm=ssd_g_norm[j].reshape(1, SSD_INNER),
                      w_out=ssd_w_out[j].astype(BF16))
            (h1, xn, route), sf, sb = _ssd_layer(h, state_ssm_fwd, state_ssm_bwd, j, lw, l, dims)
            new_sf.append(sf)
            new_sb.append(sb)
        h = _moe_dense(h1, xn, route, lw, l, fg, l == depth - 1, dims)

    y_prompt = h[:t_p].reshape(nb_p, seq_p, D)
    y_sample = h[t_p:].reshape(nb_s, seq_s, D)
    cat = lambda xs: jnp.concatenate(xs, axis=1)
    return (y_prompt, y_sample, cat(new_k), cat(new_v), cat(new_sf), cat(new_sb))
```

```python
import functools
import math

import jax
import jax.numpy as jnp
from jax import lax
from jax.experimental import pallas as pl
from jax.experimental.pallas import tpu as pltpu

F32 = jnp.float32
BF16 = jnp.bfloat16

D = 1024
EPS = 1e-6
NEG = -1e30
N_MOD = 6
MOD_ROWS = 8
GRID_W = 64
FNET_GROUPS = 8
NA_HEADS = 16
NA_HD = 64
NA_WIN_ROWS = 8
NA_WIN_COLS = 16
GMLP_CHUNK = 128
GMLP_DFF = 2048
GMLP_GROUPS = 8
SSD_INNER = 2048
SSD_HD = 64
SSD_HEADS = 32
SSD_GROUPS = 8
SSD_STATE = 128
SSD_CHUNK = 128
SSD_HPG = SSD_HEADS // SSD_GROUPS
MOE_GROUPS = 4
MOE_EPG = 4
MOE_EXPERTS = 16
MOE_DFF = 512
ROUTE_W = 128
ROUTE_E0 = 4

TM = 512
VMEM_LIMIT = 56 * 1024 * 1024


def _dot(a, b):
    return jnp.dot(a, b, preferred_element_type=F32)


def _dot_nt(a, b):
    return lax.dot_general(a, b, (((1,), (1,)), ((), ())), preferred_element_type=F32)


def _silu(x):
    return x * (1.0 / (1.0 + jnp.exp(-x)))


def _rms(x, g):
    return x * lax.rsqrt(jnp.mean(x * x, axis=-1, keepdims=True) + EPS) * g


def _normmod(x, g, shift, scale):
    return _rms(x, g) * (1.0 + scale) + shift


def _split3(x):
    hi = x.astype(BF16)
    r = x - hi.astype(F32)
    mid = r.astype(BF16)
    lo = (r - mid.astype(F32)).astype(BF16)
    return hi, mid, lo


def _params(n_axes):
    return pltpu.CompilerParams(dimension_semantics=("arbitrary",) * n_axes,
                                vmem_limit_bytes=VMEM_LIMIT)


def _full(shape):
    nd = len(shape)
    return pl.BlockSpec(shape, lambda *_: (0,) * nd)


def _mod_spec(layer, k, row_fn):
    return pl.BlockSpec((1, 1, D), lambda *idx: ((layer * MOD_ROWS + row_fn(*idx)) * N_MOD + k, 0, 0))


def _ada_kernel(c_ref, w_ref, b_ref, o_ref):
    c = c_ref[...]
    o_ref[0] = _dot(_silu(c).astype(BF16), w_ref[0].astype(BF16)) + b_ref[0]


def _ada_table(cond, ada_w, ada_b):
    depth = ada_w.shape[0]
    n = N_MOD * D
    tn = 1536
    out = pl.pallas_call(
        _ada_kernel,
        grid=(depth, n // tn),
        in_specs=[_full((MOD_ROWS, D)),
                  pl.BlockSpec((1, D, tn), lambda l, j: (l, 0, j)),
                  pl.BlockSpec((1, 1, tn), lambda l, j: (l, 0, j))],
        out_specs=pl.BlockSpec((1, MOD_ROWS, tn), lambda l, j: (l, 0, j)),
        out_shape=jax.ShapeDtypeStruct((depth, MOD_ROWS, n), F32),
        compiler_params=_params(2),
        name="ada_table",
    )(cond, ada_w, ada_b.reshape(depth, 1, n))
    return out.reshape(depth * MOD_ROWS * N_MOD, 1, D)


def _route(h1, g2, sh2, sc2, wr_hi, wr_lo, br):
    xn = _normmod(h1, g2, sh2, sc2)
    xh = xn.astype(BF16)
    xl = (xn - xh.astype(F32)).astype(BF16)
    logits = _dot(xh, wr_hi) + _dot(xh, wr_lo) + _dot(xl, wr_hi) + br
    lane = lax.broadcasted_iota(jnp.int32, logits.shape, 1).astype(F32)
    far = float(ROUTE_W)
    gl = jnp.where(lane < MOE_GROUPS, logits, NEG)
    gmax = jnp.max(gl, axis=-1, keepdims=True)
    g_p = 1.0 / jnp.sum(jnp.exp(gl - gmax), axis=-1, keepdims=True)
    gidx = jnp.min(jnp.where(gl == gmax, lane, far), axis=-1, keepdims=True)
    lo = ROUTE_E0 + MOE_EPG * gidx
    el = jnp.where((lane >= lo) & (lane < lo + MOE_EPG), logits, NEG)
    m1 = jnp.max(el, axis=-1, keepdims=True)
    i1 = jnp.min(jnp.where(el == m1, lane, far), axis=-1, keepdims=True)
    el2 = jnp.where(lane == i1, NEG, el)
    m2 = jnp.max(el2, axis=-1, keepdims=True)
    i2 = jnp.min(jnp.where(el2 == m2, lane, far), axis=-1, keepdims=True)
    e2 = jnp.exp(m2 - m1)
    w1 = g_p / (1.0 + e2)
    w2 = w1 * e2
    route = jnp.where(lane == i1, w1, 0.0) + jnp.where(lane == i2, w2, 0.0)
    route = jnp.where(lane == 0.0, gidx, route)
    return xh, route


def _route_specs(layer, row_fn):
    return [_full((1, D)), _mod_spec(layer, 3, row_fn), _mod_spec(layer, 4, row_fn),
            _full((D, ROUTE_W)), _full((D, ROUTE_W)), _full((1, ROUTE_W))]


def _route_args(lw):
    return [lw["g2"], lw["mod"], lw["mod"], lw["wr_hi"], lw["wr_lo"], lw["br"]]


def _finish(h, o, gate, rt_refs, h1_ref, xn_ref, route_ref):
    g2, sh2, sc2, wr_hi, wr_lo, br = rt_refs
    h1 = h + gate * o
    h1_ref[...] = h1
    xn, route = _route(h1, g2[...], sh2[0], sc2[0], wr_hi[...], wr_lo[...], br[...])
    xn_ref[...] = xn
    route_ref[...] = route


def _stream_outs(t):
    return [jax.ShapeDtypeStruct((t, D), F32), jax.ShapeDtypeStruct((t, D), BF16),
            jax.ShapeDtypeStruct((t, ROUTE_W), F32)]


def _fnet_kernel(h_ref, g1, sh1, sc1, gt1, csc_ref, fs_ref, wo_ref, g2, sh2, sc2, wr_hi, wr_lo, br,
                 *rest, seq):
    h1_ref, xn_ref, route_ref, ab_ref = rest[-4:]
    h = h_ref[...]
    a = _normmod(h, g1[...], sh1[0], sc1[0]).astype(BF16)
    gd = D // FNET_GROUPS
    for g in range(FNET_GROUPS):
        ab = _dot(a[:, g * gd:(g + 1) * gd], csc_ref[...])
        ab_ref[0:seq, g * gd:(g + 1) * gd] = ab[:, :gd].astype(BF16)
        ab_ref[seq:2 * seq, g * gd:(g + 1) * gd] = ab[:, gd:].astype(BF16)
    f = _dot(fs_ref[...], ab_ref[...])
    o = _dot(f.astype(BF16), wo_ref[...])
    _finish(h, o, gt1[0], (g2, sh2, sc2, wr_hi, wr_lo, br), h1_ref, xn_ref, route_ref)


def _dft_tables(n):
    k = jnp.arange(n, dtype=jnp.int32)
    ang = ((k[:, None] * k[None, :]) % n).astype(F32) * (2.0 * math.pi / n)
    s = 1.0 / math.sqrt(n)
    return jnp.cos(ang) * s, jnp.sin(ang) * s


def _fnet_layer(h, lw, layer, dims):
    t, t_p = dims["t"], dims["t_p"]
    gd = D // FNET_GROUPS
    cc, sc = _dft_tables(gd)
    csc = jnp.concatenate([cc, sc], axis=1).astype(BF16)

    split_in = isinstance(h, tuple)

    def make_call(seq, nb, off, n_alias):
        cs, ss = _dft_tables(seq)
        fs = jnp.concatenate([cs, -ss], axis=1).astype(BF16)
        row_fn = (lambda b: 0) if off == 0 else (lambda b: 1 + b)
        in_off = 0 if split_in else off
        tile = lambda: pl.BlockSpec((seq, D), lambda b: (off + b, 0))
        in_specs = ([pl.BlockSpec((seq, D), lambda b: (in_off + b, 0)), _full((1, D))]
                    + [_mod_spec(layer, k, row_fn) for k in (0, 1, 2)]
                    + [_full((gd, 2 * gd)), _full((seq, 2 * seq)), _full((D, D))]
                    + _route_specs(layer, row_fn))
        aliases = {}
        if n_alias:
            base = len(in_specs)
            in_specs = in_specs + [pl.BlockSpec(memory_space=pl.ANY)] * n_alias
            aliases = {base + i: i for i in range(n_alias)}
        call = pl.pallas_call(
            functools.partial(_fnet_kernel, seq=seq),
            grid=(nb,),
            in_specs=in_specs,
            out_specs=[tile(), tile(), pl.BlockSpec((seq, ROUTE_W), lambda b: (off + b, 0))],
            out_shape=_stream_outs(t),
            scratch_shapes=[pltpu.VMEM((2 * seq, D), BF16)],
            input_output_aliases=aliases,
            compiler_params=_params(1),
            name="fnet_seq%d" % seq,
        )
        return lambda *a: call(*a[:2], *a[2:5], a[5], fs, *a[6:])

    h_p, h_s = h if split_in else (h, h)
    args = [lw["g1"], lw["mod"], lw["mod"], lw["mod"], csc, lw["w_o"]] + _route_args(lw)
    outs = make_call(dims["seq_p"], dims["nb_p"], 0, None)(h_p, *args)
    return make_call(dims["seq_s"], dims["nb_s"], t_p // dims["seq_s"], 3)(h_s, *args, *outs)


def _pre_kernel(h_ref, g1, sh1, sc1, w_ref, o_ref, a_ref):
    @pl.when(pl.program_id(1) == 0)
    def _():
        a_ref[...] = _normmod(h_ref[...], g1[...], sh1[0], sc1[0]).astype(BF16)

    o_ref[...] = _dot(a_ref[...], w_ref[...]).astype(o_ref.dtype)


def _tile_row_fn(dims, tm=TM):
    npt = dims["t_p"] // tm
    tps = dims["seq_s"] // tm
    return lambda i, *_: jnp.where(i < npt, 0, 1 + (i - npt) // tps)


def _pre_proj(h, lw, layer, w, tn, out_dtype, dims):
    t = dims["t"]
    n = w.shape[1]
    row_fn = _tile_row_fn(dims)
    return pl.pallas_call(
        _pre_kernel,
        grid=(t // TM, n // tn),
        in_specs=[pl.BlockSpec((TM, D), lambda i, j: (i, 0)), _full((1, D)),
                  _mod_spec(layer, 0, row_fn), _mod_spec(layer, 1, row_fn),
                  pl.BlockSpec((D, tn), lambda i, j: (0, j))],
        out_specs=pl.BlockSpec((TM, tn), lambda i, j: (i, j)),
        out_shape=jax.ShapeDtypeStruct((t, n), out_dtype),
        scratch_shapes=[pltpu.VMEM((TM, D), BF16)],
        compiler_params=_params(2),
        name="pre_proj_l%d_n%d" % (layer, n),
    )(h, lw["g1"], lw["mod"], lw["mod"], w)


def _out_kernel(o_ref, w_ref, h_ref, gt1, g2, sh2, sc2, wr_hi, wr_lo, br, h1_ref, xn_ref, route_ref):
    o = _dot(o_ref[...], w_ref[...])
    _finish(h_ref[...], o, gt1[0], (g2, sh2, sc2, wr_hi, wr_lo, br), h1_ref, xn_ref, route_ref)


def _out_proj(o, w, h, lw, layer, dims):
    t = dims["t"]
    k = o.shape[1]
    row_fn = _tile_row_fn(dims)
    tile = lambda: pl.BlockSpec((TM, D), lambda i: (i, 0))
    return pl.pallas_call(
        _out_kernel,
        grid=(t // TM,),
        in_specs=[pl.BlockSpec((TM, k), lambda i: (i, 0)), _full((k, D)), tile(),
                  _mod_spec(layer, 2, row_fn)] + _route_specs(layer, row_fn),
        out_specs=[tile(), tile(), pl.BlockSpec((TM, ROUTE_W), lambda i: (i, 0))],
        out_shape=_stream_outs(t),
        compiler_params=_params(1),
        name="out_proj_l%d" % layer,
    )(o, w, h, lw["mod"], *_route_args(lw))


def _na_ctx_kernel(q_ref, k_ref, v_ref, o_ref, kc_ref, vc_ref):
    for hd in range(NA_HEADS):
        sl = slice(hd * NA_HD, (hd + 1) * NA_HD)
        q = q_ref[:, sl]
        k = k_ref[:, sl]
        v = v_ref[:, sl]
        s = _dot_nt(q, k) * (NA_HD ** -0.5)
        p = jnp.exp(s - jnp.max(s, axis=-1, keepdims=True))
        l = jnp.sum(p, axis=-1, keepdims=True)
        o = _dot(p.astype(BF16), v) / l
        o_ref[:, sl] = o.astype(BF16)
        kc_ref[0, 0, hd] = k.astype(F32)
        vc_ref[0, 0, hd] = v.astype(F32)


def _na_window_bias(bias_ref, hh, qr, rows):
    kr = min(NA_WIN_ROWS, rows)
    rs = min(max(qr - kr // 2, 0), rows - kr)
    blocks = []
    for m in range(rows // 2):
        ok0 = rs <= 2 * m < rs + kr
        ok1 = rs <= 2 * m + 1 < rs + kr
        e = 2 * m - qr + NA_WIN_ROWS
        if ok0 and ok1:
            blocks.append(bias_ref[hh, 0, e])
        elif ok1:
            blocks.append(bias_ref[hh, 1, e])
        elif ok0:
            blocks.append(bias_ref[hh, 2, e])
        else:
            blocks.append(jnp.full((GRID_W, 2 * GRID_W), NEG, F32))
    return jnp.concatenate(blocks, axis=1)


def _na_lat_kernel(q_ref, k_ref, v_ref, bias_ref, kc_ref, vc_ref, o_in, o_ref, *, seq, qb):
    del o_in
    rows = seq // GRID_W
    for hh in range(2):
        sl = slice(hh * NA_HD, (hh + 1) * NA_HD)
        k = k_ref[:, sl]
        v = v_ref[:, sl]
        kc = kc_ref[0, 0, hh].astype(BF16)
        vc = vc_ref[0, 0, hh].astype(BF16)
        for b0 in range(0, seq, qb):
            q = q_ref[b0:b0 + qb, sl]
            bias = jnp.concatenate([_na_window_bias(bias_ref, hh, qr, rows)
                                    for qr in range(b0 // GRID_W, (b0 + qb) // GRID_W)], axis=0)
            s1 = _dot_nt(q, k) * (NA_HD ** -0.5) + bias
            s2 = _dot_nt(q, kc) * (NA_HD ** -0.5)
            m = jnp.maximum(jnp.max(s1, axis=-1, keepdims=True), jnp.max(s2, axis=-1, keepdims=True))
            p1 = jnp.exp(s1 - m)
            p2 = jnp.exp(s2 - m)
            l = jnp.sum(p1, axis=-1, keepdims=True) + jnp.sum(p2, axis=-1, keepdims=True)
            o = (_dot(p1.astype(BF16), v) + _dot(p2.astype(BF16), vc)) / l
            o_ref[b0:b0 + qb, sl] = o.astype(BF16)


def _na_bias_tables(rpb):
    c = jnp.arange(GRID_W)
    win0 = jnp.clip(c - NA_WIN_COLS // 2, 0, GRID_W - NA_WIN_COLS)
    ok_c = (c[None, :] >= win0[:, None]) & (c[None, :] < win0[:, None] + NA_WIN_COLS)
    dc = jnp.clip(c[None, :] - c[:, None], 1 - NA_WIN_COLS, NA_WIN_COLS - 1) + NA_WIN_COLS - 1
    cm = jnp.where(ok_c[None, None], rpb[:, :, dc], NEG).astype(F32)
    neg = jnp.full_like(cm[:, :1], NEG)
    ext = jnp.concatenate([neg, cm, neg], axis=1)
    a, b = ext[:, :-1], ext[:, 1:]
    negs = jnp.full_like(a, NEG)
    pair = lambda x, y: jnp.concatenate([x, y], axis=-1)
    return jnp.stack([pair(a, b), pair(negs, b), pair(a, negs)], axis=1)


def _na_layer(h, cache_k, cache_v, j, lw, layer, dims):
    t, t_p, seq_p, seq_s = dims["t"], dims["t_p"], dims["seq_p"], dims["seq_s"]
    nb_p, nb_s = dims["nb_p"], dims["nb_s"]
    qkv = _pre_proj(h, lw, layer, lw["w_qkv"], 1536, BF16, dims)
    cshape = (nb_p, 1, NA_HEADS, seq_p, NA_HD)
    cspec = lambda: pl.BlockSpec((1, 1, NA_HEADS, seq_p, NA_HD), lambda b: (b, 0, 0, 0, 0))
    o, kc, vc = pl.pallas_call(
        _na_ctx_kernel,
        grid=(nb_p,),
        in_specs=[pl.BlockSpec((seq_p, D), lambda b: (b, 0)), pl.BlockSpec((seq_p, D), lambda b: (b, 1)),
                  pl.BlockSpec((seq_p, D), lambda b: (b, 2))],
        out_specs=[pl.BlockSpec((seq_p, D), lambda b: (b, 0)), cspec(), cspec()],
        out_shape=[jax.ShapeDtypeStruct((t, D), BF16), jax.ShapeDtypeStruct(cshape, F32),
                   jax.ShapeDtypeStruct(cshape, F32)],
        compiler_params=_params(1),
        name="na_context",
    )(qkv, qkv, qkv)

    bias = _na_bias_tables(lw["rpb"])
    past = cache_k.shape[3]
    off = t_p // seq_s
    npair = NA_HEADS // 2
    pw = 2 * NA_HD
    pspec = lambda: pl.BlockSpec((1, 1, 2, past, NA_HD), lambda hp, b: (b, j, hp, 0, 0))
    o = pl.pallas_call(
        functools.partial(_na_lat_kernel, seq=seq_s, qb=256),
        grid=(npair, nb_s),
        in_specs=[pl.BlockSpec((seq_s, pw), lambda hp, b: (off + b, hp)),
                  pl.BlockSpec((seq_s, pw), lambda hp, b: (off + b, npair + hp)),
                  pl.BlockSpec((seq_s, pw), lambda hp, b: (off + b, 2 * npair + hp)),
                  pl.BlockSpec((2, 3, 2 * NA_WIN_ROWS, GRID_W, 2 * GRID_W), lambda hp, b: (hp, 0, 0, 0, 0)),
                  pspec(), pspec(), pl.BlockSpec(memory_space=pl.ANY)],
        out_specs=pl.BlockSpec((seq_s, pw), lambda hp, b: (off + b, hp)),
        out_shape=jax.ShapeDtypeStruct((t, D), BF16),
        input_output_aliases={6: 0},
        compiler_params=_params(2),
        name="na_latent",
    )(qkv, qkv, qkv, bias, cache_k, cache_v, o)
    outs = _out_proj(o, lw["w_o"], h, lw, layer, dims)
    return outs, kc, vc


def _gelu_tanh(x):
    return 0.5 * x * (1.0 + jnp.tanh(math.sqrt(2.0 / math.pi) * (x + 0.044715 * (x * x * x))))


def _gmlp_kernel(h_ref, g1, sh1, sc1, gt1, win_ref, gv_ref, ws_ref, bs_ref, wout_ref,
                 g2, sh2, sc2, wr_hi, wr_lo, br, h1_ref, xn_ref, route_ref, m_ref):
    h = h_ref[...]
    a = _normmod(h, g1[...], sh1[0], sc1[0]).astype(BF16)
    u = _gelu_tanh(_dot(a, win_ref[:, :GMLP_DFF]))
    v = _gelu_tanh(_dot(a, win_ref[:, GMLP_DFF:]))
    v = _rms(v, gv_ref[...]).astype(BF16)
    gw = GMLP_DFF // GMLP_GROUPS
    for c in range(TM // GMLP_CHUNK):
        rows = slice(c * GMLP_CHUNK, (c + 1) * GMLP_CHUNK)
        for g in range(GMLP_GROUPS):
            cols = slice(g * gw, (g + 1) * gw)
            vs = _dot(ws_ref[g], v[rows, cols]) + bs_ref[g]
            m_ref[rows, cols] = (u[rows, cols] * vs).astype(BF16)
    o = _dot(m_ref[...], wout_ref[...])
    _finish(h, o, gt1[0], (g2, sh2, sc2, wr_hi, wr_lo, br), h1_ref, xn_ref, route_ref)


def _gmlp_layer(h, lw, layer, dims):
    t = dims["t"]
    row_fn = _tile_row_fn(dims)
    gw = GMLP_DFF // GMLP_GROUPS
    tile = lambda: pl.BlockSpec((TM, D), lambda i: (i, 0))
    one = pl.Buffered(1)
    return pl.pallas_call(
        _gmlp_kernel,
        grid=(t // TM,),
        in_specs=[tile(), _full((1, D))] + [_mod_spec(layer, k, row_fn) for k in (0, 1, 2)]
                 + [pl.BlockSpec((D, 2 * GMLP_DFF), lambda i: (0, 0), pipeline_mode=one),
                    _full((1, GMLP_DFF)), _full((GMLP_GROUPS, GMLP_CHUNK, GMLP_CHUNK)),
                    _full((GMLP_GROUPS, GMLP_CHUNK, gw)),
                    pl.BlockSpec((GMLP_DFF, D), lambda i: (0, 0), pipeline_mode=one)]
                 + _route_specs(layer, row_fn),
        out_specs=[tile(), tile(), pl.BlockSpec((TM, ROUTE_W), lambda i: (i, 0))],
        out_shape=_stream_outs(t),
        scratch_shapes=[pltpu.VMEM((TM, GMLP_DFF), BF16)],
        compiler_params=_params(1),
        name="gmlp",
    )(h, lw["g1"], lw["mod"], lw["mod"], lw["mod"], lw["w_in"], lw["g_v"], lw["w_s"], lw["b_s"],
      lw["w_out"], *_route_args(lw))


HALO = 16


def _ssd_conv_kernel(x_ref, bc_ref, cw_ref, cb_ref, *rest, seq):
    o_ref, cat = rest[-2:]
    L = SSD_CHUNK
    nc = seq // L
    c = pl.program_id(1)
    r0 = pl.multiple_of(c * L, L)
    rp = pl.multiple_of(jnp.maximum(r0 - HALO, 0), HALO)
    rn = pl.multiple_of(jnp.minimum(r0 + L, seq - HALO), HALO)
    has_prev = (c > 0).astype(F32)
    has_next = (c < nc - 1).astype(F32)
    for src, lo in ((x_ref, 0), (bc_ref, SSD_INNER)):
        cols = slice(lo, lo + SSD_INNER)
        cat[0:HALO, cols] = src[pl.ds(rp, HALO), :].astype(F32) * has_prev
        cat[HALO:HALO + L, cols] = src[pl.ds(r0, L), :].astype(F32)
        cat[HALO + L:2 * HALO + L, cols] = src[pl.ds(rn, HALO), :].astype(F32) * has_next
    conv = cb_ref[...] + sum(cat[HALO - 2 + k:HALO - 2 + k + L, :] * cw_ref[k:k + 1, :] for k in range(4))
    o_ref[...] = _silu(conv).astype(BF16)


def _ssd_conv(zxbc, lw, dims):
    t, t_p = dims["t"], dims["t_p"]
    L = SSD_CHUNK

    def make_call(seq, nb, off, aliased):
        nc = seq // L
        in_specs = [pl.BlockSpec((seq, SSD_INNER), lambda b, c: (off + b, 1)),
                    pl.BlockSpec((seq, SSD_INNER), lambda b, c: (off + b, 2)),
                    _full((4, 2 * SSD_INNER)), _full((1, 2 * SSD_INNER))]
        if aliased:
            in_specs.append(pl.BlockSpec(memory_space=pl.ANY))
        return pl.pallas_call(
            functools.partial(_ssd_conv_kernel, seq=seq),
            grid=(nb, nc),
            in_specs=in_specs,
            out_specs=pl.BlockSpec((L, 2 * SSD_INNER), lambda b, c: ((off + b) * nc + c, 0)),
            out_shape=jax.ShapeDtypeStruct((t, 2 * SSD_INNER), BF16),
            scratch_shapes=[pltpu.VMEM((L + 2 * HALO, 2 * SSD_INNER), F32)],
            input_output_aliases={4: 0} if aliased else {},
            compiler_params=_params(2),
            name="ssd_conv_seq%d" % seq,
        )

    args = [zxbc, zxbc, lw["conv_w"], lw["conv_b"]]
    xbc = make_call(dims["seq_p"], dims["nb_p"], 0, False)(*args)
    return make_call(dims["seq_s"], dims["nb_s"], t_p // dims["seq_s"], True)(*args, xbc)


def _ssd_scan_kernel(*refs, seq, rev, has_h0, want_state, add_skip):
    (xbc_ref, dt_ref, dtb_ref, alog_ref, dsk_ref, tri_ref, rep_ref) = refs[:7]
    pos = 7
    h0_ref = None
    if has_h0:
        h0_ref = refs[pos]
        pos += 1
    n_alias = len(refs) - pos - (2 if want_state else 1) - 1
    pos += n_alias
    y_ref = refs[pos]
    st_ref = refs[pos + 1] if want_state else None
    state = refs[-1]

    L = SSD_CHUNK
    nc = seq // L
    c = pl.program_id(1)

    @pl.when(c == 0)
    def _():
        if has_h0:
            for i in range(SSD_INNER // L):
                hpb = L // SSD_HD
                blk = h0_ref[0, 0, i * hpb:(i + 1) * hpb].reshape(L, SSD_STATE)
                state[:, i * L:(i + 1) * L] = blk.T
        else:
            state[...] = jnp.zeros_like(state)

    xc = xbc_ref[:, :SSD_INNER].astype(F32)
    bm = xbc_ref[:, SSD_INNER:SSD_INNER + SSD_GROUPS * SSD_STATE]
    cm = xbc_ref[:, SSD_INNER + SSD_GROUPS * SSD_STATE:]

    dtr = dt_ref[...] + dtb_ref[...]
    dt = jnp.maximum(dtr, 0.0) + jnp.log(1.0 + jnp.exp(-jnp.abs(dtr)))
    dta = dt * (-jnp.exp(alog_ref[...]))
    tri = tri_ref[...]
    p = sum(_dot(tri, part) for part in _split3(dta))
    pt = p.T
    edge = 0 if rev else L - 1
    p_edge = p[edge:edge + 1, :]
    rep = rep_ref[...]
    dt_x = _dot(dt.astype(BF16), rep)
    ep_x = _dot(jnp.exp(p).astype(BF16), rep)
    dte_x = _dot(jnp.exp(p_edge - p).astype(BF16), rep)
    cdec_x = _dot(jnp.broadcast_to(jnp.exp(p_edge), (8, p.shape[1])).astype(BF16), rep)[0:1, :]

    dtx = xc * dt_x
    dtxb = dtx.astype(BF16)
    xdte = (dtx * dte_x).astype(BF16)
    li = lax.broadcasted_iota(jnp.int32, (L, L), 0)
    si = lax.broadcasted_iota(jnp.int32, (L, L), 1)
    keep = (li <= si) if rev else (li >= si)
    lane0 = SSD_HEADS if rev else 0
    gw = SSD_HPG * SSD_HD
    for g in range(SSD_GROUPS):
        gcols = slice(g * gw, (g + 1) * gw)
        b_g = bm[:, g * SSD_STATE:(g + 1) * SSD_STATE]
        c_g = cm[:, g * SSD_STATE:(g + 1) * SSD_STATE]
        cb = _dot_nt(c_g, b_g)
        st_prev = state[:, gcols]
        y_g = _dot(c_g, st_prev.astype(BF16)) * ep_x[:, gcols]
        yd = []
        for hh in range(SSD_HPG):
            hl = lane0 + g * SSD_HPG + hh
            seg = p[:, hl:hl + 1] - pt[hl:hl + 1, :]
            mat = cb * jnp.exp(jnp.where(keep, seg, NEG))
            hc = slice((g * SSD_HPG + hh) * SSD_HD, (g * SSD_HPG + hh + 1) * SSD_HD)
            yd.append(_dot(mat.astype(BF16), dtxb[:, hc]))
        y_g = y_g + jnp.concatenate(yd, axis=1)
        if add_skip:
            y_g = y_g + dsk_ref[:, gcols] * xc[:, gcols]
        y_ref[:, gcols] = y_g
        state[:, gcols] = st_prev * cdec_x[:, gcols] + _dot(b_g.astype(F32).T.astype(BF16), xdte[:, gcols])

    if want_state:
        @pl.when(c == nc - 1)
        def _():
            for i in range(SSD_INNER // L):
                blk = state[:, i * L:(i + 1) * L].T
                st_ref[0, 0, i * (L // SSD_HD):(i + 1) * (L // SSD_HD)] = blk.reshape(L // SSD_HD, SSD_HD, SSD_STATE)


def _ssd_scan(xbc, dt_raw, lw, h0, j, rev, dims):
    t, t_p, seq_p, seq_s = dims["t"], dims["t_p"], dims["seq_p"], dims["seq_s"]
    nb_p, nb_s = dims["nb_p"], dims["nb_s"]
    L = SSD_CHUNK
    d = 1 if rev else 0
    li = jnp.arange(L)
    tri = ((li[:, None] <= li[None, :]) if rev else (li[:, None] >= li[None, :])).astype(BF16)
    lane = jnp.arange(128)
    col_head = jnp.arange(SSD_INNER) // SSD_HD
    rep = (lane[:, None] == (d * SSD_HEADS + col_head)[None, :]).astype(BF16)
    dsk = jnp.repeat(lw["d_skip"], SSD_HD)[None, :].astype(F32)
    st_shape = (nb_p, 1, SSD_HEADS, SSD_HD, SSD_STATE)

    def make_call(seq, nb, off, has_h0, want_state, n_alias):
        nc = seq // L
        chunk = (lambda b, c: (off * nc + b * nc + (nc - 1 - c), 0)) if rev else (lambda b, c: (off * nc + b * nc + c, 0))
        in_specs = [pl.BlockSpec((L, 2 * SSD_INNER), chunk), pl.BlockSpec((L, 128), chunk),
                    _full((1, 128)), _full((1, 128)),
                    _full((1, SSD_INNER)), _full((L, L)), _full((128, SSD_INNER))]
        if has_h0:
            in_specs.append(pl.BlockSpec((1, 1, SSD_HEADS, SSD_HD, SSD_STATE), lambda b, c: (b, j, 0, 0, 0)))
        aliases = {}
        if n_alias:
            aliases = {len(in_specs): 0}
            in_specs.append(pl.BlockSpec(memory_space=pl.ANY))
        out_specs = [pl.BlockSpec((L, SSD_INNER), chunk)]
        out_shape = [jax.ShapeDtypeStruct((t, SSD_INNER), F32)]
        if want_state:
            out_specs.append(pl.BlockSpec((1, 1, SSD_HEADS, SSD_HD, SSD_STATE), lambda b, c: (b, 0, 0, 0, 0)))
            out_shape.append(jax.ShapeDtypeStruct(st_shape, F32))
        return pl.pallas_call(
            functools.partial(_ssd_scan_kernel, seq=seq, rev=rev, has_h0=has_h0, want_state=want_state,
                              add_skip=not rev),
            grid=(nb, nc),
            in_specs=in_specs,
            out_specs=out_specs,
            out_shape=out_shape,
            scratch_shapes=[pltpu.VMEM((SSD_STATE, SSD_INNER), F32)],
            input_output_aliases=aliases,
            compiler_params=_params(2),
            name="ssd_scan_%s_seq%d" % ("bwd" if rev else "fwd", seq),
        )

    common = [xbc, dt_raw, lw["dt_bias"], lw["a_log"], dsk, tri, rep]
    y, st = make_call(seq_p, nb_p, 0, False, True, 0)(*common)
    (y,) = make_call(seq_s, nb_s, t_p // seq_s, True, False, 1)(*common, h0, y)
    return y, st


def _ssd_out_kernel(yf_ref, yb_ref, z_ref, gn_ref, w_ref, h_ref, gt1, g2, sh2, sc2, wr_hi, wr_lo, br,
                    h1_ref, xn_ref, route_ref):
    y = (yf_ref[...] + yb_ref[...]) * _silu(z_ref[...].astype(F32))
    o = _dot(_rms(y, gn_ref[...]).astype(BF16), w_ref[...])
    _finish(h_ref[...], o, gt1[0], (g2, sh2, sc2, wr_hi, wr_lo, br), h1_ref, xn_ref, route_ref)


def _ssd_layer(h, state_f, state_b, j, lw, layer, dims):
    t = dims["t"]
    zxbc = _pre_proj(h, lw, layer, lw["w_in"], 1536, BF16, dims)
    dt_raw = _pre_proj(h, lw, layer, lw["w_dt"], 128, F32, dims)
    xbc = _ssd_conv(zxbc, lw, dims)
    y_f, st_f = _ssd_scan(xbc, dt_raw, lw, state_f, j, False, dims)
    y_b, st_b = _ssd_scan(xbc, dt_raw, lw, state_b, j, True, dims)
    row_fn = _tile_row_fn(dims)
    tile = lambda: pl.BlockSpec((TM, D), lambda i: (i, 0))
    wide = lambda: pl.BlockSpec((TM, SSD_INNER), lambda i: (i, 0))
    outs = pl.pallas_call(
        _ssd_out_kernel,
        grid=(t // TM,),
        in_specs=[wide(), wide(), wide(), _full((1, SSD_INNER)), _full((SSD_INNER, D)), tile(),
                  _mod_spec(layer, 2, row_fn)] + _route_specs(layer, row_fn),
        out_specs=[tile(), tile(), pl.BlockSpec((TM, ROUTE_W), lambda i: (i, 0))],
        out_shape=_stream_outs(t),
        compiler_params=_params(1),
        name="ssd_out",
    )(y_f, y_b, zxbc, lw["g_norm"], lw["w_out"], h, lw["mod"], *_route_args(lw))
    return outs, st_f, st_b


TS = 256
SRC = 256


def _moe_plan(route, t):
    i32 = jnp.int32
    gid = route[:, 0].astype(i32)
    oh = (gid[:, None] == jnp.arange(MOE_GROUPS, dtype=i32)[None, :]).astype(i32)
    csum = jnp.cumsum(oh, axis=0)
    cnt = csum[-1]
    padded = ((cnt + TS - 1) // TS) * TS
    gend = jnp.cumsum(padded)
    pos = jnp.sum(oh * (csum - 1 + (gend - padded)[None, :]), axis=1)
    n_tiles = t // TS + MOE_GROUPS
    tile0 = jnp.arange(n_tiles, dtype=i32) * TS
    tile_grp = jnp.minimum(jnp.sum((tile0[:, None] >= gend[None, :]).astype(i32), axis=1), MOE_GROUPS - 1)
    n_used = (gend[-1] // TS).reshape(1)
    st = pos // TS
    big = jnp.int32(1 << 20)
    src_tile = jnp.arange(t, dtype=i32) // SRC
    hit = st[None, :] == jnp.arange(n_tiles, dtype=i32)[:, None]
    g_lo = jnp.min(jnp.where(hit, src_tile[None, :], big), axis=1)
    g_hi = jnp.max(jnp.where(hit, src_tile[None, :], -1), axis=1)
    st_t = st.reshape(t // SRC, SRC, 1)
    oh_t = oh.reshape(t // SRC, SRC, MOE_GROUPS) > 0
    u_lo = jnp.min(jnp.where(oh_t, st_t, big), axis=1).reshape(-1)
    u_hi = jnp.max(jnp.where(oh_t, st_t, -1), axis=1).reshape(-1)
    return dict(pos_row=pos.reshape(t // SRC, 1, SRC), pos_col=jnp.broadcast_to(pos[:, None], (t, 128)),
                tile_grp=tile_grp, n_used=n_used, g_lo=g_lo, g_hi=g_hi, u_lo=u_lo, u_hi=u_hi, n_tiles=n_tiles)


def _moe_gather_kernel(lo_ref, hi_ref, xn_ref, rt_ref, pos_ref, xs_ref, cws_ref, acc_ref, accr_ref):
    i = pl.program_id(0)
    acc_ref[...] = jnp.zeros_like(acc_ref)
    accr_ref[...] = jnp.zeros_like(accr_ref)
    rows = i * TS + lax.broadcasted_iota(jnp.int32, (TS, SRC), 0)

    def body(s, carry):
        p = jnp.where(pos_ref[s] == rows, 1.0, 0.0).astype(BF16)
        r0 = pl.multiple_of(s * SRC, SRC)
        acc_ref[...] += _dot(p, xn_ref[pl.ds(r0, SRC), :])
        accr_ref[...] += sum(_dot(p, part) for part in _split3(rt_ref[pl.ds(r0, SRC), :]))
        return carry

    lax.fori_loop(lo_ref[i], hi_ref[i] + 1, body, 0)
    xs_ref[...] = acc_ref[...].astype(BF16)
    cws_ref[...] = accr_ref[...]


def _moe_expert_kernel(grp_ref, nused_ref, xs_ref, cws_ref, wg_ref, wu_ref, wd_ref, y_ref):
    i = pl.program_id(0)

    @pl.when(i < nused_ref[0])
    def _():
        x = xs_ref[...]
        cws = cws_ref[...]
        lane = lax.broadcasted_iota(jnp.int32, cws.shape, 1)
        e0 = ROUTE_E0 + grp_ref[i] * MOE_EPG
        acc = None
        for e in range(MOE_EPG):
            hg = _dot(x, wg_ref[e])
            hu = _dot(x, wu_ref[e])
            cw = jnp.sum(jnp.where(lane == e0 + e, cws, 0.0), axis=-1, keepdims=True)
            y = _dot((_silu(hg) * hu * cw).astype(BF16), wd_ref[e])
            acc = y if acc is None else acc + y
        y_ref[...] = acc.astype(BF16)

    @pl.when(i >= nused_ref[0])
    def _():
        y_ref[...] = jnp.zeros_like(y_ref)


def _moe_ungather_kernel(lo_ref, hi_ref, ys_ref, pos_ref, h1_ref, gt2, fg_ref, *rest, n_prompt_tiles):
    acc_ref = rest[-1]
    out_refs = rest[:-1]
    n = pl.program_id(0)
    acc_ref[...] = jnp.zeros_like(acc_ref)
    pos = pos_ref[...]
    lane = lax.broadcasted_iota(jnp.int32, pos.shape, 1)
    for g in range(MOE_GROUPS):
        def body(st, carry):
            base = pl.multiple_of(st * TS, TS)
            q = jnp.concatenate([jnp.where(pos == base + k * 128 + lane, 1.0, 0.0) for k in range(TS // 128)],
                                axis=1).astype(BF16)
            acc_ref[...] += _dot(q, ys_ref[pl.ds(base, TS), :])
            return carry

        lax.fori_loop(lo_ref[n * MOE_GROUPS + g], hi_ref[n * MOE_GROUPS + g] + 1, body, 0)
    h2 = h1_ref[...] + gt2[0] * acc_ref[...]
    if n_prompt_tiles is None:
        out_refs[0][...] = h2
    else:
        y = _rms(h2, fg_ref[...])

        @pl.when(n < n_prompt_tiles)
        def _():
            out_refs[0][...] = y

        @pl.when(n >= n_prompt_tiles)
        def _():
            out_refs[1][...] = y


def _moe_sparse(h1, xn, route, lw, layer, final_g, final, dims):
    t = dims["t"]
    plan = _moe_plan(route, t)
    n_tiles = plan["n_tiles"]
    n_rows = n_tiles * TS
    one = pl.Buffered(1)
    xs, cws = pl.pallas_call(
        _moe_gather_kernel,
        grid_spec=pltpu.PrefetchScalarGridSpec(
            num_scalar_prefetch=2,
            grid=(n_tiles,),
            in_specs=[pl.BlockSpec((t, D), lambda i, lo, hi: (0, 0), pipeline_mode=one),
                      pl.BlockSpec((t, ROUTE_W), lambda i, lo, hi: (0, 0), pipeline_mode=one),
                      pl.BlockSpec((t // SRC, 1, SRC), lambda i, lo, hi: (0, 0, 0))],
            out_specs=[pl.BlockSpec((TS, D), lambda i, lo, hi: (i, 0)),
                       pl.BlockSpec((TS, ROUTE_W), lambda i, lo, hi: (i, 0))],
            scratch_shapes=[pltpu.VMEM((TS, D), F32), pltpu.VMEM((TS, ROUTE_W), F32)]),
        out_shape=[jax.ShapeDtypeStruct((n_rows, D), BF16), jax.ShapeDtypeStruct((n_rows, ROUTE_W), F32)],
        compiler_params=_params(1),
        name="moe_gather_l%d" % layer,
    )(plan["g_lo"], plan["g_hi"], xn, route, plan["pos_row"])

    ex = lambda i, g, nu: (g[i], 0, 0)
    ys = pl.pallas_call(
        _moe_expert_kernel,
        grid_spec=pltpu.PrefetchScalarGridSpec(
            num_scalar_prefetch=2,
            grid=(n_tiles,),
            in_specs=[pl.BlockSpec((TS, D), lambda i, g, nu: (i, 0)),
                      pl.BlockSpec((TS, ROUTE_W), lambda i, g, nu: (i, 0)),
                      pl.BlockSpec((MOE_EPG, D, MOE_DFF), ex), pl.BlockSpec((MOE_EPG, D, MOE_DFF), ex),
                      pl.BlockSpec((MOE_EPG, MOE_DFF, D), ex)],
            out_specs=pl.BlockSpec((TS, D), lambda i, g, nu: (i, 0))),
        out_shape=jax.ShapeDtypeStruct((n_rows, D), BF16),
        compiler_params=_params(1),
        name="moe_expert_l%d" % layer,
    )(plan["tile_grp"], plan["n_used"], xs, cws, lw["w_gate"], lw["w_up"], lw["w_down"])

    row_fn = _tile_row_fn(dims, SRC)
    tile = lambda: pl.BlockSpec((SRC, D), lambda n, lo, hi: (n, 0))
    if final:
        npt = dims["t_p"] // SRC
        out_specs = [pl.BlockSpec((SRC, D), lambda n, lo, hi: (jnp.minimum(n, npt - 1), 0)),
                     pl.BlockSpec((SRC, D), lambda n, lo, hi: (jnp.maximum(n - npt, 0), 0))]
        out_shape = [jax.ShapeDtypeStruct((dims["t_p"], D), F32), jax.ShapeDtypeStruct((t - dims["t_p"], D), F32)]
    else:
        npt = None
        out_specs = [tile()]
        out_shape = [jax.ShapeDtypeStruct((t, D), F32)]
    return pl.pallas_call(
        functools.partial(_moe_ungather_kernel, n_prompt_tiles=npt),
        grid_spec=pltpu.PrefetchScalarGridSpec(
            num_scalar_prefetch=2,
            grid=(t // SRC,),
            in_specs=[pl.BlockSpec((n_rows, D), lambda n, lo, hi: (0, 0), pipeline_mode=one),
                      pl.BlockSpec((SRC, 128), lambda n, lo, hi: (n, 0)), tile(),
                      pl.BlockSpec((1, 1, D), lambda n, lo, hi: ((layer * MOD_ROWS + row_fn(n)) * N_MOD + 5, 0, 0)),
                      pl.BlockSpec((1, D), lambda n, lo, hi: (0, 0))],
            out_specs=out_specs,
            scratch_shapes=[pltpu.VMEM((SRC, D), F32)]),
        out_shape=out_shape,
        compiler_params=_params(1),
        name="moe_ungather_l%d" % layer,
    )(plan["u_lo"], plan["u_hi"], ys, plan["pos_col"], h1, lw["mod"], final_g)


def kernel(x_prompt, x_sample, cache_k, cache_v, state_ssm_fwd, state_ssm_bwd, c, c_ctx, ada_w, ada_b, norm1_g, norm2_g, final_g, fnet_w_o, na_w_qkv, na_w_o, na_rpb, gmlp_w_in, gmlp_g_v, gmlp_w_s, gmlp_b_s, gmlp_w_out, ssd_w_in, ssd_conv_w, ssd_conv_b, ssd_a_log, ssd_dt_bias, ssd_d_skip, ssd_g_norm, ssd_w_out, moe_w_gr, moe_b_gr, moe_w_er, moe_b_er, moe_w_gate, moe_w_up, moe_w_down):
    nb_p, seq_p, _ = x_prompt.shape
    nb_s, seq_s, _ = x_sample.shape
    depth = ada_w.shape[0]
    t_p, t_s = nb_p * seq_p, nb_s * seq_s
    dims = dict(t=t_p + t_s, t_p=t_p, seq_p=seq_p, seq_s=seq_s, nb_p=nb_p, nb_s=nb_s)
    assert 1 + nb_s <= MOD_ROWS and t_p % seq_s == 0 and t_p % TM == 0 and seq_s % TM == 0

    cond = jnp.zeros((MOD_ROWS, D), F32).at[0].set(c_ctx).at[1:1 + nb_s].set(c)
    mod = _ada_table(cond, ada_w, ada_b)
    h = (x_prompt.reshape(t_p, D), x_sample.reshape(t_s, D))
    fg = final_g.reshape(1, D)

    new_k, new_v, new_sf, new_sb = [], [], [], []
    for l in range(depth):
        kind, j = l % 4, l // 4
        w_r = jnp.concatenate([moe_w_gr[l], moe_w_er[l]], axis=1)
        w_r = jnp.pad(w_r, ((0, 0), (0, ROUTE_W - MOE_GROUPS - MOE_EXPERTS)))
        b_r = jnp.pad(jnp.concatenate([moe_b_gr[l], moe_b_er[l]]), (0, ROUTE_W - MOE_GROUPS - MOE_EXPERTS))
        wr_hi = w_r.astype(BF16)
        lw = dict(mod=mod, g1=norm1_g[l].reshape(1, D), g2=norm2_g[l].reshape(1, D),
                  wr_hi=wr_hi, wr_lo=(w_r - wr_hi.astype(F32)).astype(BF16), br=b_r.reshape(1, ROUTE_W),
                  w_gate=moe_w_gate[l].astype(BF16), w_up=moe_w_up[l].astype(BF16),
                  w_down=moe_w_down[l].astype(BF16))
        if kind == 0:
            lw.update(w_o=fnet_w_o[j].astype(BF16))
            h1, xn, route = _fnet_layer(h, lw, l, dims)
        elif kind == 1:
            lw.update(w_qkv=na_w_qkv[j].astype(BF16), w_o=na_w_o[j].astype(BF16), rpb=na_rpb[j])
            (h1, xn, route), kc, vc = _na_layer(h, cache_k, cache_v, j, lw, l, dims)
            new_k.append(kc)
            new_v.append(vc)
        elif kind == 2:
            gw = GMLP_DFF // GMLP_GROUPS
            lw.update(w_in=gmlp_w_in[j].astype(BF16), g_v=gmlp_g_v[j].reshape(1, GMLP_DFF),
                      w_s=gmlp_w_s[j].astype(BF16),
                      b_s=jnp.broadcast_to(gmlp_b_s[j][:, :, None], (GMLP_GROUPS, GMLP_CHUNK, gw)),
                      w_out=gmlp_w_out[j].astype(BF16))
            h1, xn, route = _gmlp_layer(h, lw, l, dims)
        else:
            n_main = 3 * SSD_INNER
            w_in = ssd_w_in[j]
            pad = lambda v: jnp.pad(v, ((0, 0), (0, 128 - 2 * SSD_HEADS)))
            lw.update(w_in=w_in[:, :n_main].astype(BF16), w_dt=pad(w_in[:, n_main:]).astype(BF16),
                      conv_w=ssd_conv_w[j], conv_b=ssd_conv_b[j].reshape(1, -1),
                      dt_bias=pad(ssd_dt_bias[j].reshape(1, -1)), a_log=pad(ssd_a_log[j].reshape(1, -1)),
                      d_skip=ssd_d_skip[j], g_norm=ssd_g_norm[j].reshape(1, SSD_INNER),
                      w_out=ssd_w_out[j].astype(BF16))
            (h1, xn, route), sf, sb = _ssd_layer(h, state_ssm_fwd, state_ssm_bwd, j, lw, l, dims)
            new_sf.append(sf)
            new_sb.append(sb)
        out = _moe_sparse(h1, xn, route, lw, l, fg, l == depth - 1, dims)
        h = out if l == depth - 1 else out[0]

    y_prompt = h[0].reshape(nb_p, seq_p, D)
    y_sample = h[1].reshape(nb_s, seq_s, D)
    cat = lambda xs: jnp.concatenate(xs, axis=1)
    return (y_prompt, y_sample, cat(new_k), cat(new_v), cat(new_sf), cat(new_sb))
```

```python
import functools
import math

import jax
import jax.numpy as jnp
from jax import lax
from jax.experimental import pallas as pl
from jax.experimental.pallas import tpu as pltpu

F32 = jnp.float32
BF16 = jnp.bfloat16

D = 1024
EPS = 1e-6
NEG = -1e30
N_MOD = 6
MOD_ROWS = 8
GRID_W = 64
FNET_GROUPS = 8
NA_HEADS = 16
NA_HD = 64
NA_WIN_ROWS = 8
NA_WIN_COLS = 16
GMLP_CHUNK = 128
GMLP_DFF = 2048
GMLP_GROUPS = 8
SSD_INNER = 2048
SSD_HD = 64
SSD_HEADS = 32
SSD_GROUPS = 8
SSD_STATE = 128
SSD_CHUNK = 128
SSD_HPG = SSD_HEADS // SSD_GROUPS
MOE_GROUPS = 4
MOE_EPG = 4
MOE_EXPERTS = 16
MOE_DFF = 512
ROUTE_W = 128
ROUTE_E0 = 4

TM = 512
VMEM_LIMIT = 56 * 1024 * 1024


def _dot(a, b):
    return jnp.dot(a, b, preferred_element_type=F32)


def _dot_nt(a, b):
    return lax.dot_general(a, b, (((1,), (1,)), ((), ())), preferred_element_type=F32)


def _silu(x):
    return x * (1.0 / (1.0 + jnp.exp(-x)))


def _rms(x, g):
    return x * lax.rsqrt(jnp.mean(x * x, axis=-1, keepdims=True) + EPS) * g


def _normmod(x, g, shift, scale):
    return _rms(x, g) * (1.0 + scale) + shift


def _split3(x):
    hi = x.astype(BF16)
    r = x - hi.astype(F32)
    mid = r.astype(BF16)
    lo = (r - mid.astype(F32)).astype(BF16)
    return hi, mid, lo


def _params(n_axes):
    return pltpu.CompilerParams(dimension_semantics=("arbitrary",) * n_axes,
                                vmem_limit_bytes=VMEM_LIMIT)


def _full(shape):
    nd = len(shape)
    return pl.BlockSpec(shape, lambda *_: (0,) * nd)


def _mod_spec(layer, k, row_fn):
    return pl.BlockSpec((1, 1, D), lambda *idx: ((layer * MOD_ROWS + row_fn(*idx)) * N_MOD + k, 0, 0))


def _ada_kernel(c_ref, w_ref, b_ref, o_ref):
    c = c_ref[...]
    o_ref[0] = _dot(_silu(c).astype(BF16), w_ref[0].astype(BF16)) + b_ref[0]


def _ada_table(cond, ada_w, ada_b):
    depth = ada_w.shape[0]
    n = N_MOD * D
    tn = 1536
    out = pl.pallas_call(
        _ada_kernel,
        grid=(depth, n // tn),
        in_specs=[_full((MOD_ROWS, D)),
                  pl.BlockSpec((1, D, tn), lambda l, j: (l, 0, j)),
                  pl.BlockSpec((1, 1, tn), lambda l, j: (l, 0, j))],
        out_specs=pl.BlockSpec((1, MOD_ROWS, tn), lambda l, j: (l, 0, j)),
        out_shape=jax.ShapeDtypeStruct((depth, MOD_ROWS, n), F32),
        compiler_params=_params(2),
        name="ada_table",
    )(cond, ada_w, ada_b.reshape(depth, 1, n))
    return out.reshape(depth * MOD_ROWS * N_MOD, 1, D)


def _route(h1, g2, sh2, sc2, wr_hi, wr_lo, br):
    xn = _normmod(h1, g2, sh2, sc2)
    xh = xn.astype(BF16)
    xl = (xn - xh.astype(F32)).astype(BF16)
    logits = _dot(xh, wr_hi) + _dot(xh, wr_lo) + _dot(xl, wr_hi) + br
    lane = lax.broadcasted_iota(jnp.int32, logits.shape, 1).astype(F32)
    far = float(ROUTE_W)
    gl = jnp.where(lane < MOE_GROUPS, logits, NEG)
    gmax = jnp.max(gl, axis=-1, keepdims=True)
    g_p = 1.0 / jnp.sum(jnp.exp(gl - gmax), axis=-1, keepdims=True)
    gidx = jnp.min(jnp.where(gl == gmax, lane, far), axis=-1, keepdims=True)
    lo = ROUTE_E0 + MOE_EPG * gidx
    el = jnp.where((lane >= lo) & (lane < lo + MOE_EPG), logits, NEG)
    m1 = jnp.max(el, axis=-1, keepdims=True)
    i1 = jnp.min(jnp.where(el == m1, lane, far), axis=-1, keepdims=True)
    el2 = jnp.where(lane == i1, NEG, el)
    m2 = jnp.max(el2, axis=-1, keepdims=True)
    i2 = jnp.min(jnp.where(el2 == m2, lane, far), axis=-1, keepdims=True)
    e2 = jnp.exp(m2 - m1)
    w1 = g_p / (1.0 + e2)
    w2 = w1 * e2
    rec = jnp.zeros_like(logits)
    for part, (a, b) in enumerate(zip(_split3(w1), _split3(w2))):
        shift = part * MOE_EPG - lo
        rec = (rec + jnp.where(lane == i1 + shift, a.astype(F32), 0.0)
               + jnp.where(lane == i2 + shift, b.astype(F32), 0.0))
    return xh, rec.astype(BF16), jnp.broadcast_to(gidx, logits.shape)


def _route_specs(layer, row_fn):
    return [_full((1, D)), _mod_spec(layer, 3, row_fn), _mod_spec(layer, 4, row_fn),
            _full((D, ROUTE_W)), _full((D, ROUTE_W)), _full((1, ROUTE_W))]


def _route_args(lw):
    return [lw["g2"], lw["mod"], lw["mod"], lw["wr_hi"], lw["wr_lo"], lw["br"]]


def _finish(h, o, gate, rt_refs, h1_ref, xn_ref, route_ref):
    g2, sh2, sc2, wr_hi, wr_lo, br = rt_refs
    h1 = h + gate * o
    h1_ref[...] = h1
    xn, rec, gid = _route(h1, g2[...], sh2[0], sc2[0], wr_hi[...], wr_lo[...], br[...])
    xn_ref[:, :D] = xn
    xn_ref[:, D:] = rec
    route_ref[...] = gid


XW = D + ROUTE_W


def _stream_outs(t):
    return [jax.ShapeDtypeStruct((t, D), F32), jax.ShapeDtypeStruct((t, XW), BF16),
            jax.ShapeDtypeStruct((t, ROUTE_W), F32)]


def _stream_specs(rows, row_block):
    return [pl.BlockSpec((rows, w), lambda *idx: (row_block(*idx), 0)) for w in (D, XW, ROUTE_W)]


def _fnet_kernel(h_ref, g1, sh1, sc1, gt1, csc_ref, fs_ref, wo_ref, g2, sh2, sc2, wr_hi, wr_lo, br,
                 *rest, seq):
    h1_ref, xn_ref, route_ref, ab_ref = rest[-4:]
    h = h_ref[...]
    a = _normmod(h, g1[...], sh1[0], sc1[0]).astype(BF16)
    gd = D // FNET_GROUPS
    for g in range(FNET_GROUPS):
        ab = _dot(a[:, g * gd:(g + 1) * gd], csc_ref[...])
        ab_ref[0:seq, g * gd:(g + 1) * gd] = ab[:, :gd].astype(BF16)
        ab_ref[seq:2 * seq, g * gd:(g + 1) * gd] = ab[:, gd:].astype(BF16)
    f = _dot(fs_ref[...], ab_ref[...])
    o = _dot(f.astype(BF16), wo_ref[...])
    _finish(h, o, gt1[0], (g2, sh2, sc2, wr_hi, wr_lo, br), h1_ref, xn_ref, route_ref)


def _dft_tables(n):
    k = jnp.arange(n, dtype=jnp.int32)
    ang = ((k[:, None] * k[None, :]) % n).astype(F32) * (2.0 * math.pi / n)
    s = 1.0 / math.sqrt(n)
    return jnp.cos(ang) * s, jnp.sin(ang) * s


def _fnet_layer(h, lw, layer, dims):
    t, t_p = dims["t"], dims["t_p"]
    gd = D // FNET_GROUPS
    cc, sc = _dft_tables(gd)
    csc = jnp.concatenate([cc, sc], axis=1).astype(BF16)

    split_in = isinstance(h, tuple)

    def make_call(seq, nb, off, n_alias):
        cs, ss = _dft_tables(seq)
        fs = jnp.concatenate([cs, -ss], axis=1).astype(BF16)
        row_fn = (lambda b: 0) if off == 0 else (lambda b: 1 + b)
        in_off = 0 if split_in else off
        tile = lambda: pl.BlockSpec((seq, D), lambda b: (off + b, 0))
        in_specs = ([pl.BlockSpec((seq, D), lambda b: (in_off + b, 0)), _full((1, D))]
                    + [_mod_spec(layer, k, row_fn) for k in (0, 1, 2)]
                    + [_full((gd, 2 * gd)), _full((seq, 2 * seq)), _full((D, D))]
                    + _route_specs(layer, row_fn))
        aliases = {}
        if n_alias:
            base = len(in_specs)
            in_specs = in_specs + [pl.BlockSpec(memory_space=pl.ANY)] * n_alias
            aliases = {base + i: i for i in range(n_alias)}
        call = pl.pallas_call(
            functools.partial(_fnet_kernel, seq=seq),
            grid=(nb,),
            in_specs=in_specs,
            out_specs=_stream_specs(seq, lambda b: off + b),
            out_shape=_stream_outs(t),
            scratch_shapes=[pltpu.VMEM((2 * seq, D), BF16)],
            input_output_aliases=aliases,
            compiler_params=_params(1),
            name="fnet_seq%d" % seq,
        )
        return lambda *a: call(*a[:2], *a[2:5], a[5], fs, *a[6:])

    h_p, h_s = h if split_in else (h, h)
    args = [lw["g1"], lw["mod"], lw["mod"], lw["mod"], csc, lw["w_o"]] + _route_args(lw)
    outs = make_call(dims["seq_p"], dims["nb_p"], 0, None)(h_p, *args)
    return make_call(dims["seq_s"], dims["nb_s"], t_p // dims["seq_s"], 3)(h_s, *args, *outs)


def _pre_kernel(h_ref, g1, sh1, sc1, w_ref, o_ref, a_ref):
    @pl.when(pl.program_id(1) == 0)
    def _():
        a_ref[...] = _normmod(h_ref[...], g1[...], sh1[0], sc1[0]).astype(BF16)

    o_ref[...] = _dot(a_ref[...], w_ref[...]).astype(o_ref.dtype)


def _tile_row_fn(dims, tm=TM):
    npt = dims["t_p"] // tm
    tps = dims["seq_s"] // tm
    return lambda i, *_: jnp.where(i < npt, 0, 1 + (i - npt) // tps)


def _pre_proj(h, lw, layer, w, tn, out_dtype, dims):
    t = dims["t"]
    n = w.shape[1]
    tm = dims["seq_s"]
    row_fn = _tile_row_fn(dims, tm)
    return pl.pallas_call(
        _pre_kernel,
        grid=(t // tm, n // tn),
        in_specs=[pl.BlockSpec((tm, D), lambda i, j: (i, 0)), _full((1, D)),
                  _mod_spec(layer, 0, row_fn), _mod_spec(layer, 1, row_fn),
                  pl.BlockSpec((D, tn), lambda i, j: (0, j))],
        out_specs=pl.BlockSpec((tm, tn), lambda i, j: (i, j)),
        out_shape=jax.ShapeDtypeStruct((t, n), out_dtype),
        scratch_shapes=[pltpu.VMEM((tm, D), BF16)],
        compiler_params=_params(2),
        name="pre_proj_l%d_n%d" % (layer, n),
    )(h, lw["g1"], lw["mod"], lw["mod"], w)


def _out_kernel(o_ref, w_ref, h_ref, gt1, g2, sh2, sc2, wr_hi, wr_lo, br, h1_ref, xn_ref, route_ref):
    o = _dot(o_ref[...], w_ref[...])
    _finish(h_ref[...], o, gt1[0], (g2, sh2, sc2, wr_hi, wr_lo, br), h1_ref, xn_ref, route_ref)


def _out_proj(o, w, h, lw, layer, dims):
    t = dims["t"]
    k = o.shape[1]
    row_fn = _tile_row_fn(dims)
    tile = lambda: pl.BlockSpec((TM, D), lambda i: (i, 0))
    return pl.pallas_call(
        _out_kernel,
        grid=(t // TM,),
        in_specs=[pl.BlockSpec((TM, k), lambda i: (i, 0)), _full((k, D)), tile(),
                  _mod_spec(layer, 2, row_fn)] + _route_specs(layer, row_fn),
        out_specs=_stream_specs(TM, lambda i: i),
        out_shape=_stream_outs(t),
        compiler_params=_params(1),
        name="out_proj_l%d" % layer,
    )(o, w, h, lw["mod"], *_route_args(lw))


NA_SCALE = NA_HD ** -0.5


def _na_ctx_kernel(q_ref, k_ref, v_ref, o_ref, kc_ref, vc_ref):
    outs = []
    for hd in range(NA_HEADS):
        sl = slice(hd * NA_HD, (hd + 1) * NA_HD)
        q = q_ref[:, sl] * NA_SCALE
        k = k_ref[:, sl]
        v = v_ref[:, sl]
        s = _dot_nt(q, k)
        p = jnp.exp(s - jnp.max(s, axis=-1, keepdims=True))
        l = jnp.sum(p, axis=-1, keepdims=True)
        outs.append((_dot(p.astype(BF16), v) / l).astype(BF16))
        kc_ref[0, 0, hd] = k.astype(F32)
        vc_ref[0, 0, hd] = v.astype(F32)
    o_ref[...] = jnp.concatenate(outs, axis=1)


def _na_row_start(qr, rows):
    kr = min(NA_WIN_ROWS, rows)
    return min(max(qr - kr // 2, 0), rows - kr)


def _na_window_bias(bias_ref, hh, qr, rows, m_lo, m_hi):
    kr = min(NA_WIN_ROWS, rows)
    rs = _na_row_start(qr, rows)
    blocks = []
    for m in range(m_lo, m_hi):
        ok0 = rs <= 2 * m < rs + kr
        ok1 = rs <= 2 * m + 1 < rs + kr
        e = 2 * m - qr + NA_WIN_ROWS
        if ok0 and ok1:
            blocks.append(bias_ref[hh, 0, e])
        elif ok1:
            blocks.append(bias_ref[hh, 1, e])
        elif ok0:
            blocks.append(bias_ref[hh, 2, e])
        else:
            blocks.append(jnp.full((GRID_W, 2 * GRID_W), NEG, F32))
    return jnp.concatenate(blocks, axis=1)


def _na_lat_kernel(q_ref, k_ref, v_ref, bias_ref, kc_ref, vc_ref, o_in, o_ref, *, seq, qb):
    del o_in
    rows = seq // GRID_W
    for hh in range(2):
        sl = slice(hh * NA_HD, (hh + 1) * NA_HD)
        k = k_ref[:, sl]
        v = v_ref[:, sl]
        kc = kc_ref[0, 0, hh].astype(BF16)
        vc = vc_ref[0, 0, hh].astype(BF16)
        for b0 in range(0, seq, qb):
            q = q_ref[b0:b0 + qb, sl] * NA_SCALE
            qrs = range(b0 // GRID_W, (b0 + qb) // GRID_W)
            m_lo = _na_row_start(qrs[0], rows) // 2
            m_hi = (_na_row_start(qrs[-1], rows) + min(NA_WIN_ROWS, rows) + 1) // 2
            keys = slice(m_lo * 2 * GRID_W, m_hi * 2 * GRID_W)
            bias = jnp.concatenate([_na_window_bias(bias_ref, hh, qr, rows, m_lo, m_hi) for qr in qrs], axis=0)
            s1 = _dot_nt(q, k[keys]) + bias
            s2 = _dot_nt(q, kc)
            m = jnp.maximum(jnp.max(s1, axis=-1, keepdims=True), jnp.max(s2, axis=-1, keepdims=True))
            p1 = jnp.exp(s1 - m)
            p2 = jnp.exp(s2 - m)
            l = jnp.sum(p1, axis=-1, keepdims=True) + jnp.sum(p2, axis=-1, keepdims=True)
            o = (_dot(p1.astype(BF16), v[keys]) + _dot(p2.astype(BF16), vc)) / l
            o_ref[b0:b0 + qb, sl] = o.astype(BF16)


def _na_bias_tables(rpb):
    c = jnp.arange(GRID_W)
    win0 = jnp.clip(c - NA_WIN_COLS // 2, 0, GRID_W - NA_WIN_COLS)
    ok_c = (c[None, :] >= win0[:, None]) & (c[None, :] < win0[:, None] + NA_WIN_COLS)
    dc = jnp.clip(c[None, :] - c[:, None], 1 - NA_WIN_COLS, NA_WIN_COLS - 1) + NA_WIN_COLS - 1
    nh, ndr, ndc = rpb.shape
    pick = (dc.reshape(1, -1) == jnp.arange(ndc)[:, None]).astype(F32)
    cm = jnp.dot(rpb.reshape(nh * ndr, ndc), pick, precision=lax.Precision.HIGHEST)
    cm = jnp.where(ok_c[None, None], cm.reshape(nh, ndr, GRID_W, GRID_W), NEG)
    neg = jnp.full_like(cm[:, :1], NEG)
    ext = jnp.concatenate([neg, cm, neg], axis=1)
    a, b = ext[:, :-1], ext[:, 1:]
    negs = jnp.full_like(a, NEG)
    pair = lambda x, y: jnp.concatenate([x, y], axis=-1)
    return jnp.stack([pair(a, b), pair(negs, b), pair(a, negs)], axis=1)


def _na_layer(h, cache_k, cache_v, j, lw, layer, dims):
    t, t_p, seq_p, seq_s = dims["t"], dims["t_p"], dims["seq_p"], dims["seq_s"]
    nb_p, nb_s = dims["nb_p"], dims["nb_s"]
    qkv = _pre_proj(h, lw, layer, lw["w_qkv"], 1536, BF16, dims)
    cshape = (nb_p, 1, NA_HEADS, seq_p, NA_HD)
    cspec = lambda: pl.BlockSpec((1, 1, NA_HEADS, seq_p, NA_HD), lambda b: (b, 0, 0, 0, 0))
    o, kc, vc = pl.pallas_call(
        _na_ctx_kernel,
        grid=(nb_p,),
        in_specs=[pl.BlockSpec((seq_p, D), lambda b: (b, 0)), pl.BlockSpec((seq_p, D), lambda b: (b, 1)),
                  pl.BlockSpec((seq_p, D), lambda b: (b, 2))],
        out_specs=[pl.BlockSpec((seq_p, D), lambda b: (b, 0)), cspec(), cspec()],
        out_shape=[jax.ShapeDtypeStruct((t, D), BF16), jax.ShapeDtypeStruct(cshape, F32),
                   jax.ShapeDtypeStruct(cshape, F32)],
        compiler_params=_params(1),
        name="na_context",
    )(qkv, qkv, qkv)

    bias = _na_bias_tables(lw["rpb"])
    past = cache_k.shape[3]
    off = t_p // seq_s
    npair = NA_HEADS // 2
    pw = 2 * NA_HD
    pspec = lambda: pl.BlockSpec((1, 1, 2, past, NA_HD), lambda hp, b: (b, j, hp, 0, 0))
    o = pl.pallas_call(
        functools.partial(_na_lat_kernel, seq=seq_s, qb=256),
        grid=(npair, nb_s),
        in_specs=[pl.BlockSpec((seq_s, pw), lambda hp, b: (off + b, hp)),
                  pl.BlockSpec((seq_s, pw), lambda hp, b: (off + b, npair + hp)),
                  pl.BlockSpec((seq_s, pw), lambda hp, b: (off + b, 2 * npair + hp)),
                  pl.BlockSpec((2, 3, 2 * NA_WIN_ROWS, GRID_W, 2 * GRID_W), lambda hp, b: (hp, 0, 0, 0, 0)),
                  pspec(), pspec(), pl.BlockSpec(memory_space=pl.ANY)],
        out_specs=pl.BlockSpec((seq_s, pw), lambda hp, b: (off + b, hp)),
        out_shape=jax.ShapeDtypeStruct((t, D), BF16),
        input_output_aliases={6: 0},
        compiler_params=_params(2),
        name="na_latent",
    )(qkv, qkv, qkv, bias, cache_k, cache_v, o)
    outs = _out_proj(o, lw["w_o"], h, lw, layer, dims)
    return outs, kc, vc


def _gelu_tanh(x):
    return 0.5 * x * (1.0 + jnp.tanh(math.sqrt(2.0 / math.pi) * (x + 0.044715 * (x * x * x))))


def _gmlp_kernel(h_ref, g1, sh1, sc1, gt1, win_ref, gv_ref, ws_ref, bs_ref, wout_ref,
                 g2, sh2, sc2, wr_hi, wr_lo, br, h1_ref, xn_ref, route_ref, m_ref):
    h = h_ref[...]
    a = _normmod(h, g1[...], sh1[0], sc1[0]).astype(BF16)
    u = _gelu_tanh(_dot(a, win_ref[:, :GMLP_DFF]))
    v = _gelu_tanh(_dot(a, win_ref[:, GMLP_DFF:]))
    v = _rms(v, gv_ref[...]).astype(BF16)
    gw = GMLP_DFF // GMLP_GROUPS
    for c in range(TM // GMLP_CHUNK):
        rows = slice(c * GMLP_CHUNK, (c + 1) * GMLP_CHUNK)
        for g in range(GMLP_GROUPS):
            cols = slice(g * gw, (g + 1) * gw)
            vs = _dot(ws_ref[g], v[rows, cols]) + bs_ref[g]
            m_ref[rows, cols] = (u[rows, cols] * vs).astype(BF16)
    o = _dot(m_ref[...], wout_ref[...])
    _finish(h, o, gt1[0], (g2, sh2, sc2, wr_hi, wr_lo, br), h1_ref, xn_ref, route_ref)


def _gmlp_layer(h, lw, layer, dims):
    t = dims["t"]
    row_fn = _tile_row_fn(dims)
    gw = GMLP_DFF // GMLP_GROUPS
    tile = lambda: pl.BlockSpec((TM, D), lambda i: (i, 0))
    one = pl.Buffered(1)
    return pl.pallas_call(
        _gmlp_kernel,
        grid=(t // TM,),
        in_specs=[tile(), _full((1, D))] + [_mod_spec(layer, k, row_fn) for k in (0, 1, 2)]
                 + [pl.BlockSpec((D, 2 * GMLP_DFF), lambda i: (0, 0), pipeline_mode=one),
                    _full((1, GMLP_DFF)), _full((GMLP_GROUPS, GMLP_CHUNK, GMLP_CHUNK)),
                    _full((GMLP_GROUPS, GMLP_CHUNK, gw)),
                    pl.BlockSpec((GMLP_DFF, D), lambda i: (0, 0), pipeline_mode=one)]
                 + _route_specs(layer, row_fn),
        out_specs=_stream_specs(TM, lambda i: i),
        out_shape=_stream_outs(t),
        scratch_shapes=[pltpu.VMEM((TM, GMLP_DFF), BF16)],
        compiler_params=_params(1),
        name="gmlp",
    )(h, lw["g1"], lw["mod"], lw["mod"], lw["mod"], lw["w_in"], lw["g_v"], lw["w_s"], lw["b_s"],
      lw["w_out"], *_route_args(lw))


HALO = 16


def _ssd_conv_kernel(x_ref, bc_ref, cw_ref, cb_ref, *rest, seq):
    o_ref, cat = rest[-2:]
    L = SSD_CHUNK
    nc = seq // L
    c = pl.program_id(1)
    r0 = pl.multiple_of(c * L, L)
    rp = pl.multiple_of(jnp.maximum(r0 - HALO, 0), HALO)
    rn = pl.multiple_of(jnp.minimum(r0 + L, seq - HALO), HALO)
    has_prev = (c > 0).astype(F32)
    has_next = (c < nc - 1).astype(F32)
    for src, lo in ((x_ref, 0), (bc_ref, SSD_INNER)):
        cols = slice(lo, lo + SSD_INNER)
        cat[0:HALO, cols] = src[pl.ds(rp, HALO), :].astype(F32) * has_prev
        cat[HALO:HALO + L, cols] = src[pl.ds(r0, L), :].astype(F32)
        cat[HALO + L:2 * HALO + L, cols] = src[pl.ds(rn, HALO), :].astype(F32) * has_next
    conv = cb_ref[...] + sum(cat[HALO - 2 + k:HALO - 2 + k + L, :] * cw_ref[k:k + 1, :] for k in range(4))
    o_ref[...] = _silu(conv).astype(BF16)


def _ssd_conv(zxbc, lw, dims):
    t, t_p = dims["t"], dims["t_p"]
    L = SSD_CHUNK

    def make_call(seq, nb, off, aliased):
        nc = seq // L
        in_specs = [pl.BlockSpec((seq, SSD_INNER), lambda b, c: (off + b, 1)),
                    pl.BlockSpec((seq, SSD_INNER), lambda b, c: (off + b, 2)),
                    _full((4, 2 * SSD_INNER)), _full((1, 2 * SSD_INNER))]
        if aliased:
            in_specs.append(pl.BlockSpec(memory_space=pl.ANY))
        return pl.pallas_call(
            functools.partial(_ssd_conv_kernel, seq=seq),
            grid=(nb, nc),
            in_specs=in_specs,
            out_specs=pl.BlockSpec((L, 2 * SSD_INNER), lambda b, c: ((off + b) * nc + c, 0)),
            out_shape=jax.ShapeDtypeStruct((t, 2 * SSD_INNER), BF16),
            scratch_shapes=[pltpu.VMEM((L + 2 * HALO, 2 * SSD_INNER), F32)],
            input_output_aliases={4: 0} if aliased else {},
            compiler_params=_params(2),
            name="ssd_conv_seq%d" % seq,
        )

    args = [zxbc, zxbc, lw["conv_w"], lw["conv_b"]]
    xbc = make_call(dims["seq_p"], dims["nb_p"], 0, False)(*args)
    return make_call(dims["seq_s"], dims["nb_s"], t_p // dims["seq_s"], True)(*args, xbc)


def _ssd_scan_kernel(*refs, seq, rev, has_h0, want_state, add_skip):
    (xbc_ref, dt_ref, dtb_ref, alog_ref, dsk_ref, tri_ref, rep_ref) = refs[:7]
    pos = 7
    h0_ref = None
    if has_h0:
        h0_ref = refs[pos]
        pos += 1
    n_alias = len(refs) - pos - (2 if want_state else 1) - 1
    pos += n_alias
    y_ref = refs[pos]
    st_ref = refs[pos + 1] if want_state else None
    state = refs[-1]

    L = SSD_CHUNK
    nc = seq // L
    c = pl.program_id(1)

    @pl.when(c == 0)
    def _():
        if has_h0:
            for i in range(SSD_INNER // L):
                hpb = L // SSD_HD
                blk = h0_ref[0, 0, i * hpb:(i + 1) * hpb].reshape(L, SSD_STATE)
                state[:, i * L:(i + 1) * L] = blk.T
        else:
            state[...] = jnp.zeros_like(state)

    xc = xbc_ref[:, :SSD_INNER].astype(F32)
    bm = xbc_ref[:, SSD_INNER:SSD_INNER + SSD_GROUPS * SSD_STATE]
    cm = xbc_ref[:, SSD_INNER + SSD_GROUPS * SSD_STATE:]

    dtr = dt_ref[...] + dtb_ref[...]
    dt = jnp.maximum(dtr, 0.0) + jnp.log(1.0 + jnp.exp(-jnp.abs(dtr)))
    dta = dt * (-jnp.exp(alog_ref[...]))
    tri = tri_ref[...]
    p = sum(_dot(tri, part) for part in _split3(dta))
    pt = p.T
    edge = 0 if rev else L - 1
    p_edge = p[edge:edge + 1, :]
    rep = rep_ref[...]
    dt_x = _dot(dt.astype(BF16), rep)
    ep_x = _dot(jnp.exp(p).astype(BF16), rep)
    dte_x = _dot(jnp.exp(p_edge - p).astype(BF16), rep)
    cdec_x = _dot(jnp.broadcast_to(jnp.exp(p_edge), (8, p.shape[1])).astype(BF16), rep)[0:1, :]

    dtx = xc * dt_x
    dtxb = dtx.astype(BF16)
    xdte = (dtx * dte_x).astype(BF16)
    li = lax.broadcasted_iota(jnp.int32, (L, L), 0)
    si = lax.broadcasted_iota(jnp.int32, (L, L), 1)
    keep = (li <= si) if rev else (li >= si)
    lane0 = SSD_HEADS if rev else 0
    gw = SSD_HPG * SSD_HD
    for g in range(SSD_GROUPS):
        gcols = slice(g * gw, (g + 1) * gw)
        b_g = bm[:, g * SSD_STATE:(g + 1) * SSD_STATE]
        c_g = cm[:, g * SSD_STATE:(g + 1) * SSD_STATE]
        cb = _dot_nt(c_g, b_g)
        st_prev = state[:, gcols]
        y_g = _dot(c_g, st_prev.astype(BF16)) * ep_x[:, gcols]
        yd = []
        for hh in range(SSD_HPG):
            hl = lane0 + g * SSD_HPG + hh
            seg = p[:, hl:hl + 1] - pt[hl:hl + 1, :]
            mat = cb * jnp.exp(jnp.where(keep, seg, NEG))
            hc = slice((g * SSD_HPG + hh) * SSD_HD, (g * SSD_HPG + hh + 1) * SSD_HD)
            yd.append(_dot(mat.astype(BF16), dtxb[:, hc]))
        y_g = y_g + jnp.concatenate(yd, axis=1)
        if add_skip:
            y_g = y_g + dsk_ref[:, gcols] * xc[:, gcols]
        y_ref[:, gcols] = y_g
        state[:, gcols] = st_prev * cdec_x[:, gcols] + _dot(b_g.astype(F32).T.astype(BF16), xdte[:, gcols])

    if want_state:
        @pl.when(c == nc - 1)
        def _():
            for i in range(SSD_INNER // L):
                blk = state[:, i * L:(i + 1) * L].T
                st_ref[0, 0, i * (L // SSD_HD):(i + 1) * (L // SSD_HD)] = blk.reshape(L // SSD_HD, SSD_HD, SSD_STATE)


def _ssd_scan(xbc, dt_raw, lw, h0, j, rev, dims):
    t, t_p, seq_p, seq_s = dims["t"], dims["t_p"], dims["seq_p"], dims["seq_s"]
    nb_p, nb_s = dims["nb_p"], dims["nb_s"]
    L = SSD_CHUNK
    d = 1 if rev else 0
    li = jnp.arange(L)
    tri = ((li[:, None] <= li[None, :]) if rev else (li[:, None] >= li[None, :])).astype(BF16)
    lane = jnp.arange(128)
    col_head = jnp.arange(SSD_INNER) // SSD_HD
    rep = (lane[:, None] == (d * SSD_HEADS + col_head)[None, :]).astype(BF16)
    dsk = jnp.repeat(lw["d_skip"], SSD_HD)[None, :].astype(F32)
    st_shape = (nb_p, 1, SSD_HEADS, SSD_HD, SSD_STATE)

    def make_call(seq, nb, off, has_h0, want_state, n_alias):
        nc = seq // L
        chunk = (lambda b, c: (off * nc + b * nc + (nc - 1 - c), 0)) if rev else (lambda b, c: (off * nc + b * nc + c, 0))
        in_specs = [pl.BlockSpec((L, 2 * SSD_INNER), chunk), pl.BlockSpec((L, 128), chunk),
                    _full((1, 128)), _full((1, 128)),
                    _full((1, SSD_INNER)), _full((L, L)), _full((128, SSD_INNER))]
        if has_h0:
            in_specs.append(pl.BlockSpec((1, 1, SSD_HEADS, SSD_HD, SSD_STATE), lambda b, c: (b, j, 0, 0, 0)))
        aliases = {}
        if n_alias:
            aliases = {len(in_specs): 0}
            in_specs.append(pl.BlockSpec(memory_space=pl.ANY))
        out_specs = [pl.BlockSpec((L, SSD_INNER), chunk)]
        out_shape = [jax.ShapeDtypeStruct((t, SSD_INNER), F32)]
        if want_state:
            out_specs.append(pl.BlockSpec((1, 1, SSD_HEADS, SSD_HD, SSD_STATE), lambda b, c: (b, 0, 0, 0, 0)))
            out_shape.append(jax.ShapeDtypeStruct(st_shape, F32))
        return pl.pallas_call(
            functools.partial(_ssd_scan_kernel, seq=seq, rev=rev, has_h0=has_h0, want_state=want_state,
                              add_skip=not rev),
            grid=(nb, nc),
            in_specs=in_specs,
            out_specs=out_specs,
            out_shape=out_shape,
            scratch_shapes=[pltpu.VMEM((SSD_STATE, SSD_INNER), F32)],
            input_output_aliases=aliases,
            compiler_params=_params(2),
            name="ssd_scan_%s_seq%d" % ("bwd" if rev else "fwd", seq),
        )

    common = [xbc, dt_raw, lw["dt_bias"], lw["a_log"], dsk, tri, rep]
    y, st = make_call(seq_p, nb_p, 0, False, True, 0)(*common)
    (y,) = make_call(seq_s, nb_s, t_p // seq_s, True, False, 1)(*common, h0, y)
    return y, st


def _ssd_out_kernel(yf_ref, yb_ref, z_ref, gn_ref, w_ref, h_ref, gt1, g2, sh2, sc2, wr_hi, wr_lo, br,
                    h1_ref, xn_ref, route_ref):
    y = (yf_ref[...] + yb_ref[...]) * _silu(z_ref[...].astype(F32))
    o = _dot(_rms(y, gn_ref[...]).astype(BF16), w_ref[...])
    _finish(h_ref[...], o, gt1[0], (g2, sh2, sc2, wr_hi, wr_lo, br), h1_ref, xn_ref, route_ref)


def _ssd_layer(h, state_f, state_b, j, lw, layer, dims):
    t = dims["t"]
    zxbc = _pre_proj(h, lw, layer, lw["w_in"], 1536, BF16, dims)
    dt_raw = _pre_proj(h, lw, layer, lw["w_dt"], 128, F32, dims)
    xbc = _ssd_conv(zxbc, lw, dims)
    y_f, st_f = _ssd_scan(xbc, dt_raw, lw, state_f, j, False, dims)
    y_b, st_b = _ssd_scan(xbc, dt_raw, lw, state_b, j, True, dims)
    row_fn = _tile_row_fn(dims)
    tile = lambda: pl.BlockSpec((TM, D), lambda i: (i, 0))
    wide = lambda: pl.BlockSpec((TM, SSD_INNER), lambda i: (i, 0))
    outs = pl.pallas_call(
        _ssd_out_kernel,
        grid=(t // TM,),
        in_specs=[wide(), wide(), wide(), _full((1, SSD_INNER)), _full((SSD_INNER, D)), tile(),
                  _mod_spec(layer, 2, row_fn)] + _route_specs(layer, row_fn),
        out_specs=_stream_specs(TM, lambda i: i),
        out_shape=_stream_outs(t),
        compiler_params=_params(1),
        name="ssd_out",
    )(y_f, y_b, zxbc, lw["g_norm"], lw["w_out"], h, lw["mod"], *_route_args(lw))
    return outs, st_f, st_b


TS = 256
SRC = 256


def _moe_plan(route, t):
    i32 = jnp.int32
    gid = route[:, 0].astype(i32)
    oh = (gid[:, None] == jnp.arange(MOE_GROUPS, dtype=i32)[None, :]).astype(i32)
    csum = jnp.cumsum(oh, axis=0)
    cnt = csum[-1]
    padded = ((cnt + TS - 1) // TS) * TS
    gend = jnp.cumsum(padded)
    pos = jnp.sum(oh * (csum - 1 + (gend - padded)[None, :]), axis=1)
    n_tiles = t // TS + MOE_GROUPS
    tile0 = jnp.arange(n_tiles, dtype=i32) * TS
    tile_grp = jnp.minimum(jnp.sum((tile0[:, None] >= gend[None, :]).astype(i32), axis=1), MOE_GROUPS - 1)
    n_used = (gend[-1] // TS).reshape(1)
    st = pos // TS
    big = jnp.int32(1 << 20)
    src_tile = jnp.arange(t, dtype=i32) // SRC
    hit = st[None, :] == jnp.arange(n_tiles, dtype=i32)[:, None]
    g_lo = jnp.min(jnp.where(hit, src_tile[None, :], big), axis=1)
    g_hi = jnp.max(jnp.where(hit, src_tile[None, :], -1), axis=1)
    st_t = st.reshape(t // SRC, SRC, 1)
    oh_t = oh.reshape(t // SRC, SRC, MOE_GROUPS) > 0
    u_lo = jnp.min(jnp.where(oh_t, st_t, big), axis=1).reshape(-1)
    u_hi = jnp.max(jnp.where(oh_t, st_t, -1), axis=1).reshape(-1)
    u_rows = jnp.clip(u_hi - u_lo + 1, 0, 2) * TS
    u_lo = jnp.clip(u_lo, 0, n_tiles - 1)
    return dict(pos_row=pos.reshape(t // SRC, 1, SRC), pos_col=jnp.broadcast_to(pos[:, None], (t, 128)),
                tile_grp=tile_grp, n_used=n_used, g_lo=g_lo, g_hi=g_hi, u_lo=u_lo, u_rows=u_rows, n_tiles=n_tiles)


def _moe_gather_kernel(lo_ref, hi_ref, xn_ref, pos_ref, xs_ref, acc_ref, *, n_src):
    i = pl.program_id(0)
    acc_ref[...] = jnp.zeros_like(acc_ref)
    rows = i * TS + lax.broadcasted_iota(jnp.int32, (TS, 2 * SRC), 0)
    col_tile = lax.broadcasted_iota(jnp.int32, (1, 2 * SRC), 1) // SRC
    lo = lo_ref[i]

    def body(it, carry):
        s = lo + 2 * it
        s0 = jnp.minimum(s, n_src - 2)
        pos2 = jnp.concatenate([pos_ref[s0], pos_ref[s0 + 1]], axis=1)
        pos2 = jnp.where(s0 + col_tile >= s, pos2, -1)
        p = jnp.where(pos2 == rows, 1.0, 0.0).astype(BF16)
        r0 = pl.multiple_of(s0 * SRC, SRC)
        acc_ref[...] += _dot(p, xn_ref[pl.ds(r0, 2 * SRC), :])
        return carry

    lax.fori_loop(0, (hi_ref[i] - lo + 2) // 2, body, 0)
    xs_ref[...] = acc_ref[...].astype(BF16)


def _moe_expert_kernel(grp_ref, nused_ref, xs_ref, wg_ref, wu_ref, wd_ref, y_ref):
    del grp_ref
    i = pl.program_id(0)

    @pl.when(i < nused_ref[0])
    def _():
        x = xs_ref[:, :D]
        rec = xs_ref[:, D:].astype(F32)
        lane = lax.broadcasted_iota(jnp.int32, rec.shape, 1)
        acc = None
        for e in range(MOE_EPG):
            hg = _dot(x, wg_ref[0, e])
            hu = _dot(x, wu_ref[0, e])
            mine = (lane % MOE_EPG == e) & (lane < 3 * MOE_EPG)
            cw = jnp.sum(jnp.where(mine, rec, 0.0), axis=-1, keepdims=True)
            y = _dot((_silu(hg) * hu * cw).astype(BF16), wd_ref[0, e])
            acc = y if acc is None else acc + y
        y_ref[...] = acc.astype(BF16)

    @pl.when(i >= nused_ref[0])
    def _():
        y_ref[...] = jnp.zeros_like(y_ref)


def _moe_ungather_kernel(lo_ref, rows_ref, ys_ref, pos_ref, h1_ref, gt2, fg_ref, *out_refs, n_prompt_tiles):
    n = pl.program_id(0)
    pos = pos_ref[...]
    lane = lax.broadcasted_iota(jnp.int32, pos.shape, 1)
    acc = None
    for g in range(MOE_GROUPS):
        base = pl.multiple_of(lo_ref[n * MOE_GROUPS + g] * TS, TS)
        valid = rows_ref[n * MOE_GROUPS + g]
        parts = []
        for c0 in range(0, 2 * TS, 128):
            first = jnp.where(c0 < valid, base + c0, -(1 << 20))
            parts.append(jnp.where(pos == first + lane, 1.0, 0.0))
        q = jnp.concatenate(parts, axis=1).astype(BF16)
        d = _dot(q, ys_ref[pl.ds(base, 2 * TS), :])
        acc = d if acc is None else acc + d
    h2 = h1_ref[...] + gt2[0] * acc
    if n_prompt_tiles is None:
        out_refs[0][...] = h2
    else:
        y = _rms(h2, fg_ref[...])

        @pl.when(n < n_prompt_tiles)
        def _():
            out_refs[0][...] = y

        @pl.when(n >= n_prompt_tiles)
        def _():
            out_refs[1][...] = y


def _moe_sparse(h1, xn, route, lw, layer, final_g, final, dims):
    t = dims["t"]
    plan = _moe_plan(route, t)
    n_tiles = plan["n_tiles"]
    one = pl.Buffered(1)
    xs = pl.pallas_call(
        functools.partial(_moe_gather_kernel, n_src=t // SRC),
        grid_spec=pltpu.PrefetchScalarGridSpec(
            num_scalar_prefetch=2,
            grid=(n_tiles,),
            in_specs=[pl.BlockSpec((t, XW), lambda i, lo, hi: (0, 0), pipeline_mode=one),
                      pl.BlockSpec((t // SRC, 1, SRC), lambda i, lo, hi: (0, 0, 0))],
            out_specs=pl.BlockSpec((TS, XW), lambda i, lo, hi: (i, 0)),
            scratch_shapes=[pltpu.VMEM((TS, XW), F32)]),
        out_shape=jax.ShapeDtypeStruct((n_tiles * TS, XW), BF16),
        compiler_params=_params(1),
        name="moe_gather_l%d" % layer,
    )(plan["g_lo"], plan["g_hi"], xn, plan["pos_row"])

    ex = lambda i, g, nu: (layer, g[jnp.minimum(i, n_tiles - 1)], 0, 0)
    ys = pl.pallas_call(
        _moe_expert_kernel,
        grid_spec=pltpu.PrefetchScalarGridSpec(
            num_scalar_prefetch=2,
            grid=(n_tiles + 1,),
            in_specs=[pl.BlockSpec((TS, XW), lambda i, g, nu: (jnp.minimum(i, n_tiles - 1), 0)),
                      pl.BlockSpec((1, MOE_EPG, D, MOE_DFF), ex), pl.BlockSpec((1, MOE_EPG, D, MOE_DFF), ex),
                      pl.BlockSpec((1, MOE_EPG, MOE_DFF, D), ex)],
            out_specs=pl.BlockSpec((TS, D), lambda i, g, nu: (i, 0))),
        out_shape=jax.ShapeDtypeStruct(((n_tiles + 1) * TS, D), BF16),
        compiler_params=_params(1),
        name="moe_expert_l%d" % layer,
    )(plan["tile_grp"], plan["n_used"], xs, lw["w_gate"], lw["w_up"], lw["w_down"])
    n_rows = (n_tiles + 1) * TS

    row_fn = _tile_row_fn(dims, SRC)
    tile = lambda: pl.BlockSpec((SRC, D), lambda n, lo, hi: (n, 0))
    if final:
        npt = dims["t_p"] // SRC
        out_specs = [pl.BlockSpec((SRC, D), lambda n, lo, hi: (jnp.minimum(n, npt - 1), 0)),
                     pl.BlockSpec((SRC, D), lambda n, lo, hi: (jnp.maximum(n - npt, 0), 0))]
        out_shape = [jax.ShapeDtypeStruct((dims["t_p"], D), F32), jax.ShapeDtypeStruct((t - dims["t_p"], D), F32)]
    else:
        npt = None
        out_specs = [tile()]
        out_shape = [jax.ShapeDtypeStruct((t, D), F32)]
    return pl.pallas_call(
        functools.partial(_moe_ungather_kernel, n_prompt_tiles=npt),
        grid_spec=pltpu.PrefetchScalarGridSpec(
            num_scalar_prefetch=2,
            grid=(t // SRC,),
            in_specs=[pl.BlockSpec((n_rows, D), lambda n, lo, hi: (0, 0), pipeline_mode=one),
                      pl.BlockSpec((SRC, 128), lambda n, lo, hi: (n, 0)), tile(),
                      pl.BlockSpec((1, 1, D), lambda n, lo, hi: ((layer * MOD_ROWS + row_fn(n)) * N_MOD + 5, 0, 0)),
                      pl.BlockSpec((1, D), lambda n, lo, hi: (0, 0))],
            out_specs=out_specs),
        out_shape=out_shape,
        compiler_params=_params(1),
        name="moe_ungather_l%d" % layer,
    )(plan["u_lo"], plan["u_rows"], ys, plan["pos_col"], h1, lw["mod"], final_g)


def kernel(x_prompt, x_sample, cache_k, cache_v, state_ssm_fwd, state_ssm_bwd, c, c_ctx, ada_w, ada_b, norm1_g, norm2_g, final_g, fnet_w_o, na_w_qkv, na_w_o, na_rpb, gmlp_w_in, gmlp_g_v, gmlp_w_s, gmlp_b_s, gmlp_w_out, ssd_w_in, ssd_conv_w, ssd_conv_b, ssd_a_log, ssd_dt_bias, ssd_d_skip, ssd_g_norm, ssd_w_out, moe_w_gr, moe_b_gr, moe_w_er, moe_b_er, moe_w_gate, moe_w_up, moe_w_down):
    nb_p, seq_p, _ = x_prompt.shape
    nb_s, seq_s, _ = x_sample.shape
    depth = ada_w.shape[0]
    t_p, t_s = nb_p * seq_p, nb_s * seq_s
    dims = dict(t=t_p + t_s, t_p=t_p, seq_p=seq_p, seq_s=seq_s, nb_p=nb_p, nb_s=nb_s)
    assert 1 + nb_s <= MOD_ROWS and t_p % seq_s == 0 and t_p % TM == 0 and seq_s % TM == 0

    cond = jnp.zeros((MOD_ROWS, D), F32).at[0].set(c_ctx).at[1:1 + nb_s].set(c)
    mod = _ada_table(cond, ada_w, ada_b)
    h = (x_prompt.reshape(t_p, D), x_sample.reshape(t_s, D))
    fg = final_g.reshape(1, D)
    w_gate, w_up, w_down = moe_w_gate.astype(BF16), moe_w_up.astype(BF16), moe_w_down.astype(BF16)

    new_k, new_v, new_sf, new_sb = [], [], [], []
    for l in range(depth):
        kind, j = l % 4, l // 4
        w_r = jnp.concatenate([moe_w_gr[l], moe_w_er[l]], axis=1)
        w_r = jnp.pad(w_r, ((0, 0), (0, ROUTE_W - MOE_GROUPS - MOE_EXPERTS)))
        b_r = jnp.pad(jnp.concatenate([moe_b_gr[l], moe_b_er[l]]), (0, ROUTE_W - MOE_GROUPS - MOE_EXPERTS))
        wr_hi = w_r.astype(BF16)
        lw = dict(mod=mod, g1=norm1_g[l].reshape(1, D), g2=norm2_g[l].reshape(1, D),
                  wr_hi=wr_hi, wr_lo=(w_r - wr_hi.astype(F32)).astype(BF16), br=b_r.reshape(1, ROUTE_W),
                  w_gate=w_gate, w_up=w_up, w_down=w_down)
        if kind == 0:
            lw.update(w_o=fnet_w_o[j].astype(BF16))
            h1, xn, route = _fnet_layer(h, lw, l, dims)
        elif kind == 1:
            lw.update(w_qkv=na_w_qkv[j].astype(BF16), w_o=na_w_o[j].astype(BF16), rpb=na_rpb[j])
            (h1, xn, route), kc, vc = _na_layer(h, cache_k, cache_v, j, lw, l, dims)
            new_k.append(kc)
            new_v.append(vc)
        elif kind == 2:
            gw = GMLP_DFF // GMLP_GROUPS
            lw.update(w_in=gmlp_w_in[j].astype(BF16), g_v=gmlp_g_v[j].reshape(1, GMLP_DFF),
                      w_s=gmlp_w_s[j].astype(BF16),
                      b_s=jnp.broadcast_to(gmlp_b_s[j][:, :, None], (GMLP_GROUPS, GMLP_CHUNK, gw)),
                      w_out=gmlp_w_out[j].astype(BF16))
            h1, xn, route = _gmlp_layer(h, lw, l, dims)
        else:
            n_main = 3 * SSD_INNER
            w_in = ssd_w_in[j]
            pad = lambda v: jnp.pad(v, ((0, 0), (0, 128 - 2 * SSD_HEADS)))
            lw.update(w_in=w_in[:, :n_main].astype(BF16), w_dt=pad(w_in[:, n_main:]).astype(BF16),
                      conv_w=ssd_conv_w[j], conv_b=ssd_conv_b[j].reshape(1, -1),
                      dt_bias=pad(ssd_dt_bias[j].reshape(1, -1)), a_log=pad(ssd_a_log[j].reshape(1, -1)),
                      d_skip=ssd_d_skip[j], g_norm=ssd_g_norm[j].reshape(1, SSD_INNER),
                      w_out=ssd_w_out[j].astype(BF16))
            (h1, xn, route), sf, sb = _ssd_layer(h, state_ssm_fwd, state_ssm_bwd, j, lw, l, dims)
            new_sf.append(sf)
            new_sb.append(sb)
        out = _moe_sparse(h1, xn, route, lw, l, fg, l == depth - 1, dims)
        h = out if l == depth - 1 else out[0]

    y_prompt = h[0].reshape(nb_p, seq_p, D)
    y_sample = h[1].reshape(nb_s, seq_s, D)
    cat = lambda xs: jnp.concatenate(xs, axis=1)
    return (y_prompt, y_sample, cat(new_k), cat(new_v), cat(new_sf), cat(new_sb))
```

```python
import functools
import math

import jax
import jax.numpy as jnp
from jax import lax
from jax.experimental import pallas as pl
from jax.experimental.pallas import tpu as pltpu

F32 = jnp.float32
BF16 = jnp.bfloat16

D = 1024
EPS = 1e-6
NEG = -1e30
N_MOD = 6
MOD_ROWS = 8
GRID_W = 64
FNET_GROUPS = 8
NA_HEADS = 16
NA_HD = 64
NA_WIN_ROWS = 8
NA_WIN_COLS = 16
GMLP_CHUNK = 128
GMLP_DFF = 2048
GMLP_GROUPS = 8
SSD_INNER = 2048
SSD_HD = 64
SSD_HEADS = 32
SSD_GROUPS = 8
SSD_STATE = 128
SSD_CHUNK = 128
SSD_HPG = SSD_HEADS // SSD_GROUPS
MOE_GROUPS = 4
MOE_EPG = 4
MOE_EXPERTS = 16
MOE_DFF = 512
ROUTE_W = 128
ROUTE_E0 = 4

TM = 512
VMEM_LIMIT = 56 * 1024 * 1024


def _dot(a, b):
    return jnp.dot(a, b, preferred_element_type=F32)


def _dot_nt(a, b):
    return lax.dot_general(a, b, (((1,), (1,)), ((), ())), preferred_element_type=F32)


def _silu(x):
    return x * (1.0 / (1.0 + jnp.exp(-x)))


def _rms(x, g):
    return x * lax.rsqrt(jnp.mean(x * x, axis=-1, keepdims=True) + EPS) * g


def _normmod(x, g, shift, scale):
    return _rms(x, g) * (1.0 + scale) + shift


def _split3(x):
    hi = x.astype(BF16)
    r = x - hi.astype(F32)
    mid = r.astype(BF16)
    lo = (r - mid.astype(F32)).astype(BF16)
    return hi, mid, lo


def _params(n_axes):
    return pltpu.CompilerParams(dimension_semantics=("arbitrary",) * n_axes,
                                vmem_limit_bytes=VMEM_LIMIT)


def _full(shape):
    nd = len(shape)
    return pl.BlockSpec(shape, lambda *_: (0,) * nd)


def _mod_spec(layer, k, row_fn):
    return pl.BlockSpec((1, 1, D), lambda *idx: ((layer * MOD_ROWS + row_fn(*idx)) * N_MOD + k, 0, 0))


def _ada_kernel(c_ref, w_ref, b_ref, o_ref):
    c = c_ref[...]
    o_ref[0] = _dot(_silu(c).astype(BF16), w_ref[0].astype(BF16)) + b_ref[0]


def _ada_table(cond, ada_w, ada_b):
    depth = ada_w.shape[0]
    n = N_MOD * D
    tn = 1536
    out = pl.pallas_call(
        _ada_kernel,
        grid=(depth, n // tn),
        in_specs=[_full((MOD_ROWS, D)),
                  pl.BlockSpec((1, D, tn), lambda l, j: (l, 0, j)),
                  pl.BlockSpec((1, 1, tn), lambda l, j: (l, 0, j))],
        out_specs=pl.BlockSpec((1, MOD_ROWS, tn), lambda l, j: (l, 0, j)),
        out_shape=jax.ShapeDtypeStruct((depth, MOD_ROWS, n), F32),
        compiler_params=_params(2),
        name="ada_table",
    )(cond, ada_w, ada_b.reshape(depth, 1, n))
    return out.reshape(depth * MOD_ROWS * N_MOD, 1, D)


def _route(h1, g2, sh2, sc2, wr_hi, wr_lo, br):
    xn = _normmod(h1, g2, sh2, sc2)
    xh = xn.astype(BF16)
    xl = (xn - xh.astype(F32)).astype(BF16)
    logits = _dot(xh, wr_hi) + _dot(xh, wr_lo) + _dot(xl, wr_hi) + br
    lane = lax.broadcasted_iota(jnp.int32, logits.shape, 1).astype(F32)
    far = float(ROUTE_W)
    gl = jnp.where(lane < MOE_GROUPS, logits, NEG)
    gmax = jnp.max(gl, axis=-1, keepdims=True)
    g_p = 1.0 / jnp.sum(jnp.exp(gl - gmax), axis=-1, keepdims=True)
    gidx = jnp.min(jnp.where(gl == gmax, lane, far), axis=-1, keepdims=True)
    lo = ROUTE_E0 + MOE_EPG * gidx
    el = jnp.where((lane >= lo) & (lane < lo + MOE_EPG), logits, NEG)
    m1 = jnp.max(el, axis=-1, keepdims=True)
    i1 = jnp.min(jnp.where(el == m1, lane, far), axis=-1, keepdims=True)
    el2 = jnp.where(lane == i1, NEG, el)
    m2 = jnp.max(el2, axis=-1, keepdims=True)
    i2 = jnp.min(jnp.where(el2 == m2, lane, far), axis=-1, keepdims=True)
    e2 = jnp.exp(m2 - m1)
    w1 = g_p / (1.0 + e2)
    w2 = w1 * e2
    rec = jnp.zeros_like(logits)
    for part, (a, b) in enumerate(zip(_split3(w1), _split3(w2))):
        shift = part * MOE_EPG - lo
        rec = (rec + jnp.where(lane == i1 + shift, a.astype(F32), 0.0)
               + jnp.where(lane == i2 + shift, b.astype(F32), 0.0))
    return xh, rec.astype(BF16), jnp.broadcast_to(gidx, logits.shape)


def _route_specs(layer, row_fn):
    return [_full((1, D)), _mod_spec(layer, 3, row_fn), _mod_spec(layer, 4, row_fn),
            _full((D, ROUTE_W)), _full((D, ROUTE_W)), _full((1, ROUTE_W))]


def _route_args(lw):
    return [lw["g2"], lw["mod"], lw["mod"], lw["wr_hi"], lw["wr_lo"], lw["br"]]


def _finish(h, o, gate, rt_refs, h1_ref, xn_ref, route_ref):
    g2, sh2, sc2, wr_hi, wr_lo, br = rt_refs
    h1 = h + gate * o
    h1_ref[...] = h1
    xn, rec, gid = _route(h1, g2[...], sh2[0], sc2[0], wr_hi[...], wr_lo[...], br[...])
    xn_ref[:, :D] = xn
    xn_ref[:, D:] = rec
    route_ref[...] = gid


XW = D + ROUTE_W


def _stream_outs(t):
    return [jax.ShapeDtypeStruct((t, D), F32), jax.ShapeDtypeStruct((t, XW), BF16),
            jax.ShapeDtypeStruct((t, ROUTE_W), F32)]


def _stream_specs(rows, row_block):
    return [pl.BlockSpec((rows, w), lambda *idx: (row_block(*idx), 0)) for w in (D, XW, ROUTE_W)]


def _fnet_kernel(h_ref, g1, sh1, sc1, gt1, csc_ref, fs_ref, wo_ref, g2, sh2, sc2, wr_hi, wr_lo, br,
                 *rest, seq):
    h1_ref, xn_ref, route_ref, ab_ref = rest[-4:]
    h = h_ref[...]
    a = _normmod(h, g1[...], sh1[0], sc1[0]).astype(BF16)
    gd = D // FNET_GROUPS
    for g in range(FNET_GROUPS):
        ab = _dot(a[:, g * gd:(g + 1) * gd], csc_ref[...])
        ab_ref[0:seq, g * gd:(g + 1) * gd] = ab[:, :gd].astype(BF16)
        ab_ref[seq:2 * seq, g * gd:(g + 1) * gd] = ab[:, gd:].astype(BF16)
    f = _dot(fs_ref[...], ab_ref[...])
    o = _dot(f.astype(BF16), wo_ref[...])
    _finish(h, o, gt1[0], (g2, sh2, sc2, wr_hi, wr_lo, br), h1_ref, xn_ref, route_ref)


def _dft_tables(n):
    k = jnp.arange(n, dtype=jnp.int32)
    ang = ((k[:, None] * k[None, :]) % n).astype(F32) * (2.0 * math.pi / n)
    s = 1.0 / math.sqrt(n)
    return jnp.cos(ang) * s, jnp.sin(ang) * s


def _fnet_layer(h, lw, layer, dims):
    t, t_p = dims["t"], dims["t_p"]
    gd = D // FNET_GROUPS
    cc, sc = _dft_tables(gd)
    csc = jnp.concatenate([cc, sc], axis=1).astype(BF16)

    split_in = isinstance(h, tuple)

    def make_call(seq, nb, off, n_alias):
        cs, ss = _dft_tables(seq)
        fs = jnp.concatenate([cs, -ss], axis=1).astype(BF16)
        row_fn = (lambda b: 0) if off == 0 else (lambda b: 1 + b)
        in_off = 0 if split_in else off
        tile = lambda: pl.BlockSpec((seq, D), lambda b: (off + b, 0))
        in_specs = ([pl.BlockSpec((seq, D), lambda b: (in_off + b, 0)), _full((1, D))]
                    + [_mod_spec(layer, k, row_fn) for k in (0, 1, 2)]
                    + [_full((gd, 2 * gd)), _full((seq, 2 * seq)), _full((D, D))]
                    + _route_specs(layer, row_fn))
        aliases = {}
        if n_alias:
            base = len(in_specs)
            in_specs = in_specs + [pl.BlockSpec(memory_space=pl.ANY)] * n_alias
            aliases = {base + i: i for i in range(n_alias)}
        call = pl.pallas_call(
            functools.partial(_fnet_kernel, seq=seq),
            grid=(nb,),
            in_specs=in_specs,
            out_specs=_stream_specs(seq, lambda b: off + b),
            out_shape=_stream_outs(t),
            scratch_shapes=[pltpu.VMEM((2 * seq, D), BF16)],
            input_output_aliases=aliases,
            compiler_params=_params(1),
            name="fnet_seq%d" % seq,
        )
        return lambda *a: call(*a[:2], *a[2:5], a[5], fs, *a[6:])

    h_p, h_s = h if split_in else (h, h)
    args = [lw["g1"], lw["mod"], lw["mod"], lw["mod"], csc, lw["w_o"]] + _route_args(lw)
    outs = make_call(dims["seq_p"], dims["nb_p"], 0, None)(h_p, *args)
    return make_call(dims["seq_s"], dims["nb_s"], t_p // dims["seq_s"], 3)(h_s, *args, *outs)


def _pre_kernel(h_ref, g1, sh1, sc1, w_ref, o_ref, a_ref):
    @pl.when(pl.program_id(1) == 0)
    def _():
        a_ref[...] = _normmod(h_ref[...], g1[...], sh1[0], sc1[0]).astype(BF16)

    o_ref[...] = _dot(a_ref[...], w_ref[...]).astype(o_ref.dtype)


def _tile_row_fn(dims, tm=TM):
    npt = dims["t_p"] // tm
    tps = dims["seq_s"] // tm
    return lambda i, *_: jnp.where(i < npt, 0, 1 + (i - npt) // tps)


def _pre_proj(h, lw, layer, w, tn, out_dtype, dims):
    t = dims["t"]
    n = w.shape[1]
    tm = dims["seq_s"]
    row_fn = _tile_row_fn(dims, tm)
    return pl.pallas_call(
        _pre_kernel,
        grid=(t // tm, n // tn),
        in_specs=[pl.BlockSpec((tm, D), lambda i, j: (i, 0)), _full((1, D)),
                  _mod_spec(layer, 0, row_fn), _mod_spec(layer, 1, row_fn),
                  pl.BlockSpec((D, tn), lambda i, j: (0, j))],
        out_specs=pl.BlockSpec((tm, tn), lambda i, j: (i, j)),
        out_shape=jax.ShapeDtypeStruct((t, n), out_dtype),
        scratch_shapes=[pltpu.VMEM((tm, D), BF16)],
        compiler_params=_params(2),
        name="pre_proj_l%d_n%d" % (layer, n),
    )(h, lw["g1"], lw["mod"], lw["mod"], w)


def _out_kernel(o_ref, w_ref, h_ref, gt1, g2, sh2, sc2, wr_hi, wr_lo, br, h1_ref, xn_ref, route_ref):
    o = _dot(o_ref[...], w_ref[...])
    _finish(h_ref[...], o, gt1[0], (g2, sh2, sc2, wr_hi, wr_lo, br), h1_ref, xn_ref, route_ref)


def _out_proj(o, w, h, lw, layer, dims):
    t = dims["t"]
    k = o.shape[1]
    row_fn = _tile_row_fn(dims)
    tile = lambda: pl.BlockSpec((TM, D), lambda i: (i, 0))
    return pl.pallas_call(
        _out_kernel,
        grid=(t // TM,),
        in_specs=[pl.BlockSpec((TM, k), lambda i: (i, 0)), _full((k, D)), tile(),
                  _mod_spec(layer, 2, row_fn)] + _route_specs(layer, row_fn),
        out_specs=_stream_specs(TM, lambda i: i),
        out_shape=_stream_outs(t),
        compiler_params=_params(1),
        name="out_proj_l%d" % layer,
    )(o, w, h, lw["mod"], *_route_args(lw))


NA_SCALE = NA_HD ** -0.5


def _na_ctx_kernel(q_ref, k_ref, v_ref, o_ref, kc_ref, vc_ref):
    outs = []
    for hd in range(NA_HEADS):
        sl = slice(hd * NA_HD, (hd + 1) * NA_HD)
        q = q_ref[:, sl] * NA_SCALE
        k = k_ref[:, sl]
        v = v_ref[:, sl]
        s = _dot_nt(q, k)
        p = jnp.exp(s - jnp.max(s, axis=-1, keepdims=True))
        l = jnp.sum(p, axis=-1, keepdims=True)
        outs.append((_dot(p.astype(BF16), v) / l).astype(BF16))
        kc_ref[0, 0, hd] = k.astype(F32)
        vc_ref[0, 0, hd] = v.astype(F32)
    o_ref[...] = jnp.concatenate(outs, axis=1)


def _na_row_start(qr, rows):
    kr = min(NA_WIN_ROWS, rows)
    return min(max(qr - kr // 2, 0), rows - kr)


def _na_window_bias(bias_ref, hh, qr, rows, m_lo, m_hi):
    kr = min(NA_WIN_ROWS, rows)
    rs = _na_row_start(qr, rows)
    blocks = []
    for m in range(m_lo, m_hi):
        ok0 = rs <= 2 * m < rs + kr
        ok1 = rs <= 2 * m + 1 < rs + kr
        e = 2 * m - qr + NA_WIN_ROWS
        if ok0 and ok1:
            blocks.append(bias_ref[hh, 0, e])
        elif ok1:
            blocks.append(bias_ref[hh, 1, e])
        elif ok0:
            blocks.append(bias_ref[hh, 2, e])
        else:
            blocks.append(jnp.full((GRID_W, 2 * GRID_W), NEG, F32))
    return jnp.concatenate(blocks, axis=1)


def _na_lat_kernel(q_ref, k_ref, v_ref, bias_ref, kc_ref, vc_ref, o_in, o_ref, *, seq, qb):
    del o_in
    rows = seq // GRID_W
    for hh in range(2):
        sl = slice(hh * NA_HD, (hh + 1) * NA_HD)
        k = k_ref[:, sl]
        v = v_ref[:, sl]
        kc = kc_ref[0, 0, hh].astype(BF16)
        vc = vc_ref[0, 0, hh].astype(BF16)
        for b0 in range(0, seq, qb):
            q = q_ref[b0:b0 + qb, sl] * NA_SCALE
            qrs = range(b0 // GRID_W, (b0 + qb) // GRID_W)
            m_lo = _na_row_start(qrs[0], rows) // 2
            m_hi = (_na_row_start(qrs[-1], rows) + min(NA_WIN_ROWS, rows) + 1) // 2
            keys = slice(m_lo * 2 * GRID_W, m_hi * 2 * GRID_W)
            bias = jnp.concatenate([_na_window_bias(bias_ref, hh, qr, rows, m_lo, m_hi) for qr in qrs], axis=0)
            s1 = _dot_nt(q, k[keys]) + bias
            s2 = _dot_nt(q, kc)
            m = jnp.maximum(jnp.max(s1, axis=-1, keepdims=True), jnp.max(s2, axis=-1, keepdims=True))
            p1 = jnp.exp(s1 - m)
            p2 = jnp.exp(s2 - m)
            l = jnp.sum(p1, axis=-1, keepdims=True) + jnp.sum(p2, axis=-1, keepdims=True)
            o = (_dot(p1.astype(BF16), v[keys]) + _dot(p2.astype(BF16), vc)) / l
            o_ref[b0:b0 + qb, sl] = o.astype(BF16)


def _na_bias_tables(rpb):
    c = jnp.arange(GRID_W)
    win0 = jnp.clip(c - NA_WIN_COLS // 2, 0, GRID_W - NA_WIN_COLS)
    ok_c = (c[None, :] >= win0[:, None]) & (c[None, :] < win0[:, None] + NA_WIN_COLS)
    dc = jnp.clip(c[None, :] - c[:, None], 1 - NA_WIN_COLS, NA_WIN_COLS - 1) + NA_WIN_COLS - 1
    nh, ndr, ndc = rpb.shape
    pick = (dc.reshape(1, -1) == jnp.arange(ndc)[:, None]).astype(F32)
    cm = jnp.dot(rpb.reshape(nh * ndr, ndc), pick, precision=lax.Precision.HIGHEST)
    cm = jnp.where(ok_c[None, None], cm.reshape(nh, ndr, GRID_W, GRID_W), NEG)
    neg = jnp.full_like(cm[:, :1], NEG)
    ext = jnp.concatenate([neg, cm, neg], axis=1)
    a, b = ext[:, :-1], ext[:, 1:]
    negs = jnp.full_like(a, NEG)
    pair = lambda x, y: jnp.concatenate([x, y], axis=-1)
    return jnp.stack([pair(a, b), pair(negs, b), pair(a, negs)], axis=1)


def _na_layer(h, cache_k, cache_v, j, lw, layer, dims):
    t, t_p, seq_p, seq_s = dims["t"], dims["t_p"], dims["seq_p"], dims["seq_s"]
    nb_p, nb_s = dims["nb_p"], dims["nb_s"]
    qkv = _pre_proj(h, lw, layer, lw["w_qkv"], 1536, BF16, dims)
    cshape = (nb_p, 1, NA_HEADS, seq_p, NA_HD)
    cspec = lambda: pl.BlockSpec((1, 1, NA_HEADS, seq_p, NA_HD), lambda b: (b, 0, 0, 0, 0))
    o, kc, vc = pl.pallas_call(
        _na_ctx_kernel,
        grid=(nb_p,),
        in_specs=[pl.BlockSpec((seq_p, D), lambda b: (b, 0)), pl.BlockSpec((seq_p, D), lambda b: (b, 1)),
                  pl.BlockSpec((seq_p, D), lambda b: (b, 2))],
        out_specs=[pl.BlockSpec((seq_p, D), lambda b: (b, 0)), cspec(), cspec()],
        out_shape=[jax.ShapeDtypeStruct((t, D), BF16), jax.ShapeDtypeStruct(cshape, F32),
                   jax.ShapeDtypeStruct(cshape, F32)],
        compiler_params=_params(1),
        name="na_context",
    )(qkv, qkv, qkv)

    bias = _na_bias_tables(lw["rpb"])
    past = cache_k.shape[3]
    off = t_p // seq_s
    npair = NA_HEADS // 2
    pw = 2 * NA_HD
    pspec = lambda: pl.BlockSpec((1, 1, 2, past, NA_HD), lambda hp, b: (b, j, hp, 0, 0))
    o = pl.pallas_call(
        functools.partial(_na_lat_kernel, seq=seq_s, qb=256),
        grid=(npair, nb_s),
        in_specs=[pl.BlockSpec((seq_s, pw), lambda hp, b: (off + b, hp)),
                  pl.BlockSpec((seq_s, pw), lambda hp, b: (off + b, npair + hp)),
                  pl.BlockSpec((seq_s, pw), lambda hp, b: (off + b, 2 * npair + hp)),
                  pl.BlockSpec((2, 3, 2 * NA_WIN_ROWS, GRID_W, 2 * GRID_W), lambda hp, b: (hp, 0, 0, 0, 0)),
                  pspec(), pspec(), pl.BlockSpec(memory_space=pl.ANY)],
        out_specs=pl.BlockSpec((seq_s, pw), lambda hp, b: (off + b, hp)),
        out_shape=jax.ShapeDtypeStruct((t, D), BF16),
        input_output_aliases={6: 0},
        compiler_params=_params(2),
        name="na_latent",
    )(qkv, qkv, qkv, bias, cache_k, cache_v, o)
    outs = _out_proj(o, lw["w_o"], h, lw, layer, dims)
    return outs, kc, vc


def _gelu_tanh(x):
    return 0.5 * x * (1.0 + jnp.tanh(math.sqrt(2.0 / math.pi) * (x + 0.044715 * (x * x * x))))


def _gmlp_kernel(h_ref, g1, sh1, sc1, gt1, win_ref, gv_ref, ws_ref, bs_ref, wout_ref,
                 g2, sh2, sc2, wr_hi, wr_lo, br, h1_ref, xn_ref, route_ref, m_ref):
    h = h_ref[...]
    a = _normmod(h, g1[...], sh1[0], sc1[0]).astype(BF16)
    u = _gelu_tanh(_dot(a, win_ref[:, :GMLP_DFF]))
    v = _gelu_tanh(_dot(a, win_ref[:, GMLP_DFF:]))
    v = _rms(v, gv_ref[...]).astype(BF16)
    gw = GMLP_DFF // GMLP_GROUPS
    for c in range(TM // GMLP_CHUNK):
        rows = slice(c * GMLP_CHUNK, (c + 1) * GMLP_CHUNK)
        for g in range(GMLP_GROUPS):
            cols = slice(g * gw, (g + 1) * gw)
            vs = _dot(ws_ref[g], v[rows, cols]) + bs_ref[g]
            m_ref[rows, cols] = (u[rows, cols] * vs).astype(BF16)
    o = _dot(m_ref[...], wout_ref[...])
    _finish(h, o, gt1[0], (g2, sh2, sc2, wr_hi, wr_lo, br), h1_ref, xn_ref, route_ref)


def _gmlp_layer(h, lw, layer, dims):
    t = dims["t"]
    row_fn = _tile_row_fn(dims)
    gw = GMLP_DFF // GMLP_GROUPS
    tile = lambda: pl.BlockSpec((TM, D), lambda i: (i, 0))
    one = pl.Buffered(1)
    return pl.pallas_call(
        _gmlp_kernel,
        grid=(t // TM,),
        in_specs=[tile(), _full((1, D))] + [_mod_spec(layer, k, row_fn) for k in (0, 1, 2)]
                 + [pl.BlockSpec((D, 2 * GMLP_DFF), lambda i: (0, 0), pipeline_mode=one),
                    _full((1, GMLP_DFF)), _full((GMLP_GROUPS, GMLP_CHUNK, GMLP_CHUNK)),
                    _full((GMLP_GROUPS, GMLP_CHUNK, gw)),
                    pl.BlockSpec((GMLP_DFF, D), lambda i: (0, 0), pipeline_mode=one)]
                 + _route_specs(layer, row_fn),
        out_specs=_stream_specs(TM, lambda i: i),
        out_shape=_stream_outs(t),
        scratch_shapes=[pltpu.VMEM((TM, GMLP_DFF), BF16)],
        compiler_params=_params(1),
        name="gmlp",
    )(h, lw["g1"], lw["mod"], lw["mod"], lw["mod"], lw["w_in"], lw["g_v"], lw["w_s"], lw["b_s"],
      lw["w_out"], *_route_args(lw))


HALO = 16


def _ssd_conv_kernel(x_ref, bc_ref, cw_ref, cb_ref, *rest, seq):
    o_ref, cat = rest[-2:]
    L = SSD_CHUNK
    nc = seq // L
    c = pl.program_id(1)
    r0 = pl.multiple_of(c * L, L)
    rp = pl.multiple_of(jnp.maximum(r0 - HALO, 0), HALO)
    rn = pl.multiple_of(jnp.minimum(r0 + L, seq - HALO), HALO)
    has_prev = (c > 0).astype(F32)
    has_next = (c < nc - 1).astype(F32)
    for src, lo in ((x_ref, 0), (bc_ref, SSD_INNER)):
        cols = slice(lo, lo + SSD_INNER)
        cat[0:HALO, cols] = src[pl.ds(rp, HALO), :].astype(F32) * has_prev
        cat[HALO:HALO + L, cols] = src[pl.ds(r0, L), :].astype(F32)
        cat[HALO + L:2 * HALO + L, cols] = src[pl.ds(rn, HALO), :].astype(F32) * has_next
    conv = cb_ref[...] + sum(cat[HALO - 2 + k:HALO - 2 + k + L, :] * cw_ref[k:k + 1, :] for k in range(4))
    o_ref[...] = _silu(conv).astype(BF16)


def _ssd_conv(zxbc, lw, dims):
    t, t_p = dims["t"], dims["t_p"]
    L = SSD_CHUNK

    def make_call(seq, nb, off, aliased):
        nc = seq // L
        in_specs = [pl.BlockSpec((seq, SSD_INNER), lambda b, c: (off + b, 1)),
                    pl.BlockSpec((seq, SSD_INNER), lambda b, c: (off + b, 2)),
                    _full((4, 2 * SSD_INNER)), _full((1, 2 * SSD_INNER))]
        if aliased:
            in_specs.append(pl.BlockSpec(memory_space=pl.ANY))
        return pl.pallas_call(
            functools.partial(_ssd_conv_kernel, seq=seq),
            grid=(nb, nc),
            in_specs=in_specs,
            out_specs=pl.BlockSpec((L, 2 * SSD_INNER), lambda b, c: ((off + b) * nc + c, 0)),
            out_shape=jax.ShapeDtypeStruct((t, 2 * SSD_INNER), BF16),
            scratch_shapes=[pltpu.VMEM((L + 2 * HALO, 2 * SSD_INNER), F32)],
            input_output_aliases={4: 0} if aliased else {},
            compiler_params=_params(2),
            name="ssd_conv_seq%d" % seq,
        )

    args = [zxbc, zxbc, lw["conv_w"], lw["conv_b"]]
    xbc = make_call(dims["seq_p"], dims["nb_p"], 0, False)(*args)
    return make_call(dims["seq_s"], dims["nb_s"], t_p // dims["seq_s"], True)(*args, xbc)


def _ssd_scan_kernel(*refs, seq, rev, has_h0, want_state, add_skip):
    (xbc_ref, dt_ref, dtb_ref, alog_ref, dsk_ref, tri_ref, rep_ref) = refs[:7]
    pos = 7
    h0_ref = None
    if has_h0:
        h0_ref = refs[pos]
        pos += 1
    n_alias = len(refs) - pos - (2 if want_state else 1) - 1
    pos += n_alias
    y_ref = refs[pos]
    st_ref = refs[pos + 1] if want_state else None
    state = refs[-1]

    L = SSD_CHUNK
    nc = seq // L
    c = pl.program_id(1)

    @pl.when(c == 0)
    def _():
        if has_h0:
            for i in range(SSD_INNER // L):
                hpb = L // SSD_HD
                blk = h0_ref[0, 0, i * hpb:(i + 1) * hpb].reshape(L, SSD_STATE)
                state[:, i * L:(i + 1) * L] = blk.T
        else:
            state[...] = jnp.zeros_like(state)

    xc = xbc_ref[:, :SSD_INNER].astype(F32)
    bm = xbc_ref[:, SSD_INNER:SSD_INNER + SSD_GROUPS * SSD_STATE]
    cm = xbc_ref[:, SSD_INNER + SSD_GROUPS * SSD_STATE:]

    dtr = dt_ref[...] + dtb_ref[...]
    dt = jnp.maximum(dtr, 0.0) + jnp.log(1.0 + jnp.exp(-jnp.abs(dtr)))
    dta = dt * (-jnp.exp(alog_ref[...]))
    tri = tri_ref[...]
    p = sum(_dot(tri, part) for part in _split3(dta))
    pt = p.T
    edge = 0 if rev else L - 1
    p_edge = p[edge:edge + 1, :]
    rep = rep_ref[...]
    dt_x = _dot(dt.astype(BF16), rep)
    ep_x = _dot(jnp.exp(p).astype(BF16), rep)
    dte_x = _dot(jnp.exp(p_edge - p).astype(BF16), rep)
    cdec_x = _dot(jnp.broadcast_to(jnp.exp(p_edge), (8, p.shape[1])).astype(BF16), rep)[0:1, :]

    dtx = xc * dt_x
    dtxb = dtx.astype(BF16)
    xdte = (dtx * dte_x).astype(BF16)
    li = lax.broadcasted_iota(jnp.int32, (L, L), 0)
    si = lax.broadcasted_iota(jnp.int32, (L, L), 1)
    keep = (li <= si) if rev else (li >= si)
    lane0 = SSD_HEADS if rev else 0
    gw = SSD_HPG * SSD_HD
    ys, new_state = [], []
    for g in range(SSD_GROUPS):
        gcols = slice(g * gw, (g + 1) * gw)
        b_g = bm[:, g * SSD_STATE:(g + 1) * SSD_STATE]
        c_g = cm[:, g * SSD_STATE:(g + 1) * SSD_STATE]
        cb = _dot_nt(c_g, b_g)
        st_prev = state[:, gcols]
        y_g = _dot(c_g, st_prev.astype(BF16)) * ep_x[:, gcols]
        yd = []
        for hh in range(SSD_HPG):
            hl = lane0 + g * SSD_HPG + hh
            seg = p[:, hl:hl + 1] - pt[hl:hl + 1, :]
            mat = cb * jnp.exp(jnp.where(keep, seg, NEG))
            hc = slice((g * SSD_HPG + hh) * SSD_HD, (g * SSD_HPG + hh + 1) * SSD_HD)
            yd.append(_dot(mat.astype(BF16), dtxb[:, hc]))
        y_g = y_g + jnp.concatenate(yd, axis=1)
        if add_skip:
            y_g = y_g + dsk_ref[:, gcols] * xc[:, gcols]
        ys.append(y_g)
        new_state.append(st_prev * cdec_x[:, gcols] + _dot(b_g.astype(F32).T.astype(BF16), xdte[:, gcols]))
    y_ref[...] = jnp.concatenate(ys, axis=1)
    state[...] = jnp.concatenate(new_state, axis=1)

    if want_state:
        @pl.when(c == nc - 1)
        def _():
            for i in range(SSD_INNER // L):
                blk = state[:, i * L:(i + 1) * L].T
                st_ref[0, 0, i * (L // SSD_HD):(i + 1) * (L // SSD_HD)] = blk.reshape(L // SSD_HD, SSD_HD, SSD_STATE)


def _ssd_scan(xbc, dt_raw, lw, h0, j, rev, dims):
    t, t_p, seq_p, seq_s = dims["t"], dims["t_p"], dims["seq_p"], dims["seq_s"]
    nb_p, nb_s = dims["nb_p"], dims["nb_s"]
    L = SSD_CHUNK
    d = 1 if rev else 0
    li = jnp.arange(L)
    tri = ((li[:, None] <= li[None, :]) if rev else (li[:, None] >= li[None, :])).astype(BF16)
    lane = jnp.arange(128)
    col_head = jnp.arange(SSD_INNER) // SSD_HD
    rep = (lane[:, None] == (d * SSD_HEADS + col_head)[None, :]).astype(BF16)
    dsk = jnp.repeat(lw["d_skip"], SSD_HD)[None, :].astype(F32)
    st_shape = (nb_p, 1, SSD_HEADS, SSD_HD, SSD_STATE)

    def make_call(seq, nb, off, has_h0, want_state, n_alias):
        nc = seq // L
        chunk = (lambda b, c: (off * nc + b * nc + (nc - 1 - c), 0)) if rev else (lambda b, c: (off * nc + b * nc + c, 0))
        in_specs = [pl.BlockSpec((L, 2 * SSD_INNER), chunk), pl.BlockSpec((L, 128), chunk),
                    _full((1, 128)), _full((1, 128)),
                    _full((1, SSD_INNER)), _full((L, L)), _full((128, SSD_INNER))]
        if has_h0:
            in_specs.append(pl.BlockSpec((1, 1, SSD_HEADS, SSD_HD, SSD_STATE), lambda b, c: (b, j, 0, 0, 0)))
        aliases = {}
        if n_alias:
            aliases = {len(in_specs): 0}
            in_specs.append(pl.BlockSpec(memory_space=pl.ANY))
        out_specs = [pl.BlockSpec((L, SSD_INNER), chunk)]
        out_shape = [jax.ShapeDtypeStruct((t, SSD_INNER), F32)]
        if want_state:
            out_specs.append(pl.BlockSpec((1, 1, SSD_HEADS, SSD_HD, SSD_STATE), lambda b, c: (b, 0, 0, 0, 0)))
            out_shape.append(jax.ShapeDtypeStruct(st_shape, F32))
        return pl.pallas_call(
            functools.partial(_ssd_scan_kernel, seq=seq, rev=rev, has_h0=has_h0, want_state=want_state,
                              add_skip=not rev),
            grid=(nb, nc),
            in_specs=in_specs,
            out_specs=out_specs,
            out_shape=out_shape,
            scratch_shapes=[pltpu.VMEM((SSD_STATE, SSD_INNER), F32)],
            input_output_aliases=aliases,
            compiler_params=_params(2),
            name="ssd_scan_%s_seq%d" % ("bwd" if rev else "fwd", seq),
        )

    common = [xbc, dt_raw, lw["dt_bias"], lw["a_log"], dsk, tri, rep]
    y, st = make_call(seq_p, nb_p, 0, False, True, 0)(*common)
    (y,) = make_call(seq_s, nb_s, t_p // seq_s, True, False, 1)(*common, h0, y)
    return y, st


def _ssd_out_kernel(yf_ref, yb_ref, z_ref, gn_ref, w_ref, h_ref, gt1, g2, sh2, sc2, wr_hi, wr_lo, br,
                    h1_ref, xn_ref, route_ref):
    y = (yf_ref[...] + yb_ref[...]) * _silu(z_ref[...].astype(F32))
    o = _dot(_rms(y, gn_ref[...]).astype(BF16), w_ref[...])
    _finish(h_ref[...], o, gt1[0], (g2, sh2, sc2, wr_hi, wr_lo, br), h1_ref, xn_ref, route_ref)


def _ssd_layer(h, state_f, state_b, j, lw, layer, dims):
    t = dims["t"]
    zxbc = _pre_proj(h, lw, layer, lw["w_in"], 1536, BF16, dims)
    dt_raw = _pre_proj(h, lw, layer, lw["w_dt"], 128, F32, dims)
    xbc = _ssd_conv(zxbc, lw, dims)
    y_f, st_f = _ssd_scan(xbc, dt_raw, lw, state_f, j, False, dims)
    y_b, st_b = _ssd_scan(xbc, dt_raw, lw, state_b, j, True, dims)
    row_fn = _tile_row_fn(dims)
    tile = lambda: pl.BlockSpec((TM, D), lambda i: (i, 0))
    wide = lambda: pl.BlockSpec((TM, SSD_INNER), lambda i: (i, 0))
    outs = pl.pallas_call(
        _ssd_out_kernel,
        grid=(t // TM,),
        in_specs=[wide(), wide(), wide(), _full((1, SSD_INNER)), _full((SSD_INNER, D)), tile(),
                  _mod_spec(layer, 2, row_fn)] + _route_specs(layer, row_fn),
        out_specs=_stream_specs(TM, lambda i: i),
        out_shape=_stream_outs(t),
        compiler_params=_params(1),
        name="ssd_out",
    )(y_f, y_b, zxbc, lw["g_norm"], lw["w_out"], h, lw["mod"], *_route_args(lw))
    return outs, st_f, st_b


TS = 512
SRC = 256


def _moe_plan(route, t):
    i32 = jnp.int32
    gid = route[:, 0].astype(i32)
    oh = (gid[:, None] == jnp.arange(MOE_GROUPS, dtype=i32)[None, :]).astype(i32)
    csum = jnp.cumsum(oh, axis=0)
    cnt = csum[-1]
    padded = ((cnt + TS - 1) // TS) * TS
    gend = jnp.cumsum(padded)
    pos = jnp.sum(oh * (csum - 1 + (gend - padded)[None, :]), axis=1)
    n_tiles = t // TS + MOE_GROUPS
    tile_all = jnp.arange(n_tiles + 1, dtype=i32) * TS
    tile_grp = jnp.sum((tile_all[:, None] >= gend[None, :]).astype(i32), axis=1)
    last_grp = jnp.max(jnp.where(cnt > 0, jnp.arange(MOE_GROUPS, dtype=i32), 0))
    tile_grp = jnp.minimum(tile_grp, last_grp)
    n_used = (gend[-1] // TS).reshape(1)
    blk = pos // SRC
    big = jnp.int32(1 << 20)
    src_tile = jnp.arange(t, dtype=i32) // SRC
    hit = blk[None, :] == jnp.arange(n_tiles * TS // SRC, dtype=i32)[:, None]
    g_lo = jnp.min(jnp.where(hit, src_tile[None, :], big), axis=1)
    g_hi = jnp.max(jnp.where(hit, src_tile[None, :], -1), axis=1)
    blk_t = blk.reshape(t // SRC, SRC, 1)
    oh_t = oh.reshape(t // SRC, SRC, MOE_GROUPS) > 0
    u_lo = jnp.min(jnp.where(oh_t, blk_t, big), axis=1).reshape(-1)
    u_hi = jnp.max(jnp.where(oh_t, blk_t, -1), axis=1).reshape(-1)
    u_rows = jnp.clip(u_hi - u_lo + 1, 0, 2) * SRC
    u_lo = jnp.clip(u_lo, 0, n_tiles * TS // SRC - 1)
    return dict(pos_row=pos.reshape(t // SRC, 1, SRC), pos_col=jnp.broadcast_to(pos[:, None], (t, 128)),
                tile_grp=tile_grp, n_used=n_used, g_lo=g_lo, g_hi=g_hi, u_lo=u_lo, u_rows=u_rows, n_tiles=n_tiles)


def _moe_gather_kernel(lo_ref, hi_ref, xn_ref, pos_ref, xs_ref, acc_ref, *, n_src):
    i = pl.program_id(0)
    acc_ref[...] = jnp.zeros_like(acc_ref)
    rows = i * SRC + lax.broadcasted_iota(jnp.int32, (SRC, 2 * SRC), 0)
    col_tile = lax.broadcasted_iota(jnp.int32, (1, 2 * SRC), 1) // SRC
    lo = lo_ref[i]

    def body(it, carry):
        s = lo + 2 * it
        s0 = jnp.minimum(s, n_src - 2)
        pos2 = jnp.concatenate([pos_ref[s0], pos_ref[s0 + 1]], axis=1)
        pos2 = jnp.where(s0 + col_tile >= s, pos2, -1)
        p = jnp.where(pos2 == rows, 1.0, 0.0).astype(BF16)
        r0 = pl.multiple_of(s0 * SRC, SRC)
        acc_ref[...] += _dot(p, xn_ref[pl.ds(r0, 2 * SRC), :])
        return carry

    lax.fori_loop(0, (hi_ref[i] - lo + 2) // 2, body, 0)
    xs_ref[...] = acc_ref[...].astype(BF16)


def _moe_expert_kernel(grp_ref, nused_ref, xs_ref, wg_ref, wu_ref, wd_ref, y_ref, wg_b, wu_b, wd_b):
    i = pl.program_id(0)
    used = i < nused_ref[0]
    new_group = (i == 0) | (grp_ref[i] != grp_ref[jnp.maximum(i - 1, 0)])

    @pl.when(used & new_group)
    def _():
        for e in range(MOE_EPG):
            wg_b[e] = wg_ref[0, e].astype(BF16)
            wu_b[e] = wu_ref[0, e].astype(BF16)
            wd_b[e] = wd_ref[0, e].astype(BF16)

    @pl.when(used)
    def _():
        x = xs_ref[:, :D]
        rec = xs_ref[:, D:].astype(F32)
        lane = lax.broadcasted_iota(jnp.int32, rec.shape, 1)
        acc = None
        for e in range(MOE_EPG):
            hg = _dot(x, wg_b[e])
            hu = _dot(x, wu_b[e])
            mine = (lane % MOE_EPG == e) & (lane < 3 * MOE_EPG)
            cw = jnp.sum(jnp.where(mine, rec, 0.0), axis=-1, keepdims=True)
            y = _dot((_silu(hg) * hu * cw).astype(BF16), wd_b[e])
            acc = y if acc is None else acc + y
        y_ref[...] = acc.astype(BF16)

    @pl.when(i >= nused_ref[0])
    def _():
        y_ref[...] = jnp.zeros_like(y_ref)


def _moe_ungather_kernel(lo_ref, rows_ref, ys_ref, pos_ref, h1_ref, gt2, fg_ref, *out_refs, n_prompt_tiles):
    n = pl.program_id(0)
    pos = pos_ref[...]
    lane = lax.broadcasted_iota(jnp.int32, pos.shape, 1)
    acc = None
    for g in range(MOE_GROUPS):
        base = pl.multiple_of(lo_ref[n * MOE_GROUPS + g] * SRC, SRC)
        valid = rows_ref[n * MOE_GROUPS + g]
        parts = []
        for c0 in range(0, 2 * SRC, 128):
            first = jnp.where(c0 < valid, base + c0, -(1 << 20))
            parts.append(jnp.where(pos == first + lane, 1.0, 0.0))
        q = jnp.concatenate(parts, axis=1).astype(BF16)
        d = _dot(q, ys_ref[pl.ds(base, 2 * SRC), :])
        acc = d if acc is None else acc + d
    h2 = h1_ref[...] + gt2[0] * acc
    if n_prompt_tiles is None:
        out_refs[0][...] = h2
    else:
        y = _rms(h2, fg_ref[...])

        @pl.when(n < n_prompt_tiles)
        def _():
            out_refs[0][...] = y

        @pl.when(n >= n_prompt_tiles)
        def _():
            out_refs[1][...] = y


def _moe_sparse(h1, xn, route, lw, layer, final_g, final, dims):
    t = dims["t"]
    plan = _moe_plan(route, t)
    n_tiles = plan["n_tiles"]
    one = pl.Buffered(1)
    xs = pl.pallas_call(
        functools.partial(_moe_gather_kernel, n_src=t // SRC),
        grid_spec=pltpu.PrefetchScalarGridSpec(
            num_scalar_prefetch=2,
            grid=(n_tiles * TS // SRC,),
            in_specs=[pl.BlockSpec((t, XW), lambda i, lo, hi: (0, 0), pipeline_mode=one),
                      pl.BlockSpec((t // SRC, 1, SRC), lambda i, lo, hi: (0, 0, 0))],
            out_specs=pl.BlockSpec((SRC, XW), lambda i, lo, hi: (i, 0)),
            scratch_shapes=[pltpu.VMEM((SRC, XW), F32)]),
        out_shape=jax.ShapeDtypeStruct((n_tiles * TS, XW), BF16),
        compiler_params=_params(1),
        name="moe_gather_l%d" % layer,
    )(plan["g_lo"], plan["g_hi"], xn, plan["pos_row"])

    ex = lambda i, g, nu: (layer, g[i], 0, 0)
    ys = pl.pallas_call(
        _moe_expert_kernel,
        grid_spec=pltpu.PrefetchScalarGridSpec(
            num_scalar_prefetch=2,
            grid=(n_tiles + 1,),
            in_specs=[pl.BlockSpec((TS, XW), lambda i, g, nu: (jnp.minimum(i, n_tiles - 1), 0)),
                      pl.BlockSpec((1, MOE_EPG, D, MOE_DFF), ex, pipeline_mode=one),
                      pl.BlockSpec((1, MOE_EPG, D, MOE_DFF), ex, pipeline_mode=one),
                      pl.BlockSpec((1, MOE_EPG, MOE_DFF, D), ex, pipeline_mode=one)],
            out_specs=pl.BlockSpec((TS, D), lambda i, g, nu: (i, 0)),
            scratch_shapes=[pltpu.VMEM((MOE_EPG, D, MOE_DFF), BF16), pltpu.VMEM((MOE_EPG, D, MOE_DFF), BF16),
                            pltpu.VMEM((MOE_EPG, MOE_DFF, D), BF16)]),
        out_shape=jax.ShapeDtypeStruct(((n_tiles + 1) * TS, D), BF16),
        compiler_params=_params(1),
        name="moe_expert_l%d" % layer,
    )(plan["tile_grp"], plan["n_used"], xs, lw["w_gate"], lw["w_up"], lw["w_down"])
    n_rows = (n_tiles + 1) * TS

    row_fn = _tile_row_fn(dims, SRC)
    tile = lambda: pl.BlockSpec((SRC, D), lambda n, lo, hi: (n, 0))
    if final:
        npt = dims["t_p"] // SRC
        out_specs = [pl.BlockSpec((SRC, D), lambda n, lo, hi: (jnp.minimum(n, npt - 1), 0)),
                     pl.BlockSpec((SRC, D), lambda n, lo, hi: (jnp.maximum(n - npt, 0), 0))]
        out_shape = [jax.ShapeDtypeStruct((dims["t_p"], D), F32), jax.ShapeDtypeStruct((t - dims["t_p"], D), F32)]
    else:
        npt = None
        out_specs = [tile()]
        out_shape = [jax.ShapeDtypeStruct((t, D), F32)]
    return pl.pallas_call(
        functools.partial(_moe_ungather_kernel, n_prompt_tiles=npt),
        grid_spec=pltpu.PrefetchScalarGridSpec(
            num_scalar_prefetch=2,
            grid=(t // SRC,),
            in_specs=[pl.BlockSpec((n_rows, D), lambda n, lo, hi: (0, 0), pipeline_mode=one),
                      pl.BlockSpec((SRC, 128), lambda n, lo, hi: (n, 0)), tile(),
                      pl.BlockSpec((1, 1, D), lambda n, lo, hi: ((layer * MOD_ROWS + row_fn(n)) * N_MOD + 5, 0, 0)),
                      pl.BlockSpec((1, D), lambda n, lo, hi: (0, 0))],
            out_specs=out_specs),
        out_shape=out_shape,
        compiler_params=_params(1),
        name="moe_ungather_l%d" % layer,
    )(plan["u_lo"], plan["u_rows"], ys, plan["pos_col"], h1, lw["mod"], final_g)


def kernel(x_prompt, x_sample, cache_k, cache_v, state_ssm_fwd, state_ssm_bwd, c, c_ctx, ada_w, ada_b, norm1_g, norm2_g, final_g, fnet_w_o, na_w_qkv, na_w_o, na_rpb, gmlp_w_in, gmlp_g_v, gmlp_w_s, gmlp_b_s, gmlp_w_out, ssd_w_in, ssd_conv_w, ssd_conv_b, ssd_a_log, ssd_dt_bias, ssd_d_skip, ssd_g_norm, ssd_w_out, moe_w_gr, moe_b_gr, moe_w_er, moe_b_er, moe_w_gate, moe_w_up, moe_w_down):
    nb_p, seq_p, _ = x_prompt.shape
    nb_s, seq_s, _ = x_sample.shape
    depth = ada_w.shape[0]
    t_p, t_s = nb_p * seq_p, nb_s * seq_s
    dims = dict(t=t_p + t_s, t_p=t_p, seq_p=seq_p, seq_s=seq_s, nb_p=nb_p, nb_s=nb_s)
    assert 1 + nb_s <= MOD_ROWS and t_p % seq_s == 0 and t_p % TM == 0 and seq_s % TM == 0

    cond = jnp.zeros((MOD_ROWS, D), F32).at[0].set(c_ctx).at[1:1 + nb_s].set(c)
    mod = _ada_table(cond, ada_w, ada_b)
    h = (x_prompt.reshape(t_p, D), x_sample.reshape(t_s, D))
    fg = final_g.reshape(1, D)

    new_k, new_v, new_sf, new_sb = [], [], [], []
    for l in range(depth):
        kind, j = l % 4, l // 4
        w_r = jnp.concatenate([moe_w_gr[l], moe_w_er[l]], axis=1)
        w_r = jnp.pad(w_r, ((0, 0), (0, ROUTE_W - MOE_GROUPS - MOE_EXPERTS)))
        b_r = jnp.pad(jnp.concatenate([moe_b_gr[l], moe_b_er[l]]), (0, ROUTE_W - MOE_GROUPS - MOE_EXPERTS))
        wr_hi = w_r.astype(BF16)
        lw = dict(mod=mod, g1=norm1_g[l].reshape(1, D), g2=norm2_g[l].reshape(1, D),
                  wr_hi=wr_hi, wr_lo=(w_r - wr_hi.astype(F32)).astype(BF16), br=b_r.reshape(1, ROUTE_W),
                  w_gate=moe_w_gate, w_up=moe_w_up, w_down=moe_w_down)
        if kind == 0:
            lw.update(w_o=fnet_w_o[j].astype(BF16))
            h1, xn, route = _fnet_layer(h, lw, l, dims)
        elif kind == 1:
            lw.update(w_qkv=na_w_qkv[j].astype(BF16), w_o=na_w_o[j].astype(BF16), rpb=na_rpb[j])
            (h1, xn, route), kc, vc = _na_layer(h, cache_k, cache_v, j, lw, l, dims)
            new_k.append(kc)
            new_v.append(vc)
        elif kind == 2:
            gw = GMLP_DFF // GMLP_GROUPS
            lw.update(w_in=gmlp_w_in[j].astype(BF16), g_v=gmlp_g_v[j].reshape(1, GMLP_DFF),
                      w_s=gmlp_w_s[j].astype(BF16),
                      b_s=jnp.broadcast_to(gmlp_b_s[j][:, :, None], (GMLP_GROUPS, GMLP_CHUNK, gw)),
                      w_out=gmlp_w_out[j].astype(BF16))
            h1, xn, route = _gmlp_layer(h, lw, l, dims)
        else:
            n_main = 3 * SSD_INNER
            w_in = ssd_w_in[j]
            pad = lambda v: jnp.pad(v, ((0, 0), (0, 128 - 2 * SSD_HEADS)))
            lw.update(w_in=w_in[:, :n_main].astype(BF16), w_dt=pad(w_in[:, n_main:]).astype(BF16),
                      conv_w=ssd_conv_w[j], conv_b=ssd_conv_b[j].reshape(1, -1),
                      dt_bias=pad(ssd_dt_bias[j].reshape(1, -1)), a_log=pad(ssd_a_log[j].reshape(1, -1)),
                      d_skip=ssd_d_skip[j], g_norm=ssd_g_norm[j].reshape(1, SSD_INNER),
                      w_out=ssd_w_out[j].astype(BF16))
            (h1, xn, route), sf, sb = _ssd_layer(h, state_ssm_fwd, state_ssm_bwd, j, lw, l, dims)
            new_sf.append(sf)
            new_sb.append(sb)
        out = _moe_sparse(h1, xn, route, lw, l, fg, l == depth - 1, dims)
        h = out if l == depth - 1 else out[0]

    y_prompt = h[0].reshape(nb_p, seq_p, D)
    y_sample = h[1].reshape(nb_s, seq_s, D)
    cat = lambda xs: jnp.concatenate(xs, axis=1)
    return (y_prompt, y_sample, cat(new_k), cat(new_v), cat(new_sf), cat(new_sb))
```

```python
import functools
import math

import jax
import jax.numpy as jnp
from jax import lax
from jax.experimental import pallas as pl
from jax.experimental.pallas import tpu as pltpu

F32 = jnp.float32
BF16 = jnp.bfloat16

D = 1024
EPS = 1e-6
NEG = -1e30
N_MOD = 6
MOD_ROWS = 8
GRID_W = 64
FNET_GROUPS = 8
NA_HEADS = 16
NA_HD = 64
NA_WIN_ROWS = 8
NA_WIN_COLS = 16
GMLP_CHUNK = 128
GMLP_DFF = 2048
GMLP_GROUPS = 8
SSD_INNER = 2048
SSD_HD = 64
SSD_HEADS = 32
SSD_GROUPS = 8
SSD_STATE = 128
SSD_CHUNK = 128
SSD_HPG = SSD_HEADS // SSD_GROUPS
MOE_GROUPS = 4
MOE_EPG = 4
MOE_EXPERTS = 16
MOE_DFF = 512
ROUTE_W = 128
ROUTE_E0 = 4

TM = 512
VMEM_LIMIT = 56 * 1024 * 1024


def _dot(a, b):
    return jnp.dot(a, b, preferred_element_type=F32)


def _dot_nt(a, b):
    return lax.dot_general(a, b, (((1,), (1,)), ((), ())), preferred_element_type=F32)


def _silu(x):
    return x * (1.0 / (1.0 + jnp.exp(-x)))


def _rms(x, g):
    return x * lax.rsqrt(jnp.mean(x * x, axis=-1, keepdims=True) + EPS) * g


def _normmod(x, g, shift, scale):
    return _rms(x, g) * (1.0 + scale) + shift


def _split3(x):
    hi = x.astype(BF16)
    r = x - hi.astype(F32)
    mid = r.astype(BF16)
    lo = (r - mid.astype(F32)).astype(BF16)
    return hi, mid, lo


def _params(n_axes):
    return pltpu.CompilerParams(dimension_semantics=("arbitrary",) * n_axes,
                                vmem_limit_bytes=VMEM_LIMIT)


def _full(shape):
    nd = len(shape)
    return pl.BlockSpec(shape, lambda *_: (0,) * nd)


def _mod_spec(layer, k, row_fn):
    return pl.BlockSpec((1, 1, D), lambda *idx: ((layer * MOD_ROWS + row_fn(*idx)) * N_MOD + k, 0, 0))


def _ada_kernel(c_ref, w_ref, b_ref, o_ref):
    c = c_ref[...]
    o_ref[0] = _dot(_silu(c).astype(BF16), w_ref[0].astype(BF16)) + b_ref[0]


def _ada_table(cond, ada_w, ada_b):
    depth = ada_w.shape[0]
    n = N_MOD * D
    tn = 1536
    out = pl.pallas_call(
        _ada_kernel,
        grid=(depth, n // tn),
        in_specs=[_full((MOD_ROWS, D)),
                  pl.BlockSpec((1, D, tn), lambda l, j: (l, 0, j)),
                  pl.BlockSpec((1, 1, tn), lambda l, j: (l, 0, j))],
        out_specs=pl.BlockSpec((1, MOD_ROWS, tn), lambda l, j: (l, 0, j)),
        out_shape=jax.ShapeDtypeStruct((depth, MOD_ROWS, n), F32),
        compiler_params=_params(2),
        name="ada_table",
    )(cond, ada_w, ada_b.reshape(depth, 1, n))
    return out.reshape(depth * MOD_ROWS * N_MOD, 1, D)


def _route(h1, g2, sh2, sc2, wr_hi, wr_lo, br):
    xn = _normmod(h1, g2, sh2, sc2)
    xh = xn.astype(BF16)
    xl = (xn - xh.astype(F32)).astype(BF16)
    logits = _dot(xh, wr_hi) + _dot(xh, wr_lo) + _dot(xl, wr_hi) + br
    lane = lax.broadcasted_iota(jnp.int32, logits.shape, 1).astype(F32)
    far = float(ROUTE_W)
    gl = jnp.where(lane < MOE_GROUPS, logits, NEG)
    gmax = jnp.max(gl, axis=-1, keepdims=True)
    g_p = 1.0 / jnp.sum(jnp.exp(gl - gmax), axis=-1, keepdims=True)
    gidx = jnp.min(jnp.where(gl == gmax, lane, far), axis=-1, keepdims=True)
    lo = ROUTE_E0 + MOE_EPG * gidx
    el = jnp.where((lane >= lo) & (lane < lo + MOE_EPG), logits, NEG)
    m1 = jnp.max(el, axis=-1, keepdims=True)
    i1 = jnp.min(jnp.where(el == m1, lane, far), axis=-1, keepdims=True)
    el2 = jnp.where(lane == i1, NEG, el)
    m2 = jnp.max(el2, axis=-1, keepdims=True)
    i2 = jnp.min(jnp.where(el2 == m2, lane, far), axis=-1, keepdims=True)
    e2 = jnp.exp(m2 - m1)
    w1 = g_p / (1.0 + e2)
    w2 = w1 * e2
    rec = jnp.zeros_like(logits)
    for part, (a, b) in enumerate(zip(_split3(w1), _split3(w2))):
        shift = part * MOE_EPG - lo
        rec = (rec + jnp.where(lane == i1 + shift, a.astype(F32), 0.0)
               + jnp.where(lane == i2 + shift, b.astype(F32), 0.0))
    return xh, rec.astype(BF16), jnp.broadcast_to(gidx, logits.shape)


def _route_specs(layer, row_fn):
    return [_full((1, D)), _mod_spec(layer, 3, row_fn), _mod_spec(layer, 4, row_fn),
            _full((D, ROUTE_W)), _full((D, ROUTE_W)), _full((1, ROUTE_W))]


def _route_args(lw):
    return [lw["g2"], lw["mod"], lw["mod"], lw["wr_hi"], lw["wr_lo"], lw["br"]]


def _finish(h, o, gate, rt_refs, h1_ref, xn_ref, route_ref):
    g2, sh2, sc2, wr_hi, wr_lo, br = rt_refs
    h1 = h + gate * o
    h1_ref[...] = h1
    xn, rec, gid = _route(h1, g2[...], sh2[0], sc2[0], wr_hi[...], wr_lo[...], br[...])
    row = jnp.concatenate([xn.astype(F32), rec.astype(F32), jnp.zeros((h.shape[0], 2 * PW - XW), F32)], axis=1)
    xn_ref[...] = _pack_pairs(row[:, :PW], row[:, PW:])
    route_ref[...] = gid


XW = D + ROUTE_W
PW = 640
U32 = jnp.uint32


def _pack_pairs(lo, hi):
    lo_bits = lax.bitcast_convert_type(lo, U32) >> 16
    hi_bits = lax.bitcast_convert_type(hi, U32) & U32(0xFFFF0000)
    return lo_bits | hi_bits


def _unpack_pairs(w):
    return lax.bitcast_convert_type(w << 16, F32), lax.bitcast_convert_type(w & U32(0xFFFF0000), F32)


def _stream_outs(t):
    return [jax.ShapeDtypeStruct((t, D), F32), jax.ShapeDtypeStruct((t, PW), U32),
            jax.ShapeDtypeStruct((t, ROUTE_W), F32)]


def _stream_specs(rows, row_block):
    return [pl.BlockSpec((rows, w), lambda *idx: (row_block(*idx), 0)) for w in (D, PW, ROUTE_W)]


def _fnet_kernel(h_ref, g1, sh1, sc1, gt1, csc_ref, fs_ref, wo_ref, g2, sh2, sc2, wr_hi, wr_lo, br,
                 *rest, seq):
    h1_ref, xn_ref, route_ref, ab_ref = rest[-4:]
    h = h_ref[...]
    a = _normmod(h, g1[...], sh1[0], sc1[0]).astype(BF16)
    gd = D // FNET_GROUPS
    for g in range(FNET_GROUPS):
        ab = _dot(a[:, g * gd:(g + 1) * gd], csc_ref[...])
        ab_ref[0:seq, g * gd:(g + 1) * gd] = ab[:, :gd].astype(BF16)
        ab_ref[seq:2 * seq, g * gd:(g + 1) * gd] = ab[:, gd:].astype(BF16)
    f = _dot(fs_ref[...], ab_ref[...])
    o = _dot(f.astype(BF16), wo_ref[...])
    _finish(h, o, gt1[0], (g2, sh2, sc2, wr_hi, wr_lo, br), h1_ref, xn_ref, route_ref)


def _dft_tables(n):
    k = jnp.arange(n, dtype=jnp.int32)
    ang = ((k[:, None] * k[None, :]) % n).astype(F32) * (2.0 * math.pi / n)
    s = 1.0 / math.sqrt(n)
    return jnp.cos(ang) * s, jnp.sin(ang) * s


def _fnet_layer(h, lw, layer, dims):
    t, t_p = dims["t"], dims["t_p"]
    gd = D // FNET_GROUPS
    cc, sc = _dft_tables(gd)
    csc = jnp.concatenate([cc, sc], axis=1).astype(BF16)

    split_in = isinstance(h, tuple)

    def make_call(seq, nb, off, n_alias):
        cs, ss = _dft_tables(seq)
        fs = jnp.concatenate([cs, -ss], axis=1).astype(BF16)
        row_fn = (lambda b: 0) if off == 0 else (lambda b: 1 + b)
        in_off = 0 if split_in else off
        tile = lambda: pl.BlockSpec((seq, D), lambda b: (off + b, 0))
        in_specs = ([pl.BlockSpec((seq, D), lambda b: (in_off + b, 0)), _full((1, D))]
                    + [_mod_spec(layer, k, row_fn) for k in (0, 1, 2)]
                    + [_full((gd, 2 * gd)), _full((seq, 2 * seq)), _full((D, D))]
                    + _route_specs(layer, row_fn))
        aliases = {}
        if n_alias:
            base = len(in_specs)
            in_specs = in_specs + [pl.BlockSpec(memory_space=pl.ANY)] * n_alias
            aliases = {base + i: i for i in range(n_alias)}
        call = pl.pallas_call(
            functools.partial(_fnet_kernel, seq=seq),
            grid=(nb,),
            in_specs=in_specs,
            out_specs=_stream_specs(seq, lambda b: off + b),
            out_shape=_stream_outs(t),
            scratch_shapes=[pltpu.VMEM((2 * seq, D), BF16)],
            input_output_aliases=aliases,
            compiler_params=_params(1),
            name="fnet_seq%d" % seq,
        )
        return lambda *a: call(*a[:2], *a[2:5], a[5], fs, *a[6:])

    h_p, h_s = h if split_in else (h, h)
    args = [lw["g1"], lw["mod"], lw["mod"], lw["mod"], csc, lw["w_o"]] + _route_args(lw)
    outs = make_call(dims["seq_p"], dims["nb_p"], 0, None)(h_p, *args)
    return make_call(dims["seq_s"], dims["nb_s"], t_p // dims["seq_s"], 3)(h_s, *args, *outs)


def _pre_kernel(h_ref, g1, sh1, sc1, w_ref, o_ref, a_ref):
    @pl.when(pl.program_id(1) == 0)
    def _():
        a_ref[...] = _normmod(h_ref[...], g1[...], sh1[0], sc1[0]).astype(BF16)

    o_ref[...] = _dot(a_ref[...], w_ref[...]).astype(o_ref.dtype)


def _tile_row_fn(dims, tm=TM):
    npt = dims["t_p"] // tm
    tps = dims["seq_s"] // tm
    return lambda i, *_: jnp.where(i < npt, 0, 1 + (i - npt) // tps)


def _pre_proj(h, lw, layer, w, tn, out_dtype, dims):
    t = dims["t"]
    n = w.shape[1]
    tm = dims["seq_s"]
    row_fn = _tile_row_fn(dims, tm)
    return pl.pallas_call(
        _pre_kernel,
        grid=(t // tm, n // tn),
        in_specs=[pl.BlockSpec((tm, D), lambda i, j: (i, 0)), _full((1, D)),
                  _mod_spec(layer, 0, row_fn), _mod_spec(layer, 1, row_fn),
                  pl.BlockSpec((D, tn), lambda i, j: (0, j))],
        out_specs=pl.BlockSpec((tm, tn), lambda i, j: (i, j)),
        out_shape=jax.ShapeDtypeStruct((t, n), out_dtype),
        scratch_shapes=[pltpu.VMEM((tm, D), BF16)],
        compiler_params=_params(2),
        name="pre_proj_l%d_n%d" % (layer, n),
    )(h, lw["g1"], lw["mod"], lw["mod"], w)


def _out_kernel(o_ref, w_ref, h_ref, gt1, g2, sh2, sc2, wr_hi, wr_lo, br, h1_ref, xn_ref, route_ref):
    o = _dot(o_ref[...], w_ref[...])
    _finish(h_ref[...], o, gt1[0], (g2, sh2, sc2, wr_hi, wr_lo, br), h1_ref, xn_ref, route_ref)


def _out_proj(o, w, h, lw, layer, dims):
    t = dims["t"]
    k = o.shape[1]
    row_fn = _tile_row_fn(dims)
    tile = lambda: pl.BlockSpec((TM, D), lambda i: (i, 0))
    return pl.pallas_call(
        _out_kernel,
        grid=(t // TM,),
        in_specs=[pl.BlockSpec((TM, k), lambda i: (i, 0)), _full((k, D)), tile(),
                  _mod_spec(layer, 2, row_fn)] + _route_specs(layer, row_fn),
        out_specs=_stream_specs(TM, lambda i: i),
        out_shape=_stream_outs(t),
        compiler_params=_params(1),
        name="out_proj_l%d" % layer,
    )(o, w, h, lw["mod"], *_route_args(lw))


NA_SCALE = NA_HD ** -0.5


def _na_ctx_kernel(q_ref, k_ref, v_ref, o_ref, kc_ref, vc_ref):
    outs = []
    for hd in range(NA_HEADS):
        sl = slice(hd * NA_HD, (hd + 1) * NA_HD)
        q = q_ref[:, sl] * NA_SCALE
        k = k_ref[:, sl]
        v = v_ref[:, sl]
        s = _dot_nt(q, k)
        p = jnp.exp(s - jnp.max(s, axis=-1, keepdims=True))
        l = jnp.sum(p, axis=-1, keepdims=True)
        outs.append((_dot(p.astype(BF16), v) / l).astype(BF16))
        kc_ref[0, 0, hd] = k.astype(F32)
        vc_ref[0, 0, hd] = v.astype(F32)
    o_ref[...] = jnp.concatenate(outs, axis=1)


def _na_row_start(qr, rows):
    kr = min(NA_WIN_ROWS, rows)
    return min(max(qr - kr // 2, 0), rows - kr)


def _na_window_bias(bias_ref, hh, qr, rows, m_lo, m_hi):
    kr = min(NA_WIN_ROWS, rows)
    rs = _na_row_start(qr, rows)
    blocks = []
    for m in range(m_lo, m_hi):
        ok0 = rs <= 2 * m < rs + kr
        ok1 = rs <= 2 * m + 1 < rs + kr
        e = 2 * m - qr + NA_WIN_ROWS
        if ok0 and ok1:
            blocks.append(bias_ref[hh, 0, e])
        elif ok1:
            blocks.append(bias_ref[hh, 1, e])
        elif ok0:
            blocks.append(bias_ref[hh, 2, e])
        else:
            blocks.append(jnp.full((GRID_W, 2 * GRID_W), NEG, F32))
    return jnp.concatenate(blocks, axis=1)


def _na_lat_kernel(q_ref, k_ref, v_ref, bias_ref, kc_ref, vc_ref, o_in, o_ref, *, seq, qb):
    del o_in
    rows = seq // GRID_W
    for hh in range(2):
        sl = slice(hh * NA_HD, (hh + 1) * NA_HD)
        k = k_ref[:, sl]
        v = v_ref[:, sl]
        kc = kc_ref[0, 0, hh].astype(BF16)
        vc = vc_ref[0, 0, hh].astype(BF16)
        for b0 in range(0, seq, qb):
            q = q_ref[b0:b0 + qb, sl] * NA_SCALE
            qrs = range(b0 // GRID_W, (b0 + qb) // GRID_W)
            m_lo = _na_row_start(qrs[0], rows) // 2
            m_hi = (_na_row_start(qrs[-1], rows) + min(NA_WIN_ROWS, rows) + 1) // 2
            keys = slice(m_lo * 2 * GRID_W, m_hi * 2 * GRID_W)
            bias = jnp.concatenate([_na_window_bias(bias_ref, hh, qr, rows, m_lo, m_hi) for qr in qrs], axis=0)
            s1 = _dot_nt(q, k[keys]) + bias
            s2 = _dot_nt(q, kc)
            m = jnp.maximum(jnp.max(s1, axis=-1, keepdims=True), jnp.max(s2, axis=-1, keepdims=True))
            p1 = jnp.exp(s1 - m)
            p2 = jnp.exp(s2 - m)
            l = jnp.sum(p1, axis=-1, keepdims=True) + jnp.sum(p2, axis=-1, keepdims=True)
            o = (_dot(p1.astype(BF16), v[keys]) + _dot(p2.astype(BF16), vc)) / l
            o_ref[b0:b0 + qb, sl] = o.astype(BF16)


def _na_bias_tables(rpb):
    c = jnp.arange(GRID_W)
    win0 = jnp.clip(c - NA_WIN_COLS // 2, 0, GRID_W - NA_WIN_COLS)
    ok_c = (c[None, :] >= win0[:, None]) & (c[None, :] < win0[:, None] + NA_WIN_COLS)
    dc = jnp.clip(c[None, :] - c[:, None], 1 - NA_WIN_COLS, NA_WIN_COLS - 1) + NA_WIN_COLS - 1
    nh, ndr, ndc = rpb.shape
    pick = (dc.reshape(1, -1) == jnp.arange(ndc)[:, None]).astype(F32)
    cm = jnp.dot(rpb.reshape(nh * ndr, ndc), pick, precision=lax.Precision.HIGHEST)
    cm = jnp.where(ok_c[None, None], cm.reshape(nh, ndr, GRID_W, GRID_W), NEG)
    neg = jnp.full_like(cm[:, :1], NEG)
    ext = jnp.concatenate([neg, cm, neg], axis=1)
    a, b = ext[:, :-1], ext[:, 1:]
    negs = jnp.full_like(a, NEG)
    pair = lambda x, y: jnp.concatenate([x, y], axis=-1)
    return jnp.stack([pair(a, b), pair(negs, b), pair(a, negs)], axis=1)


def _na_layer(h, cache_k, cache_v, j, lw, layer, dims):
    t, t_p, seq_p, seq_s = dims["t"], dims["t_p"], dims["seq_p"], dims["seq_s"]
    nb_p, nb_s = dims["nb_p"], dims["nb_s"]
    qkv = _pre_proj(h, lw, layer, lw["w_qkv"], 1536, BF16, dims)
    cshape = (nb_p, 1, NA_HEADS, seq_p, NA_HD)
    cspec = lambda: pl.BlockSpec((1, 1, NA_HEADS, seq_p, NA_HD), lambda b: (b, 0, 0, 0, 0))
    o, kc, vc = pl.pallas_call(
        _na_ctx_kernel,
        grid=(nb_p,),
        in_specs=[pl.BlockSpec((seq_p, D), lambda b: (b, 0)), pl.BlockSpec((seq_p, D), lambda b: (b, 1)),
                  pl.BlockSpec((seq_p, D), lambda b: (b, 2))],
        out_specs=[pl.BlockSpec((seq_p, D), lambda b: (b, 0)), cspec(), cspec()],
        out_shape=[jax.ShapeDtypeStruct((t, D), BF16), jax.ShapeDtypeStruct(cshape, F32),
                   jax.ShapeDtypeStruct(cshape, F32)],
        compiler_params=_params(1),
        name="na_context",
    )(qkv, qkv, qkv)

    bias = _na_bias_tables(lw["rpb"])
    past = cache_k.shape[3]
    off = t_p // seq_s
    npair = NA_HEADS // 2
    pw = 2 * NA_HD
    pspec = lambda: pl.BlockSpec((1, 1, 2, past, NA_HD), lambda hp, b: (b, j, hp, 0, 0))
    o = pl.pallas_call(
        functools.partial(_na_lat_kernel, seq=seq_s, qb=256),
        grid=(npair, nb_s),
        in_specs=[pl.BlockSpec((seq_s, pw), lambda hp, b: (off + b, hp)),
                  pl.BlockSpec((seq_s, pw), lambda hp, b: (off + b, npair + hp)),
                  pl.BlockSpec((seq_s, pw), lambda hp, b: (off + b, 2 * npair + hp)),
                  pl.BlockSpec((2, 3, 2 * NA_WIN_ROWS, GRID_W, 2 * GRID_W), lambda hp, b: (hp, 0, 0, 0, 0)),
                  pspec(), pspec(), pl.BlockSpec(memory_space=pl.ANY)],
        out_specs=pl.BlockSpec((seq_s, pw), lambda hp, b: (off + b, hp)),
        out_shape=jax.ShapeDtypeStruct((t, D), BF16),
        input_output_aliases={6: 0},
        compiler_params=_params(2),
        name="na_latent",
    )(qkv, qkv, qkv, bias, cache_k, cache_v, o)
    outs = _out_proj(o, lw["w_o"], h, lw, layer, dims)
    return outs, kc, vc


def _gelu_tanh(x):
    return 0.5 * x * (1.0 + jnp.tanh(math.sqrt(2.0 / math.pi) * (x + 0.044715 * (x * x * x))))


def _gmlp_kernel(h_ref, g1, sh1, sc1, gt1, win_ref, gv_ref, ws_ref, bs_ref, wout_ref,
                 g2, sh2, sc2, wr_hi, wr_lo, br, h1_ref, xn_ref, route_ref, m_ref):
    h = h_ref[...]
    a = _normmod(h, g1[...], sh1[0], sc1[0]).astype(BF16)
    u = _gelu_tanh(_dot(a, win_ref[:, :GMLP_DFF]))
    v = _gelu_tanh(_dot(a, win_ref[:, GMLP_DFF:]))
    v = _rms(v, gv_ref[...]).astype(BF16)
    gw = GMLP_DFF // GMLP_GROUPS
    for c in range(TM // GMLP_CHUNK):
        rows = slice(c * GMLP_CHUNK, (c + 1) * GMLP_CHUNK)
        for g in range(GMLP_GROUPS):
            cols = slice(g * gw, (g + 1) * gw)
            vs = _dot(ws_ref[g], v[rows, cols]) + bs_ref[g]
            m_ref[rows, cols] = (u[rows, cols] * vs).astype(BF16)
    o = _dot(m_ref[...], wout_ref[...])
    _finish(h, o, gt1[0], (g2, sh2, sc2, wr_hi, wr_lo, br), h1_ref, xn_ref, route_ref)


def _gmlp_layer(h, lw, layer, dims):
    t = dims["t"]
    row_fn = _tile_row_fn(dims)
    gw = GMLP_DFF // GMLP_GROUPS
    tile = lambda: pl.BlockSpec((TM, D), lambda i: (i, 0))
    one = pl.Buffered(1)
    return pl.pallas_call(
        _gmlp_kernel,
        grid=(t // TM,),
        in_specs=[tile(), _full((1, D))] + [_mod_spec(layer, k, row_fn) for k in (0, 1, 2)]
                 + [pl.BlockSpec((D, 2 * GMLP_DFF), lambda i: (0, 0), pipeline_mode=one),
                    _full((1, GMLP_DFF)), _full((GMLP_GROUPS, GMLP_CHUNK, GMLP_CHUNK)),
                    _full((GMLP_GROUPS, GMLP_CHUNK, gw)),
                    pl.BlockSpec((GMLP_DFF, D), lambda i: (0, 0), pipeline_mode=one)]
                 + _route_specs(layer, row_fn),
        out_specs=_stream_specs(TM, lambda i: i),
        out_shape=_stream_outs(t),
        scratch_shapes=[pltpu.VMEM((TM, GMLP_DFF), BF16)],
        compiler_params=_params(1),
        name="gmlp",
    )(h, lw["g1"], lw["mod"], lw["mod"], lw["mod"], lw["w_in"], lw["g_v"], lw["w_s"], lw["b_s"],
      lw["w_out"], *_route_args(lw))


HALO = 16


def _ssd_conv_kernel(x_ref, bc_ref, cw_ref, cb_ref, *rest, seq):
    o_ref, cat = rest[-2:]
    L = SSD_CHUNK
    nc = seq // L
    c = pl.program_id(1)
    r0 = pl.multiple_of(c * L, L)
    rp = pl.multiple_of(jnp.maximum(r0 - HALO, 0), HALO)
    rn = pl.multiple_of(jnp.minimum(r0 + L, seq - HALO), HALO)
    has_prev = (c > 0).astype(F32)
    has_next = (c < nc - 1).astype(F32)
    for src, lo in ((x_ref, 0), (bc_ref, SSD_INNER)):
        cols = slice(lo, lo + SSD_INNER)
        cat[0:HALO, cols] = src[pl.ds(rp, HALO), :].astype(F32) * has_prev
        cat[HALO:HALO + L, cols] = src[pl.ds(r0, L), :].astype(F32)
        cat[HALO + L:2 * HALO + L, cols] = src[pl.ds(rn, HALO), :].astype(F32) * has_next
    conv = cb_ref[...] + sum(cat[HALO - 2 + k:HALO - 2 + k + L, :] * cw_ref[k:k + 1, :] for k in range(4))
    o_ref[...] = _silu(conv).astype(BF16)


def _ssd_conv(zxbc, lw, dims):
    t, t_p = dims["t"], dims["t_p"]
    L = SSD_CHUNK

    def make_call(seq, nb, off, aliased):
        nc = seq // L
        in_specs = [pl.BlockSpec((seq, SSD_INNER), lambda b, c: (off + b, 1)),
                    pl.BlockSpec((seq, SSD_INNER), lambda b, c: (off + b, 2)),
                    _full((4, 2 * SSD_INNER)), _full((1, 2 * SSD_INNER))]
        if aliased:
            in_specs.append(pl.BlockSpec(memory_space=pl.ANY))
        return pl.pallas_call(
            functools.partial(_ssd_conv_kernel, seq=seq),
            grid=(nb, nc),
            in_specs=in_specs,
            out_specs=pl.BlockSpec((L, 2 * SSD_INNER), lambda b, c: ((off + b) * nc + c, 0)),
            out_shape=jax.ShapeDtypeStruct((t, 2 * SSD_INNER), BF16),
            scratch_shapes=[pltpu.VMEM((L + 2 * HALO, 2 * SSD_INNER), F32)],
            input_output_aliases={4: 0} if aliased else {},
            compiler_params=_params(2),
            name="ssd_conv_seq%d" % seq,
        )

    args = [zxbc, zxbc, lw["conv_w"], lw["conv_b"]]
    xbc = make_call(dims["seq_p"], dims["nb_p"], 0, False)(*args)
    return make_call(dims["seq_s"], dims["nb_s"], t_p // dims["seq_s"], True)(*args, xbc)


def _ssd_scan_kernel(*refs, seq, rev, has_h0, want_state, add_skip):
    (xbc_ref, dt_ref, dtb_ref, alog_ref, dsk_ref, tri_ref, rep_ref) = refs[:7]
    pos = 7
    h0_ref = None
    if has_h0:
        h0_ref = refs[pos]
        pos += 1
    n_alias = len(refs) - pos - (2 if want_state else 1) - 1
    pos += n_alias
    y_ref = refs[pos]
    st_ref = refs[pos + 1] if want_state else None
    state = refs[-1]

    L = SSD_CHUNK
    nc = seq // L
    c = pl.program_id(1)

    @pl.when(c == 0)
    def _():
        if has_h0:
            for i in range(SSD_INNER // L):
                hpb = L // SSD_HD
                blk = h0_ref[0, 0, i * hpb:(i + 1) * hpb].reshape(L, SSD_STATE)
                state[:, i * L:(i + 1) * L] = blk.T
        else:
            state[...] = jnp.zeros_like(state)

    xc = xbc_ref[:, :SSD_INNER].astype(F32)
    bm = xbc_ref[:, SSD_INNER:SSD_INNER + SSD_GROUPS * SSD_STATE]
    cm = xbc_ref[:, SSD_INNER + SSD_GROUPS * SSD_STATE:]

    dtr = dt_ref[...] + dtb_ref[...]
    dt = jnp.maximum(dtr, 0.0) + jnp.log(1.0 + jnp.exp(-jnp.abs(dtr)))
    dta = dt * (-jnp.exp(alog_ref[...]))
    tri = tri_ref[...]
    p = sum(_dot(tri, part) for part in _split3(dta))
    pt = p.T
    edge = 0 if rev else L - 1
    p_edge = p[edge:edge + 1, :]
    rep = rep_ref[...]
    dt_x = _dot(dt.astype(BF16), rep)
    ep_x = _dot(jnp.exp(p).astype(BF16), rep)
    dte_x = _dot(jnp.exp(p_edge - p).astype(BF16), rep)
    cdec_x = _dot(jnp.broadcast_to(jnp.exp(p_edge), (8, p.shape[1])).astype(BF16), rep)[0:1, :]

    dtx = xc * dt_x
    dtxb = dtx.astype(BF16)
    xdte = (dtx * dte_x).astype(BF16)
    li = lax.broadcasted_iota(jnp.int32, (L, L), 0)
    si = lax.broadcasted_iota(jnp.int32, (L, L), 1)
    keep = (li <= si) if rev else (li >= si)
    lane0 = SSD_HEADS if rev else 0
    gw = SSD_HPG * SSD_HD
    ys, new_state = [], []
    for g in range(SSD_GROUPS):
        gcols = slice(g * gw, (g + 1) * gw)
        b_g = bm[:, g * SSD_STATE:(g + 1) * SSD_STATE]
        c_g = cm[:, g * SSD_STATE:(g + 1) * SSD_STATE]
        cb = _dot_nt(c_g, b_g)
        st_prev = state[:, gcols]
        y_g = _dot(c_g, st_prev.astype(BF16)) * ep_x[:, gcols]
        yd = []
        for hh in range(SSD_HPG):
            hl = lane0 + g * SSD_HPG + hh
            seg = p[:, hl:hl + 1] - pt[hl:hl + 1, :]
            mat = cb * jnp.exp(jnp.where(keep, seg, NEG))
            hc = slice((g * SSD_HPG + hh) * SSD_HD, (g * SSD_HPG + hh + 1) * SSD_HD)
            yd.append(_dot(mat.astype(BF16), dtxb[:, hc]))
        y_g = y_g + jnp.concatenate(yd, axis=1)
        if add_skip:
            y_g = y_g + dsk_ref[:, gcols] * xc[:, gcols]
        ys.append(y_g)
        new_state.append(st_prev * cdec_x[:, gcols] + _dot(b_g.astype(F32).T.astype(BF16), xdte[:, gcols]))
    y_ref[...] = jnp.concatenate(ys, axis=1)
    state[...] = jnp.concatenate(new_state, axis=1)

    if want_state:
        @pl.when(c == nc - 1)
        def _():
            for i in range(SSD_INNER // L):
                blk = state[:, i * L:(i + 1) * L].T
                st_ref[0, 0, i * (L // SSD_HD):(i + 1) * (L // SSD_HD)] = blk.reshape(L // SSD_HD, SSD_HD, SSD_STATE)


def _ssd_scan(xbc, dt_raw, lw, h0, j, rev, dims):
    t, t_p, seq_p, seq_s = dims["t"], dims["t_p"], dims["seq_p"], dims["seq_s"]
    nb_p, nb_s = dims["nb_p"], dims["nb_s"]
    L = SSD_CHUNK
    d = 1 if rev else 0
    li = jnp.arange(L)
    tri = ((li[:, None] <= li[None, :]) if rev else (li[:, None] >= li[None, :])).astype(BF16)
    lane = jnp.arange(128)
    col_head = jnp.arange(SSD_INNER) // SSD_HD
    rep = (lane[:, None] == (d * SSD_HEADS + col_head)[None, :]).astype(BF16)
    dsk = jnp.repeat(lw["d_skip"], SSD_HD)[None, :].astype(F32)
    st_shape = (nb_p, 1, SSD_HEADS, SSD_HD, SSD_STATE)

    def make_call(seq, nb, off, has_h0, want_state, n_alias):
        nc = seq // L
        chunk = (lambda b, c: (off * nc + b * nc + (nc - 1 - c), 0)) if rev else (lambda b, c: (off * nc + b * nc + c, 0))
        in_specs = [pl.BlockSpec((L, 2 * SSD_INNER), chunk), pl.BlockSpec((L, 128), chunk),
                    _full((1, 128)), _full((1, 128)),
                    _full((1, SSD_INNER)), _full((L, L)), _full((128, SSD_INNER))]
        if has_h0:
            in_specs.append(pl.BlockSpec((1, 1, SSD_HEADS, SSD_HD, SSD_STATE), lambda b, c: (b, j, 0, 0, 0)))
        aliases = {}
        if n_alias:
            aliases = {len(in_specs): 0}
            in_specs.append(pl.BlockSpec(memory_space=pl.ANY))
        out_specs = [pl.BlockSpec((L, SSD_INNER), chunk)]
        out_shape = [jax.ShapeDtypeStruct((t, SSD_INNER), F32)]
        if want_state:
            out_specs.append(pl.BlockSpec((1, 1, SSD_HEADS, SSD_HD, SSD_STATE), lambda b, c: (b, 0, 0, 0, 0)))
            out_shape.append(jax.ShapeDtypeStruct(st_shape, F32))
        return pl.pallas_call(
            functools.partial(_ssd_scan_kernel, seq=seq, rev=rev, has_h0=has_h0, want_state=want_state,
                              add_skip=not rev),
            grid=(nb, nc),
            in_specs=in_specs,
            out_specs=out_specs,
            out_shape=out_shape,
            scratch_shapes=[pltpu.VMEM((SSD_STATE, SSD_INNER), F32)],
            input_output_aliases=aliases,
            compiler_params=_params(2),
            name="ssd_scan_%s_seq%d" % ("bwd" if rev else "fwd", seq),
        )

    common = [xbc, dt_raw, lw["dt_bias"], lw["a_log"], dsk, tri, rep]
    y, st = make_call(seq_p, nb_p, 0, False, True, 0)(*common)
    (y,) = make_call(seq_s, nb_s, t_p // seq_s, True, False, 1)(*common, h0, y)
    return y, st


def _ssd_out_kernel(yf_ref, yb_ref, z_ref, gn_ref, w_ref, h_ref, gt1, g2, sh2, sc2, wr_hi, wr_lo, br,
                    h1_ref, xn_ref, route_ref):
    y = (yf_ref[...] + yb_ref[...]) * _silu(z_ref[...].astype(F32))
    o = _dot(_rms(y, gn_ref[...]).astype(BF16), w_ref[...])
    _finish(h_ref[...], o, gt1[0], (g2, sh2, sc2, wr_hi, wr_lo, br), h1_ref, xn_ref, route_ref)


def _ssd_layer(h, state_f, state_b, j, lw, layer, dims):
    t = dims["t"]
    zxbc = _pre_proj(h, lw, layer, lw["w_in"], 1536, BF16, dims)
    dt_raw = _pre_proj(h, lw, layer, lw["w_dt"], 128, F32, dims)
    xbc = _ssd_conv(zxbc, lw, dims)
    y_f, st_f = _ssd_scan(xbc, dt_raw, lw, state_f, j, False, dims)
    y_b, st_b = _ssd_scan(xbc, dt_raw, lw, state_b, j, True, dims)
    row_fn = _tile_row_fn(dims)
    tile = lambda: pl.BlockSpec((TM, D), lambda i: (i, 0))
    wide = lambda: pl.BlockSpec((TM, SSD_INNER), lambda i: (i, 0))
    outs = pl.pallas_call(
        _ssd_out_kernel,
        grid=(t // TM,),
        in_specs=[wide(), wide(), wide(), _full((1, SSD_INNER)), _full((SSD_INNER, D)), tile(),
                  _mod_spec(layer, 2, row_fn)] + _route_specs(layer, row_fn),
        out_specs=_stream_specs(TM, lambda i: i),
        out_shape=_stream_outs(t),
        compiler_params=_params(1),
        name="ssd_out",
    )(y_f, y_b, zxbc, lw["g_norm"], lw["w_out"], h, lw["mod"], *_route_args(lw))
    return outs, st_f, st_b


TS = 512
SRC = 256


def _moe_plan(route, t):
    i32 = jnp.int32
    gid = route[:, 0].astype(i32)
    oh = (gid[:, None] == jnp.arange(MOE_GROUPS, dtype=i32)[None, :]).astype(i32)
    csum = jnp.cumsum(oh, axis=0)
    cnt = csum[-1]
    padded = ((cnt + TS - 1) // TS) * TS
    gend = jnp.cumsum(padded)
    pos = jnp.sum(oh * (csum - 1 + (gend - padded)[None, :]), axis=1)
    n_tiles = t // TS + MOE_GROUPS
    tile0 = jnp.arange(n_tiles, dtype=i32) * TS
    tile_grp = jnp.sum((tile0[:, None] >= gend[None, :]).astype(i32), axis=1)
    last_grp = jnp.max(jnp.where(cnt > 0, jnp.arange(MOE_GROUPS, dtype=i32), 0))
    tile_grp = jnp.minimum(tile_grp, last_grp)
    n_used = (gend[-1] // TS).reshape(1)
    return dict(pos=pos, tile_grp=tile_grp, n_used=n_used, n_tiles=n_tiles)


ROW_UNROLL = 8


def _moe_scatter_kernel(pos_ref, x_ref, xs_ref):
    n = pl.program_id(0)

    @pl.when(n == 0)
    def _():
        xs_ref[...] = jnp.zeros_like(xs_ref)

    def body(jj, carry):
        for r in range(ROW_UNROLL):
            j = jj * ROW_UNROLL + r
            xs_ref[pl.ds(pos_ref[n * SRC + j], 1), :] = x_ref[pl.ds(j, 1), :]
        return carry

    lax.fori_loop(0, SRC // ROW_UNROLL, body, 0)


def _moe_expert_kernel(grp_ref, nused_ref, xs_ref, wg_ref, wu_ref, wd_ref, y_ref, wg_b, wu_b, wd_b):
    i = pl.program_id(0)
    used = i < nused_ref[0]
    new_group = (i == 0) | (grp_ref[i] != grp_ref[jnp.maximum(i - 1, 0)])

    @pl.when(used & new_group)
    def _():
        for e in range(MOE_EPG):
            wg_b[e] = wg_ref[0, e].astype(BF16)
            wu_b[e] = wu_ref[0, e].astype(BF16)
            wd_b[e] = wd_ref[0, e].astype(BF16)

    @pl.when(used)
    def _():
        lo, hi = _unpack_pairs(xs_ref[...])
        x = jnp.concatenate([lo, hi[:, :D - PW]], axis=1).astype(BF16)
        rec = hi[:, D - PW:XW - PW]
        lane = lax.broadcasted_iota(jnp.int32, rec.shape, 1)
        acc = None
        for e in range(MOE_EPG):
            hg = _dot(x, wg_b[e])
            hu = _dot(x, wu_b[e])
            mine = (lane % MOE_EPG == e) & (lane < 3 * MOE_EPG)
            cw = jnp.sum(jnp.where(mine, rec, 0.0), axis=-1, keepdims=True)
            y = _dot((_silu(hg) * hu * cw).astype(BF16), wd_b[e])
            acc = y if acc is None else acc + y
        acc = acc.astype(BF16).astype(F32)
        y_ref[...] = _pack_pairs(acc[:, :D // 2], acc[:, D // 2:])

    @pl.when(i >= nused_ref[0])
    def _():
        y_ref[...] = jnp.zeros_like(y_ref)


def _moe_ungather_kernel(pos_ref, ys_ref, h1_ref, gt2, fg_ref, *rest, n_prompt_tiles):
    rows_ref = rest[-1]
    out_refs = rest[:-1]
    n = pl.program_id(0)

    def body(jj, carry):
        for r in range(ROW_UNROLL):
            j = jj * ROW_UNROLL + r
            rows_ref[pl.ds(j, 1), :] = ys_ref[pl.ds(pos_ref[n * SRC + j], 1), :]
        return carry

    lax.fori_loop(0, SRC // ROW_UNROLL, body, 0)
    h2 = h1_ref[...] + gt2[0] * jnp.concatenate(_unpack_pairs(rows_ref[...]), axis=1)
    if n_prompt_tiles is None:
        out_refs[0][...] = h2
    else:
        y = _rms(h2, fg_ref[...])

        @pl.when(n < n_prompt_tiles)
        def _():
            out_refs[0][...] = y

        @pl.when(n >= n_prompt_tiles)
        def _():
            out_refs[1][...] = y


def _moe_sparse(h1, xn, route, lw, layer, final_g, final, dims):
    t = dims["t"]
    plan = _moe_plan(route, t)
    n_tiles = plan["n_tiles"]
    n_rows = n_tiles * TS
    one = pl.Buffered(1)
    xs = pl.pallas_call(
        _moe_scatter_kernel,
        grid_spec=pltpu.PrefetchScalarGridSpec(
            num_scalar_prefetch=1,
            grid=(t // SRC,),
            in_specs=[pl.BlockSpec((SRC, PW), lambda n, pos: (n, 0))],
            out_specs=pl.BlockSpec((n_rows, PW), lambda n, pos: (0, 0))),
        out_shape=jax.ShapeDtypeStruct((n_rows, PW), U32),
        compiler_params=_params(1),
        name="moe_scatter_l%d" % layer,
    )(plan["pos"], xn)

    ex = lambda i, g, nu: (layer, g[i], 0, 0)
    ys = pl.pallas_call(
        _moe_expert_kernel,
        grid_spec=pltpu.PrefetchScalarGridSpec(
            num_scalar_prefetch=2,
            grid=(n_tiles,),
            in_specs=[pl.BlockSpec((TS, PW), lambda i, g, nu: (i, 0)),
                      pl.BlockSpec((1, MOE_EPG, D, MOE_DFF), ex, pipeline_mode=one),
                      pl.BlockSpec((1, MOE_EPG, D, MOE_DFF), ex, pipeline_mode=one),
                      pl.BlockSpec((1, MOE_EPG, MOE_DFF, D), ex, pipeline_mode=one)],
            out_specs=pl.BlockSpec((TS, D // 2), lambda i, g, nu: (i, 0)),
            scratch_shapes=[pltpu.VMEM((MOE_EPG, D, MOE_DFF), BF16), pltpu.VMEM((MOE_EPG, D, MOE_DFF), BF16),
                            pltpu.VMEM((MOE_EPG, MOE_DFF, D), BF16)]),
        out_shape=jax.ShapeDtypeStruct((n_rows, D // 2), U32),
        compiler_params=_params(1),
        name="moe_expert_l%d" % layer,
    )(plan["tile_grp"], plan["n_used"], xs, lw["w_gate"], lw["w_up"], lw["w_down"])

    row_fn = _tile_row_fn(dims, SRC)
    tile = lambda: pl.BlockSpec((SRC, D), lambda n, pos: (n, 0))
    if final:
        npt = dims["t_p"] // SRC
        out_specs = [pl.BlockSpec((SRC, D), lambda n, pos: (jnp.minimum(n, npt - 1), 0)),
                     pl.BlockSpec((SRC, D), lambda n, pos: (jnp.maximum(n - npt, 0), 0))]
        out_shape = [jax.ShapeDtypeStruct((dims["t_p"], D), F32), jax.ShapeDtypeStruct((t - dims["t_p"], D), F32)]
    else:
        npt = None
        out_specs = [tile()]
        out_shape = [jax.ShapeDtypeStruct((t, D), F32)]
    return pl.pallas_call(
        functools.partial(_moe_ungather_kernel, n_prompt_tiles=npt),
        grid_spec=pltpu.PrefetchScalarGridSpec(
            num_scalar_prefetch=1,
            grid=(t // SRC,),
            in_specs=[pl.BlockSpec((n_rows, D // 2), lambda n, pos: (0, 0)), tile(),
                      pl.BlockSpec((1, 1, D), lambda n, pos: ((layer * MOD_ROWS + row_fn(n)) * N_MOD + 5, 0, 0)),
                      pl.BlockSpec((1, D), lambda n, pos: (0, 0))],
            out_specs=out_specs,
            scratch_shapes=[pltpu.VMEM((SRC, D // 2), U32)]),
        out_shape=out_shape,
        compiler_params=_params(1),
        name="moe_ungather_l%d" % layer,
    )(plan["pos"], ys, h1, lw["mod"], final_g)


def kernel(x_prompt, x_sample, cache_k, cache_v, state_ssm_fwd, state_ssm_bwd, c, c_ctx, ada_w, ada_b, norm1_g, norm2_g, final_g, fnet_w_o, na_w_qkv, na_w_o, na_rpb, gmlp_w_in, gmlp_g_v, gmlp_w_s, gmlp_b_s, gmlp_w_out, ssd_w_in, ssd_conv_w, ssd_conv_b, ssd_a_log, ssd_dt_bias, ssd_d_skip, ssd_g_norm, ssd_w_out, moe_w_gr, moe_b_gr, moe_w_er, moe_b_er, moe_w_gate, moe_w_up, moe_w_down):
    nb_p, seq_p, _ = x_prompt.shape
    nb_s, seq_s, _ = x_sample.shape
    depth = ada_w.shape[0]
    t_p, t_s = nb_p * seq_p, nb_s * seq_s
    dims = dict(t=t_p + t_s, t_p=t_p, seq_p=seq_p, seq_s=seq_s, nb_p=nb_p, nb_s=nb_s)
    assert 1 + nb_s <= MOD_ROWS and t_p % seq_s == 0 and t_p % TM == 0 and seq_s % TM == 0

    cond = jnp.zeros((MOD_ROWS, D), F32).at[0].set(c_ctx).at[1:1 + nb_s].set(c)
    mod = _ada_table(cond, ada_w, ada_b)
    h = (x_prompt.reshape(t_p, D), x_sample.reshape(t_s, D))
    fg = final_g.reshape(1, D)

    new_k, new_v, new_sf, new_sb = [], [], [], []
    for l in range(depth):
        kind, j = l % 4, l // 4
        w_r = jnp.concatenate([moe_w_gr[l], moe_w_er[l]], axis=1)
        w_r = jnp.pad(w_r, ((0, 0), (0, ROUTE_W - MOE_GROUPS - MOE_EXPERTS)))
        b_r = jnp.pad(jnp.concatenate([moe_b_gr[l], moe_b_er[l]]), (0, ROUTE_W - MOE_GROUPS - MOE_EXPERTS))
        wr_hi = w_r.astype(BF16)
        lw = dict(mod=mod, g1=norm1_g[l].reshape(1, D), g2=norm2_g[l].reshape(1, D),
                  wr_hi=wr_hi, wr_lo=(w_r - wr_hi.astype(F32)).astype(BF16), br=b_r.reshape(1, ROUTE_W),
                  w_gate=moe_w_gate, w_up=moe_w_up, w_down=moe_w_down)
        if kind == 0:
            lw.update(w_o=fnet_w_o[j].astype(BF16))
            h1, xn, route = _fnet_layer(h, lw, l, dims)
        elif kind == 1:
            lw.update(w_qkv=na_w_qkv[j].astype(BF16), w_o=na_w_o[j].astype(BF16), rpb=na_rpb[j])
            (h1, xn, route), kc, vc = _na_layer(h, cache_k, cache_v, j, lw, l, dims)
            new_k.append(kc)
            new_v.append(vc)
        elif kind == 2:
            gw = GMLP_DFF // GMLP_GROUPS
            lw.update(w_in=gmlp_w_in[j].astype(BF16), g_v=gmlp_g_v[j].reshape(1, GMLP_DFF),
                      w_s=gmlp_w_s[j].astype(BF16),
                      b_s=jnp.broadcast_to(gmlp_b_s[j][:, :, None], (GMLP_GROUPS, GMLP_CHUNK, gw)),
                      w_out=gmlp_w_out[j].astype(BF16))
            h1, xn, route = _gmlp_layer(h, lw, l, dims)
        else:
            n_main = 3 * SSD_INNER
            w_in = ssd_w_in[j]
            pad = lambda v: jnp.pad(v, ((0, 0), (0, 128 - 2 * SSD_HEADS)))
            lw.update(w_in=w_in[:, :n_main].astype(BF16), w_dt=pad(w_in[:, n_main:]).astype(BF16),
                      conv_w=ssd_conv_w[j], conv_b=ssd_conv_b[j].reshape(1, -1),
                      dt_bias=pad(ssd_dt_bias[j].reshape(1, -1)), a_log=pad(ssd_a_log[j].reshape(1, -1)),
                      d_skip=ssd_d_skip[j], g_norm=ssd_g_norm[j].reshape(1, SSD_INNER),
                      w_out=ssd_w_out[j].astype(BF16))
            (h1, xn, route), sf, sb = _ssd_layer(h, state_ssm_fwd, state_ssm_bwd, j, lw, l, dims)
            new_sf.append(sf)
            new_sb.append(sb)
        out = _moe_sparse(h1, xn, route, lw, l, fg, l == depth - 1, dims)
        h = out if l == depth - 1 else out[0]

    y_prompt = h[0].reshape(nb_p, seq_p, D)
    y_sample = h[1].reshape(nb_s, seq_s, D)
    cat = lambda xs: jnp.concatenate(xs, axis=1)
    return (y_prompt, y_sample, cat(new_k), cat(new_v), cat(new_sf), cat(new_sb))
```

```python
import functools
import math

import jax
import jax.numpy as jnp
from jax import lax
from jax.experimental import pallas as pl
from jax.experimental.pallas import tpu as pltpu

F32 = jnp.float32
BF16 = jnp.bfloat16

D = 1024
EPS = 1e-6
NEG = -1e30
N_MOD = 6
MOD_ROWS = 8
GRID_W = 64
FNET_GROUPS = 8
NA_HEADS = 16
NA_HD = 64
NA_WIN_ROWS = 8
NA_WIN_COLS = 16
GMLP_CHUNK = 128
GMLP_DFF = 2048
GMLP_GROUPS = 8
SSD_INNER = 2048
SSD_HD = 64
SSD_HEADS = 32
SSD_GROUPS = 8
SSD_STATE = 128
SSD_CHUNK = 128
SSD_HPG = SSD_HEADS // SSD_GROUPS
MOE_GROUPS = 4
MOE_EPG = 4
MOE_EXPERTS = 16
MOE_DFF = 512
ROUTE_W = 128
ROUTE_E0 = 4
PAIR_ORDER = ((0, 1), (0, 2), (0, 3), (1, 3), (1, 2), (2, 3))

TM = 512
VMEM_LIMIT = 56 * 1024 * 1024


def _dot(a, b):
    return jnp.dot(a, b, preferred_element_type=F32)


def _dot_nt(a, b):
    return lax.dot_general(a, b, (((1,), (1,)), ((), ())), preferred_element_type=F32)


def _silu(x):
    return x * (1.0 / (1.0 + jnp.exp(-x)))


def _rms(x, g):
    return x * lax.rsqrt(jnp.mean(x * x, axis=-1, keepdims=True) + EPS) * g


def _normmod(x, g, shift, scale):
    return _rms(x, g) * (1.0 + scale) + shift


def _split3(x):
    hi = x.astype(BF16)
    r = x - hi.astype(F32)
    mid = r.astype(BF16)
    lo = (r - mid.astype(F32)).astype(BF16)
    return hi, mid, lo


def _params(n_axes):
    return pltpu.CompilerParams(dimension_semantics=("arbitrary",) * n_axes,
                                vmem_limit_bytes=VMEM_LIMIT)


def _full(shape):
    nd = len(shape)
    return pl.BlockSpec(shape, lambda *_: (0,) * nd)


def _mod_spec(layer, k, row_fn):
    return pl.BlockSpec((1, 1, D), lambda *idx: ((layer * MOD_ROWS + row_fn(*idx)) * N_MOD + k, 0, 0))


def _ada_kernel(c_ref, w_ref, b_ref, o_ref):
    c = c_ref[...]
    o_ref[0] = _dot(_silu(c).astype(BF16), w_ref[0].astype(BF16)) + b_ref[0]


def _ada_table(cond, ada_w, ada_b):
    depth = ada_w.shape[0]
    n = N_MOD * D
    tn = 1536
    out = pl.pallas_call(
        _ada_kernel,
        grid=(depth, n // tn),
        in_specs=[_full((MOD_ROWS, D)),
                  pl.BlockSpec((1, D, tn), lambda l, j: (l, 0, j)),
                  pl.BlockSpec((1, 1, tn), lambda l, j: (l, 0, j))],
        out_specs=pl.BlockSpec((1, MOD_ROWS, tn), lambda l, j: (l, 0, j)),
        out_shape=jax.ShapeDtypeStruct((depth, MOD_ROWS, n), F32),
        compiler_params=_params(2),
        name="ada_table",
    )(cond, ada_w, ada_b.reshape(depth, 1, n))
    return out.reshape(depth * MOD_ROWS * N_MOD, 1, D)


def _route(h1, g2, sh2, sc2, wr_hi, wr_lo, br):
    xn = _normmod(h1, g2, sh2, sc2)
    xh = xn.astype(BF16)
    xl = (xn - xh.astype(F32)).astype(BF16)
    logits = _dot(xh, wr_hi) + _dot(xh, wr_lo) + _dot(xl, wr_hi) + br
    lane = lax.broadcasted_iota(jnp.int32, logits.shape, 1).astype(F32)
    far = float(ROUTE_W)
    gl = jnp.where(lane < MOE_GROUPS, logits, NEG)
    gmax = jnp.max(gl, axis=-1, keepdims=True)
    g_p = 1.0 / jnp.sum(jnp.exp(gl - gmax), axis=-1, keepdims=True)
    gidx = jnp.min(jnp.where(gl == gmax, lane, far), axis=-1, keepdims=True)
    lo = ROUTE_E0 + MOE_EPG * gidx
    el = jnp.where((lane >= lo) & (lane < lo + MOE_EPG), logits, NEG)
    m1 = jnp.max(el, axis=-1, keepdims=True)
    i1 = jnp.min(jnp.where(el == m1, lane, far), axis=-1, keepdims=True)
    el2 = jnp.where(lane == i1, NEG, el)
    m2 = jnp.max(el2, axis=-1, keepdims=True)
    i2 = jnp.min(jnp.where(el2 == m2, lane, far), axis=-1, keepdims=True)
    e2 = jnp.exp(m2 - m1)
    w1 = g_p / (1.0 + e2)
    w2 = w1 * e2
    rec = jnp.zeros_like(logits)
    for part, (a, b) in enumerate(zip(_split3(w1), _split3(w2))):
        shift = part * MOE_EPG - lo
        rec = (rec + jnp.where(lane == i1 + shift, a.astype(F32), 0.0)
               + jnp.where(lane == i2 + shift, b.astype(F32), 0.0))
    ea = jnp.minimum(i1, i2) - lo
    eb = jnp.maximum(i1, i2) - lo
    pair = sum(jnp.where((ea == a) & (eb == b), float(k), 0.0) for k, (a, b) in enumerate(PAIR_ORDER))
    key = gidx * float(len(PAIR_ORDER)) + pair
    return xh, rec.astype(BF16), jnp.broadcast_to(key, logits.shape)


def _route_specs(layer, row_fn):
    return [_full((1, D)), _mod_spec(layer, 3, row_fn), _mod_spec(layer, 4, row_fn),
            _full((D, ROUTE_W)), _full((D, ROUTE_W)), _full((1, ROUTE_W))]


def _route_args(lw):
    return [lw["g2"], lw["mod"], lw["mod"], lw["wr_hi"], lw["wr_lo"], lw["br"]]


def _finish(h, o, gate, rt_refs, h1_ref, xn_ref, route_ref):
    g2, sh2, sc2, wr_hi, wr_lo, br = rt_refs
    h1 = h + gate * o
    h1_ref[...] = h1
    xn, rec, gid = _route(h1, g2[...], sh2[0], sc2[0], wr_hi[...], wr_lo[...], br[...])
    row = jnp.concatenate([xn.astype(F32), rec.astype(F32), jnp.zeros((h.shape[0], 2 * PW - XW), F32)], axis=1)
    xn_ref[...] = _pack_pairs(row[:, :PW], row[:, PW:])
    route_ref[...] = gid


XW = D + ROUTE_W
PW = 640
U32 = jnp.uint32


def _pack_pairs(lo, hi):
    lo_bits = lax.bitcast_convert_type(lo, U32) >> 16
    hi_bits = lax.bitcast_convert_type(hi, U32) & U32(0xFFFF0000)
    return lo_bits | hi_bits


def _unpack_pairs(w):
    return lax.bitcast_convert_type(w << 16, F32), lax.bitcast_convert_type(w & U32(0xFFFF0000), F32)


def _stream_outs(t):
    return [jax.ShapeDtypeStruct((t, D), F32), jax.ShapeDtypeStruct((t, PW), U32),
            jax.ShapeDtypeStruct((t, ROUTE_W), F32)]


def _stream_specs(rows, row_block):
    return [pl.BlockSpec((rows, w), lambda *idx: (row_block(*idx), 0)) for w in (D, PW, ROUTE_W)]


def _fnet_kernel(h_ref, g1, sh1, sc1, gt1, csc_ref, fs_ref, wo_ref, g2, sh2, sc2, wr_hi, wr_lo, br,
                 *rest, seq):
    h1_ref, xn_ref, route_ref, ab_ref = rest[-4:]
    h = h_ref[...]
    a = _normmod(h, g1[...], sh1[0], sc1[0]).astype(BF16)
    gd = D // FNET_GROUPS
    for g in range(FNET_GROUPS):
        ab = _dot(a[:, g * gd:(g + 1) * gd], csc_ref[...])
        ab_ref[0:seq, g * gd:(g + 1) * gd] = ab[:, :gd].astype(BF16)
        ab_ref[seq:2 * seq, g * gd:(g + 1) * gd] = ab[:, gd:].astype(BF16)
    f = _dot(fs_ref[...], ab_ref[...])
    o = _dot(f.astype(BF16), wo_ref[...])
    _finish(h, o, gt1[0], (g2, sh2, sc2, wr_hi, wr_lo, br), h1_ref, xn_ref, route_ref)


def _dft_tables(n):
    k = jnp.arange(n, dtype=jnp.int32)
    ang = ((k[:, None] * k[None, :]) % n).astype(F32) * (2.0 * math.pi / n)
    s = 1.0 / math.sqrt(n)
    return jnp.cos(ang) * s, jnp.sin(ang) * s


def _fnet_layer(h, lw, layer, dims):
    t, t_p = dims["t"], dims["t_p"]
    gd = D // FNET_GROUPS
    cc, sc = _dft_tables(gd)
    csc = jnp.concatenate([cc, sc], axis=1).astype(BF16)

    split_in = isinstance(h, tuple)

    def make_call(seq, nb, off, n_alias):
        cs, ss = _dft_tables(seq)
        fs = jnp.concatenate([cs, -ss], axis=1).astype(BF16)
        row_fn = (lambda b: 0) if off == 0 else (lambda b: 1 + b)
        in_off = 0 if split_in else off
        tile = lambda: pl.BlockSpec((seq, D), lambda b: (off + b, 0))
        in_specs = ([pl.BlockSpec((seq, D), lambda b: (in_off + b, 0)), _full((1, D))]
                    + [_mod_spec(layer, k, row_fn) for k in (0, 1, 2)]
                    + [_full((gd, 2 * gd)), _full((seq, 2 * seq)), _full((D, D))]
                    + _route_specs(layer, row_fn))
        aliases = {}
        if n_alias:
            base = len(in_specs)
            in_specs = in_specs + [pl.BlockSpec(memory_space=pl.ANY)] * n_alias
            aliases = {base + i: i for i in range(n_alias)}
        call = pl.pallas_call(
            functools.partial(_fnet_kernel, seq=seq),
            grid=(nb,),
            in_specs=in_specs,
            out_specs=_stream_specs(seq, lambda b: off + b),
            out_shape=_stream_outs(t),
            scratch_shapes=[pltpu.VMEM((2 * seq, D), BF16)],
            input_output_aliases=aliases,
            compiler_params=_params(1),
            name="fnet_seq%d" % seq,
        )
        return lambda *a: call(*a[:2], *a[2:5], a[5], fs, *a[6:])

    h_p, h_s = h if split_in else (h, h)
    args = [lw["g1"], lw["mod"], lw["mod"], lw["mod"], csc, lw["w_o"]] + _route_args(lw)
    outs = make_call(dims["seq_p"], dims["nb_p"], 0, None)(h_p, *args)
    return make_call(dims["seq_s"], dims["nb_s"], t_p // dims["seq_s"], 3)(h_s, *args, *outs)


def _pre_kernel(h_ref, g1, sh1, sc1, w_ref, o_ref, a_ref):
    @pl.when(pl.program_id(1) == 0)
    def _():
        a_ref[...] = _normmod(h_ref[...], g1[...], sh1[0], sc1[0]).astype(BF16)

    o_ref[...] = _dot(a_ref[...], w_ref[...]).astype(o_ref.dtype)


def _tile_row_fn(dims, tm=TM):
    npt = dims["t_p"] // tm
    tps = dims["seq_s"] // tm
    return lambda i, *_: jnp.where(i < npt, 0, 1 + (i - npt) // tps)


def _pre_proj(h, lw, layer, w, tn, out_dtype, dims):
    t = dims["t"]
    n = w.shape[1]
    tm = dims["seq_s"]
    row_fn = _tile_row_fn(dims, tm)
    return pl.pallas_call(
        _pre_kernel,
        grid=(t // tm, n // tn),
        in_specs=[pl.BlockSpec((tm, D), lambda i, j: (i, 0)), _full((1, D)),
                  _mod_spec(layer, 0, row_fn), _mod_spec(layer, 1, row_fn),
                  pl.BlockSpec((D, tn), lambda i, j: (0, j))],
        out_specs=pl.BlockSpec((tm, tn), lambda i, j: (i, j)),
        out_shape=jax.ShapeDtypeStruct((t, n), out_dtype),
        scratch_shapes=[pltpu.VMEM((tm, D), BF16)],
        compiler_params=_params(2),
        name="pre_proj_l%d_n%d" % (layer, n),
    )(h, lw["g1"], lw["mod"], lw["mod"], w)


def _out_kernel(o_ref, w_ref, h_ref, gt1, g2, sh2, sc2, wr_hi, wr_lo, br, h1_ref, xn_ref, route_ref):
    o = _dot(o_ref[...], w_ref[...])
    _finish(h_ref[...], o, gt1[0], (g2, sh2, sc2, wr_hi, wr_lo, br), h1_ref, xn_ref, route_ref)


def _out_proj(o, w, h, lw, layer, dims):
    t = dims["t"]
    k = o.shape[1]
    row_fn = _tile_row_fn(dims)
    tile = lambda: pl.BlockSpec((TM, D), lambda i: (i, 0))
    return pl.pallas_call(
        _out_kernel,
        grid=(t // TM,),
        in_specs=[pl.BlockSpec((TM, k), lambda i: (i, 0)), _full((k, D)), tile(),
                  _mod_spec(layer, 2, row_fn)] + _route_specs(layer, row_fn),
        out_specs=_stream_specs(TM, lambda i: i),
        out_shape=_stream_outs(t),
        compiler_params=_params(1),
        name="out_proj_l%d" % layer,
    )(o, w, h, lw["mod"], *_route_args(lw))


NA_SCALE = NA_HD ** -0.5


def _na_ctx_kernel(q_ref, k_ref, v_ref, o_ref, kc_ref, vc_ref):
    outs = []
    for hd in range(NA_HEADS):
        sl = slice(hd * NA_HD, (hd + 1) * NA_HD)
        q = q_ref[:, sl] * NA_SCALE
        k = k_ref[:, sl]
        v = v_ref[:, sl]
        s = _dot_nt(q, k)
        p = jnp.exp(s - jnp.max(s, axis=-1, keepdims=True))
        l = jnp.sum(p, axis=-1, keepdims=True)
        outs.append((_dot(p.astype(BF16), v) / l).astype(BF16))
        kc_ref[0, 0, hd] = k.astype(F32)
        vc_ref[0, 0, hd] = v.astype(F32)
    o_ref[...] = jnp.concatenate(outs, axis=1)


def _na_row_start(qr, rows):
    kr = min(NA_WIN_ROWS, rows)
    return min(max(qr - kr // 2, 0), rows - kr)


def _na_window_bias(bias_ref, hh, qr, rows, m_lo, m_hi):
    kr = min(NA_WIN_ROWS, rows)
    rs = _na_row_start(qr, rows)
    blocks = []
    for m in range(m_lo, m_hi):
        ok0 = rs <= 2 * m < rs + kr
        ok1 = rs <= 2 * m + 1 < rs + kr
        e = 2 * m - qr + NA_WIN_ROWS
        if ok0 and ok1:
            blocks.append(bias_ref[hh, 0, e])
        elif ok1:
            blocks.append(bias_ref[hh, 1, e])
        elif ok0:
            blocks.append(bias_ref[hh, 2, e])
        else:
            blocks.append(jnp.full((GRID_W, 2 * GRID_W), NEG, F32))
    return jnp.concatenate(blocks, axis=1)


def _na_lat_kernel(q_ref, k_ref, v_ref, bias_ref, kc_ref, vc_ref, o_in, o_ref, *, seq, qb):
    del o_in
    rows = seq // GRID_W
    for hh in range(2):
        sl = slice(hh * NA_HD, (hh + 1) * NA_HD)
        k = k_ref[:, sl]
        v = v_ref[:, sl]
        kc = kc_ref[0, 0, hh].astype(BF16)
        vc = vc_ref[0, 0, hh].astype(BF16)
        for b0 in range(0, seq, qb):
            q = q_ref[b0:b0 + qb, sl] * NA_SCALE
            qrs = range(b0 // GRID_W, (b0 + qb) // GRID_W)
            m_lo = _na_row_start(qrs[0], rows) // 2
            m_hi = (_na_row_start(qrs[-1], rows) + min(NA_WIN_ROWS, rows) + 1) // 2
            keys = slice(m_lo * 2 * GRID_W, m_hi * 2 * GRID_W)
            bias = jnp.concatenate([_na_window_bias(bias_ref, hh, qr, rows, m_lo, m_hi) for qr in qrs], axis=0)
            s1 = _dot_nt(q, k[keys]) + bias
            s2 = _dot_nt(q, kc)
            m = jnp.maximum(jnp.max(s1, axis=-1, keepdims=True), jnp.max(s2, axis=-1, keepdims=True))
            p1 = jnp.exp(s1 - m)
            p2 = jnp.exp(s2 - m)
            l = jnp.sum(p1, axis=-1, keepdims=True) + jnp.sum(p2, axis=-1, keepdims=True)
            o = (_dot(p1.astype(BF16), v[keys]) + _dot(p2.astype(BF16), vc)) / l
            o_ref[b0:b0 + qb, sl] = o.astype(BF16)


def _na_bias_tables(rpb):
    c = jnp.arange(GRID_W)
    win0 = jnp.clip(c - NA_WIN_COLS // 2, 0, GRID_W - NA_WIN_COLS)
    ok_c = (c[None, :] >= win0[:, None]) & (c[None, :] < win0[:, None] + NA_WIN_COLS)
    dc = jnp.clip(c[None, :] - c[:, None], 1 - NA_WIN_COLS, NA_WIN_COLS - 1) + NA_WIN_COLS - 1
    nh, ndr, ndc = rpb.shape
    pick = (dc.reshape(1, -1) == jnp.arange(ndc)[:, None]).astype(F32)
    cm = jnp.dot(rpb.reshape(nh * ndr, ndc), pick, precision=lax.Precision.HIGHEST)
    cm = jnp.where(ok_c[None, None], cm.reshape(nh, ndr, GRID_W, GRID_W), NEG)
    neg = jnp.full_like(cm[:, :1], NEG)
    ext = jnp.concatenate([neg, cm, neg], axis=1)
    a, b = ext[:, :-1], ext[:, 1:]
    negs = jnp.full_like(a, NEG)
    pair = lambda x, y: jnp.concatenate([x, y], axis=-1)
    return jnp.stack([pair(a, b), pair(negs, b), pair(a, negs)], axis=1)


def _na_layer(h, cache_k, cache_v, j, lw, layer, dims):
    t, t_p, seq_p, seq_s = dims["t"], dims["t_p"], dims["seq_p"], dims["seq_s"]
    nb_p, nb_s = dims["nb_p"], dims["nb_s"]
    qkv = _pre_proj(h, lw, layer, lw["w_qkv"], 1536, BF16, dims)
    cshape = (nb_p, 1, NA_HEADS, seq_p, NA_HD)
    cspec = lambda: pl.BlockSpec((1, 1, NA_HEADS, seq_p, NA_HD), lambda b: (b, 0, 0, 0, 0))
    o, kc, vc = pl.pallas_call(
        _na_ctx_kernel,
        grid=(nb_p,),
        in_specs=[pl.BlockSpec((seq_p, D), lambda b: (b, 0)), pl.BlockSpec((seq_p, D), lambda b: (b, 1)),
                  pl.BlockSpec((seq_p, D), lambda b: (b, 2))],
        out_specs=[pl.BlockSpec((seq_p, D), lambda b: (b, 0)), cspec(), cspec()],
        out_shape=[jax.ShapeDtypeStruct((t, D), BF16), jax.ShapeDtypeStruct(cshape, F32),
                   jax.ShapeDtypeStruct(cshape, F32)],
        compiler_params=_params(1),
        name="na_context",
    )(qkv, qkv, qkv)

    bias = _na_bias_tables(lw["rpb"])
    past = cache_k.shape[3]
    off = t_p // seq_s
    npair = NA_HEADS // 2
    pw = 2 * NA_HD
    pspec = lambda: pl.BlockSpec((1, 1, 2, past, NA_HD), lambda hp, b: (b, j, hp, 0, 0))
    o = pl.pallas_call(
        functools.partial(_na_lat_kernel, seq=seq_s, qb=256),
        grid=(npair, nb_s),
        in_specs=[pl.BlockSpec((seq_s, pw), lambda hp, b: (off + b, hp)),
                  pl.BlockSpec((seq_s, pw), lambda hp, b: (off + b, npair + hp)),
                  pl.BlockSpec((seq_s, pw), lambda hp, b: (off + b, 2 * npair + hp)),
                  pl.BlockSpec((2, 3, 2 * NA_WIN_ROWS, GRID_W, 2 * GRID_W), lambda hp, b: (hp, 0, 0, 0, 0)),
                  pspec(), pspec(), pl.BlockSpec(memory_space=pl.ANY)],
        out_specs=pl.BlockSpec((seq_s, pw), lambda hp, b: (off + b, hp)),
        out_shape=jax.ShapeDtypeStruct((t, D), BF16),
        input_output_aliases={6: 0},
        compiler_params=_params(2),
        name="na_latent",
    )(qkv, qkv, qkv, bias, cache_k, cache_v, o)
    outs = _out_proj(o, lw["w_o"], h, lw, layer, dims)
    return outs, kc, vc


def _gelu_tanh(x):
    return 0.5 * x * (1.0 + jnp.tanh(math.sqrt(2.0 / math.pi) * (x + 0.044715 * (x * x * x))))


def _gmlp_kernel(h_ref, g1, sh1, sc1, gt1, win_ref, gv_ref, ws_ref, bs_ref, wout_ref,
                 g2, sh2, sc2, wr_hi, wr_lo, br, h1_ref, xn_ref, route_ref, m_ref):
    h = h_ref[...]
    a = _normmod(h, g1[...], sh1[0], sc1[0]).astype(BF16)
    u = _gelu_tanh(_dot(a, win_ref[:, :GMLP_DFF]))
    v = _gelu_tanh(_dot(a, win_ref[:, GMLP_DFF:]))
    v = _rms(v, gv_ref[...]).astype(BF16)
    gw = GMLP_DFF // GMLP_GROUPS
    for c in range(TM // GMLP_CHUNK):
        rows = slice(c * GMLP_CHUNK, (c + 1) * GMLP_CHUNK)
        for g in range(GMLP_GROUPS):
            cols = slice(g * gw, (g + 1) * gw)
            vs = _dot(ws_ref[g], v[rows, cols]) + bs_ref[g]
            m_ref[rows, cols] = (u[rows, cols] * vs).astype(BF16)
    o = _dot(m_ref[...], wout_ref[...])
    _finish(h, o, gt1[0], (g2, sh2, sc2, wr_hi, wr_lo, br), h1_ref, xn_ref, route_ref)


def _gmlp_layer(h, lw, layer, dims):
    t = dims["t"]
    row_fn = _tile_row_fn(dims)
    gw = GMLP_DFF // GMLP_GROUPS
    tile = lambda: pl.BlockSpec((TM, D), lambda i: (i, 0))
    one = pl.Buffered(1)
    return pl.pallas_call(
        _gmlp_kernel,
        grid=(t // TM,),
        in_specs=[tile(), _full((1, D))] + [_mod_spec(layer, k, row_fn) for k in (0, 1, 2)]
                 + [pl.BlockSpec((D, 2 * GMLP_DFF), lambda i: (0, 0), pipeline_mode=one),
                    _full((1, GMLP_DFF)), _full((GMLP_GROUPS, GMLP_CHUNK, GMLP_CHUNK)),
                    _full((GMLP_GROUPS, GMLP_CHUNK, gw)),
                    pl.BlockSpec((GMLP_DFF, D), lambda i: (0, 0), pipeline_mode=one)]
                 + _route_specs(layer, row_fn),
        out_specs=_stream_specs(TM, lambda i: i),
        out_shape=_stream_outs(t),
        scratch_shapes=[pltpu.VMEM((TM, GMLP_DFF), BF16)],
        compiler_params=_params(1),
        name="gmlp",
    )(h, lw["g1"], lw["mod"], lw["mod"], lw["mod"], lw["w_in"], lw["g_v"], lw["w_s"], lw["b_s"],
      lw["w_out"], *_route_args(lw))


HALO = 16


def _ssd_conv_kernel(x_ref, bc_ref, cw_ref, cb_ref, *rest, seq):
    o_ref, cat = rest[-2:]
    L = SSD_CHUNK
    nc = seq // L
    c = pl.program_id(1)
    r0 = pl.multiple_of(c * L, L)
    rp = pl.multiple_of(jnp.maximum(r0 - HALO, 0), HALO)
    rn = pl.multiple_of(jnp.minimum(r0 + L, seq - HALO), HALO)
    has_prev = (c > 0).astype(F32)
    has_next = (c < nc - 1).astype(F32)
    for src, lo in ((x_ref, 0), (bc_ref, SSD_INNER)):
        cols = slice(lo, lo + SSD_INNER)
        cat[0:HALO, cols] = src[pl.ds(rp, HALO), :].astype(F32) * has_prev
        cat[HALO:HALO + L, cols] = src[pl.ds(r0, L), :].astype(F32)
        cat[HALO + L:2 * HALO + L, cols] = src[pl.ds(rn, HALO), :].astype(F32) * has_next
    conv = cb_ref[...] + sum(cat[HALO - 2 + k:HALO - 2 + k + L, :] * cw_ref[k:k + 1, :] for k in range(4))
    o_ref[...] = _silu(conv).astype(BF16)


def _ssd_conv(zxbc, lw, dims):
    t, t_p = dims["t"], dims["t_p"]
    L = SSD_CHUNK

    def make_call(seq, nb, off, aliased):
        nc = seq // L
        in_specs = [pl.BlockSpec((seq, SSD_INNER), lambda b, c: (off + b, 1)),
                    pl.BlockSpec((seq, SSD_INNER), lambda b, c: (off + b, 2)),
                    _full((4, 2 * SSD_INNER)), _full((1, 2 * SSD_INNER))]
        if aliased:
            in_specs.append(pl.BlockSpec(memory_space=pl.ANY))
        return pl.pallas_call(
            functools.partial(_ssd_conv_kernel, seq=seq),
            grid=(nb, nc),
            in_specs=in_specs,
            out_specs=pl.BlockSpec((L, 2 * SSD_INNER), lambda b, c: ((off + b) * nc + c, 0)),
            out_shape=jax.ShapeDtypeStruct((t, 2 * SSD_INNER), BF16),
            scratch_shapes=[pltpu.VMEM((L + 2 * HALO, 2 * SSD_INNER), F32)],
            input_output_aliases={4: 0} if aliased else {},
            compiler_params=_params(2),
            name="ssd_conv_seq%d" % seq,
        )

    args = [zxbc, zxbc, lw["conv_w"], lw["conv_b"]]
    xbc = make_call(dims["seq_p"], dims["nb_p"], 0, False)(*args)
    return make_call(dims["seq_s"], dims["nb_s"], t_p // dims["seq_s"], True)(*args, xbc)


def _ssd_scan_kernel(*refs, seq, rev, has_h0, want_state, add_skip):
    (xbc_ref, dt_ref, dtb_ref, alog_ref, dsk_ref, tri_ref, rep_ref) = refs[:7]
    pos = 7
    h0_ref = None
    if has_h0:
        h0_ref = refs[pos]
        pos += 1
    n_alias = len(refs) - pos - (2 if want_state else 1) - 1
    pos += n_alias
    y_ref = refs[pos]
    st_ref = refs[pos + 1] if want_state else None
    state = refs[-1]

    L = SSD_CHUNK
    nc = seq // L
    c = pl.program_id(1)

    @pl.when(c == 0)
    def _():
        if has_h0:
            for i in range(SSD_INNER // L):
                hpb = L // SSD_HD
                blk = h0_ref[0, 0, i * hpb:(i + 1) * hpb].reshape(L, SSD_STATE)
                state[:, i * L:(i + 1) * L] = blk.T
        else:
            state[...] = jnp.zeros_like(state)

    xc = xbc_ref[:, :SSD_INNER].astype(F32)
    bm = xbc_ref[:, SSD_INNER:SSD_INNER + SSD_GROUPS * SSD_STATE]
    cm = xbc_ref[:, SSD_INNER + SSD_GROUPS * SSD_STATE:]

    dtr = dt_ref[...] + dtb_ref[...]
    dt = jnp.maximum(dtr, 0.0) + jnp.log(1.0 + jnp.exp(-jnp.abs(dtr)))
    dta = dt * (-jnp.exp(alog_ref[...]))
    tri = tri_ref[...]
    p = sum(_dot(tri, part) for part in _split3(dta))
    pt = p.T
    edge = 0 if rev else L - 1
    p_edge = p[edge:edge + 1, :]
    rep = rep_ref[...]
    dt_x = _dot(dt.astype(BF16), rep)
    ep_x = _dot(jnp.exp(p).astype(BF16), rep)
    dte_x = _dot(jnp.exp(p_edge - p).astype(BF16), rep)
    cdec_x = _dot(jnp.broadcast_to(jnp.exp(p_edge), (8, p.shape[1])).astype(BF16), rep)[0:1, :]

    dtx = xc * dt_x
    dtxb = dtx.astype(BF16)
    xdte = (dtx * dte_x).astype(BF16)
    li = lax.broadcasted_iota(jnp.int32, (L, L), 0)
    si = lax.broadcasted_iota(jnp.int32, (L, L), 1)
    keep = (li <= si) if rev else (li >= si)
    lane0 = SSD_HEADS if rev else 0
    gw = SSD_HPG * SSD_HD
    ys, new_state = [], []
    for g in range(SSD_GROUPS):
        gcols = slice(g * gw, (g + 1) * gw)
        b_g = bm[:, g * SSD_STATE:(g + 1) * SSD_STATE]
        c_g = cm[:, g * SSD_STATE:(g + 1) * SSD_STATE]
        cb = _dot_nt(c_g, b_g)
        st_prev = state[:, gcols]
        y_g = _dot(c_g, st_prev.astype(BF16)) * ep_x[:, gcols]
        yd = []
        for hh in range(SSD_HPG):
            hl = lane0 + g * SSD_HPG + hh
            seg = p[:, hl:hl + 1] - pt[hl:hl + 1, :]
            mat = cb * jnp.exp(jnp.where(keep, seg, NEG))
            hc = slice((g * SSD_HPG + hh) * SSD_HD, (g * SSD_HPG + hh + 1) * SSD_HD)
            yd.append(_dot(mat.astype(BF16), dtxb[:, hc]))
        y_g = y_g + jnp.concatenate(yd, axis=1)
        if add_skip:
            y_g = y_g + dsk_ref[:, gcols] * xc[:, gcols]
        ys.append(y_g)
        new_state.append(st_prev * cdec_x[:, gcols] + _dot(b_g.astype(F32).T.astype(BF16), xdte[:, gcols]))
    y_ref[...] = jnp.concatenate(ys, axis=1)
    state[...] = jnp.concatenate(new_state, axis=1)

    if want_state:
        @pl.when(c == nc - 1)
        def _():
            for i in range(SSD_INNER // L):
                blk = state[:, i * L:(i + 1) * L].T
                st_ref[0, 0, i * (L // SSD_HD):(i + 1) * (L // SSD_HD)] = blk.reshape(L // SSD_HD, SSD_HD, SSD_STATE)


def _ssd_scan(xbc, dt_raw, lw, h0, j, rev, dims):
    t, t_p, seq_p, seq_s = dims["t"], dims["t_p"], dims["seq_p"], dims["seq_s"]
    nb_p, nb_s = dims["nb_p"], dims["nb_s"]
    L = SSD_CHUNK
    d = 1 if rev else 0
    li = jnp.arange(L)
    tri = ((li[:, None] <= li[None, :]) if rev else (li[:, None] >= li[None, :])).astype(BF16)
    lane = jnp.arange(128)
    col_head = jnp.arange(SSD_INNER) // SSD_HD
    rep = (lane[:, None] == (d * SSD_HEADS + col_head)[None, :]).astype(BF16)
    dsk = jnp.repeat(lw["d_skip"], SSD_HD)[None, :].astype(F32)
    st_shape = (nb_p, 1, SSD_HEADS, SSD_HD, SSD_STATE)

    def make_call(seq, nb, off, has_h0, want_state, n_alias):
        nc = seq // L
        chunk = (lambda b, c: (off * nc + b * nc + (nc - 1 - c), 0)) if rev else (lambda b, c: (off * nc + b * nc + c, 0))
        in_specs = [pl.BlockSpec((L, 2 * SSD_INNER), chunk), pl.BlockSpec((L, 128), chunk),
                    _full((1, 128)), _full((1, 128)),
                    _full((1, SSD_INNER)), _full((L, L)), _full((128, SSD_INNER))]
        if has_h0:
            in_specs.append(pl.BlockSpec((1, 1, SSD_HEADS, SSD_HD, SSD_STATE), lambda b, c: (b, j, 0, 0, 0)))
        aliases = {}
        if n_alias:
            aliases = {len(in_specs): 0}
            in_specs.append(pl.BlockSpec(memory_space=pl.ANY))
        out_specs = [pl.BlockSpec((L, SSD_INNER), chunk)]
        out_shape = [jax.ShapeDtypeStruct((t, SSD_INNER), F32)]
        if want_state:
            out_specs.append(pl.BlockSpec((1, 1, SSD_HEADS, SSD_HD, SSD_STATE), lambda b, c: (b, 0, 0, 0, 0)))
            out_shape.append(jax.ShapeDtypeStruct(st_shape, F32))
        return pl.pallas_call(
            functools.partial(_ssd_scan_kernel, seq=seq, rev=rev, has_h0=has_h0, want_state=want_state,
                              add_skip=not rev),
            grid=(nb, nc),
            in_specs=in_specs,
            out_specs=out_specs,
            out_shape=out_shape,
            scratch_shapes=[pltpu.VMEM((SSD_STATE, SSD_INNER), F32)],
            input_output_aliases=aliases,
            compiler_params=_params(2),
            name="ssd_scan_%s_seq%d" % ("bwd" if rev else "fwd", seq),
        )

    common = [xbc, dt_raw, lw["dt_bias"], lw["a_log"], dsk, tri, rep]
    y, st = make_call(seq_p, nb_p, 0, False, True, 0)(*common)
    (y,) = make_call(seq_s, nb_s, t_p // seq_s, True, False, 1)(*common, h0, y)
    return y, st


def _ssd_out_kernel(yf_ref, yb_ref, z_ref, gn_ref, w_ref, h_ref, gt1, g2, sh2, sc2, wr_hi, wr_lo, br,
                    h1_ref, xn_ref, route_ref):
    y = (yf_ref[...] + yb_ref[...]) * _silu(z_ref[...].astype(F32))
    o = _dot(_rms(y, gn_ref[...]).astype(BF16), w_ref[...])
    _finish(h_ref[...], o, gt1[0], (g2, sh2, sc2, wr_hi, wr_lo, br), h1_ref, xn_ref, route_ref)


def _ssd_layer(h, state_f, state_b, j, lw, layer, dims):
    t = dims["t"]
    zxbc = _pre_proj(h, lw, layer, lw["w_in"], 1536, BF16, dims)
    dt_raw = _pre_proj(h, lw, layer, lw["w_dt"], 128, F32, dims)
    xbc = _ssd_conv(zxbc, lw, dims)
    y_f, st_f = _ssd_scan(xbc, dt_raw, lw, state_f, j, False, dims)
    y_b, st_b = _ssd_scan(xbc, dt_raw, lw, state_b, j, True, dims)
    row_fn = _tile_row_fn(dims)
    tile = lambda: pl.BlockSpec((TM, D), lambda i: (i, 0))
    wide = lambda: pl.BlockSpec((TM, SSD_INNER), lambda i: (i, 0))
    outs = pl.pallas_call(
        _ssd_out_kernel,
        grid=(t // TM,),
        in_specs=[wide(), wide(), wide(), _full((1, SSD_INNER)), _full((SSD_INNER, D)), tile(),
                  _mod_spec(layer, 2, row_fn)] + _route_specs(layer, row_fn),
        out_specs=_stream_specs(TM, lambda i: i),
        out_shape=_stream_outs(t),
        compiler_params=_params(1),
        name="ssd_out",
    )(y_f, y_b, zxbc, lw["g_norm"], lw["w_out"], h, lw["mod"], *_route_args(lw))
    return outs, st_f, st_b


TS = 256
SRC = 256


def _moe_plan(route, t):
    i32 = jnp.int32
    npair = len(PAIR_ORDER)
    ncls = MOE_GROUPS * npair
    key = route[:, 0].astype(i32)
    oh = (key[:, None] == jnp.arange(ncls, dtype=i32)[None, :]).astype(i32)
    csum = jnp.cumsum(oh, axis=0)
    ccnt = csum[-1]
    cnt = ccnt.reshape(MOE_GROUPS, npair).sum(axis=1)
    padded = ((cnt + TS - 1) // TS) * TS
    gend = jnp.cumsum(padded)
    in_grp = jnp.cumsum(ccnt.reshape(MOE_GROUPS, npair), axis=1) - ccnt.reshape(MOE_GROUPS, npair)
    cstart = ((gend - padded)[:, None] + in_grp).reshape(ncls)
    pos = jnp.sum(oh * (csum - 1 + cstart[None, :]), axis=1)
    n_tiles = t // TS + MOE_GROUPS
    tile0 = jnp.arange(n_tiles, dtype=i32) * TS
    tile_grp = jnp.sum((tile0[:, None] >= gend[None, :]).astype(i32), axis=1)
    last_grp = jnp.max(jnp.where(cnt > 0, jnp.arange(MOE_GROUPS, dtype=i32), 0))
    tile_grp = jnp.minimum(tile_grp, last_grp)
    n_used = (gend[-1] // TS).reshape(1)
    touch = ((cstart[None, :] < tile0[:, None] + TS) & (cstart[None, :] + ccnt[None, :] > tile0[:, None])
             & (ccnt[None, :] > 0))
    uses = jnp.array([[int(e in PAIR_ORDER[c % npair]) for e in range(MOE_EPG)] for c in range(ncls)], i32)
    need = (jnp.dot(touch.astype(i32), uses) > 0).astype(i32).reshape(-1)
    return dict(pos=pos, tile_grp=tile_grp, n_used=n_used, need=need, n_tiles=n_tiles)


ROW_UNROLL = 8


def _moe_scatter_kernel(pos_ref, x_ref, xs_ref):
    n = pl.program_id(0)

    @pl.when(n == 0)
    def _():
        xs_ref[...] = jnp.zeros_like(xs_ref)

    def body(jj, carry):
        for r in range(ROW_UNROLL):
            j = jj * ROW_UNROLL + r
            xs_ref[pl.ds(pos_ref[n * SRC + j], 1), :] = x_ref[pl.ds(j, 1), :]
        return carry

    lax.fori_loop(0, SRC // ROW_UNROLL, body, 0)


def _moe_expert_kernel(grp_ref, nused_ref, need_ref, xs_ref, wg_ref, wu_ref, wd_ref, y_ref,
                       wg_b, wu_b, wd_b, acc_ref):
    i = pl.program_id(0)
    used = i < nused_ref[0]
    new_group = (i == 0) | (grp_ref[i] != grp_ref[jnp.maximum(i - 1, 0)])

    @pl.when(used & new_group)
    def _():
        for e in range(MOE_EPG):
            wg_b[e] = wg_ref[0, e].astype(BF16)
            wu_b[e] = wu_ref[0, e].astype(BF16)
            wd_b[e] = wd_ref[0, e].astype(BF16)

    @pl.when(used)
    def _():
        lo, hi = _unpack_pairs(xs_ref[...])
        x = jnp.concatenate([lo, hi[:, :D - PW]], axis=1).astype(BF16)
        rec = hi[:, D - PW:XW - PW]
        lane = lax.broadcasted_iota(jnp.int32, rec.shape, 1)
        acc_ref[...] = jnp.zeros_like(acc_ref)
        for e in range(MOE_EPG):
            @pl.when(need_ref[i * MOE_EPG + e] > 0)
            def _():
                hg = _dot(x, wg_b[e])
                hu = _dot(x, wu_b[e])
                mine = (lane % MOE_EPG == e) & (lane < 3 * MOE_EPG)
                cw = jnp.sum(jnp.where(mine, rec, 0.0), axis=-1, keepdims=True)
                acc_ref[...] += _dot((_silu(hg) * hu * cw).astype(BF16), wd_b[e])

        acc = acc_ref[...].astype(BF16).astype(F32)
        y_ref[...] = _pack_pairs(acc[:, :D // 2], acc[:, D // 2:])

    @pl.when(i >= nused_ref[0])
    def _():
        y_ref[...] = jnp.zeros_like(y_ref)


def _moe_ungather_kernel(pos_ref, ys_ref, h1_ref, gt2, fg_ref, *rest, n_prompt_tiles):
    rows_ref = rest[-1]
    out_refs = rest[:-1]
    n = pl.program_id(0)

    def body(jj, carry):
        for r in range(ROW_UNROLL):
            j = jj * ROW_UNROLL + r
            rows_ref[pl.ds(j, 1), :] = ys_ref[pl.ds(pos_ref[n * SRC + j], 1), :]
        return carry

    lax.fori_loop(0, SRC // ROW_UNROLL, body, 0)
    h2 = h1_ref[...] + gt2[0] * jnp.concatenate(_unpack_pairs(rows_ref[...]), axis=1)
    if n_prompt_tiles is None:
        out_refs[0][...] = h2
    else:
        y = _rms(h2, fg_ref[...])

        @pl.when(n < n_prompt_tiles)
        def _():
            out_refs[0][...] = y

        @pl.when(n >= n_prompt_tiles)
        def _():
            out_refs[1][...] = y


def _moe_sparse(h1, xn, route, lw, layer, final_g, final, dims):
    t = dims["t"]
    plan = _moe_plan(route, t)
    n_tiles = plan["n_tiles"]
    n_rows = n_tiles * TS
    one = pl.Buffered(1)
    xs = pl.pallas_call(
        _moe_scatter_kernel,
        grid_spec=pltpu.PrefetchScalarGridSpec(
            num_scalar_prefetch=1,
            grid=(t // SRC,),
            in_specs=[pl.BlockSpec((SRC, PW), lambda n, pos: (n, 0))],
            out_specs=pl.BlockSpec((n_rows, PW), lambda n, pos: (0, 0))),
        out_shape=jax.ShapeDtypeStruct((n_rows, PW), U32),
        compiler_params=_params(1),
        name="moe_scatter_l%d" % layer,
    )(plan["pos"], xn)

    ex = lambda i, g, nu, nd: (layer, g[i], 0, 0)
    ys = pl.pallas_call(
        _moe_expert_kernel,
        grid_spec=pltpu.PrefetchScalarGridSpec(
            num_scalar_prefetch=3,
            grid=(n_tiles,),
            in_specs=[pl.BlockSpec((TS, PW), lambda i, g, nu, nd: (i, 0)),
                      pl.BlockSpec((1, MOE_EPG, D, MOE_DFF), ex, pipeline_mode=one),
                      pl.BlockSpec((1, MOE_EPG, D, MOE_DFF), ex, pipeline_mode=one),
                      pl.BlockSpec((1, MOE_EPG, MOE_DFF, D), ex, pipeline_mode=one)],
            out_specs=pl.BlockSpec((TS, D // 2), lambda i, g, nu, nd: (i, 0)),
            scratch_shapes=[pltpu.VMEM((MOE_EPG, D, MOE_DFF), BF16), pltpu.VMEM((MOE_EPG, D, MOE_DFF), BF16),
                            pltpu.VMEM((MOE_EPG, MOE_DFF, D), BF16), pltpu.VMEM((TS, D), F32)]),
        out_shape=jax.ShapeDtypeStruct((n_rows, D // 2), U32),
        compiler_params=_params(1),
        name="moe_expert_l%d" % layer,
    )(plan["tile_grp"], plan["n_used"], plan["need"], xs, lw["w_gate"], lw["w_up"], lw["w_down"])

    row_fn = _tile_row_fn(dims, SRC)
    tile = lambda: pl.BlockSpec((SRC, D), lambda n, pos: (n, 0))
    if final:
        npt = dims["t_p"] // SRC
        out_specs = [pl.BlockSpec((SRC, D), lambda n, pos: (jnp.minimum(n, npt - 1), 0)),
                     pl.BlockSpec((SRC, D), lambda n, pos: (jnp.maximum(n - npt, 0), 0))]
        out_shape = [jax.ShapeDtypeStruct((dims["t_p"], D), F32), jax.ShapeDtypeStruct((t - dims["t_p"], D), F32)]
    else:
        npt = None
        out_specs = [tile()]
        out_shape = [jax.ShapeDtypeStruct((t, D), F32)]
    return pl.pallas_call(
        functools.partial(_moe_ungather_kernel, n_prompt_tiles=npt),
        grid_spec=pltpu.PrefetchScalarGridSpec(
            num_scalar_prefetch=1,
            grid=(t // SRC,),
            in_specs=[pl.BlockSpec((n_rows, D // 2), lambda n, pos: (0, 0)), tile(),
                      pl.BlockSpec((1, 1, D), lambda n, pos: ((layer * MOD_ROWS + row_fn(n)) * N_MOD + 5, 0, 0)),
                      pl.BlockSpec((1, D), lambda n, pos: (0, 0))],
            out_specs=out_specs,
            scratch_shapes=[pltpu.VMEM((SRC, D // 2), U32)]),
        out_shape=out_shape,
        compiler_params=_params(1),
        name="moe_ungather_l%d" % layer,
    )(plan["pos"], ys, h1, lw["mod"], final_g)


def kernel(x_prompt, x_sample, cache_k, cache_v, state_ssm_fwd, state_ssm_bwd, c, c_ctx, ada_w, ada_b, norm1_g, norm2_g, final_g, fnet_w_o, na_w_qkv, na_w_o, na_rpb, gmlp_w_in, gmlp_g_v, gmlp_w_s, gmlp_b_s, gmlp_w_out, ssd_w_in, ssd_conv_w, ssd_conv_b, ssd_a_log, ssd_dt_bias, ssd_d_skip, ssd_g_norm, ssd_w_out, moe_w_gr, moe_b_gr, moe_w_er, moe_b_er, moe_w_gate, moe_w_up, moe_w_down):
    nb_p, seq_p, _ = x_prompt.shape
    nb_s, seq_s, _ = x_sample.shape
    depth = ada_w.shape[0]
    t_p, t_s = nb_p * seq_p, nb_s * seq_s
    dims = dict(t=t_p + t_s, t_p=t_p, seq_p=seq_p, seq_s=seq_s, nb_p=nb_p, nb_s=nb_s)
    assert 1 + nb_s <= MOD_ROWS and t_p % seq_s == 0 and t_p % TM == 0 and seq_s % TM == 0

    cond = jnp.zeros((MOD_ROWS, D), F32).at[0].set(c_ctx).at[1:1 + nb_s].set(c)
    mod = _ada_table(cond, ada_w, ada_b)
    h = (x_prompt.reshape(t_p, D), x_sample.reshape(t_s, D))
    fg = final_g.reshape(1, D)

    new_k, new_v, new_sf, new_sb = [], [], [], []
    for l in range(depth):
        kind, j = l % 4, l // 4
        w_r = jnp.concatenate([moe_w_gr[l], moe_w_er[l]], axis=1)
        w_r = jnp.pad(w_r, ((0, 0), (0, ROUTE_W - MOE_GROUPS - MOE_EXPERTS)))
        b_r = jnp.pad(jnp.concatenate([moe_b_gr[l], moe_b_er[l]]), (0, ROUTE_W - MOE_GROUPS - MOE_EXPERTS))
        wr_hi = w_r.astype(BF16)
        lw = dict(mod=mod, g1=norm1_g[l].reshape(1, D), g2=norm2_g[l].reshape(1, D),
                  wr_hi=wr_hi, wr_lo=(w_r - wr_hi.astype(F32)).astype(BF16), br=b_r.reshape(1, ROUTE_W),
                  w_gate=moe_w_gate, w_up=moe_w_up, w_down=moe_w_down)
        if kind == 0:
            lw.update(w_o=fnet_w_o[j].astype(BF16))
            h1, xn, route = _fnet_layer(h, lw, l, dims)
        elif kind == 1:
            lw.update(w_qkv=na_w_qkv[j].astype(BF16), w_o=na_w_o[j].astype(BF16), rpb=na_rpb[j])
            (h1, xn, route), kc, vc = _na_layer(h, cache_k, cache_v, j, lw, l, dims)
            new_k.append(kc)
            new_v.append(vc)
        elif kind == 2:
            gw = GMLP_DFF // GMLP_GROUPS
            lw.update(w_in=gmlp_w_in[j].astype(BF16), g_v=gmlp_g_v[j].reshape(1, GMLP_DFF),
                      w_s=gmlp_w_s[j].astype(BF16),
                      b_s=jnp.broadcast_to(gmlp_b_s[j][:, :, None], (GMLP_GROUPS, GMLP_CHUNK, gw)),
                      w_out=gmlp_w_out[j].astype(BF16))
            h1, xn, route = _gmlp_layer(h, lw, l, dims)
        else:
            n_main = 3 * SSD_INNER
            w_in = ssd_w_in[j]
            pad = lambda v: jnp.pad(v, ((0, 0), (0, 128 - 2 * SSD_HEADS)))
            lw.update(w_in=w_in[:, :n_main].astype(BF16), w_dt=pad(w_in[:, n_main:]).astype(BF16),
                      conv_w=ssd_conv_w[j], conv_b=ssd_conv_b[j].reshape(1, -1),
                      dt_bias=pad(ssd_dt_bias[j].reshape(1, -1)), a_log=pad(ssd_a_log[j].reshape(1, -1)),
                      d_skip=ssd_d_skip[j], g_norm=ssd_g_norm[j].reshape(1, SSD_INNER),
                      w_out=ssd_w_out[j].astype(BF16))
            (h1, xn, route), sf, sb = _ssd_layer(h, state_ssm_fwd, state_ssm_bwd, j, lw, l, dims)
            new_sf.append(sf)
            new_sb.append(sb)
        out = _moe_sparse(h1, xn, route, lw, l, fg, l == depth - 1, dims)
        h = out if l == depth - 1 else out[0]

    y_prompt = h[0].reshape(nb_p, seq_p, D)
    y_sample = h[1].reshape(nb_s, seq_s, D)
    cat = lambda xs: jnp.concatenate(xs, axis=1)
    return (y_prompt, y_sample, cat(new_k), cat(new_v), cat(new_sf), cat(new_sb))
```

```python
import functools
import math

import jax
import jax.numpy as jnp
from jax import lax
from jax.experimental import pallas as pl
from jax.experimental.pallas import tpu as pltpu

F32 = jnp.float32
BF16 = jnp.bfloat16

D = 1024
EPS = 1e-6
NEG = -1e30
N_MOD = 6
MOD_ROWS = 8
GRID_W = 64
FNET_GROUPS = 8
NA_HEADS = 16
NA_HD = 64
NA_WIN_ROWS = 8
NA_WIN_COLS = 16
GMLP_CHUNK = 128
GMLP_DFF = 2048
GMLP_GROUPS = 8
SSD_INNER = 2048
SSD_HD = 64
SSD_HEADS = 32
SSD_GROUPS = 8
SSD_STATE = 128
SSD_CHUNK = 128
SSD_HPG = SSD_HEADS // SSD_GROUPS
MOE_GROUPS = 4
MOE_EPG = 4
MOE_EXPERTS = 16
MOE_DFF = 512
ROUTE_W = 128
ROUTE_E0 = 4
PAIR_ORDER = ((0, 1), (0, 2), (0, 3), (1, 3), (1, 2), (2, 3))

TM = 512
VMEM_LIMIT = 56 * 1024 * 1024


def _dot(a, b):
    return jnp.dot(a, b, preferred_element_type=F32)


def _dot_nt(a, b):
    return lax.dot_general(a, b, (((1,), (1,)), ((), ())), preferred_element_type=F32)


def _silu(x):
    return x * (1.0 / (1.0 + jnp.exp(-x)))


def _rms(x, g):
    return x * lax.rsqrt(jnp.mean(x * x, axis=-1, keepdims=True) + EPS) * g


def _normmod(x, g, shift, scale):
    return _rms(x, g) * (1.0 + scale) + shift


def _split3(x):
    hi = x.astype(BF16)
    r = x - hi.astype(F32)
    mid = r.astype(BF16)
    lo = (r - mid.astype(F32)).astype(BF16)
    return hi, mid, lo


def _params(n_axes):
    return pltpu.CompilerParams(dimension_semantics=("arbitrary",) * n_axes,
                                vmem_limit_bytes=VMEM_LIMIT)


def _full(shape):
    nd = len(shape)
    return pl.BlockSpec(shape, lambda *_: (0,) * nd)


def _mod_spec(layer, k, row_fn):
    return pl.BlockSpec((1, 1, D), lambda *idx: ((layer * MOD_ROWS + row_fn(*idx)) * N_MOD + k, 0, 0))


def _ada_kernel(c_ref, w_ref, b_ref, o_ref):
    c = c_ref[...]
    o_ref[0] = _dot(_silu(c).astype(BF16), w_ref[0].astype(BF16)) + b_ref[0]


def _ada_table(cond, ada_w, ada_b):
    depth = ada_w.shape[0]
    n = N_MOD * D
    tn = 1536
    out = pl.pallas_call(
        _ada_kernel,
        grid=(depth, n // tn),
        in_specs=[_full((MOD_ROWS, D)),
                  pl.BlockSpec((1, D, tn), lambda l, j: (l, 0, j)),
                  pl.BlockSpec((1, 1, tn), lambda l, j: (l, 0, j))],
        out_specs=pl.BlockSpec((1, MOD_ROWS, tn), lambda l, j: (l, 0, j)),
        out_shape=jax.ShapeDtypeStruct((depth, MOD_ROWS, n), F32),
        compiler_params=_params(2),
        name="ada_table",
    )(cond, ada_w, ada_b.reshape(depth, 1, n))
    return out.reshape(depth * MOD_ROWS * N_MOD, 1, D)


def _route(h1, g2, sh2, sc2, wr_hi, wr_lo, br):
    xn = _normmod(h1, g2, sh2, sc2)
    xh = xn.astype(BF16)
    xl = (xn - xh.astype(F32)).astype(BF16)
    logits = _dot(xh, wr_hi) + _dot(xh, wr_lo) + _dot(xl, wr_hi) + br
    lane = lax.broadcasted_iota(jnp.int32, logits.shape, 1).astype(F32)
    far = float(ROUTE_W)
    gl = jnp.where(lane < MOE_GROUPS, logits, NEG)
    gmax = jnp.max(gl, axis=-1, keepdims=True)
    g_p = 1.0 / jnp.sum(jnp.exp(gl - gmax), axis=-1, keepdims=True)
    gidx = jnp.min(jnp.where(gl == gmax, lane, far), axis=-1, keepdims=True)
    lo = ROUTE_E0 + MOE_EPG * gidx
    el = jnp.where((lane >= lo) & (lane < lo + MOE_EPG), logits, NEG)
    m1 = jnp.max(el, axis=-1, keepdims=True)
    i1 = jnp.min(jnp.where(el == m1, lane, far), axis=-1, keepdims=True)
    el2 = jnp.where(lane == i1, NEG, el)
    m2 = jnp.max(el2, axis=-1, keepdims=True)
    i2 = jnp.min(jnp.where(el2 == m2, lane, far), axis=-1, keepdims=True)
    e2 = jnp.exp(m2 - m1)
    w1 = g_p / (1.0 + e2)
    w2 = w1 * e2
    rec = jnp.zeros_like(logits)
    for part, (a, b) in enumerate(zip(_split3(w1), _split3(w2))):
        shift = part * MOE_EPG - lo
        rec = (rec + jnp.where(lane == i1 + shift, a.astype(F32), 0.0)
               + jnp.where(lane == i2 + shift, b.astype(F32), 0.0))
    ea = jnp.minimum(i1, i2) - lo
    eb = jnp.maximum(i1, i2) - lo
    pair = sum(jnp.where((ea == a) & (eb == b), float(k), 0.0) for k, (a, b) in enumerate(PAIR_ORDER))
    key = gidx * float(len(PAIR_ORDER)) + pair
    return xh, rec.astype(BF16), jnp.broadcast_to(key, logits.shape)


def _route_specs(layer, row_fn):
    return [_full((1, D)), _mod_spec(layer, 3, row_fn), _mod_spec(layer, 4, row_fn),
            _full((D, ROUTE_W)), _full((D, ROUTE_W)), _full((1, ROUTE_W))]


def _route_args(lw):
    return [lw["g2"], lw["mod"], lw["mod"], lw["wr_hi"], lw["wr_lo"], lw["br"]]


def _finish(h, o, gate, rt_refs, h1_ref, xn_ref, route_ref):
    g2, sh2, sc2, wr_hi, wr_lo, br = rt_refs
    h1 = h + gate * o
    h1_ref[...] = h1
    xn, rec, gid = _route(h1, g2[...], sh2[0], sc2[0], wr_hi[...], wr_lo[...], br[...])
    row = jnp.concatenate([xn.astype(F32), rec.astype(F32), jnp.zeros((h.shape[0], 2 * PW - XW), F32)], axis=1)
    xn_ref[...] = _pack_pairs(row[:, :PW], row[:, PW:])
    route_ref[...] = gid


XW = D + ROUTE_W
PW = 640
U32 = jnp.uint32


def _pack_pairs(lo, hi):
    lo_bits = lax.bitcast_convert_type(lo, U32) >> 16
    hi_bits = lax.bitcast_convert_type(hi, U32) & U32(0xFFFF0000)
    return lo_bits | hi_bits


def _unpack_pairs(w):
    return lax.bitcast_convert_type(w << 16, F32), lax.bitcast_convert_type(w & U32(0xFFFF0000), F32)


def _stream_outs(t):
    return [jax.ShapeDtypeStruct((t, D), F32), jax.ShapeDtypeStruct((t, PW), U32),
            jax.ShapeDtypeStruct((t, ROUTE_W), F32)]


def _stream_specs(rows, row_block):
    return [pl.BlockSpec((rows, w), lambda *idx: (row_block(*idx), 0)) for w in (D, PW, ROUTE_W)]


def _fnet_kernel(h_ref, g1, sh1, sc1, gt1, csc_ref, fs_ref, wo_ref, g2, sh2, sc2, wr_hi, wr_lo, br,
                 *rest, seq):
    h1_ref, xn_ref, route_ref, ab_ref = rest[-4:]
    h = h_ref[...]
    a = _normmod(h, g1[...], sh1[0], sc1[0]).astype(BF16)
    gd = D // FNET_GROUPS
    for g in range(FNET_GROUPS):
        ab = _dot(a[:, g * gd:(g + 1) * gd], csc_ref[...])
        ab_ref[0:seq, g * gd:(g + 1) * gd] = ab[:, :gd].astype(BF16)
        ab_ref[seq:2 * seq, g * gd:(g + 1) * gd] = ab[:, gd:].astype(BF16)
    f = _dot(fs_ref[...], ab_ref[...])
    o = _dot(f.astype(BF16), wo_ref[...])
    _finish(h, o, gt1[0], (g2, sh2, sc2, wr_hi, wr_lo, br), h1_ref, xn_ref, route_ref)


def _dft_tables(n):
    k = jnp.arange(n, dtype=jnp.int32)
    ang = ((k[:, None] * k[None, :]) % n).astype(F32) * (2.0 * math.pi / n)
    s = 1.0 / math.sqrt(n)
    return jnp.cos(ang) * s, jnp.sin(ang) * s


def _fnet_layer(h, lw, layer, dims):
    t, t_p = dims["t"], dims["t_p"]
    gd = D // FNET_GROUPS
    cc, sc = _dft_tables(gd)
    csc = jnp.concatenate([cc, sc], axis=1).astype(BF16)

    split_in = isinstance(h, tuple)

    def make_call(seq, nb, off, n_alias):
        cs, ss = _dft_tables(seq)
        fs = jnp.concatenate([cs, -ss], axis=1).astype(BF16)
        row_fn = (lambda b: 0) if off == 0 else (lambda b: 1 + b)
        in_off = 0 if split_in else off
        tile = lambda: pl.BlockSpec((seq, D), lambda b: (off + b, 0))
        in_specs = ([pl.BlockSpec((seq, D), lambda b: (in_off + b, 0)), _full((1, D))]
                    + [_mod_spec(layer, k, row_fn) for k in (0, 1, 2)]
                    + [_full((gd, 2 * gd)), _full((seq, 2 * seq)), _full((D, D))]
                    + _route_specs(layer, row_fn))
        aliases = {}
        if n_alias:
            base = len(in_specs)
            in_specs = in_specs + [pl.BlockSpec(memory_space=pl.ANY)] * n_alias
            aliases = {base + i: i for i in range(n_alias)}
        call = pl.pallas_call(
            functools.partial(_fnet_kernel, seq=seq),
            grid=(nb,),
            in_specs=in_specs,
            out_specs=_stream_specs(seq, lambda b: off + b),
            out_shape=_stream_outs(t),
            scratch_shapes=[pltpu.VMEM((2 * seq, D), BF16)],
            input_output_aliases=aliases,
            compiler_params=_params(1),
            name="fnet_seq%d" % seq,
        )
        return lambda *a: call(*a[:2], *a[2:5], a[5], fs, *a[6:])

    h_p, h_s = h if split_in else (h, h)
    args = [lw["g1"], lw["mod"], lw["mod"], lw["mod"], csc, lw["w_o"]] + _route_args(lw)
    outs = make_call(dims["seq_p"], dims["nb_p"], 0, None)(h_p, *args)
    return make_call(dims["seq_s"], dims["nb_s"], t_p // dims["seq_s"], 3)(h_s, *args, *outs)


def _pre_kernel(h_ref, g1, sh1, sc1, w_ref, o_ref, a_ref):
    @pl.when(pl.program_id(1) == 0)
    def _():
        a_ref[...] = _normmod(h_ref[...], g1[...], sh1[0], sc1[0]).astype(BF16)

    o_ref[...] = _dot(a_ref[...], w_ref[...].astype(BF16)).astype(o_ref.dtype)


def _tile_row_fn(dims, tm=TM):
    npt = dims["t_p"] // tm
    tps = dims["seq_s"] // tm
    return lambda i, *_: jnp.where(i < npt, 0, 1 + (i - npt) // tps)


def _pre_proj(h, lw, layer, w, tn, out_dtype, dims, n=None):
    t = dims["t"]
    n = w.shape[1] if n is None else n
    tm = dims["seq_s"]
    row_fn = _tile_row_fn(dims, tm)
    return pl.pallas_call(
        _pre_kernel,
        grid=(t // tm, n // tn),
        in_specs=[pl.BlockSpec((tm, D), lambda i, j: (i, 0)), _full((1, D)),
                  _mod_spec(layer, 0, row_fn), _mod_spec(layer, 1, row_fn),
                  pl.BlockSpec((D, tn), lambda i, j: (0, j))],
        out_specs=pl.BlockSpec((tm, tn), lambda i, j: (i, j)),
        out_shape=jax.ShapeDtypeStruct((t, n), out_dtype),
        scratch_shapes=[pltpu.VMEM((tm, D), BF16)],
        compiler_params=_params(2),
        name="pre_proj_l%d_n%d" % (layer, n),
    )(h, lw["g1"], lw["mod"], lw["mod"], w)


def _out_kernel(o_ref, w_ref, h_ref, gt1, g2, sh2, sc2, wr_hi, wr_lo, br, h1_ref, xn_ref, route_ref):
    o = _dot(o_ref[...], w_ref[...])
    _finish(h_ref[...], o, gt1[0], (g2, sh2, sc2, wr_hi, wr_lo, br), h1_ref, xn_ref, route_ref)


def _out_proj(o, w, h, lw, layer, dims):
    t = dims["t"]
    k = o.shape[1]
    row_fn = _tile_row_fn(dims)
    tile = lambda: pl.BlockSpec((TM, D), lambda i: (i, 0))
    return pl.pallas_call(
        _out_kernel,
        grid=(t // TM,),
        in_specs=[pl.BlockSpec((TM, k), lambda i: (i, 0)), _full((k, D)), tile(),
                  _mod_spec(layer, 2, row_fn)] + _route_specs(layer, row_fn),
        out_specs=_stream_specs(TM, lambda i: i),
        out_shape=_stream_outs(t),
        compiler_params=_params(1),
        name="out_proj_l%d" % layer,
    )(o, w, h, lw["mod"], *_route_args(lw))


NA_SCALE = NA_HD ** -0.5


def _na_ctx_kernel(q_ref, k_ref, v_ref, o_ref, kc_ref, vc_ref):
    outs = []
    for hd in range(NA_HEADS):
        sl = slice(hd * NA_HD, (hd + 1) * NA_HD)
        q = q_ref[:, sl] * NA_SCALE
        k = k_ref[:, sl]
        v = v_ref[:, sl]
        s = _dot_nt(q, k)
        p = jnp.exp(s - jnp.max(s, axis=-1, keepdims=True))
        l = jnp.sum(p, axis=-1, keepdims=True)
        outs.append((_dot(p.astype(BF16), v) / l).astype(BF16))
        kc_ref[0, 0, hd] = k.astype(F32)
        vc_ref[0, 0, hd] = v.astype(F32)
    o_ref[...] = jnp.concatenate(outs, axis=1)


def _na_row_start(qr, rows):
    kr = min(NA_WIN_ROWS, rows)
    return min(max(qr - kr // 2, 0), rows - kr)


def _na_window_bias(bias_ref, hh, qr, rows, m_lo, m_hi):
    kr = min(NA_WIN_ROWS, rows)
    rs = _na_row_start(qr, rows)
    blocks = []
    for m in range(m_lo, m_hi):
        ok0 = rs <= 2 * m < rs + kr
        ok1 = rs <= 2 * m + 1 < rs + kr
        e = 2 * m - qr + NA_WIN_ROWS
        if ok0 and ok1:
            blocks.append(bias_ref[hh, 0, e])
        elif ok1:
            blocks.append(bias_ref[hh, 1, e])
        elif ok0:
            blocks.append(bias_ref[hh, 2, e])
        else:
            blocks.append(jnp.full((GRID_W, 2 * GRID_W), NEG, F32))
    return jnp.concatenate(blocks, axis=1)


def _na_lat_kernel(q_ref, k_ref, v_ref, bias_ref, kc_ref, vc_ref, o_in, o_ref, *, seq, qb):
    del o_in
    rows = seq // GRID_W
    for hh in range(2):
        sl = slice(hh * NA_HD, (hh + 1) * NA_HD)
        k = k_ref[:, sl]
        v = v_ref[:, sl]
        kc = kc_ref[0, 0, hh].astype(BF16)
        vc = vc_ref[0, 0, hh].astype(BF16)
        for b0 in range(0, seq, qb):
            q = q_ref[b0:b0 + qb, sl] * NA_SCALE
            qrs = range(b0 // GRID_W, (b0 + qb) // GRID_W)
            m_lo = _na_row_start(qrs[0], rows) // 2
            m_hi = (_na_row_start(qrs[-1], rows) + min(NA_WIN_ROWS, rows) + 1) // 2
            keys = slice(m_lo * 2 * GRID_W, m_hi * 2 * GRID_W)
            bias = jnp.concatenate([_na_window_bias(bias_ref, hh, qr, rows, m_lo, m_hi) for qr in qrs], axis=0)
            s1 = _dot_nt(q, k[keys]) + bias
            s2 = _dot_nt(q, kc)
            m = jnp.maximum(jnp.max(s1, axis=-1, keepdims=True), jnp.max(s2, axis=-1, keepdims=True))
            p1 = jnp.exp(s1 - m)
            p2 = jnp.exp(s2 - m)
            l = jnp.sum(p1, axis=-1, keepdims=True) + jnp.sum(p2, axis=-1, keepdims=True)
            o = (_dot(p1.astype(BF16), v[keys]) + _dot(p2.astype(BF16), vc)) / l
            o_ref[b0:b0 + qb, sl] = o.astype(BF16)


def _na_bias_tables(rpb):
    c = jnp.arange(GRID_W)
    win0 = jnp.clip(c - NA_WIN_COLS // 2, 0, GRID_W - NA_WIN_COLS)
    ok_c = (c[None, :] >= win0[:, None]) & (c[None, :] < win0[:, None] + NA_WIN_COLS)
    dc = jnp.clip(c[None, :] - c[:, None], 1 - NA_WIN_COLS, NA_WIN_COLS - 1) + NA_WIN_COLS - 1
    nh, ndr, ndc = rpb.shape
    pick = (dc.reshape(1, -1) == jnp.arange(ndc)[:, None]).astype(F32)
    cm = jnp.dot(rpb.reshape(nh * ndr, ndc), pick, precision=lax.Precision.HIGHEST)
    cm = jnp.where(ok_c[None, None], cm.reshape(nh, ndr, GRID_W, GRID_W), NEG)
    neg = jnp.full_like(cm[:, :1], NEG)
    ext = jnp.concatenate([neg, cm, neg], axis=1)
    a, b = ext[:, :-1], ext[:, 1:]
    negs = jnp.full_like(a, NEG)
    pair = lambda x, y: jnp.concatenate([x, y], axis=-1)
    return jnp.stack([pair(a, b), pair(negs, b), pair(a, negs)], axis=1)


def _na_layer(h, cache_k, cache_v, j, lw, layer, dims):
    t, t_p, seq_p, seq_s = dims["t"], dims["t_p"], dims["seq_p"], dims["seq_s"]
    nb_p, nb_s = dims["nb_p"], dims["nb_s"]
    qkv = _pre_proj(h, lw, layer, lw["w_qkv"], 1536, BF16, dims)
    cshape = (nb_p, 1, NA_HEADS, seq_p, NA_HD)
    cspec = lambda: pl.BlockSpec((1, 1, NA_HEADS, seq_p, NA_HD), lambda b: (b, 0, 0, 0, 0))
    o, kc, vc = pl.pallas_call(
        _na_ctx_kernel,
        grid=(nb_p,),
        in_specs=[pl.BlockSpec((seq_p, D), lambda b: (b, 0)), pl.BlockSpec((seq_p, D), lambda b: (b, 1)),
                  pl.BlockSpec((seq_p, D), lambda b: (b, 2))],
        out_specs=[pl.BlockSpec((seq_p, D), lambda b: (b, 0)), cspec(), cspec()],
        out_shape=[jax.ShapeDtypeStruct((t, D), BF16), jax.ShapeDtypeStruct(cshape, F32),
                   jax.ShapeDtypeStruct(cshape, F32)],
        compiler_params=_params(1),
        name="na_context",
    )(qkv, qkv, qkv)

    bias = _na_bias_tables(lw["rpb"])
    past = cache_k.shape[3]
    off = t_p // seq_s
    npair = NA_HEADS // 2
    pw = 2 * NA_HD
    pspec = lambda: pl.BlockSpec((1, 1, 2, past, NA_HD), lambda hp, b: (b, j, hp, 0, 0))
    o = pl.pallas_call(
        functools.partial(_na_lat_kernel, seq=seq_s, qb=256),
        grid=(npair, nb_s),
        in_specs=[pl.BlockSpec((seq_s, pw), lambda hp, b: (off + b, hp)),
                  pl.BlockSpec((seq_s, pw), lambda hp, b: (off + b, npair + hp)),
                  pl.BlockSpec((seq_s, pw), lambda hp, b: (off + b, 2 * npair + hp)),
                  pl.BlockSpec((2, 3, 2 * NA_WIN_ROWS, GRID_W, 2 * GRID_W), lambda hp, b: (hp, 0, 0, 0, 0)),
                  pspec(), pspec(), pl.BlockSpec(memory_space=pl.ANY)],
        out_specs=pl.BlockSpec((seq_s, pw), lambda hp, b: (off + b, hp)),
        out_shape=jax.ShapeDtypeStruct((t, D), BF16),
        input_output_aliases={6: 0},
        compiler_params=_params(2),
        name="na_latent",
    )(qkv, qkv, qkv, bias, cache_k, cache_v, o)
    outs = _out_proj(o, lw["w_o"], h, lw, layer, dims)
    return outs, kc, vc


def _gelu_tanh(x):
    return 0.5 * x * (1.0 + jnp.tanh(math.sqrt(2.0 / math.pi) * (x + 0.044715 * (x * x * x))))


def _gmlp_kernel(h_ref, g1, sh1, sc1, gt1, win_ref, gv_ref, ws_ref, bs_ref, wout_ref,
                 g2, sh2, sc2, wr_hi, wr_lo, br, h1_ref, xn_ref, route_ref, m_ref):
    h = h_ref[...]
    a = _normmod(h, g1[...], sh1[0], sc1[0]).astype(BF16)
    u = _gelu_tanh(_dot(a, win_ref[:, :GMLP_DFF]))
    v = _gelu_tanh(_dot(a, win_ref[:, GMLP_DFF:]))
    v = _rms(v, gv_ref[...]).astype(BF16)
    gw = GMLP_DFF // GMLP_GROUPS
    for c in range(TM // GMLP_CHUNK):
        rows = slice(c * GMLP_CHUNK, (c + 1) * GMLP_CHUNK)
        for g in range(GMLP_GROUPS):
            cols = slice(g * gw, (g + 1) * gw)
            vs = _dot(ws_ref[g], v[rows, cols]) + bs_ref[g]
            m_ref[rows, cols] = (u[rows, cols] * vs).astype(BF16)
    o = _dot(m_ref[...], wout_ref[...])
    _finish(h, o, gt1[0], (g2, sh2, sc2, wr_hi, wr_lo, br), h1_ref, xn_ref, route_ref)


def _gmlp_layer(h, lw, layer, dims):
    t = dims["t"]
    row_fn = _tile_row_fn(dims)
    gw = GMLP_DFF // GMLP_GROUPS
    tile = lambda: pl.BlockSpec((TM, D), lambda i: (i, 0))
    one = pl.Buffered(1)
    return pl.pallas_call(
        _gmlp_kernel,
        grid=(t // TM,),
        in_specs=[tile(), _full((1, D))] + [_mod_spec(layer, k, row_fn) for k in (0, 1, 2)]
                 + [pl.BlockSpec((D, 2 * GMLP_DFF), lambda i: (0, 0), pipeline_mode=one),
                    _full((1, GMLP_DFF)), _full((GMLP_GROUPS, GMLP_CHUNK, GMLP_CHUNK)),
                    _full((GMLP_GROUPS, GMLP_CHUNK, gw)),
                    pl.BlockSpec((GMLP_DFF, D), lambda i: (0, 0), pipeline_mode=one)]
                 + _route_specs(layer, row_fn),
        out_specs=_stream_specs(TM, lambda i: i),
        out_shape=_stream_outs(t),
        scratch_shapes=[pltpu.VMEM((TM, GMLP_DFF), BF16)],
        compiler_params=_params(1),
        name="gmlp",
    )(h, lw["g1"], lw["mod"], lw["mod"], lw["mod"], lw["w_in"], lw["g_v"], lw["w_s"], lw["b_s"],
      lw["w_out"], *_route_args(lw))


HALO = 16


def _ssd_conv_kernel(x_ref, bc_ref, cw_ref, cb_ref, *rest, seq):
    o_ref, cat = rest[-2:]
    L = SSD_CHUNK
    nc = seq // L
    c = pl.program_id(1)
    r0 = pl.multiple_of(c * L, L)
    rp = pl.multiple_of(jnp.maximum(r0 - HALO, 0), HALO)
    rn = pl.multiple_of(jnp.minimum(r0 + L, seq - HALO), HALO)
    has_prev = (c > 0).astype(F32)
    has_next = (c < nc - 1).astype(F32)
    for src, lo in ((x_ref, 0), (bc_ref, SSD_INNER)):
        cols = slice(lo, lo + SSD_INNER)
        cat[0:HALO, cols] = src[pl.ds(rp, HALO), :].astype(F32) * has_prev
        cat[HALO:HALO + L, cols] = src[pl.ds(r0, L), :].astype(F32)
        cat[HALO + L:2 * HALO + L, cols] = src[pl.ds(rn, HALO), :].astype(F32) * has_next
    conv = cb_ref[...] + sum(cat[HALO - 2 + k:HALO - 2 + k + L, :] * cw_ref[k:k + 1, :] for k in range(4))
    o_ref[...] = _silu(conv).astype(BF16)


def _ssd_conv(zxbc, lw, dims):
    t, t_p = dims["t"], dims["t_p"]
    L = SSD_CHUNK

    def make_call(seq, nb, off, aliased):
        nc = seq // L
        in_specs = [pl.BlockSpec((seq, SSD_INNER), lambda b, c: (off + b, 1)),
                    pl.BlockSpec((seq, SSD_INNER), lambda b, c: (off + b, 2)),
                    _full((4, 2 * SSD_INNER)), _full((1, 2 * SSD_INNER))]
        if aliased:
            in_specs.append(pl.BlockSpec(memory_space=pl.ANY))
        return pl.pallas_call(
            functools.partial(_ssd_conv_kernel, seq=seq),
            grid=(nb, nc),
            in_specs=in_specs,
            out_specs=pl.BlockSpec((L, 2 * SSD_INNER), lambda b, c: ((off + b) * nc + c, 0)),
            out_shape=jax.ShapeDtypeStruct((t, 2 * SSD_INNER), BF16),
            scratch_shapes=[pltpu.VMEM((L + 2 * HALO, 2 * SSD_INNER), F32)],
            input_output_aliases={4: 0} if aliased else {},
            compiler_params=_params(2),
            name="ssd_conv_seq%d" % seq,
        )

    args = [zxbc, zxbc, lw["conv_w"], lw["conv_b"]]
    xbc = make_call(dims["seq_p"], dims["nb_p"], 0, False)(*args)
    return make_call(dims["seq_s"], dims["nb_s"], t_p // dims["seq_s"], True)(*args, xbc)


def _ssd_scan_kernel(*refs, seq, rev, has_h0, want_state, add_skip):
    (xbc_ref, dt_ref, dtb_ref, alog_ref, dsk_ref, tri_ref, rep_ref) = refs[:7]
    pos = 7
    h0_ref = None
    if has_h0:
        h0_ref = refs[pos]
        pos += 1
    n_alias = len(refs) - pos - (2 if want_state else 1) - 1
    pos += n_alias
    y_ref = refs[pos]
    st_ref = refs[pos + 1] if want_state else None
    state = refs[-1]

    L = SSD_CHUNK
    nc = seq // L
    c = pl.program_id(1)

    @pl.when(c == 0)
    def _():
        if has_h0:
            for i in range(SSD_INNER // L):
                hpb = L // SSD_HD
                blk = h0_ref[0, 0, i * hpb:(i + 1) * hpb].reshape(L, SSD_STATE)
                state[:, i * L:(i + 1) * L] = blk.T
        else:
            state[...] = jnp.zeros_like(state)

    xc = xbc_ref[:, :SSD_INNER].astype(F32)
    bm = xbc_ref[:, SSD_INNER:SSD_INNER + SSD_GROUPS * SSD_STATE]
    cm = xbc_ref[:, SSD_INNER + SSD_GROUPS * SSD_STATE:]

    dtr = dt_ref[...] + dtb_ref[...]
    dt = jnp.maximum(dtr, 0.0) + jnp.log(1.0 + jnp.exp(-jnp.abs(dtr)))
    dta = dt * (-jnp.exp(alog_ref[...]))
    tri = tri_ref[...]
    p = sum(_dot(tri, part) for part in _split3(dta))
    pt = p.T
    edge = 0 if rev else L - 1
    p_edge = p[edge:edge + 1, :]
    rep = rep_ref[...]
    dt_x = _dot(dt.astype(BF16), rep)
    ep_x = _dot(jnp.exp(p).astype(BF16), rep)
    dte_x = _dot(jnp.exp(p_edge - p).astype(BF16), rep)
    cdec_x = _dot(jnp.broadcast_to(jnp.exp(p_edge), (8, p.shape[1])).astype(BF16), rep)[0:1, :]

    dtx = xc * dt_x
    dtxb = dtx.astype(BF16)
    xdte = (dtx * dte_x).astype(BF16)
    li = lax.broadcasted_iota(jnp.int32, (L, L), 0)
    si = lax.broadcasted_iota(jnp.int32, (L, L), 1)
    keep = (li <= si) if rev else (li >= si)
    lane0 = SSD_HEADS if rev else 0
    gw = SSD_HPG * SSD_HD
    ys, new_state = [], []
    for g in range(SSD_GROUPS):
        gcols = slice(g * gw, (g + 1) * gw)
        b_g = bm[:, g * SSD_STATE:(g + 1) * SSD_STATE]
        c_g = cm[:, g * SSD_STATE:(g + 1) * SSD_STATE]
        cb = _dot_nt(c_g, b_g)
        st_prev = state[:, gcols]
        y_g = _dot(c_g, st_prev.astype(BF16)) * ep_x[:, gcols]
        yd = []
        for hh in range(SSD_HPG):
            hl = lane0 + g * SSD_HPG + hh
            seg = p[:, hl:hl + 1] - pt[hl:hl + 1, :]
            mat = cb * jnp.exp(jnp.where(keep, seg, NEG))
            hc = slice((g * SSD_HPG + hh) * SSD_HD, (g * SSD_HPG + hh + 1) * SSD_HD)
            yd.append(_dot(mat.astype(BF16), dtxb[:, hc]))
        y_g = y_g + jnp.concatenate(yd, axis=1)
        if add_skip:
            y_g = y_g + dsk_ref[:, gcols] * xc[:, gcols]
        ys.append(y_g)
        new_state.append(st_prev * cdec_x[:, gcols] + _dot(b_g.astype(F32).T.astype(BF16), xdte[:, gcols]))
    y_ref[...] = jnp.concatenate(ys, axis=1)
    state[...] = jnp.concatenate(new_state, axis=1)

    if want_state:
        @pl.when(c == nc - 1)
        def _():
            for i in range(SSD_INNER // L):
                blk = state[:, i * L:(i + 1) * L].T
                st_ref[0, 0, i * (L // SSD_HD):(i + 1) * (L // SSD_HD)] = blk.reshape(L // SSD_HD, SSD_HD, SSD_STATE)


def _ssd_scan(xbc, dt_raw, lw, h0, j, rev, dims):
    t, t_p, seq_p, seq_s = dims["t"], dims["t_p"], dims["seq_p"], dims["seq_s"]
    nb_p, nb_s = dims["nb_p"], dims["nb_s"]
    L = SSD_CHUNK
    d = 1 if rev else 0
    li = jnp.arange(L)
    tri = ((li[:, None] <= li[None, :]) if rev else (li[:, None] >= li[None, :])).astype(BF16)
    lane = jnp.arange(128)
    col_head = jnp.arange(SSD_INNER) // SSD_HD
    rep = (lane[:, None] == (d * SSD_HEADS + col_head)[None, :]).astype(BF16)
    dsk = jnp.repeat(lw["d_skip"], SSD_HD)[None, :].astype(F32)
    st_shape = (nb_p, 1, SSD_HEADS, SSD_HD, SSD_STATE)

    def make_call(seq, nb, off, has_h0, want_state, n_alias):
        nc = seq // L
        chunk = (lambda b, c: (off * nc + b * nc + (nc - 1 - c), 0)) if rev else (lambda b, c: (off * nc + b * nc + c, 0))
        in_specs = [pl.BlockSpec((L, 2 * SSD_INNER), chunk), pl.BlockSpec((L, 128), chunk),
                    _full((1, 128)), _full((1, 128)),
                    _full((1, SSD_INNER)), _full((L, L)), _full((128, SSD_INNER))]
        if has_h0:
            in_specs.append(pl.BlockSpec((1, 1, SSD_HEADS, SSD_HD, SSD_STATE), lambda b, c: (b, j, 0, 0, 0)))
        aliases = {}
        if n_alias:
            aliases = {len(in_specs): 0}
            in_specs.append(pl.BlockSpec(memory_space=pl.ANY))
        out_specs = [pl.BlockSpec((L, SSD_INNER), chunk)]
        out_shape = [jax.ShapeDtypeStruct((t, SSD_INNER), F32)]
        if want_state:
            out_specs.append(pl.BlockSpec((1, 1, SSD_HEADS, SSD_HD, SSD_STATE), lambda b, c: (b, 0, 0, 0, 0)))
            out_shape.append(jax.ShapeDtypeStruct(st_shape, F32))
        return pl.pallas_call(
            functools.partial(_ssd_scan_kernel, seq=seq, rev=rev, has_h0=has_h0, want_state=want_state,
                              add_skip=not rev),
            grid=(nb, nc),
            in_specs=in_specs,
            out_specs=out_specs,
            out_shape=out_shape,
            scratch_shapes=[pltpu.VMEM((SSD_STATE, SSD_INNER), F32)],
            input_output_aliases=aliases,
            compiler_params=_params(2),
            name="ssd_scan_%s_seq%d" % ("bwd" if rev else "fwd", seq),
        )

    common = [xbc, dt_raw, lw["dt_bias"], lw["a_log"], dsk, tri, rep]
    y, st = make_call(seq_p, nb_p, 0, False, True, 0)(*common)
    (y,) = make_call(seq_s, nb_s, t_p // seq_s, True, False, 1)(*common, h0, y)
    return y, st


def _ssd_out_kernel(yf_ref, yb_ref, z_ref, gn_ref, w_ref, h_ref, gt1, g2, sh2, sc2, wr_hi, wr_lo, br,
                    h1_ref, xn_ref, route_ref):
    y = (yf_ref[...] + yb_ref[...]) * _silu(z_ref[...].astype(F32))
    o = _dot(_rms(y, gn_ref[...]).astype(BF16), w_ref[...])
    _finish(h_ref[...], o, gt1[0], (g2, sh2, sc2, wr_hi, wr_lo, br), h1_ref, xn_ref, route_ref)


def _ssd_layer(h, state_f, state_b, j, lw, layer, dims):
    t = dims["t"]
    zxbc = _pre_proj(h, lw, layer, lw["w_in"], 1536, BF16, dims, n=3 * SSD_INNER)
    dt_raw = _pre_proj(h, lw, layer, lw["w_dt"], 128, F32, dims)
    xbc = _ssd_conv(zxbc, lw, dims)
    y_f, st_f = _ssd_scan(xbc, dt_raw, lw, state_f, j, False, dims)
    y_b, st_b = _ssd_scan(xbc, dt_raw, lw, state_b, j, True, dims)
    row_fn = _tile_row_fn(dims)
    tile = lambda: pl.BlockSpec((TM, D), lambda i: (i, 0))
    wide = lambda: pl.BlockSpec((TM, SSD_INNER), lambda i: (i, 0))
    outs = pl.pallas_call(
        _ssd_out_kernel,
        grid=(t // TM,),
        in_specs=[wide(), wide(), wide(), _full((1, SSD_INNER)), _full((SSD_INNER, D)), tile(),
                  _mod_spec(layer, 2, row_fn)] + _route_specs(layer, row_fn),
        out_specs=_stream_specs(TM, lambda i: i),
        out_shape=_stream_outs(t),
        compiler_params=_params(1),
        name="ssd_out",
    )(y_f, y_b, zxbc, lw["g_norm"], lw["w_out"], h, lw["mod"], *_route_args(lw))
    return outs, st_f, st_b


TS = 256
SRC = 512


def _moe_plan(route, t):
    i32 = jnp.int32
    npair = len(PAIR_ORDER)
    ncls = MOE_GROUPS * npair
    key = route[:, 0].astype(i32)
    oh = (key[:, None] == jnp.arange(ncls, dtype=i32)[None, :]).astype(i32)
    csum = jnp.cumsum(oh, axis=0)
    ccnt = csum[-1]
    cnt = ccnt.reshape(MOE_GROUPS, npair).sum(axis=1)
    padded = ((cnt + TS - 1) // TS) * TS
    gend = jnp.cumsum(padded)
    in_grp = jnp.cumsum(ccnt.reshape(MOE_GROUPS, npair), axis=1) - ccnt.reshape(MOE_GROUPS, npair)
    cstart = ((gend - padded)[:, None] + in_grp).reshape(ncls)
    pos = jnp.sum(oh * (csum - 1 + cstart[None, :]), axis=1)
    n_tiles = t // TS + MOE_GROUPS
    tile0 = jnp.arange(n_tiles, dtype=i32) * TS
    tile_grp = jnp.sum((tile0[:, None] >= gend[None, :]).astype(i32), axis=1)
    last_grp = jnp.max(jnp.where(cnt > 0, jnp.arange(MOE_GROUPS, dtype=i32), 0))
    tile_grp = jnp.minimum(tile_grp, last_grp)
    n_used = (gend[-1] // TS).reshape(1)
    touch = ((cstart[None, :] < tile0[:, None] + TS) & (cstart[None, :] + ccnt[None, :] > tile0[:, None])
             & (ccnt[None, :] > 0))
    uses = jnp.array([[int(e in PAIR_ORDER[c % npair]) for e in range(MOE_EPG)] for c in range(ncls)], i32)
    need = (jnp.dot(touch.astype(i32), uses) > 0).astype(i32).reshape(-1)
    return dict(pos=pos, tile_grp=tile_grp, n_used=n_used, need=need, n_tiles=n_tiles)


ROW_UNROLL = 8


def _moe_scatter_kernel(pos_ref, x_ref, xs_ref):
    n = pl.program_id(0)

    @pl.when(n == 0)
    def _():
        xs_ref[...] = jnp.zeros_like(xs_ref)

    def body(jj, carry):
        for r in range(ROW_UNROLL):
            j = jj * ROW_UNROLL + r
            xs_ref[pl.ds(pos_ref[n * SRC + j], 1), :] = x_ref[pl.ds(j, 1), :]
        return carry

    lax.fori_loop(0, SRC // ROW_UNROLL, body, 0)


def _moe_expert_kernel(grp_ref, nused_ref, need_ref, xs_ref, wg_ref, wu_ref, wd_ref, y_ref,
                       wg_b, wu_b, wd_b, acc_ref):
    i = pl.program_id(0)
    used = i < nused_ref[0]
    new_group = (i == 0) | (grp_ref[i] != grp_ref[jnp.maximum(i - 1, 0)])

    @pl.when(used & new_group)
    def _():
        for e in range(MOE_EPG):
            wg_b[e] = wg_ref[0, e].astype(BF16)
            wu_b[e] = wu_ref[0, e].astype(BF16)
            wd_b[e] = wd_ref[0, e].astype(BF16)

    @pl.when(used)
    def _():
        lo, hi = _unpack_pairs(xs_ref[...])
        x = jnp.concatenate([lo, hi[:, :D - PW]], axis=1).astype(BF16)
        rec = hi[:, D - PW:XW - PW]
        lane = lax.broadcasted_iota(jnp.int32, rec.shape, 1)
        acc_ref[...] = jnp.zeros_like(acc_ref)
        for e in range(MOE_EPG):
            @pl.when(need_ref[i * MOE_EPG + e] > 0)
            def _():
                hg = _dot(x, wg_b[e])
                hu = _dot(x, wu_b[e])
                mine = (lane % MOE_EPG == e) & (lane < 3 * MOE_EPG)
                cw = jnp.sum(jnp.where(mine, rec, 0.0), axis=-1, keepdims=True)
                acc_ref[...] += _dot((_silu(hg) * hu * cw).astype(BF16), wd_b[e])

        acc = acc_ref[...].astype(BF16).astype(F32)
        y_ref[...] = _pack_pairs(acc[:, :D // 2], acc[:, D // 2:])

    @pl.when(i >= nused_ref[0])
    def _():
        y_ref[...] = jnp.zeros_like(y_ref)


def _moe_ungather_kernel(pos_ref, ys_ref, h1_ref, gt2, fg_ref, *rest, n_prompt_tiles):
    rows_ref = rest[-1]
    out_refs = rest[:-1]
    n = pl.program_id(0)

    def body(jj, carry):
        for r in range(ROW_UNROLL):
            j = jj * ROW_UNROLL + r
            rows_ref[pl.ds(j, 1), :] = ys_ref[pl.ds(pos_ref[n * SRC + j], 1), :]
        return carry

    lax.fori_loop(0, SRC // ROW_UNROLL, body, 0)
    h2 = h1_ref[...] + gt2[0] * jnp.concatenate(_unpack_pairs(rows_ref[...]), axis=1)
    if n_prompt_tiles is None:
        out_refs[0][...] = h2
    else:
        y = _rms(h2, fg_ref[...])

        @pl.when(n < n_prompt_tiles)
        def _():
            out_refs[0][...] = y

        @pl.when(n >= n_prompt_tiles)
        def _():
            out_refs[1][...] = y


def _moe_sparse(h1, xn, route, lw, layer, final_g, final, dims):
    t = dims["t"]
    plan = _moe_plan(route, t)
    n_tiles = plan["n_tiles"]
    n_rows = n_tiles * TS
    one = pl.Buffered(1)
    xs = pl.pallas_call(
        _moe_scatter_kernel,
        grid_spec=pltpu.PrefetchScalarGridSpec(
            num_scalar_prefetch=1,
            grid=(t // SRC,),
            in_specs=[pl.BlockSpec((SRC, PW), lambda n, pos: (n, 0))],
            out_specs=pl.BlockSpec((n_rows, PW), lambda n, pos: (0, 0))),
        out_shape=jax.ShapeDtypeStruct((n_rows, PW), U32),
        compiler_params=_params(1),
        name="moe_scatter_l%d" % layer,
    )(plan["pos"], xn)

    ex = lambda i, g, nu, nd: (layer, g[i], 0, 0)
    ys = pl.pallas_call(
        _moe_expert_kernel,
        grid_spec=pltpu.PrefetchScalarGridSpec(
            num_scalar_prefetch=3,
            grid=(n_tiles,),
            in_specs=[pl.BlockSpec((TS, PW), lambda i, g, nu, nd: (i, 0)),
                      pl.BlockSpec((1, MOE_EPG, D, MOE_DFF), ex, pipeline_mode=one),
                      pl.BlockSpec((1, MOE_EPG, D, MOE_DFF), ex, pipeline_mode=one),
                      pl.BlockSpec((1, MOE_EPG, MOE_DFF, D), ex, pipeline_mode=one)],
            out_specs=pl.BlockSpec((TS, D // 2), lambda i, g, nu, nd: (i, 0)),
            scratch_shapes=[pltpu.VMEM((MOE_EPG, D, MOE_DFF), BF16), pltpu.VMEM((MOE_EPG, D, MOE_DFF), BF16),
                            pltpu.VMEM((MOE_EPG, MOE_DFF, D), BF16), pltpu.VMEM((TS, D), F32)]),
        out_shape=jax.ShapeDtypeStruct((n_rows, D // 2), U32),
        compiler_params=_params(1),
        name="moe_expert_l%d" % layer,
    )(plan["tile_grp"], plan["n_used"], plan["need"], xs, lw["w_gate"], lw["w_up"], lw["w_down"])

    row_fn = _tile_row_fn(dims, SRC)
    tile = lambda: pl.BlockSpec((SRC, D), lambda n, pos: (n, 0))
    if final:
        npt = dims["t_p"] // SRC
        out_specs = [pl.BlockSpec((SRC, D), lambda n, pos: (jnp.minimum(n, npt - 1), 0)),
                     pl.BlockSpec((SRC, D), lambda n, pos: (jnp.maximum(n - npt, 0), 0))]
        out_shape = [jax.ShapeDtypeStruct((dims["t_p"], D), F32), jax.ShapeDtypeStruct((t - dims["t_p"], D), F32)]
    else:
        npt = None
        out_specs = [tile()]
        out_shape = [jax.ShapeDtypeStruct((t, D), F32)]
    return pl.pallas_call(
        functools.partial(_moe_ungather_kernel, n_prompt_tiles=npt),
        grid_spec=pltpu.PrefetchScalarGridSpec(
            num_scalar_prefetch=1,
            grid=(t // SRC,),
            in_specs=[pl.BlockSpec((n_rows, D // 2), lambda n, pos: (0, 0)), tile(),
                      pl.BlockSpec((1, 1, D), lambda n, pos: ((layer * MOD_ROWS + row_fn(n)) * N_MOD + 5, 0, 0)),
                      pl.BlockSpec((1, D), lambda n, pos: (0, 0))],
            out_specs=out_specs,
            scratch_shapes=[pltpu.VMEM((SRC, D // 2), U32)]),
        out_shape=out_shape,
        compiler_params=_params(1),
        name="moe_ungather_l%d" % layer,
    )(plan["pos"], ys, h1, lw["mod"], final_g)


def kernel(x_prompt, x_sample, cache_k, cache_v, state_ssm_fwd, state_ssm_bwd, c, c_ctx, ada_w, ada_b, norm1_g, norm2_g, final_g, fnet_w_o, na_w_qkv, na_w_o, na_rpb, gmlp_w_in, gmlp_g_v, gmlp_w_s, gmlp_b_s, gmlp_w_out, ssd_w_in, ssd_conv_w, ssd_conv_b, ssd_a_log, ssd_dt_bias, ssd_d_skip, ssd_g_norm, ssd_w_out, moe_w_gr, moe_b_gr, moe_w_er, moe_b_er, moe_w_gate, moe_w_up, moe_w_down):
    nb_p, seq_p, _ = x_prompt.shape
    nb_s, seq_s, _ = x_sample.shape
    depth = ada_w.shape[0]
    t_p, t_s = nb_p * seq_p, nb_s * seq_s
    dims = dict(t=t_p + t_s, t_p=t_p, seq_p=seq_p, seq_s=seq_s, nb_p=nb_p, nb_s=nb_s)
    assert 1 + nb_s <= MOD_ROWS and t_p % seq_s == 0 and t_p % TM == 0 and seq_s % TM == 0

    cond = jnp.zeros((MOD_ROWS, D), F32).at[0].set(c_ctx).at[1:1 + nb_s].set(c)
    mod = _ada_table(cond, ada_w, ada_b)
    h = (x_prompt.reshape(t_p, D), x_sample.reshape(t_s, D))
    fg = final_g.reshape(1, D)

    new_k, new_v, new_sf, new_sb = [], [], [], []
    for l in range(depth):
        kind, j = l % 4, l // 4
        w_r = jnp.concatenate([moe_w_gr[l], moe_w_er[l]], axis=1)
        w_r = jnp.pad(w_r, ((0, 0), (0, ROUTE_W - MOE_GROUPS - MOE_EXPERTS)))
        b_r = jnp.pad(jnp.concatenate([moe_b_gr[l], moe_b_er[l]]), (0, ROUTE_W - MOE_GROUPS - MOE_EXPERTS))
        wr_hi = w_r.astype(BF16)
        lw = dict(mod=mod, g1=norm1_g[l].reshape(1, D), g2=norm2_g[l].reshape(1, D),
                  wr_hi=wr_hi, wr_lo=(w_r - wr_hi.astype(F32)).astype(BF16), br=b_r.reshape(1, ROUTE_W),
                  w_gate=moe_w_gate, w_up=moe_w_up, w_down=moe_w_down)
        if kind == 0:
            lw.update(w_o=fnet_w_o[j].astype(BF16))
            h1, xn, route = _fnet_layer(h, lw, l, dims)
        elif kind == 1:
            lw.update(w_qkv=na_w_qkv[j], w_o=na_w_o[j].astype(BF16), rpb=na_rpb[j])
            (h1, xn, route), kc, vc = _na_layer(h, cache_k, cache_v, j, lw, l, dims)
            new_k.append(kc)
            new_v.append(vc)
        elif kind == 2:
            gw = GMLP_DFF // GMLP_GROUPS
            lw.update(w_in=gmlp_w_in[j].astype(BF16), g_v=gmlp_g_v[j].reshape(1, GMLP_DFF),
                      w_s=gmlp_w_s[j].astype(BF16),
                      b_s=jnp.broadcast_to(gmlp_b_s[j][:, :, None], (GMLP_GROUPS, GMLP_CHUNK, gw)),
                      w_out=gmlp_w_out[j].astype(BF16))
            h1, xn, route = _gmlp_layer(h, lw, l, dims)
        else:
            n_main = 3 * SSD_INNER
            w_in = ssd_w_in[j]
            pad = lambda v: jnp.pad(v, ((0, 0), (0, 128 - 2 * SSD_HEADS)))
            lw.update(w_in=w_in, w_dt=pad(w_in[:, n_main:]).astype(BF16),
                      conv_w=ssd_conv_w[j], conv_b=ssd_conv_b[j].reshape(1, -1),
                      dt_bias=pad(ssd_dt_bias[j].reshape(1, -1)), a_log=pad(ssd_a_log[j].reshape(1, -1)),
                      d_skip=ssd_d_skip[j], g_norm=ssd_g_norm[j].reshape(1, SSD_INNER),
                      w_out=ssd_w_out[j].astype(BF16))
            (h1, xn, route), sf, sb = _ssd_layer(h, state_ssm_fwd, state_ssm_bwd, j, lw, l, dims)
            new_sf.append(sf)
            new_sb.append(sb)
        out = _moe_sparse(h1, xn, route, lw, l, fg, l == depth - 1, dims)
        h = out if l == depth - 1 else out[0]

    y_prompt = h[0].reshape(nb_p, seq_p, D)
    y_sample = h[1].reshape(nb_s, seq_s, D)
    cat = lambda xs: jnp.concatenate(xs, axis=1)
    return (y_prompt, y_sample, cat(new_k), cat(new_v), cat(new_sf), cat(new_sb))
```

```python
import functools
import math

import jax
import jax.numpy as jnp
from jax import lax
from jax.experimental import pallas as pl
from jax.experimental.pallas import tpu as pltpu

F32 = jnp.float32
BF16 = jnp.bfloat16

D = 1024
EPS = 1e-6
NEG = -1e30
N_MOD = 6
MOD_ROWS = 8
GRID_W = 64
FNET_GROUPS = 8
NA_HEADS = 16
NA_HD = 64
NA_WIN_ROWS = 8
NA_WIN_COLS = 16
GMLP_CHUNK = 128
GMLP_DFF = 2048
GMLP_GROUPS = 8
SSD_INNER = 2048
SSD_HD = 64
SSD_HEADS = 32
SSD_GROUPS = 8
SSD_STATE = 128
SSD_CHUNK = 128
SSD_HPG = SSD_HEADS // SSD_GROUPS
MOE_GROUPS = 4
MOE_EPG = 4
MOE_EXPERTS = 16
MOE_DFF = 512
ROUTE_W = 128
ROUTE_E0 = 4
PAIR_ORDER = ((0, 1), (0, 2), (0, 3), (1, 3), (1, 2), (2, 3))

TM = 512
VMEM_LIMIT = 56 * 1024 * 1024


def _dot(a, b):
    return jnp.dot(a, b, preferred_element_type=F32)


def _dot_nt(a, b):
    return lax.dot_general(a, b, (((1,), (1,)), ((), ())), preferred_element_type=F32)


def _silu(x):
    hx = 0.5 * x
    return hx * (1.0 + jnp.tanh(hx))


def _rms(x, g):
    return x * lax.rsqrt(jnp.mean(x * x, axis=-1, keepdims=True) + EPS) * g


def _normmod(x, g, shift, scale):
    return _rms(x, g) * (1.0 + scale) + shift


def _split3(x):
    hi = x.astype(BF16)
    r = x - hi.astype(F32)
    mid = r.astype(BF16)
    lo = (r - mid.astype(F32)).astype(BF16)
    return hi, mid, lo


def _params(n_axes):
    return pltpu.CompilerParams(dimension_semantics=("arbitrary",) * n_axes,
                                vmem_limit_bytes=VMEM_LIMIT)


def _full(shape):
    nd = len(shape)
    return pl.BlockSpec(shape, lambda *_: (0,) * nd)


def _mod_spec(layer, k, row_fn):
    return pl.BlockSpec((1, 1, D), lambda *idx: ((layer * MOD_ROWS + row_fn(*idx)) * N_MOD + k, 0, 0))


def _ada_kernel(c_ref, w_ref, b_ref, o_ref):
    c = c_ref[...]
    o_ref[0] = _dot(_silu(c).astype(BF16), w_ref[0].astype(BF16)) + b_ref[0]


def _ada_table(cond, ada_w, ada_b):
    depth = ada_w.shape[0]
    n = N_MOD * D
    tn = 1536
    out = pl.pallas_call(
        _ada_kernel,
        grid=(depth, n // tn),
        in_specs=[_full((MOD_ROWS, D)),
                  pl.BlockSpec((1, D, tn), lambda l, j: (l, 0, j)),
                  pl.BlockSpec((1, 1, tn), lambda l, j: (l, 0, j))],
        out_specs=pl.BlockSpec((1, MOD_ROWS, tn), lambda l, j: (l, 0, j)),
        out_shape=jax.ShapeDtypeStruct((depth, MOD_ROWS, n), F32),
        compiler_params=_params(2),
        name="ada_table",
    )(cond, ada_w, ada_b.reshape(depth, 1, n))
    return out.reshape(depth * MOD_ROWS * N_MOD, 1, D)


def _route(h1, g2, sh2, sc2, wr_hi, wr_lo, br):
    xn = _normmod(h1, g2, sh2, sc2)
    xh = xn.astype(BF16)
    xl = (xn - xh.astype(F32)).astype(BF16)
    logits = _dot(xh, wr_hi) + _dot(xh, wr_lo) + _dot(xl, wr_hi) + br
    lane = lax.broadcasted_iota(jnp.int32, logits.shape, 1).astype(F32)
    far = float(ROUTE_W)
    gl = jnp.where(lane < MOE_GROUPS, logits, NEG)
    gmax = jnp.max(gl, axis=-1, keepdims=True)
    g_p = 1.0 / jnp.sum(jnp.exp(gl - gmax), axis=-1, keepdims=True)
    gidx = jnp.min(jnp.where(gl == gmax, lane, far), axis=-1, keepdims=True)
    lo = ROUTE_E0 + MOE_EPG * gidx
    el = jnp.where((lane >= lo) & (lane < lo + MOE_EPG), logits, NEG)
    m1 = jnp.max(el, axis=-1, keepdims=True)
    i1 = jnp.min(jnp.where(el == m1, lane, far), axis=-1, keepdims=True)
    el2 = jnp.where(lane == i1, NEG, el)
    m2 = jnp.max(el2, axis=-1, keepdims=True)
    i2 = jnp.min(jnp.where(el2 == m2, lane, far), axis=-1, keepdims=True)
    e2 = jnp.exp(m2 - m1)
    w1 = g_p / (1.0 + e2)
    w2 = w1 * e2
    rec = jnp.zeros_like(logits)
    for part, (a, b) in enumerate(zip(_split3(w1), _split3(w2))):
        shift = part * MOE_EPG - lo
        rec = (rec + jnp.where(lane == i1 + shift, a.astype(F32), 0.0)
               + jnp.where(lane == i2 + shift, b.astype(F32), 0.0))
    ea = jnp.minimum(i1, i2) - lo
    eb = jnp.maximum(i1, i2) - lo
    pair = sum(jnp.where((ea == a) & (eb == b), float(k), 0.0) for k, (a, b) in enumerate(PAIR_ORDER))
    key = gidx * float(len(PAIR_ORDER)) + pair
    return xh, rec.astype(BF16), jnp.broadcast_to(key, logits.shape)


def _route_specs(layer, row_fn):
    return [_full((1, D)), _mod_spec(layer, 3, row_fn), _mod_spec(layer, 4, row_fn),
            _full((D, ROUTE_W)), _full((D, ROUTE_W)), _full((1, ROUTE_W))]


def _route_args(lw):
    return [lw["g2"], lw["mod"], lw["mod"], lw["wr_hi"], lw["wr_lo"], lw["br"]]


def _finish(h, o, gate, rt_refs, h1_ref, xn_ref, route_ref):
    g2, sh2, sc2, wr_hi, wr_lo, br = rt_refs
    h1 = h + gate * o
    h1_ref[...] = h1
    xn, rec, gid = _route(h1, g2[...], sh2[0], sc2[0], wr_hi[...], wr_lo[...], br[...])
    row = jnp.concatenate([xn.astype(F32), rec.astype(F32), jnp.zeros((h.shape[0], 2 * PW - XW), F32)], axis=1)
    xn_ref[...] = _pack_pairs(row[:, :PW], row[:, PW:])
    route_ref[...] = gid


XW = D + ROUTE_W
PW = 640
U32 = jnp.uint32


def _pack_pairs(lo, hi):
    lo_bits = lax.bitcast_convert_type(lo, U32) >> 16
    hi_bits = lax.bitcast_convert_type(hi, U32) & U32(0xFFFF0000)
    return lo_bits | hi_bits


def _unpack_pairs(w):
    return lax.bitcast_convert_type(w << 16, F32), lax.bitcast_convert_type(w & U32(0xFFFF0000), F32)


def _stream_outs(t):
    return [jax.ShapeDtypeStruct((t, D), F32), jax.ShapeDtypeStruct((t, PW), U32),
            jax.ShapeDtypeStruct((t, ROUTE_W), F32)]


def _stream_specs(rows, row_block):
    return [pl.BlockSpec((rows, w), lambda *idx: (row_block(*idx), 0)) for w in (D, PW, ROUTE_W)]


def _fnet_kernel(h_ref, g1, sh1, sc1, gt1, csc_ref, fs_ref, wo_ref, g2, sh2, sc2, wr_hi, wr_lo, br,
                 *rest, seq):
    h1_ref, xn_ref, route_ref, ab_ref = rest[-4:]
    h = h_ref[...]
    a = _normmod(h, g1[...], sh1[0], sc1[0]).astype(BF16)
    gd = D // FNET_GROUPS
    for g in range(FNET_GROUPS):
        ab = _dot(a[:, g * gd:(g + 1) * gd], csc_ref[...])
        ab_ref[0:seq, g * gd:(g + 1) * gd] = ab[:, :gd].astype(BF16)
        ab_ref[seq:2 * seq, g * gd:(g + 1) * gd] = ab[:, gd:].astype(BF16)
    f = _dot(fs_ref[...], ab_ref[...])
    o = _dot(f.astype(BF16), wo_ref[...])
    _finish(h, o, gt1[0], (g2, sh2, sc2, wr_hi, wr_lo, br), h1_ref, xn_ref, route_ref)


def _dft_tables(n):
    k = jnp.arange(n, dtype=jnp.int32)
    ang = ((k[:, None] * k[None, :]) % n).astype(F32) * (2.0 * math.pi / n)
    s = 1.0 / math.sqrt(n)
    return jnp.cos(ang) * s, jnp.sin(ang) * s


def _fnet_layer(h, lw, layer, dims):
    t, t_p = dims["t"], dims["t_p"]
    gd = D // FNET_GROUPS
    cc, sc = _dft_tables(gd)
    csc = jnp.concatenate([cc, sc], axis=1).astype(BF16)

    split_in = isinstance(h, tuple)

    def make_call(seq, nb, off, n_alias):
        cs, ss = _dft_tables(seq)
        fs = jnp.concatenate([cs, -ss], axis=1).astype(BF16)
        row_fn = (lambda b: 0) if off == 0 else (lambda b: 1 + b)
        in_off = 0 if split_in else off
        tile = lambda: pl.BlockSpec((seq, D), lambda b: (off + b, 0))
        in_specs = ([pl.BlockSpec((seq, D), lambda b: (in_off + b, 0)), _full((1, D))]
                    + [_mod_spec(layer, k, row_fn) for k in (0, 1, 2)]
                    + [_full((gd, 2 * gd)), _full((seq, 2 * seq)), _full((D, D))]
                    + _route_specs(layer, row_fn))
        aliases = {}
        if n_alias:
            base = len(in_specs)
            in_specs = in_specs + [pl.BlockSpec(memory_space=pl.ANY)] * n_alias
            aliases = {base + i: i for i in range(n_alias)}
        call = pl.pallas_call(
            functools.partial(_fnet_kernel, seq=seq),
            grid=(nb,),
            in_specs=in_specs,
            out_specs=_stream_specs(seq, lambda b: off + b),
            out_shape=_stream_outs(t),
            scratch_shapes=[pltpu.VMEM((2 * seq, D), BF16)],
            input_output_aliases=aliases,
            compiler_params=_params(1),
            name="fnet_seq%d" % seq,
        )
        return lambda *a: call(*a[:2], *a[2:5], a[5], fs, *a[6:])

    h_p, h_s = h if split_in else (h, h)
    args = [lw["g1"], lw["mod"], lw["mod"], lw["mod"], csc, lw["w_o"]] + _route_args(lw)
    outs = make_call(dims["seq_p"], dims["nb_p"], 0, None)(h_p, *args)
    return make_call(dims["seq_s"], dims["nb_s"], t_p // dims["seq_s"], 3)(h_s, *args, *outs)


def _pre_kernel(h_ref, g1, sh1, sc1, w_ref, o_ref, a_ref):
    @pl.when(pl.program_id(1) == 0)
    def _():
        a_ref[...] = _normmod(h_ref[...], g1[...], sh1[0], sc1[0]).astype(BF16)

    o_ref[...] = _dot(a_ref[...], w_ref[...].astype(BF16)).astype(o_ref.dtype)


def _tile_row_fn(dims, tm=TM):
    npt = dims["t_p"] // tm
    tps = dims["seq_s"] // tm
    return lambda i, *_: jnp.where(i < npt, 0, 1 + (i - npt) // tps)


def _pre_proj(h, lw, layer, w, tn, out_dtype, dims, n=None):
    t = dims["t"]
    n = w.shape[1] if n is None else n
    tm = dims["seq_s"]
    row_fn = _tile_row_fn(dims, tm)
    return pl.pallas_call(
        _pre_kernel,
        grid=(t // tm, n // tn),
        in_specs=[pl.BlockSpec((tm, D), lambda i, j: (i, 0)), _full((1, D)),
                  _mod_spec(layer, 0, row_fn), _mod_spec(layer, 1, row_fn),
                  pl.BlockSpec((D, tn), lambda i, j: (0, j))],
        out_specs=pl.BlockSpec((tm, tn), lambda i, j: (i, j)),
        out_shape=jax.ShapeDtypeStruct((t, n), out_dtype),
        scratch_shapes=[pltpu.VMEM((tm, D), BF16)],
        compiler_params=_params(2),
        name="pre_proj_l%d_n%d" % (layer, n),
    )(h, lw["g1"], lw["mod"], lw["mod"], w)


def _out_kernel(o_ref, w_ref, h_ref, gt1, g2, sh2, sc2, wr_hi, wr_lo, br, h1_ref, xn_ref, route_ref):
    o = _dot(o_ref[...], w_ref[...])
    _finish(h_ref[...], o, gt1[0], (g2, sh2, sc2, wr_hi, wr_lo, br), h1_ref, xn_ref, route_ref)


def _out_proj(o, w, h, lw, layer, dims):
    t = dims["t"]
    k = o.shape[1]
    row_fn = _tile_row_fn(dims)
    tile = lambda: pl.BlockSpec((TM, D), lambda i: (i, 0))
    return pl.pallas_call(
        _out_kernel,
        grid=(t // TM,),
        in_specs=[pl.BlockSpec((TM, k), lambda i: (i, 0)), _full((k, D)), tile(),
                  _mod_spec(layer, 2, row_fn)] + _route_specs(layer, row_fn),
        out_specs=_stream_specs(TM, lambda i: i),
        out_shape=_stream_outs(t),
        compiler_params=_params(1),
        name="out_proj_l%d" % layer,
    )(o, w, h, lw["mod"], *_route_args(lw))


NA_SCALE = NA_HD ** -0.5


def _na_ctx_kernel(q_ref, k_ref, v_ref, o_ref, kc_ref, vc_ref):
    outs = []
    for hd in range(NA_HEADS):
        sl = slice(hd * NA_HD, (hd + 1) * NA_HD)
        q = q_ref[:, sl] * NA_SCALE
        k = k_ref[:, sl]
        v = v_ref[:, sl]
        s = _dot_nt(q, k)
        p = jnp.exp(s - jnp.max(s, axis=-1, keepdims=True))
        l = jnp.sum(p, axis=-1, keepdims=True)
        outs.append((_dot(p.astype(BF16), v) / l).astype(BF16))
        kc_ref[0, 0, hd] = k.astype(F32)
        vc_ref[0, 0, hd] = v.astype(F32)
    o_ref[...] = jnp.concatenate(outs, axis=1)


def _na_row_start(qr, rows):
    kr = min(NA_WIN_ROWS, rows)
    return min(max(qr - kr // 2, 0), rows - kr)


def _na_window_bias(bias_ref, hh, qr, rows, m_lo, m_hi):
    kr = min(NA_WIN_ROWS, rows)
    rs = _na_row_start(qr, rows)
    blocks = []
    for m in range(m_lo, m_hi):
        ok0 = rs <= 2 * m < rs + kr
        ok1 = rs <= 2 * m + 1 < rs + kr
        e = 2 * m - qr + NA_WIN_ROWS
        if ok0 and ok1:
            blocks.append(bias_ref[hh, 0, e])
        elif ok1:
            blocks.append(bias_ref[hh, 1, e])
        elif ok0:
            blocks.append(bias_ref[hh, 2, e])
        else:
            blocks.append(jnp.full((GRID_W, 2 * GRID_W), NEG, F32))
    return jnp.concatenate(blocks, axis=1)


def _na_lat_kernel(q_ref, k_ref, v_ref, bias_ref, kc_ref, vc_ref, o_in, o_ref, *, seq, qb):
    del o_in
    rows = seq // GRID_W
    for hh in range(2):
        sl = slice(hh * NA_HD, (hh + 1) * NA_HD)
        k = k_ref[:, sl]
        v = v_ref[:, sl]
        kc = kc_ref[0, 0, hh].astype(BF16)
        vc = vc_ref[0, 0, hh].astype(BF16)
        for b0 in range(0, seq, qb):
            q = q_ref[b0:b0 + qb, sl] * NA_SCALE
            qrs = range(b0 // GRID_W, (b0 + qb) // GRID_W)
            m_lo = _na_row_start(qrs[0], rows) // 2
            m_hi = (_na_row_start(qrs[-1], rows) + min(NA_WIN_ROWS, rows) + 1) // 2
            keys = slice(m_lo * 2 * GRID_W, m_hi * 2 * GRID_W)
            bias = jnp.concatenate([_na_window_bias(bias_ref, hh, qr, rows, m_lo, m_hi) for qr in qrs], axis=0)
            s1 = _dot_nt(q, k[keys]) + bias
            s2 = _dot_nt(q, kc)
            m = jnp.maximum(jnp.max(s1, axis=-1, keepdims=True), jnp.max(s2, axis=-1, keepdims=True))
            p1 = jnp.exp(s1 - m)
            p2 = jnp.exp(s2 - m)
            l = jnp.sum(p1, axis=-1, keepdims=True) + jnp.sum(p2, axis=-1, keepdims=True)
            o = (_dot(p1.astype(BF16), v[keys]) + _dot(p2.astype(BF16), vc)) / l
            o_ref[b0:b0 + qb, sl] = o.astype(BF16)


def _na_bias_tables(rpb):
    c = jnp.arange(GRID_W)
    win0 = jnp.clip(c - NA_WIN_COLS // 2, 0, GRID_W - NA_WIN_COLS)
    ok_c = (c[None, :] >= win0[:, None]) & (c[None, :] < win0[:, None] + NA_WIN_COLS)
    dc = jnp.clip(c[None, :] - c[:, None], 1 - NA_WIN_COLS, NA_WIN_COLS - 1) + NA_WIN_COLS - 1
    nh, ndr, ndc = rpb.shape
    pick = (dc.reshape(1, -1) == jnp.arange(ndc)[:, None]).astype(F32)
    cm = jnp.dot(rpb.reshape(nh * ndr, ndc), pick, precision=lax.Precision.HIGHEST)
    cm = jnp.where(ok_c[None, None], cm.reshape(nh, ndr, GRID_W, GRID_W), NEG)
    neg = jnp.full_like(cm[:, :1], NEG)
    ext = jnp.concatenate([neg, cm, neg], axis=1)
    a, b = ext[:, :-1], ext[:, 1:]
    negs = jnp.full_like(a, NEG)
    pair = lambda x, y: jnp.concatenate([x, y], axis=-1)
    return jnp.stack([pair(a, b), pair(negs, b), pair(a, negs)], axis=1)


def _na_layer(h, cache_k, cache_v, j, lw, layer, dims):
    t, t_p, seq_p, seq_s = dims["t"], dims["t_p"], dims["seq_p"], dims["seq_s"]
    nb_p, nb_s = dims["nb_p"], dims["nb_s"]
    qkv = _pre_proj(h, lw, layer, lw["w_qkv"], 1536, BF16, dims)
    cshape = (nb_p, 1, NA_HEADS, seq_p, NA_HD)
    cspec = lambda: pl.BlockSpec((1, 1, NA_HEADS, seq_p, NA_HD), lambda b: (b, 0, 0, 0, 0))
    o, kc, vc = pl.pallas_call(
        _na_ctx_kernel,
        grid=(nb_p,),
        in_specs=[pl.BlockSpec((seq_p, D), lambda b: (b, 0)), pl.BlockSpec((seq_p, D), lambda b: (b, 1)),
                  pl.BlockSpec((seq_p, D), lambda b: (b, 2))],
        out_specs=[pl.BlockSpec((seq_p, D), lambda b: (b, 0)), cspec(), cspec()],
        out_shape=[jax.ShapeDtypeStruct((t, D), BF16), jax.ShapeDtypeStruct(cshape, F32),
                   jax.ShapeDtypeStruct(cshape, F32)],
        compiler_params=_params(1),
        name="na_context",
    )(qkv, qkv, qkv)

    bias = _na_bias_tables(lw["rpb"])
    past = cache_k.shape[3]
    off = t_p // seq_s
    npair = NA_HEADS // 2
    pw = 2 * NA_HD
    pspec = lambda: pl.BlockSpec((1, 1, 2, past, NA_HD), lambda hp, b: (b, j, hp, 0, 0))
    o = pl.pallas_call(
        functools.partial(_na_lat_kernel, seq=seq_s, qb=256),
        grid=(npair, nb_s),
        in_specs=[pl.BlockSpec((seq_s, pw), lambda hp, b: (off + b, hp)),
                  pl.BlockSpec((seq_s, pw), lambda hp, b: (off + b, npair + hp)),
                  pl.BlockSpec((seq_s, pw), lambda hp, b: (off + b, 2 * npair + hp)),
                  pl.BlockSpec((2, 3, 2 * NA_WIN_ROWS, GRID_W, 2 * GRID_W), lambda hp, b: (hp, 0, 0, 0, 0)),
                  pspec(), pspec(), pl.BlockSpec(memory_space=pl.ANY)],
        out_specs=pl.BlockSpec((seq_s, pw), lambda hp, b: (off + b, hp)),
        out_shape=jax.ShapeDtypeStruct((t, D), BF16),
        input_output_aliases={6: 0},
        compiler_params=_params(2),
        name="na_latent",
    )(qkv, qkv, qkv, bias, cache_k, cache_v, o)
    outs = _out_proj(o, lw["w_o"], h, lw, layer, dims)
    return outs, kc, vc


def _gelu_tanh(x):
    return 0.5 * x * (1.0 + jnp.tanh(math.sqrt(2.0 / math.pi) * (x + 0.044715 * (x * x * x))))


def _gmlp_kernel(h_ref, g1, sh1, sc1, gt1, win_ref, gv_ref, ws_ref, bs_ref, wout_ref,
                 g2, sh2, sc2, wr_hi, wr_lo, br, h1_ref, xn_ref, route_ref, m_ref):
    h = h_ref[...]
    a = _normmod(h, g1[...], sh1[0], sc1[0]).astype(BF16)
    u = _gelu_tanh(_dot(a, win_ref[:, :GMLP_DFF]))
    v = _gelu_tanh(_dot(a, win_ref[:, GMLP_DFF:]))
    v = _rms(v, gv_ref[...]).astype(BF16)
    gw = GMLP_DFF // GMLP_GROUPS
    for c in range(TM // GMLP_CHUNK):
        rows = slice(c * GMLP_CHUNK, (c + 1) * GMLP_CHUNK)
        for g in range(GMLP_GROUPS):
            cols = slice(g * gw, (g + 1) * gw)
            vs = _dot(ws_ref[g], v[rows, cols]) + bs_ref[g]
            m_ref[rows, cols] = (u[rows, cols] * vs).astype(BF16)
    o = _dot(m_ref[...], wout_ref[...])
    _finish(h, o, gt1[0], (g2, sh2, sc2, wr_hi, wr_lo, br), h1_ref, xn_ref, route_ref)


def _gmlp_layer(h, lw, layer, dims):
    t = dims["t"]
    row_fn = _tile_row_fn(dims)
    gw = GMLP_DFF // GMLP_GROUPS
    tile = lambda: pl.BlockSpec((TM, D), lambda i: (i, 0))
    one = pl.Buffered(1)
    return pl.pallas_call(
        _gmlp_kernel,
        grid=(t // TM,),
        in_specs=[tile(), _full((1, D))] + [_mod_spec(layer, k, row_fn) for k in (0, 1, 2)]
                 + [pl.BlockSpec((D, 2 * GMLP_DFF), lambda i: (0, 0), pipeline_mode=one),
                    _full((1, GMLP_DFF)), _full((GMLP_GROUPS, GMLP_CHUNK, GMLP_CHUNK)),
                    _full((GMLP_GROUPS, GMLP_CHUNK, gw)),
                    pl.BlockSpec((GMLP_DFF, D), lambda i: (0, 0), pipeline_mode=one)]
                 + _route_specs(layer, row_fn),
        out_specs=_stream_specs(TM, lambda i: i),
        out_shape=_stream_outs(t),
        scratch_shapes=[pltpu.VMEM((TM, GMLP_DFF), BF16)],
        compiler_params=_params(1),
        name="gmlp",
    )(h, lw["g1"], lw["mod"], lw["mod"], lw["mod"], lw["w_in"], lw["g_v"], lw["w_s"], lw["b_s"],
      lw["w_out"], *_route_args(lw))


HALO = 16


CONV_TAPS = 4
CONV_LEFT = CONV_TAPS // 2
CONV_SHIFTED = tuple(k for k in range(CONV_TAPS) if k != CONV_LEFT)


def _ssd_conv_kernel(x_ref, bc_ref, cw_ref, cb_ref, sh_ref, *rest, seq):
    o_ref = rest[-1]
    L = SSD_CHUNK
    nc = seq // L
    c = pl.program_id(1)
    r0 = pl.multiple_of(c * L, L)
    rp = pl.multiple_of(jnp.maximum(r0 - HALO, 0), HALO)
    rn = pl.multiple_of(jnp.minimum(r0 + L, seq - HALO), HALO)
    outs = []
    for src, lo in ((x_ref, 0), (bc_ref, SSD_INNER)):
        cur = src[pl.ds(r0, L), :]
        prev = src[pl.ds(rp, HALO), :]
        nxt = src[pl.ds(rn, HALO), :]
        win = jnp.concatenate([jnp.where(c > 0, prev, jnp.zeros_like(prev)), cur,
                               jnp.where(c < nc - 1, nxt, jnp.zeros_like(nxt))], axis=0)
        shifted = _dot(sh_ref[...], win)
        w = lambda k: cw_ref[k:k + 1, lo:lo + SSD_INNER]
        conv = cb_ref[:, lo:lo + SSD_INNER] + w(CONV_LEFT) * cur.astype(F32)
        for i, k in enumerate(CONV_SHIFTED):
            conv = conv + w(k) * shifted[i * L:(i + 1) * L, :]
        outs.append(_silu(conv).astype(BF16))
    o_ref[...] = jnp.concatenate(outs, axis=1)


def _ssd_conv(zxbc, lw, dims):
    t, t_p = dims["t"], dims["t_p"]
    L = SSD_CHUNK

    def make_call(seq, nb, off, aliased):
        nc = seq // L
        in_specs = [pl.BlockSpec((seq, SSD_INNER), lambda b, c: (off + b, 1)),
                    pl.BlockSpec((seq, SSD_INNER), lambda b, c: (off + b, 2)),
                    _full((4, 2 * SSD_INNER)), _full((1, 2 * SSD_INNER)),
                    _full((len(CONV_SHIFTED) * L, L + 2 * HALO))]
        if aliased:
            in_specs.append(pl.BlockSpec(memory_space=pl.ANY))
        return pl.pallas_call(
            functools.partial(_ssd_conv_kernel, seq=seq),
            grid=(nb, nc),
            in_specs=in_specs,
            out_specs=pl.BlockSpec((L, 2 * SSD_INNER), lambda b, c: ((off + b) * nc + c, 0)),
            out_shape=jax.ShapeDtypeStruct((t, 2 * SSD_INNER), BF16),
            input_output_aliases={5: 0} if aliased else {},
            compiler_params=_params(2),
            name="ssd_conv_seq%d" % seq,
        )

    taps = jnp.asarray(CONV_SHIFTED, jnp.int32)
    want = (HALO - CONV_LEFT + taps[:, None] + jnp.arange(L)[None, :]).reshape(-1, 1)
    shift = (want == jnp.arange(L + 2 * HALO)[None, :]).astype(BF16)
    args = [zxbc, zxbc, lw["conv_w"], lw["conv_b"], shift]
    xbc = make_call(dims["seq_p"], dims["nb_p"], 0, False)(*args)
    return make_call(dims["seq_s"], dims["nb_s"], t_p // dims["seq_s"], True)(*args, xbc)


def _ssd_scan_kernel(*refs, seq, rev, has_h0, want_state, add_skip):
    (xbc_ref, dt_ref, dtb_ref, alog_ref, dsk_ref, tri_ref, rep_ref) = refs[:7]
    pos = 7
    h0_ref = None
    if has_h0:
        h0_ref = refs[pos]
        pos += 1
    n_alias = len(refs) - pos - (2 if want_state else 1) - 1
    pos += n_alias
    y_ref = refs[pos]
    st_ref = refs[pos + 1] if want_state else None
    state = refs[-1]

    L = SSD_CHUNK
    nc = seq // L
    c = pl.program_id(1)

    @pl.when(c == 0)
    def _():
        if has_h0:
            for i in range(SSD_INNER // L):
                hpb = L // SSD_HD
                blk = h0_ref[0, 0, i * hpb:(i + 1) * hpb].reshape(L, SSD_STATE)
                state[:, i * L:(i + 1) * L] = blk.T
        else:
            state[...] = jnp.zeros_like(state)

    xc = xbc_ref[:, :SSD_INNER].astype(F32)
    bm = xbc_ref[:, SSD_INNER:SSD_INNER + SSD_GROUPS * SSD_STATE]
    cm = xbc_ref[:, SSD_INNER + SSD_GROUPS * SSD_STATE:]

    dtr = dt_ref[...] + dtb_ref[...]
    dt = jnp.maximum(dtr, 0.0) + jnp.log(1.0 + jnp.exp(-jnp.abs(dtr)))
    dta = dt * (-jnp.exp(alog_ref[...]))
    tri = tri_ref[...]
    p = sum(_dot(tri, part) for part in _split3(dta))
    pt = p.T
    edge = 0 if rev else L - 1
    p_edge = p[edge:edge + 1, :]
    rep = rep_ref[...]
    dt_x = _dot(dt.astype(BF16), rep)
    ep_x = _dot(jnp.exp(p).astype(BF16), rep)
    dte_x = _dot(jnp.exp(p_edge - p).astype(BF16), rep)
    cdec_x = _dot(jnp.broadcast_to(jnp.exp(p_edge), (8, p.shape[1])).astype(BF16), rep)[0:1, :]

    dtx = xc * dt_x
    dtxb = dtx.astype(BF16)
    xdte = (dtx * dte_x).astype(BF16)
    li = lax.broadcasted_iota(jnp.int32, (L, L), 0)
    si = lax.broadcasted_iota(jnp.int32, (L, L), 1)
    keep = (li <= si) if rev else (li >= si)
    lane0 = SSD_HEADS if rev else 0
    gw = SSD_HPG * SSD_HD
    ys, new_state = [], []
    for g in range(SSD_GROUPS):
        gcols = slice(g * gw, (g + 1) * gw)
        b_g = bm[:, g * SSD_STATE:(g + 1) * SSD_STATE]
        c_g = cm[:, g * SSD_STATE:(g + 1) * SSD_STATE]
        cb = _dot_nt(c_g, b_g)
        st_prev = state[:, gcols]
        y_g = _dot(c_g, st_prev.astype(BF16)) * ep_x[:, gcols]
        yd = []
        for hh in range(SSD_HPG):
            hl = lane0 + g * SSD_HPG + hh
            seg = p[:, hl:hl + 1] - pt[hl:hl + 1, :]
            mat = cb * jnp.exp(jnp.where(keep, seg, NEG))
            hc = slice((g * SSD_HPG + hh) * SSD_HD, (g * SSD_HPG + hh + 1) * SSD_HD)
            yd.append(_dot(mat.astype(BF16), dtxb[:, hc]))
        y_g = y_g + jnp.concatenate(yd, axis=1)
        if add_skip:
            y_g = y_g + dsk_ref[:, gcols] * xc[:, gcols]
        ys.append(y_g)
        new_state.append(st_prev * cdec_x[:, gcols] + _dot(b_g.astype(F32).T.astype(BF16), xdte[:, gcols]))
    y_ref[...] = jnp.concatenate(ys, axis=1)
    state[...] = jnp.concatenate(new_state, axis=1)

    if want_state:
        @pl.when(c == nc - 1)
        def _():
            for i in range(SSD_INNER // L):
                blk = state[:, i * L:(i + 1) * L].T
                st_ref[0, 0, i * (L // SSD_HD):(i + 1) * (L // SSD_HD)] = blk.reshape(L // SSD_HD, SSD_HD, SSD_STATE)


def _ssd_scan(xbc, dt_raw, lw, h0, j, rev, dims):
    t, t_p, seq_p, seq_s = dims["t"], dims["t_p"], dims["seq_p"], dims["seq_s"]
    nb_p, nb_s = dims["nb_p"], dims["nb_s"]
    L = SSD_CHUNK
    d = 1 if rev else 0
    li = jnp.arange(L)
    tri = ((li[:, None] <= li[None, :]) if rev else (li[:, None] >= li[None, :])).astype(BF16)
    lane = jnp.arange(128)
    col_head = jnp.arange(SSD_INNER) // SSD_HD
    rep = (lane[:, None] == (d * SSD_HEADS + col_head)[None, :]).astype(BF16)
    dsk = jnp.repeat(lw["d_skip"], SSD_HD)[None, :].astype(F32)
    st_shape = (nb_p, 1, SSD_HEADS, SSD_HD, SSD_STATE)

    def make_call(seq, nb, off, has_h0, want_state, n_alias):
        nc = seq // L
        chunk = (lambda b, c: (off * nc + b * nc + (nc - 1 - c), 0)) if rev else (lambda b, c: (off * nc + b * nc + c, 0))
        in_specs = [pl.BlockSpec((L, 2 * SSD_INNER), chunk), pl.BlockSpec((L, 128), chunk),
                    _full((1, 128)), _full((1, 128)),
                    _full((1, SSD_INNER)), _full((L, L)), _full((128, SSD_INNER))]
        if has_h0:
            in_specs.append(pl.BlockSpec((1, 1, SSD_HEADS, SSD_HD, SSD_STATE), lambda b, c: (b, j, 0, 0, 0)))
        aliases = {}
        if n_alias:
            aliases = {len(in_specs): 0}
            in_specs.append(pl.BlockSpec(memory_space=pl.ANY))
        out_specs = [pl.BlockSpec((L, SSD_INNER), chunk)]
        out_shape = [jax.ShapeDtypeStruct((t, SSD_INNER), F32)]
        if want_state:
            out_specs.append(pl.BlockSpec((1, 1, SSD_HEADS, SSD_HD, SSD_STATE), lambda b, c: (b, 0, 0, 0, 0)))
            out_shape.append(jax.ShapeDtypeStruct(st_shape, F32))
        return pl.pallas_call(
            functools.partial(_ssd_scan_kernel, seq=seq, rev=rev, has_h0=has_h0, want_state=want_state,
                              add_skip=not rev),
            grid=(nb, nc),
            in_specs=in_specs,
            out_specs=out_specs,
            out_shape=out_shape,
            scratch_shapes=[pltpu.VMEM((SSD_STATE, SSD_INNER), F32)],
            input_output_aliases=aliases,
            compiler_params=_params(2),
            name="ssd_scan_%s_seq%d" % ("bwd" if rev else "fwd", seq),
        )

    common = [xbc, dt_raw, lw["dt_bias"], lw["a_log"], dsk, tri, rep]
    y, st = make_call(seq_p, nb_p, 0, False, True, 0)(*common)
    (y,) = make_call(seq_s, nb_s, t_p // seq_s, True, False, 1)(*common, h0, y)
    return y, st


def _ssd_out_kernel(yf_ref, yb_ref, z_ref, gn_ref, w_ref, h_ref, gt1, g2, sh2, sc2, wr_hi, wr_lo, br,
                    h1_ref, xn_ref, route_ref):
    y = (yf_ref[...] + yb_ref[...]) * _silu(z_ref[...].astype(F32))
    o = _dot(_rms(y, gn_ref[...]).astype(BF16), w_ref[...])
    _finish(h_ref[...], o, gt1[0], (g2, sh2, sc2, wr_hi, wr_lo, br), h1_ref, xn_ref, route_ref)


def _ssd_layer(h, state_f, state_b, j, lw, layer, dims):
    t = dims["t"]
    zxbc = _pre_proj(h, lw, layer, lw["w_in"], 1536, BF16, dims, n=3 * SSD_INNER)
    dt_raw = _pre_proj(h, lw, layer, lw["w_dt"], 128, F32, dims)
    xbc = _ssd_conv(zxbc, lw, dims)
    y_f, st_f = _ssd_scan(xbc, dt_raw, lw, state_f, j, False, dims)
    y_b, st_b = _ssd_scan(xbc, dt_raw, lw, state_b, j, True, dims)
    row_fn = _tile_row_fn(dims)
    tile = lambda: pl.BlockSpec((TM, D), lambda i: (i, 0))
    wide = lambda: pl.BlockSpec((TM, SSD_INNER), lambda i: (i, 0))
    outs = pl.pallas_call(
        _ssd_out_kernel,
        grid=(t // TM,),
        in_specs=[wide(), wide(), wide(), _full((1, SSD_INNER)), _full((SSD_INNER, D)), tile(),
                  _mod_spec(layer, 2, row_fn)] + _route_specs(layer, row_fn),
        out_specs=_stream_specs(TM, lambda i: i),
        out_shape=_stream_outs(t),
        compiler_params=_params(1),
        name="ssd_out",
    )(y_f, y_b, zxbc, lw["g_norm"], lw["w_out"], h, lw["mod"], *_route_args(lw))
    return outs, st_f, st_b


TS = 256
SRC = 512


def _moe_plan(route, t):
    i32 = jnp.int32
    npair = len(PAIR_ORDER)
    ncls = MOE_GROUPS * npair
    key = route[:, 0].astype(i32)
    oh = (key[:, None] == jnp.arange(ncls, dtype=i32)[None, :]).astype(i32)
    csum = jnp.cumsum(oh, axis=0)
    ccnt = csum[-1]
    cnt = ccnt.reshape(MOE_GROUPS, npair).sum(axis=1)
    padded = ((cnt + TS - 1) // TS) * TS
    gend = jnp.cumsum(padded)
    in_grp = jnp.cumsum(ccnt.reshape(MOE_GROUPS, npair), axis=1) - ccnt.reshape(MOE_GROUPS, npair)
    cstart = ((gend - padded)[:, None] + in_grp).reshape(ncls)
    pos = jnp.sum(oh * (csum - 1 + cstart[None, :]), axis=1)
    n_tiles = t // TS + MOE_GROUPS
    tile0 = jnp.arange(n_tiles, dtype=i32) * TS
    tile_grp = jnp.sum((tile0[:, None] >= gend[None, :]).astype(i32), axis=1)
    last_grp = jnp.max(jnp.where(cnt > 0, jnp.arange(MOE_GROUPS, dtype=i32), 0))
    tile_grp = jnp.minimum(tile_grp, last_grp)
    n_used = (gend[-1] // TS).reshape(1)
    touch = ((cstart[None, :] < tile0[:, None] + TS) & (cstart[None, :] + ccnt[None, :] > tile0[:, None])
             & (ccnt[None, :] > 0))
    uses = jnp.array([[int(e in PAIR_ORDER[c % npair]) for e in range(MOE_EPG)] for c in range(ncls)], i32)
    need = (jnp.dot(touch.astype(i32), uses) > 0).astype(i32).reshape(-1)
    return dict(pos=pos, tile_grp=tile_grp, n_used=n_used, need=need, n_tiles=n_tiles)


ROW_UNROLL = 8


def _moe_scatter_kernel(pos_ref, x_ref, xs_ref):
    n = pl.program_id(0)

    @pl.when(n == 0)
    def _():
        xs_ref[...] = jnp.zeros_like(xs_ref)

    def body(jj, carry):
        for r in range(ROW_UNROLL):
            j = jj * ROW_UNROLL + r
            xs_ref[pl.ds(pos_ref[n * SRC + j], 1), :] = x_ref[pl.ds(j, 1), :]
        return carry

    lax.fori_loop(0, SRC // ROW_UNROLL, body, 0)


def _moe_expert_kernel(grp_ref, nused_ref, need_ref, xs_ref, wg_ref, wu_ref, wd_ref, y_ref,
                       wg_b, wu_b, wd_b, acc_ref):
    i = pl.program_id(0)
    used = i < nused_ref[0]
    new_group = (i == 0) | (grp_ref[i] != grp_ref[jnp.maximum(i - 1, 0)])

    @pl.when(used & new_group)
    def _():
        for e in range(MOE_EPG):
            wg_b[e] = wg_ref[0, e].astype(BF16)
            wu_b[e] = wu_ref[0, e].astype(BF16)
            wd_b[e] = wd_ref[0, e].astype(BF16)

    @pl.when(used)
    def _():
        lo, hi = _unpack_pairs(xs_ref[...])
        x = jnp.concatenate([lo, hi[:, :D - PW]], axis=1).astype(BF16)
        rec = hi[:, D - PW:XW - PW]
        lane = lax.broadcasted_iota(jnp.int32, rec.shape, 1)
        acc_ref[...] = jnp.zeros_like(acc_ref)
        for e in range(MOE_EPG):
            @pl.when(need_ref[i * MOE_EPG + e] > 0)
            def _():
                hg = _dot(x, wg_b[e])
                hu = _dot(x, wu_b[e])
                mine = (lane % MOE_EPG == e) & (lane < 3 * MOE_EPG)
                cw = jnp.sum(jnp.where(mine, rec, 0.0), axis=-1, keepdims=True)
                acc_ref[...] += _dot((_silu(hg) * hu * cw).astype(BF16), wd_b[e])

        acc = acc_ref[...].astype(BF16).astype(F32)
        y_ref[...] = _pack_pairs(acc[:, :D // 2], acc[:, D // 2:])

    @pl.when(i >= nused_ref[0])
    def _():
        y_ref[...] = jnp.zeros_like(y_ref)


def _moe_ungather_kernel(pos_ref, ys_ref, h1_ref, gt2, fg_ref, *rest, n_prompt_tiles):
    rows_ref = rest[-1]
    out_refs = rest[:-1]
    n = pl.program_id(0)

    def body(jj, carry):
        for r in range(ROW_UNROLL):
            j = jj * ROW_UNROLL + r
            rows_ref[pl.ds(j, 1), :] = ys_ref[pl.ds(pos_ref[n * SRC + j], 1), :]
        return carry

    lax.fori_loop(0, SRC // ROW_UNROLL, body, 0)
    h2 = h1_ref[...] + gt2[0] * jnp.concatenate(_unpack_pairs(rows_ref[...]), axis=1)
    if n_prompt_tiles is None:
        out_refs[0][...] = h2
    else:
        y = _rms(h2, fg_ref[...])

        @pl.when(n < n_prompt_tiles)
        def _():
            out_refs[0][...] = y

        @pl.when(n >= n_prompt_tiles)
        def _():
            out_refs[1][...] = y


def _moe_sparse(h1, xn, route, lw, layer, final_g, final, dims):
    t = dims["t"]
    plan = _moe_plan(route, t)
    n_tiles = plan["n_tiles"]
    n_rows = n_tiles * TS
    one = pl.Buffered(1)
    xs = pl.pallas_call(
        _moe_scatter_kernel,
        grid_spec=pltpu.PrefetchScalarGridSpec(
            num_scalar_prefetch=1,
            grid=(t // SRC,),
            in_specs=[pl.BlockSpec((SRC, PW), lambda n, pos: (n, 0))],
            out_specs=pl.BlockSpec((n_rows, PW), lambda n, pos: (0, 0))),
        out_shape=jax.ShapeDtypeStruct((n_rows, PW), U32),
        compiler_params=_params(1),
        name="moe_scatter_l%d" % layer,
    )(plan["pos"], xn)

    ex = lambda i, g, nu, nd: (layer, g[i], 0, 0)
    ys = pl.pallas_call(
        _moe_expert_kernel,
        grid_spec=pltpu.PrefetchScalarGridSpec(
            num_scalar_prefetch=3,
            grid=(n_tiles,),
            in_specs=[pl.BlockSpec((TS, PW), lambda i, g, nu, nd: (i, 0)),
                      pl.BlockSpec((1, MOE_EPG, D, MOE_DFF), ex, pipeline_mode=one),
                      pl.BlockSpec((1, MOE_EPG, D, MOE_DFF), ex, pipeline_mode=one),
                      pl.BlockSpec((1, MOE_EPG, MOE_DFF, D), ex, pipeline_mode=one)],
            out_specs=pl.BlockSpec((TS, D // 2), lambda i, g, nu, nd: (i, 0)),
            scratch_shapes=[pltpu.VMEM((MOE_EPG, D, MOE_DFF), BF16), pltpu.VMEM((MOE_EPG, D, MOE_DFF), BF16),
                            pltpu.VMEM((MOE_EPG, MOE_DFF, D), BF16), pltpu.VMEM((TS, D), F32)]),
        out_shape=jax.ShapeDtypeStruct((n_rows, D // 2), U32),
        compiler_params=_params(1),
        name="moe_expert_l%d" % layer,
    )(plan["tile_grp"], plan["n_used"], plan["need"], xs, lw["w_gate"], lw["w_up"], lw["w_down"])

    row_fn = _tile_row_fn(dims, SRC)
    tile = lambda: pl.BlockSpec((SRC, D), lambda n, pos: (n, 0))
    if final:
        npt = dims["t_p"] // SRC
        out_specs = [pl.BlockSpec((SRC, D), lambda n, pos: (jnp.minimum(n, npt - 1), 0)),
                     pl.BlockSpec((SRC, D), lambda n, pos: (jnp.maximum(n - npt, 0), 0))]
        out_shape = [jax.ShapeDtypeStruct((dims["t_p"], D), F32), jax.ShapeDtypeStruct((t - dims["t_p"], D), F32)]
    else:
        npt = None
        out_specs = [tile()]
        out_shape = [jax.ShapeDtypeStruct((t, D), F32)]
    return pl.pallas_call(
        functools.partial(_moe_ungather_kernel, n_prompt_tiles=npt),
        grid_spec=pltpu.PrefetchScalarGridSpec(
            num_scalar_prefetch=1,
            grid=(t // SRC,),
            in_specs=[pl.BlockSpec((n_rows, D // 2), lambda n, pos: (0, 0)), tile(),
                      pl.BlockSpec((1, 1, D), lambda n, pos: ((layer * MOD_ROWS + row_fn(n)) * N_MOD + 5, 0, 0)),
                      pl.BlockSpec((1, D), lambda n, pos: (0, 0))],
            out_specs=out_specs,
            scratch_shapes=[pltpu.VMEM((SRC, D // 2), U32)]),
        out_shape=out_shape,
        compiler_params=_params(1),
        name="moe_ungather_l%d" % layer,
    )(plan["pos"], ys, h1, lw["mod"], final_g)


def kernel(x_prompt, x_sample, cache_k, cache_v, state_ssm_fwd, state_ssm_bwd, c, c_ctx, ada_w, ada_b, norm1_g, norm2_g, final_g, fnet_w_o, na_w_qkv, na_w_o, na_rpb, gmlp_w_in, gmlp_g_v, gmlp_w_s, gmlp_b_s, gmlp_w_out, ssd_w_in, ssd_conv_w, ssd_conv_b, ssd_a_log, ssd_dt_bias, ssd_d_skip, ssd_g_norm, ssd_w_out, moe_w_gr, moe_b_gr, moe_w_er, moe_b_er, moe_w_gate, moe_w_up, moe_w_down):
    nb_p, seq_p, _ = x_prompt.shape
    nb_s, seq_s, _ = x_sample.shape
    depth = ada_w.shape[0]
    t_p, t_s = nb_p * seq_p, nb_s * seq_s
    dims = dict(t=t_p + t_s, t_p=t_p, seq_p=seq_p, seq_s=seq_s, nb_p=nb_p, nb_s=nb_s)
    assert 1 + nb_s <= MOD_ROWS and t_p % seq_s == 0 and t_p % TM == 0 and seq_s % TM == 0

    cond = jnp.zeros((MOD_ROWS, D), F32).at[0].set(c_ctx).at[1:1 + nb_s].set(c)
    mod = _ada_table(cond, ada_w, ada_b)
    h = (x_prompt.reshape(t_p, D), x_sample.reshape(t_s, D))
    fg = final_g.reshape(1, D)

    new_k, new_v, new_sf, new_sb = [], [], [], []
    for l in range(depth):
        kind, j = l % 4, l // 4
        w_r = jnp.concatenate([moe_w_gr[l], moe_w_er[l]], axis=1)
        w_r = jnp.pad(w_r, ((0, 0), (0, ROUTE_W - MOE_GROUPS - MOE_EXPERTS)))
        b_r = jnp.pad(jnp.concatenate([moe_b_gr[l], moe_b_er[l]]), (0, ROUTE_W - MOE_GROUPS - MOE_EXPERTS))
        wr_hi = w_r.astype(BF16)
        lw = dict(mod=mod, g1=norm1_g[l].reshape(1, D), g2=norm2_g[l].reshape(1, D),
                  wr_hi=wr_hi, wr_lo=(w_r - wr_hi.astype(F32)).astype(BF16), br=b_r.reshape(1, ROUTE_W),
                  w_gate=moe_w_gate, w_up=moe_w_up, w_down=moe_w_down)
        if kind == 0:
            lw.update(w_o=fnet_w_o[j].astype(BF16))
            h1, xn, route = _fnet_layer(h, lw, l, dims)
        elif kind == 1:
            lw.update(w_qkv=na_w_qkv[j], w_o=na_w_o[j].astype(BF16), rpb=na_rpb[j])
            (h1, xn, route), kc, vc = _na_layer(h, cache_k, cache_v, j, lw, l, dims)
            new_k.append(kc)
            new_v.append(vc)
        elif kind == 2:
            gw = GMLP_DFF // GMLP_GROUPS
            lw.update(w_in=gmlp_w_in[j].astype(BF16), g_v=gmlp_g_v[j].reshape(1, GMLP_DFF),
                      w_s=gmlp_w_s[j].astype(BF16),
                      b_s=jnp.broadcast_to(gmlp_b_s[j][:, :, None], (GMLP_GROUPS, GMLP_CHUNK, gw)),
                      w_out=gmlp_w_out[j].astype(BF16))
            h1, xn, route = _gmlp_layer(h, lw, l, dims)
        else:
            n_main = 3 * SSD_INNER
            w_in = ssd_w_in[j]
            pad = lambda v: jnp.pad(v, ((0, 0), (0, 128 - 2 * SSD_HEADS)))
            lw.update(w_in=w_in, w_dt=pad(w_in[:, n_main:]),
                      conv_w=ssd_conv_w[j], conv_b=ssd_conv_b[j].reshape(1, -1),
                      dt_bias=pad(ssd_dt_bias[j].reshape(1, -1)), a_log=pad(ssd_a_log[j].reshape(1, -1)),
                      d_skip=ssd_d_skip[j], g_norm=ssd_g_norm[j].reshape(1, SSD_INNER),
                      w_out=ssd_w_out[j].astype(BF16))
            (h1, xn, route), sf, sb = _ssd_layer(h, state_ssm_fwd, state_ssm_bwd, j, lw, l, dims)
            new_sf.append(sf)
            new_sb.append(sb)
        out = _moe_sparse(h1, xn, route, lw, l, fg, l == depth - 1, dims)
        h = out if l == depth - 1 else out[0]

    y_prompt = h[0].reshape(nb_p, seq_p, D)
    y_sample = h[1].reshape(nb_s, seq_s, D)
    cat = lambda xs: jnp.concatenate(xs, axis=1)
    return (y_prompt, y_sample, cat(new_k), cat(new_v), cat(new_sf), cat(new_sb))
```

```python
import functools
import math

import jax
import jax.numpy as jnp
from jax import lax
from jax.experimental import pallas as pl
from jax.experimental.pallas import tpu as pltpu

F32 = jnp.float32
BF16 = jnp.bfloat16

D = 1024
EPS = 1e-6
NEG = -1e30
N_MOD = 6
MOD_ROWS = 8
GRID_W = 64
FNET_GROUPS = 8
NA_HEADS = 16
NA_HD = 64
NA_WIN_ROWS = 8
NA_WIN_COLS = 16
GMLP_CHUNK = 128
GMLP_DFF = 2048
GMLP_GROUPS = 8
SSD_INNER = 2048
SSD_HD = 64
SSD_HEADS = 32
SSD_GROUPS = 8
SSD_STATE = 128
SSD_CHUNK = 128
SSD_HPG = SSD_HEADS // SSD_GROUPS
MOE_GROUPS = 4
MOE_EPG = 4
MOE_EXPERTS = 16
MOE_DFF = 512
ROUTE_W = 128
ROUTE_E0 = 4
PAIR_ORDER = ((0, 1), (0, 2), (0, 3), (1, 3), (1, 2), (2, 3))

TM = 512
VMEM_LIMIT = 56 * 1024 * 1024


def _dot(a, b):
    return jnp.dot(a, b, preferred_element_type=F32)


def _dot_nt(a, b):
    return lax.dot_general(a, b, (((1,), (1,)), ((), ())), preferred_element_type=F32)


def _dot_tn(a, b):
    return lax.dot_general(a, b, (((0,), (0,)), ((), ())), preferred_element_type=F32)


def _silu(x):
    hx = 0.5 * x
    return hx * (1.0 + jnp.tanh(hx))


def _rms(x, g):
    return x * lax.rsqrt(jnp.mean(x * x, axis=-1, keepdims=True) + EPS) * g


def _normmod(x, g, shift, scale):
    return _rms(x, g) * (1.0 + scale) + shift


def _split3(x):
    hi = x.astype(BF16)
    r = x - hi.astype(F32)
    mid = r.astype(BF16)
    lo = (r - mid.astype(F32)).astype(BF16)
    return hi, mid, lo


def _params(n_axes):
    return pltpu.CompilerParams(dimension_semantics=("arbitrary",) * n_axes,
                                vmem_limit_bytes=VMEM_LIMIT)


def _full(shape):
    nd = len(shape)
    return pl.BlockSpec(shape, lambda *_: (0,) * nd)


def _mod_spec(layer, k, row_fn):
    return pl.BlockSpec((1, 1, D), lambda *idx: ((layer * MOD_ROWS + row_fn(*idx)) * N_MOD + k, 0, 0))


def _ada_kernel(c_ref, w_ref, b_ref, o_ref):
    c = c_ref[...]
    o_ref[0] = _dot(_silu(c).astype(BF16), w_ref[0].astype(BF16)) + b_ref[0]


def _ada_table(cond, ada_w, ada_b):
    depth = ada_w.shape[0]
    n = N_MOD * D
    tn = 1536
    out = pl.pallas_call(
        _ada_kernel,
        grid=(depth, n // tn),
        in_specs=[_full((MOD_ROWS, D)),
                  pl.BlockSpec((1, D, tn), lambda l, j: (l, 0, j)),
                  pl.BlockSpec((1, 1, tn), lambda l, j: (l, 0, j))],
        out_specs=pl.BlockSpec((1, MOD_ROWS, tn), lambda l, j: (l, 0, j)),
        out_shape=jax.ShapeDtypeStruct((depth, MOD_ROWS, n), F32),
        compiler_params=_params(2),
        name="ada_table",
    )(cond, ada_w, ada_b.reshape(depth, 1, n))
    return out.reshape(depth * MOD_ROWS * N_MOD, 1, D)


def _route(h1, g2, sh2, sc2, wr_hi, wr_lo, br):
    xn = _normmod(h1, g2, sh2, sc2)
    xh = xn.astype(BF16)
    xl = (xn - xh.astype(F32)).astype(BF16)
    logits = _dot(xh, wr_hi) + _dot(xh, wr_lo) + _dot(xl, wr_hi) + br
    lane = lax.broadcasted_iota(jnp.int32, logits.shape, 1).astype(F32)
    far = float(ROUTE_W)
    gl = jnp.where(lane < MOE_GROUPS, logits, NEG)
    gmax = jnp.max(gl, axis=-1, keepdims=True)
    g_p = 1.0 / jnp.sum(jnp.exp(gl - gmax), axis=-1, keepdims=True)
    gidx = jnp.min(jnp.where(gl == gmax, lane, far), axis=-1, keepdims=True)
    lo = ROUTE_E0 + MOE_EPG * gidx
    el = jnp.where((lane >= lo) & (lane < lo + MOE_EPG), logits, NEG)
    m1 = jnp.max(el, axis=-1, keepdims=True)
    i1 = jnp.min(jnp.where(el == m1, lane, far), axis=-1, keepdims=True)
    el2 = jnp.where(lane == i1, NEG, el)
    m2 = jnp.max(el2, axis=-1, keepdims=True)
    i2 = jnp.min(jnp.where(el2 == m2, lane, far), axis=-1, keepdims=True)
    e2 = jnp.exp(m2 - m1)
    w1 = g_p / (1.0 + e2)
    w2 = w1 * e2
    rec = jnp.zeros_like(logits)
    for part, (a, b) in enumerate(zip(_split3(w1), _split3(w2))):
        shift = part * MOE_EPG - lo
        rec = (rec + jnp.where(lane == i1 + shift, a.astype(F32), 0.0)
               + jnp.where(lane == i2 + shift, b.astype(F32), 0.0))
    ea = jnp.minimum(i1, i2) - lo
    eb = jnp.maximum(i1, i2) - lo
    pair = sum(jnp.where((ea == a) & (eb == b), float(k), 0.0) for k, (a, b) in enumerate(PAIR_ORDER))
    key = gidx * float(len(PAIR_ORDER)) + pair
    return xh, rec.astype(BF16), jnp.broadcast_to(key, logits.shape)


def _route_specs(layer, row_fn):
    return [_full((1, D)), _mod_spec(layer, 3, row_fn), _mod_spec(layer, 4, row_fn),
            _full((D, ROUTE_W)), _full((D, ROUTE_W)), _full((1, ROUTE_W))]


def _route_args(lw):
    return [lw["g2"], lw["mod"], lw["mod"], lw["wr_hi"], lw["wr_lo"], lw["br"]]


def _finish(h, o, gate, rt_refs, h1_ref, xn_ref, route_ref):
    g2, sh2, sc2, wr_hi, wr_lo, br = rt_refs
    h1 = h + gate * o
    h1_ref[...] = h1
    xn, rec, gid = _route(h1, g2[...], sh2[0], sc2[0], wr_hi[...], wr_lo[...], br[...])
    row = jnp.concatenate([xn.astype(F32), rec.astype(F32), jnp.zeros((h.shape[0], 2 * PW - XW), F32)], axis=1)
    xn_ref[...] = _pack_pairs(row[:, :PW], row[:, PW:])
    route_ref[...] = gid


XW = D + ROUTE_W
PW = 640
U32 = jnp.uint32


def _pack_pairs(lo, hi):
    lo_bits = lax.bitcast_convert_type(lo, U32) >> 16
    hi_bits = lax.bitcast_convert_type(hi, U32) & U32(0xFFFF0000)
    return lo_bits | hi_bits


def _unpack_pairs(w):
    return lax.bitcast_convert_type(w << 16, F32), lax.bitcast_convert_type(w & U32(0xFFFF0000), F32)


def _stream_outs(t):
    return [jax.ShapeDtypeStruct((t, D), F32), jax.ShapeDtypeStruct((t, PW), U32),
            jax.ShapeDtypeStruct((t, ROUTE_W), F32)]


def _stream_specs(rows, row_block):
    return [pl.BlockSpec((rows, w), lambda *idx: (row_block(*idx), 0)) for w in (D, PW, ROUTE_W)]


def _fnet_kernel(h_ref, g1, sh1, sc1, gt1, csc_ref, fs_ref, wo_ref, g2, sh2, sc2, wr_hi, wr_lo, br,
                 *rest, seq):
    h1_ref, xn_ref, route_ref, ab_ref = rest[-4:]
    h = h_ref[...]
    a = _normmod(h, g1[...], sh1[0], sc1[0]).astype(BF16)
    gd = D // FNET_GROUPS
    for g in range(FNET_GROUPS):
        ab = _dot(a[:, g * gd:(g + 1) * gd], csc_ref[...])
        ab_ref[0:seq, g * gd:(g + 1) * gd] = ab[:, :gd].astype(BF16)
        ab_ref[seq:2 * seq, g * gd:(g + 1) * gd] = ab[:, gd:].astype(BF16)
    f = _dot(fs_ref[...], ab_ref[...])
    o = _dot(f.astype(BF16), wo_ref[...])
    _finish(h, o, gt1[0], (g2, sh2, sc2, wr_hi, wr_lo, br), h1_ref, xn_ref, route_ref)


def _dft_tables(n):
    k = jnp.arange(n, dtype=jnp.int32)
    ang = ((k[:, None] * k[None, :]) % n).astype(F32) * (2.0 * math.pi / n)
    s = 1.0 / math.sqrt(n)
    return jnp.cos(ang) * s, jnp.sin(ang) * s


def _fnet_layer(h, lw, layer, dims):
    t, t_p = dims["t"], dims["t_p"]
    gd = D // FNET_GROUPS
    cc, sc = _dft_tables(gd)
    csc = jnp.concatenate([cc, sc], axis=1).astype(BF16)

    split_in = isinstance(h, tuple)

    def make_call(seq, nb, off, n_alias):
        cs, ss = _dft_tables(seq)
        fs = jnp.concatenate([cs, -ss], axis=1).astype(BF16)
        row_fn = (lambda b: 0) if off == 0 else (lambda b: 1 + b)
        in_off = 0 if split_in else off
        tile = lambda: pl.BlockSpec((seq, D), lambda b: (off + b, 0))
        in_specs = ([pl.BlockSpec((seq, D), lambda b: (in_off + b, 0)), _full((1, D))]
                    + [_mod_spec(layer, k, row_fn) for k in (0, 1, 2)]
                    + [_full((gd, 2 * gd)), _full((seq, 2 * seq)), _full((D, D))]
                    + _route_specs(layer, row_fn))
        aliases = {}
        if n_alias:
            base = len(in_specs)
            in_specs = in_specs + [pl.BlockSpec(memory_space=pl.ANY)] * n_alias
            aliases = {base + i: i for i in range(n_alias)}
        call = pl.pallas_call(
            functools.partial(_fnet_kernel, seq=seq),
            grid=(nb,),
            in_specs=in_specs,
            out_specs=_stream_specs(seq, lambda b: off + b),
            out_shape=_stream_outs(t),
            scratch_shapes=[pltpu.VMEM((2 * seq, D), BF16)],
            input_output_aliases=aliases,
            compiler_params=_params(1),
            name="fnet_seq%d" % seq,
        )
        return lambda *a: call(*a[:2], *a[2:5], a[5], fs, *a[6:])

    h_p, h_s = h if split_in else (h, h)
    args = [lw["g1"], lw["mod"], lw["mod"], lw["mod"], csc, lw["w_o"]] + _route_args(lw)
    outs = make_call(dims["seq_p"], dims["nb_p"], 0, None)(h_p, *args)
    return make_call(dims["seq_s"], dims["nb_s"], t_p // dims["seq_s"], 3)(h_s, *args, *outs)


def _pre_kernel(h_ref, g1, sh1, sc1, w_ref, o_ref, a_ref):
    @pl.when(pl.program_id(1) == 0)
    def _():
        a_ref[...] = _normmod(h_ref[...], g1[...], sh1[0], sc1[0]).astype(BF16)

    o_ref[...] = _dot(a_ref[...], w_ref[...].astype(BF16)).astype(o_ref.dtype)


def _tile_row_fn(dims, tm=TM):
    npt = dims["t_p"] // tm
    tps = dims["seq_s"] // tm
    return lambda i, *_: jnp.where(i < npt, 0, 1 + (i - npt) // tps)


def _pre_proj(h, lw, layer, w, tn, out_dtype, dims, n=None):
    t = dims["t"]
    n = w.shape[1] if n is None else n
    tm = dims["seq_s"]
    row_fn = _tile_row_fn(dims, tm)
    return pl.pallas_call(
        _pre_kernel,
        grid=(t // tm, n // tn),
        in_specs=[pl.BlockSpec((tm, D), lambda i, j: (i, 0)), _full((1, D)),
                  _mod_spec(layer, 0, row_fn), _mod_spec(layer, 1, row_fn),
                  pl.BlockSpec((D, tn), lambda i, j: (0, j))],
        out_specs=pl.BlockSpec((tm, tn), lambda i, j: (i, j)),
        out_shape=jax.ShapeDtypeStruct((t, n), out_dtype),
        scratch_shapes=[pltpu.VMEM((tm, D), BF16)],
        compiler_params=_params(2),
        name="pre_proj_l%d_n%d" % (layer, n),
    )(h, lw["g1"], lw["mod"], lw["mod"], w)


def _out_kernel(o_ref, w_ref, h_ref, gt1, g2, sh2, sc2, wr_hi, wr_lo, br, h1_ref, xn_ref, route_ref):
    o = _dot(o_ref[...], w_ref[...])
    _finish(h_ref[...], o, gt1[0], (g2, sh2, sc2, wr_hi, wr_lo, br), h1_ref, xn_ref, route_ref)


def _out_proj(o, w, h, lw, layer, dims):
    t = dims["t"]
    k = o.shape[1]
    row_fn = _tile_row_fn(dims)
    tile = lambda: pl.BlockSpec((TM, D), lambda i: (i, 0))
    return pl.pallas_call(
        _out_kernel,
        grid=(t // TM,),
        in_specs=[pl.BlockSpec((TM, k), lambda i: (i, 0)), _full((k, D)), tile(),
                  _mod_spec(layer, 2, row_fn)] + _route_specs(layer, row_fn),
        out_specs=_stream_specs(TM, lambda i: i),
        out_shape=_stream_outs(t),
        compiler_params=_params(1),
        name="out_proj_l%d" % layer,
    )(o, w, h, lw["mod"], *_route_args(lw))


NA_SCALE = NA_HD ** -0.5


def _na_ctx_kernel(q_ref, k_ref, v_ref, o_ref, kc_ref, vc_ref):
    outs = []
    for hd in range(NA_HEADS):
        sl = slice(hd * NA_HD, (hd + 1) * NA_HD)
        q = q_ref[:, sl] * NA_SCALE
        k = k_ref[:, sl]
        v = v_ref[:, sl]
        s = _dot_nt(q, k)
        p = jnp.exp(s - jnp.max(s, axis=-1, keepdims=True))
        l = jnp.sum(p, axis=-1, keepdims=True)
        outs.append((_dot(p.astype(BF16), v) / l).astype(BF16))
        kc_ref[0, 0, hd] = k.astype(F32)
        vc_ref[0, 0, hd] = v.astype(F32)
    o_ref[...] = jnp.concatenate(outs, axis=1)


def _na_row_start(qr, rows):
    kr = min(NA_WIN_ROWS, rows)
    return min(max(qr - kr // 2, 0), rows - kr)


def _na_window_bias(bias_ref, hh, qr, rows, m_lo, m_hi):
    kr = min(NA_WIN_ROWS, rows)
    rs = _na_row_start(qr, rows)
    blocks = []
    for m in range(m_lo, m_hi):
        ok0 = rs <= 2 * m < rs + kr
        ok1 = rs <= 2 * m + 1 < rs + kr
        e = 2 * m - qr + NA_WIN_ROWS
        if ok0 and ok1:
            blocks.append(bias_ref[hh, 0, e])
        elif ok1:
            blocks.append(bias_ref[hh, 1, e])
        elif ok0:
            blocks.append(bias_ref[hh, 2, e])
        else:
            blocks.append(jnp.full((GRID_W, 2 * GRID_W), NEG, F32))
    return jnp.concatenate(blocks, axis=1)


def _na_lat_kernel(q_ref, k_ref, v_ref, bias_ref, kc_ref, vc_ref, o_in, o_ref, *, seq, qb):
    del o_in
    rows = seq // GRID_W
    for hh in range(2):
        sl = slice(hh * NA_HD, (hh + 1) * NA_HD)
        k = k_ref[:, sl]
        v = v_ref[:, sl]
        kc = kc_ref[0, 0, hh].astype(BF16)
        vc = vc_ref[0, 0, hh].astype(BF16)
        for b0 in range(0, seq, qb):
            q = q_ref[b0:b0 + qb, sl] * NA_SCALE
            qrs = range(b0 // GRID_W, (b0 + qb) // GRID_W)
            m_lo = _na_row_start(qrs[0], rows) // 2
            m_hi = (_na_row_start(qrs[-1], rows) + min(NA_WIN_ROWS, rows) + 1) // 2
            keys = slice(m_lo * 2 * GRID_W, m_hi * 2 * GRID_W)
            bias = jnp.concatenate([_na_window_bias(bias_ref, hh, qr, rows, m_lo, m_hi) for qr in qrs], axis=0)
            s1 = _dot_nt(q, k[keys]) + bias
            s2 = _dot_nt(q, kc)
            m = jnp.maximum(jnp.max(s1, axis=-1, keepdims=True), jnp.max(s2, axis=-1, keepdims=True))
            p1 = jnp.exp(s1 - m)
            p2 = jnp.exp(s2 - m)
            l = jnp.sum(p1, axis=-1, keepdims=True) + jnp.sum(p2, axis=-1, keepdims=True)
            o = (_dot(p1.astype(BF16), v[keys]) + _dot(p2.astype(BF16), vc)) / l
            o_ref[b0:b0 + qb, sl] = o.astype(BF16)


def _na_bias_tables(rpb):
    c = jnp.arange(GRID_W)
    win0 = jnp.clip(c - NA_WIN_COLS // 2, 0, GRID_W - NA_WIN_COLS)
    ok_c = (c[None, :] >= win0[:, None]) & (c[None, :] < win0[:, None] + NA_WIN_COLS)
    dc = jnp.clip(c[None, :] - c[:, None], 1 - NA_WIN_COLS, NA_WIN_COLS - 1) + NA_WIN_COLS - 1
    nh, ndr, ndc = rpb.shape
    pick = (dc.reshape(1, -1) == jnp.arange(ndc)[:, None]).astype(F32)
    cm = jnp.dot(rpb.reshape(nh * ndr, ndc), pick, precision=lax.Precision.HIGHEST)
    cm = jnp.where(ok_c[None, None], cm.reshape(nh, ndr, GRID_W, GRID_W), NEG)
    neg = jnp.full_like(cm[:, :1], NEG)
    ext = jnp.concatenate([neg, cm, neg], axis=1)
    a, b = ext[:, :-1], ext[:, 1:]
    negs = jnp.full_like(a, NEG)
    pair = lambda x, y: jnp.concatenate([x, y], axis=-1)
    return jnp.stack([pair(a, b), pair(negs, b), pair(a, negs)], axis=1)


def _na_layer(h, cache_k, cache_v, j, lw, layer, dims):
    t, t_p, seq_p, seq_s = dims["t"], dims["t_p"], dims["seq_p"], dims["seq_s"]
    nb_p, nb_s = dims["nb_p"], dims["nb_s"]
    qkv = _pre_proj(h, lw, layer, lw["w_qkv"], 1536, BF16, dims)
    cshape = (nb_p, 1, NA_HEADS, seq_p, NA_HD)
    cspec = lambda: pl.BlockSpec((1, 1, NA_HEADS, seq_p, NA_HD), lambda b: (b, 0, 0, 0, 0))
    o, kc, vc = pl.pallas_call(
        _na_ctx_kernel,
        grid=(nb_p,),
        in_specs=[pl.BlockSpec((seq_p, D), lambda b: (b, 0)), pl.BlockSpec((seq_p, D), lambda b: (b, 1)),
                  pl.BlockSpec((seq_p, D), lambda b: (b, 2))],
        out_specs=[pl.BlockSpec((seq_p, D), lambda b: (b, 0)), cspec(), cspec()],
        out_shape=[jax.ShapeDtypeStruct((t, D), BF16), jax.ShapeDtypeStruct(cshape, F32),
                   jax.ShapeDtypeStruct(cshape, F32)],
        compiler_params=_params(1),
        name="na_context",
    )(qkv, qkv, qkv)

    bias = _na_bias_tables(lw["rpb"])
    past = cache_k.shape[3]
    off = t_p // seq_s
    npair = NA_HEADS // 2
    pw = 2 * NA_HD
    pspec = lambda: pl.BlockSpec((1, 1, 2, past, NA_HD), lambda hp, b: (b, j, hp, 0, 0))
    o = pl.pallas_call(
        functools.partial(_na_lat_kernel, seq=seq_s, qb=256),
        grid=(npair, nb_s),
        in_specs=[pl.BlockSpec((seq_s, pw), lambda hp, b: (off + b, hp)),
                  pl.BlockSpec((seq_s, pw), lambda hp, b: (off + b, npair + hp)),
                  pl.BlockSpec((seq_s, pw), lambda hp, b: (off + b, 2 * npair + hp)),
                  pl.BlockSpec((2, 3, 2 * NA_WIN_ROWS, GRID_W, 2 * GRID_W), lambda hp, b: (hp, 0, 0, 0, 0)),
                  pspec(), pspec(), pl.BlockSpec(memory_space=pl.ANY)],
        out_specs=pl.BlockSpec((seq_s, pw), lambda hp, b: (off + b, hp)),
        out_shape=jax.ShapeDtypeStruct((t, D), BF16),
        input_output_aliases={6: 0},
        compiler_params=_params(2),
        name="na_latent",
    )(qkv, qkv, qkv, bias, cache_k, cache_v, o)
    outs = _out_proj(o, lw["w_o"], h, lw, layer, dims)
    return outs, kc, vc


def _gelu_tanh(x):
    return 0.5 * x * (1.0 + jnp.tanh(math.sqrt(2.0 / math.pi) * (x + 0.044715 * (x * x * x))))


def _gmlp_kernel(h_ref, g1, sh1, sc1, gt1, win_ref, gv_ref, ws_ref, bs_ref, wout_ref,
                 g2, sh2, sc2, wr_hi, wr_lo, br, h1_ref, xn_ref, route_ref, m_ref):
    h = h_ref[...]
    a = _normmod(h, g1[...], sh1[0], sc1[0]).astype(BF16)
    u = _gelu_tanh(_dot(a, win_ref[:, :GMLP_DFF]))
    v = _gelu_tanh(_dot(a, win_ref[:, GMLP_DFF:]))
    v = _rms(v, gv_ref[...]).astype(BF16)
    gw = GMLP_DFF // GMLP_GROUPS
    for c in range(TM // GMLP_CHUNK):
        rows = slice(c * GMLP_CHUNK, (c + 1) * GMLP_CHUNK)
        for g in range(GMLP_GROUPS):
            cols = slice(g * gw, (g + 1) * gw)
            vs = _dot(ws_ref[g], v[rows, cols]) + bs_ref[g]
            m_ref[rows, cols] = (u[rows, cols] * vs).astype(BF16)
    o = _dot(m_ref[...], wout_ref[...])
    _finish(h, o, gt1[0], (g2, sh2, sc2, wr_hi, wr_lo, br), h1_ref, xn_ref, route_ref)


def _gmlp_layer(h, lw, layer, dims):
    t = dims["t"]
    row_fn = _tile_row_fn(dims)
    gw = GMLP_DFF // GMLP_GROUPS
    tile = lambda: pl.BlockSpec((TM, D), lambda i: (i, 0))
    one = pl.Buffered(1)
    return pl.pallas_call(
        _gmlp_kernel,
        grid=(t // TM,),
        in_specs=[tile(), _full((1, D))] + [_mod_spec(layer, k, row_fn) for k in (0, 1, 2)]
                 + [pl.BlockSpec((D, 2 * GMLP_DFF), lambda i: (0, 0), pipeline_mode=one),
                    _full((1, GMLP_DFF)), _full((GMLP_GROUPS, GMLP_CHUNK, GMLP_CHUNK)),
                    _full((GMLP_GROUPS, GMLP_CHUNK, gw)),
                    pl.BlockSpec((GMLP_DFF, D), lambda i: (0, 0), pipeline_mode=one)]
                 + _route_specs(layer, row_fn),
        out_specs=_stream_specs(TM, lambda i: i),
        out_shape=_stream_outs(t),
        scratch_shapes=[pltpu.VMEM((TM, GMLP_DFF), BF16)],
        compiler_params=_params(1),
        name="gmlp",
    )(h, lw["g1"], lw["mod"], lw["mod"], lw["mod"], lw["w_in"], lw["g_v"], lw["w_s"], lw["b_s"],
      lw["w_out"], *_route_args(lw))


HALO = 16


CONV_TAPS = 4
CONV_LEFT = CONV_TAPS // 2
CONV_SHIFTED = tuple(k for k in range(CONV_TAPS) if k != CONV_LEFT)


def _ssd_conv_kernel(x_ref, bc_ref, cw_ref, cb_ref, sh_ref, *rest, seq):
    o_ref = rest[-1]
    L = SSD_CHUNK
    nc = seq // L
    c = pl.program_id(1)
    r0 = pl.multiple_of(c * L, L)
    rp = pl.multiple_of(jnp.maximum(r0 - HALO, 0), HALO)
    rn = pl.multiple_of(jnp.minimum(r0 + L, seq - HALO), HALO)
    outs = []
    for src, lo in ((x_ref, 0), (bc_ref, SSD_INNER)):
        cur = src[pl.ds(r0, L), :]
        prev = src[pl.ds(rp, HALO), :]
        nxt = src[pl.ds(rn, HALO), :]
        win = jnp.concatenate([jnp.where(c > 0, prev, jnp.zeros_like(prev)), cur,
                               jnp.where(c < nc - 1, nxt, jnp.zeros_like(nxt))], axis=0)
        shifted = _dot(sh_ref[...], win)
        w = lambda k: cw_ref[k:k + 1, lo:lo + SSD_INNER]
        conv = cb_ref[:, lo:lo + SSD_INNER] + w(CONV_LEFT) * cur.astype(F32)
        for i, k in enumerate(CONV_SHIFTED):
            conv = conv + w(k) * shifted[i * L:(i + 1) * L, :]
        outs.append(_silu(conv).astype(BF16))
    o_ref[...] = jnp.concatenate(outs, axis=1)


def _ssd_conv(zxbc, lw, dims):
    t, t_p = dims["t"], dims["t_p"]
    L = SSD_CHUNK

    def make_call(seq, nb, off, aliased):
        nc = seq // L
        in_specs = [pl.BlockSpec((seq, SSD_INNER), lambda b, c: (off + b, 1)),
                    pl.BlockSpec((seq, SSD_INNER), lambda b, c: (off + b, 2)),
                    _full((4, 2 * SSD_INNER)), _full((1, 2 * SSD_INNER)),
                    _full((len(CONV_SHIFTED) * L, L + 2 * HALO))]
        if aliased:
            in_specs.append(pl.BlockSpec(memory_space=pl.ANY))
        return pl.pallas_call(
            functools.partial(_ssd_conv_kernel, seq=seq),
            grid=(nb, nc),
            in_specs=in_specs,
            out_specs=pl.BlockSpec((L, 2 * SSD_INNER), lambda b, c: ((off + b) * nc + c, 0)),
            out_shape=jax.ShapeDtypeStruct((t, 2 * SSD_INNER), BF16),
            input_output_aliases={5: 0} if aliased else {},
            compiler_params=_params(2),
            name="ssd_conv_seq%d" % seq,
        )

    taps = jnp.asarray(CONV_SHIFTED, jnp.int32)
    want = (HALO - CONV_LEFT + taps[:, None] + jnp.arange(L)[None, :]).reshape(-1, 1)
    shift = (want == jnp.arange(L + 2 * HALO)[None, :]).astype(BF16)
    args = [zxbc, zxbc, lw["conv_w"], lw["conv_b"], shift]
    xbc = make_call(dims["seq_p"], dims["nb_p"], 0, False)(*args)
    return make_call(dims["seq_s"], dims["nb_s"], t_p // dims["seq_s"], True)(*args, xbc)


def _ssd_scan_kernel(*refs, seq, rev, has_h0, want_state, add_skip):
    (xbc_ref, dt_ref, dtb_ref, alog_ref, dsk_ref, tri_ref, rep_ref) = refs[:7]
    pos = 7
    h0_ref = None
    if has_h0:
        h0_ref = refs[pos]
        pos += 1
    n_alias = len(refs) - pos - (2 if want_state else 1) - 1
    pos += n_alias
    y_ref = refs[pos]
    st_ref = refs[pos + 1] if want_state else None
    state = refs[-1]

    L = SSD_CHUNK
    nc = seq // L
    c = pl.program_id(1)

    @pl.when(c == 0)
    def _():
        if has_h0:
            for i in range(SSD_INNER // L):
                hpb = L // SSD_HD
                blk = h0_ref[0, 0, i * hpb:(i + 1) * hpb].reshape(L, SSD_STATE)
                state[:, i * L:(i + 1) * L] = blk.T
        else:
            state[...] = jnp.zeros_like(state)

    xc = xbc_ref[:, :SSD_INNER].astype(F32)
    bm = xbc_ref[:, SSD_INNER:SSD_INNER + SSD_GROUPS * SSD_STATE]
    cm = xbc_ref[:, SSD_INNER + SSD_GROUPS * SSD_STATE:]

    dtr = dt_ref[...] + dtb_ref[...]
    dt = jnp.maximum(dtr, 0.0) + jnp.log(1.0 + jnp.exp(-jnp.abs(dtr)))
    dta = dt * (-jnp.exp(alog_ref[...]))
    tri = tri_ref[...]
    p = sum(_dot(tri, part) for part in _split3(dta))
    pt = p.T
    edge = 0 if rev else L - 1
    p_edge = p[edge:edge + 1, :]
    rep = rep_ref[...]
    dt_x = _dot(dt.astype(BF16), rep)
    ep_x = _dot(jnp.exp(p).astype(BF16), rep)
    dte_x = _dot(jnp.exp(p_edge - p).astype(BF16), rep)
    cdec_x = _dot(jnp.broadcast_to(jnp.exp(p_edge), (8, p.shape[1])).astype(BF16), rep)[0:1, :]

    dtx = xc * dt_x
    dtxb = dtx.astype(BF16)
    xdte = (dtx * dte_x).astype(BF16)
    li = lax.broadcasted_iota(jnp.int32, (L, L), 0)
    si = lax.broadcasted_iota(jnp.int32, (L, L), 1)
    keep = (li <= si) if rev else (li >= si)
    lane0 = SSD_HEADS if rev else 0
    gw = SSD_HPG * SSD_HD
    ys, new_state = [], []
    for g in range(SSD_GROUPS):
        gcols = slice(g * gw, (g + 1) * gw)
        b_g = bm[:, g * SSD_STATE:(g + 1) * SSD_STATE]
        c_g = cm[:, g * SSD_STATE:(g + 1) * SSD_STATE]
        cb = _dot_nt(c_g, b_g)
        st_prev = state[:, gcols]
        y_g = _dot(c_g, st_prev.astype(BF16)) * ep_x[:, gcols]
        xg = dtxb[:, gcols]
        head_of_lane = lax.broadcasted_iota(jnp.int32, xg.shape, 1) // SSD_HD
        mats, blocks = [], []
        for hh in range(SSD_HPG):
            hl = lane0 + g * SSD_HPG + hh
            seg = p[:, hl:hl + 1] - pt[hl:hl + 1, :]
            mats.append((cb * jnp.exp(jnp.where(keep, seg, NEG))).astype(BF16))
            blocks.append(jnp.where(head_of_lane == hh, xg, jnp.zeros_like(xg)))
        y_g = y_g + _dot(jnp.concatenate(mats, axis=1), jnp.concatenate(blocks, axis=0))
        if add_skip:
            y_g = y_g + dsk_ref[:, gcols] * xc[:, gcols]
        ys.append(y_g)
        new_state.append(st_prev * cdec_x[:, gcols] + _dot_tn(b_g, xdte[:, gcols]))
    y_ref[...] = jnp.concatenate(ys, axis=1)
    state[...] = jnp.concatenate(new_state, axis=1)

    if want_state:
        @pl.when(c == nc - 1)
        def _():
            for i in range(SSD_INNER // L):
                blk = state[:, i * L:(i + 1) * L].T
                st_ref[0, 0, i * (L // SSD_HD):(i + 1) * (L // SSD_HD)] = blk.reshape(L // SSD_HD, SSD_HD, SSD_STATE)


def _ssd_scan(xbc, dt_raw, lw, h0, j, rev, dims):
    t, t_p, seq_p, seq_s = dims["t"], dims["t_p"], dims["seq_p"], dims["seq_s"]
    nb_p, nb_s = dims["nb_p"], dims["nb_s"]
    L = SSD_CHUNK
    d = 1 if rev else 0
    li = jnp.arange(L)
    tri = ((li[:, None] <= li[None, :]) if rev else (li[:, None] >= li[None, :])).astype(BF16)
    lane = jnp.arange(128)
    col_head = jnp.arange(SSD_INNER) // SSD_HD
    rep = (lane[:, None] == (d * SSD_HEADS + col_head)[None, :]).astype(BF16)
    dsk = jnp.repeat(lw["d_skip"], SSD_HD)[None, :].astype(F32)
    st_shape = (nb_p, 1, SSD_HEADS, SSD_HD, SSD_STATE)

    def make_call(seq, nb, off, has_h0, want_state, n_alias):
        nc = seq // L
        chunk = (lambda b, c: (off * nc + b * nc + (nc - 1 - c), 0)) if rev else (lambda b, c: (off * nc + b * nc + c, 0))
        in_specs = [pl.BlockSpec((L, 2 * SSD_INNER), chunk), pl.BlockSpec((L, 128), chunk),
                    _full((1, 128)), _full((1, 128)),
                    _full((1, SSD_INNER)), _full((L, L)), _full((128, SSD_INNER))]
        if has_h0:
            in_specs.append(pl.BlockSpec((1, 1, SSD_HEADS, SSD_HD, SSD_STATE), lambda b, c: (b, j, 0, 0, 0)))
        aliases = {}
        if n_alias:
            aliases = {len(in_specs): 0}
            in_specs.append(pl.BlockSpec(memory_space=pl.ANY))
        out_specs = [pl.BlockSpec((L, SSD_INNER), chunk)]
        out_shape = [jax.ShapeDtypeStruct((t, SSD_INNER), F32)]
        if want_state:
            out_specs.append(pl.BlockSpec((1, 1, SSD_HEADS, SSD_HD, SSD_STATE), lambda b, c: (b, 0, 0, 0, 0)))
            out_shape.append(jax.ShapeDtypeStruct(st_shape, F32))
        return pl.pallas_call(
            functools.partial(_ssd_scan_kernel, seq=seq, rev=rev, has_h0=has_h0, want_state=want_state,
                              add_skip=not rev),
            grid=(nb, nc),
            in_specs=in_specs,
            out_specs=out_specs,
            out_shape=out_shape,
            scratch_shapes=[pltpu.VMEM((SSD_STATE, SSD_INNER), F32)],
            input_output_aliases=aliases,
            compiler_params=_params(2),
            name="ssd_scan_%s_seq%d" % ("bwd" if rev else "fwd", seq),
        )

    common = [xbc, dt_raw, lw["dt_bias"], lw["a_log"], dsk, tri, rep]
    y, st = make_call(seq_p, nb_p, 0, False, True, 0)(*common)
    (y,) = make_call(seq_s, nb_s, t_p // seq_s, True, False, 1)(*common, h0, y)
    return y, st


def _ssd_out_kernel(yf_ref, yb_ref, z_ref, gn_ref, w_ref, h_ref, gt1, g2, sh2, sc2, wr_hi, wr_lo, br,
                    h1_ref, xn_ref, route_ref):
    y = (yf_ref[...] + yb_ref[...]) * _silu(z_ref[...].astype(F32))
    o = _dot(_rms(y, gn_ref[...]).astype(BF16), w_ref[...])
    _finish(h_ref[...], o, gt1[0], (g2, sh2, sc2, wr_hi, wr_lo, br), h1_ref, xn_ref, route_ref)


def _ssd_layer(h, state_f, state_b, j, lw, layer, dims):
    t = dims["t"]
    zxbc = _pre_proj(h, lw, layer, lw["w_in"], 1536, BF16, dims, n=3 * SSD_INNER)
    dt_raw = _pre_proj(h, lw, layer, lw["w_dt"], 128, F32, dims)
    xbc = _ssd_conv(zxbc, lw, dims)
    y_f, st_f = _ssd_scan(xbc, dt_raw, lw, state_f, j, False, dims)
    y_b, st_b = _ssd_scan(xbc, dt_raw, lw, state_b, j, True, dims)
    row_fn = _tile_row_fn(dims)
    tile = lambda: pl.BlockSpec((TM, D), lambda i: (i, 0))
    wide = lambda: pl.BlockSpec((TM, SSD_INNER), lambda i: (i, 0))
    outs = pl.pallas_call(
        _ssd_out_kernel,
        grid=(t // TM,),
        in_specs=[wide(), wide(), wide(), _full((1, SSD_INNER)), _full((SSD_INNER, D)), tile(),
                  _mod_spec(layer, 2, row_fn)] + _route_specs(layer, row_fn),
        out_specs=_stream_specs(TM, lambda i: i),
        out_shape=_stream_outs(t),
        compiler_params=_params(1),
        name="ssd_out",
    )(y_f, y_b, zxbc, lw["g_norm"], lw["w_out"], h, lw["mod"], *_route_args(lw))
    return outs, st_f, st_b


TS = 256
SRC = 512


def _moe_plan(route, t):
    i32 = jnp.int32
    npair = len(PAIR_ORDER)
    ncls = MOE_GROUPS * npair
    key = route[:, 0].astype(i32)
    oh = (key[:, None] == jnp.arange(ncls, dtype=i32)[None, :]).astype(i32)
    csum = jnp.cumsum(oh, axis=0)
    ccnt = csum[-1]
    cnt = ccnt.reshape(MOE_GROUPS, npair).sum(axis=1)
    padded = ((cnt + TS - 1) // TS) * TS
    gend = jnp.cumsum(padded)
    in_grp = jnp.cumsum(ccnt.reshape(MOE_GROUPS, npair), axis=1) - ccnt.reshape(MOE_GROUPS, npair)
    cstart = ((gend - padded)[:, None] + in_grp).reshape(ncls)
    pos = jnp.sum(oh * (csum - 1 + cstart[None, :]), axis=1)
    n_tiles = t // TS + MOE_GROUPS
    tile0 = jnp.arange(n_tiles, dtype=i32) * TS
    tile_grp = jnp.sum((tile0[:, None] >= gend[None, :]).astype(i32), axis=1)
    last_grp = jnp.max(jnp.where(cnt > 0, jnp.arange(MOE_GROUPS, dtype=i32), 0))
    tile_grp = jnp.minimum(tile_grp, last_grp)
    n_used = (gend[-1] // TS).reshape(1)
    touch = ((cstart[None, :] < tile0[:, None] + TS) & (cstart[None, :] + ccnt[None, :] > tile0[:, None])
             & (ccnt[None, :] > 0))
    uses = jnp.array([[int(e in PAIR_ORDER[c % npair]) for e in range(MOE_EPG)] for c in range(ncls)], i32)
    need = (jnp.dot(touch.astype(i32), uses) > 0).astype(i32).reshape(-1)
    return dict(pos=pos, tile_grp=tile_grp, n_used=n_used, need=need, n_tiles=n_tiles)


ROW_UNROLL = 8


def _moe_scatter_kernel(pos_ref, x_ref, xs_ref):
    n = pl.program_id(0)

    @pl.when(n == 0)
    def _():
        xs_ref[...] = jnp.zeros_like(xs_ref)

    def body(jj, carry):
        for r in range(ROW_UNROLL):
            j = jj * ROW_UNROLL + r
            xs_ref[pl.ds(pos_ref[n * SRC + j], 1), :] = x_ref[pl.ds(j, 1), :]
        return carry

    lax.fori_loop(0, SRC // ROW_UNROLL, body, 0)


def _moe_expert_kernel(grp_ref, nused_ref, need_ref, xs_ref, wg_ref, wu_ref, wd_ref, y_ref,
                       wg_b, wu_b, wd_b, acc_ref):
    i = pl.program_id(0)
    used = i < nused_ref[0]
    new_group = (i == 0) | (grp_ref[i] != grp_ref[jnp.maximum(i - 1, 0)])

    @pl.when(used & new_group)
    def _():
        for e in range(MOE_EPG):
            wg_b[e] = wg_ref[0, e].astype(BF16)
            wu_b[e] = wu_ref[0, e].astype(BF16)
            wd_b[e] = wd_ref[0, e].astype(BF16)

    @pl.when(used)
    def _():
        lo, hi = _unpack_pairs(xs_ref[...])
        x = jnp.concatenate([lo, hi[:, :D - PW]], axis=1).astype(BF16)
        rec = hi[:, D - PW:XW - PW]
        lane = lax.broadcasted_iota(jnp.int32, rec.shape, 1)
        acc_ref[...] = jnp.zeros_like(acc_ref)
        for e in range(MOE_EPG):
            @pl.when(need_ref[i * MOE_EPG + e] > 0)
            def _():
                hg = _dot(x, wg_b[e])
                hu = _dot(x, wu_b[e])
                mine = (lane % MOE_EPG == e) & (lane < 3 * MOE_EPG)
                cw = jnp.sum(jnp.where(mine, rec, 0.0), axis=-1, keepdims=True)
                acc_ref[...] += _dot((_silu(hg) * hu * cw).astype(BF16), wd_b[e])

        acc = acc_ref[...].astype(BF16).astype(F32)
        y_ref[...] = _pack_pairs(acc[:, :D // 2], acc[:, D // 2:])

    @pl.when(i >= nused_ref[0])
    def _():
        y_ref[...] = jnp.zeros_like(y_ref)


def _moe_ungather_kernel(pos_ref, ys_ref, h1_ref, gt2, fg_ref, *rest, n_prompt_tiles):
    rows_ref = rest[-1]
    out_refs = rest[:-1]
    n = pl.program_id(0)

    def body(jj, carry):
        for r in range(ROW_UNROLL):
            j = jj * ROW_UNROLL + r
            rows_ref[pl.ds(j, 1), :] = ys_ref[pl.ds(pos_ref[n * SRC + j], 1), :]
        return carry

    lax.fori_loop(0, SRC // ROW_UNROLL, body, 0)
    h2 = h1_ref[...] + gt2[0] * jnp.concatenate(_unpack_pairs(rows_ref[...]), axis=1)
    if n_prompt_tiles is None:
        out_refs[0][...] = h2
    else:
        y = _rms(h2, fg_ref[...])

        @pl.when(n < n_prompt_tiles)
        def _():
            out_refs[0][...] = y

        @pl.when(n >= n_prompt_tiles)
        def _():
            out_refs[1][...] = y


def _moe_sparse(h1, xn, route, lw, layer, final_g, final, dims):
    t = dims["t"]
    plan = _moe_plan(route, t)
    n_tiles = plan["n_tiles"]
    n_rows = n_tiles * TS
    one = pl.Buffered(1)
    xs = pl.pallas_call(
        _moe_scatter_kernel,
        grid_spec=pltpu.PrefetchScalarGridSpec(
            num_scalar_prefetch=1,
            grid=(t // SRC,),
            in_specs=[pl.BlockSpec((SRC, PW), lambda n, pos: (n, 0))],
            out_specs=pl.BlockSpec((n_rows, PW), lambda n, pos: (0, 0))),
        out_shape=jax.ShapeDtypeStruct((n_rows, PW), U32),
        compiler_params=_params(1),
        name="moe_scatter_l%d" % layer,
    )(plan["pos"], xn)

    ex = lambda i, g, nu, nd: (layer, g[i], 0, 0)
    ys = pl.pallas_call(
        _moe_expert_kernel,
        grid_spec=pltpu.PrefetchScalarGridSpec(
            num_scalar_prefetch=3,
            grid=(n_tiles,),
            in_specs=[pl.BlockSpec((TS, PW), lambda i, g, nu, nd: (i, 0)),
                      pl.BlockSpec((1, MOE_EPG, D, MOE_DFF), ex, pipeline_mode=one),
                      pl.BlockSpec((1, MOE_EPG, D, MOE_DFF), ex, pipeline_mode=one),
                      pl.BlockSpec((1, MOE_EPG, MOE_DFF, D), ex, pipeline_mode=one)],
            out_specs=pl.BlockSpec((TS, D // 2), lambda i, g, nu, nd: (i, 0)),
            scratch_shapes=[pltpu.VMEM((MOE_EPG, D, MOE_DFF), BF16), pltpu.VMEM((MOE_EPG, D, MOE_DFF), BF16),
                            pltpu.VMEM((MOE_EPG, MOE_DFF, D), BF16), pltpu.VMEM((TS, D), F32)]),
        out_shape=jax.ShapeDtypeStruct((n_rows, D // 2), U32),
        compiler_params=_params(1),
        name="moe_expert_l%d" % layer,
    )(plan["tile_grp"], plan["n_used"], plan["need"], xs, lw["w_gate"], lw["w_up"], lw["w_down"])

    row_fn = _tile_row_fn(dims, SRC)
    tile = lambda: pl.BlockSpec((SRC, D), lambda n, pos: (n, 0))
    if final:
        npt = dims["t_p"] // SRC
        out_specs = [pl.BlockSpec((SRC, D), lambda n, pos: (jnp.minimum(n, npt - 1), 0)),
                     pl.BlockSpec((SRC, D), lambda n, pos: (jnp.maximum(n - npt, 0), 0))]
        out_shape = [jax.ShapeDtypeStruct((dims["t_p"], D), F32), jax.ShapeDtypeStruct((t - dims["t_p"], D), F32)]
    else:
        npt = None
        out_specs = [tile()]
        out_shape = [jax.ShapeDtypeStruct((t, D), F32)]
    return pl.pallas_call(
        functools.partial(_moe_ungather_kernel, n_prompt_tiles=npt),
        grid_spec=pltpu.PrefetchScalarGridSpec(
            num_scalar_prefetch=1,
            grid=(t // SRC,),
            in_specs=[pl.BlockSpec((n_rows, D // 2), lambda n, pos: (0, 0)), tile(),
                      pl.BlockSpec((1, 1, D), lambda n, pos: ((layer * MOD_ROWS + row_fn(n)) * N_MOD + 5, 0, 0)),
                      pl.BlockSpec((1, D), lambda n, pos: (0, 0))],
            out_specs=out_specs,
            scratch_shapes=[pltpu.VMEM((SRC, D // 2), U32)]),
        out_shape=out_shape,
        compiler_params=_params(1),
        name="moe_ungather_l%d" % layer,
    )(plan["pos"], ys, h1, lw["mod"], final_g)


def kernel(x_prompt, x_sample, cache_k, cache_v, state_ssm_fwd, state_ssm_bwd, c, c_ctx, ada_w, ada_b, norm1_g, norm2_g, final_g, fnet_w_o, na_w_qkv, na_w_o, na_rpb, gmlp_w_in, gmlp_g_v, gmlp_w_s, gmlp_b_s, gmlp_w_out, ssd_w_in, ssd_conv_w, ssd_conv_b, ssd_a_log, ssd_dt_bias, ssd_d_skip, ssd_g_norm, ssd_w_out, moe_w_gr, moe_b_gr, moe_w_er, moe_b_er, moe_w_gate, moe_w_up, moe_w_down):
    nb_p, seq_p, _ = x_prompt.shape
    nb_s, seq_s, _ = x_sample.shape
    depth = ada_w.shape[0]
    t_p, t_s = nb_p * seq_p, nb_s * seq_s
    dims = dict(t=t_p + t_s, t_p=t_p, seq_p=seq_p, seq_s=seq_s, nb_p=nb_p, nb_s=nb_s)
    assert 1 + nb_s <= MOD_ROWS and t_p % seq_s == 0 and t_p % TM == 0 and seq_s % TM == 0

    cond = jnp.zeros((MOD_ROWS, D), F32).at[0].set(c_ctx).at[1:1 + nb_s].set(c)
    mod = _ada_table(cond, ada_w, ada_b)
    h = (x_prompt.reshape(t_p, D), x_sample.reshape(t_s, D))
    fg = final_g.reshape(1, D)

    new_k, new_v, new_sf, new_sb = [], [], [], []
    for l in range(depth):
        kind, j = l % 4, l // 4
        w_r = jnp.concatenate([moe_w_gr[l], moe_w_er[l]], axis=1)
        w_r = jnp.pad(w_r, ((0, 0), (0, ROUTE_W - MOE_GROUPS - MOE_EXPERTS)))
        b_r = jnp.pad(jnp.concatenate([moe_b_gr[l], moe_b_er[l]]), (0, ROUTE_W - MOE_GROUPS - MOE_EXPERTS))
        wr_hi = w_r.astype(BF16)
        lw = dict(mod=mod, g1=norm1_g[l].reshape(1, D), g2=norm2_g[l].reshape(1, D),
                  wr_hi=wr_hi, wr_lo=(w_r - wr_hi.astype(F32)).astype(BF16), br=b_r.reshape(1, ROUTE_W),
                  w_gate=moe_w_gate, w_up=moe_w_up, w_down=moe_w_down)
        if kind == 0:
            lw.update(w_o=fnet_w_o[j].astype(BF16))
            h1, xn, route = _fnet_layer(h, lw, l, dims)
        elif kind == 1:
            lw.update(w_qkv=na_w_qkv[j], w_o=na_w_o[j].astype(BF16), rpb=na_rpb[j])
            (h1, xn, route), kc, vc = _na_layer(h, cache_k, cache_v, j, lw, l, dims)
            new_k.append(kc)
            new_v.append(vc)
        elif kind == 2:
            gw = GMLP_DFF // GMLP_GROUPS
            lw.update(w_in=gmlp_w_in[j].astype(BF16), g_v=gmlp_g_v[j].reshape(1, GMLP_DFF),
                      w_s=gmlp_w_s[j].astype(BF16),
                      b_s=jnp.broadcast_to(gmlp_b_s[j][:, :, None], (GMLP_GROUPS, GMLP_CHUNK, gw)),
                      w_out=gmlp_w_out[j].astype(BF16))
            h1, xn, route = _gmlp_layer(h, lw, l, dims)
        else:
            n_main = 3 * SSD_INNER
            w_in = ssd_w_in[j]
            pad = lambda v: jnp.pad(v, ((0, 0), (0, 128 - 2 * SSD_HEADS)))
            lw.update(w_in=w_in, w_dt=pad(w_in[:, n_main:]),
                      conv_w=ssd_conv_w[j], conv_b=ssd_conv_b[j].reshape(1, -1),
                      dt_bias=pad(ssd_dt_bias[j].reshape(1, -1)), a_log=pad(ssd_a_log[j].reshape(1, -1)),
                      d_skip=ssd_d_skip[j], g_norm=ssd_g_norm[j].reshape(1, SSD_INNER),
                      w_out=ssd_w_out[j].astype(BF16))
            (h1, xn, route), sf, sb = _ssd_layer(h, state_ssm_fwd, state_ssm_bwd, j, lw, l, dims)
            new_sf.append(sf)
            new_sb.append(sb)
        out = _moe_sparse(h1, xn, route, lw, l, fg, l == depth - 1, dims)
        h = out if l == depth - 1 else out[0]

    y_prompt = h[0].reshape(nb_p, seq_p, D)
    y_sample = h[1].reshape(nb_s, seq_s, D)
    cat = lambda xs: jnp.concatenate(xs, axis=1)
    return (y_prompt, y_sample, cat(new_k), cat(new_v), cat(new_sf), cat(new_sb))
```

```python
import functools
import math

import jax
import jax.numpy as jnp
from jax import lax
from jax.experimental import pallas as pl
from jax.experimental.pallas import tpu as pltpu

F32 = jnp.float32
BF16 = jnp.bfloat16

D = 1024
EPS = 1e-6
NEG = -1e30
N_MOD = 6
MOD_ROWS = 8
GRID_W = 64
FNET_GROUPS = 8
NA_HEADS = 16
NA_HD = 64
NA_WIN_ROWS = 8
NA_WIN_COLS = 16
GMLP_CHUNK = 128
GMLP_DFF = 2048
GMLP_GROUPS = 8
SSD_INNER = 2048
SSD_HD = 64
SSD_HEADS = 32
SSD_GROUPS = 8
SSD_STATE = 128
SSD_CHUNK = 128
SSD_HPG = SSD_HEADS // SSD_GROUPS
MOE_GROUPS = 4
MOE_EPG = 4
MOE_EXPERTS = 16
MOE_DFF = 512
ROUTE_W = 128
ROUTE_E0 = 4
PAIR_ORDER = ((0, 1), (0, 2), (0, 3), (1, 3), (1, 2), (2, 3))

TM = 512
VMEM_LIMIT = 56 * 1024 * 1024


def _dot(a, b):
    return jnp.dot(a, b, preferred_element_type=F32)


def _dot_nt(a, b):
    return lax.dot_general(a, b, (((1,), (1,)), ((), ())), preferred_element_type=F32)


def _dot_tn(a, b):
    return lax.dot_general(a, b, (((0,), (0,)), ((), ())), preferred_element_type=F32)


def _silu(x):
    hx = 0.5 * x
    return hx * (1.0 + jnp.tanh(hx))


def _rms(x, g):
    return x * lax.rsqrt(jnp.mean(x * x, axis=-1, keepdims=True) + EPS) * g


def _normmod(x, g, shift, scale):
    return _rms(x, g) * (1.0 + scale) + shift


def _split3(x):
    hi = x.astype(BF16)
    r = x - hi.astype(F32)
    mid = r.astype(BF16)
    lo = (r - mid.astype(F32)).astype(BF16)
    return hi, mid, lo


def _params(n_axes):
    return pltpu.CompilerParams(dimension_semantics=("arbitrary",) * n_axes,
                                vmem_limit_bytes=VMEM_LIMIT)


def _full(shape):
    nd = len(shape)
    return pl.BlockSpec(shape, lambda *_: (0,) * nd)


def _mod_spec(layer, k, row_fn):
    return pl.BlockSpec((1, 1, D), lambda *idx: ((layer * MOD_ROWS + row_fn(*idx)) * N_MOD + k, 0, 0))


def _ada_kernel(c_ref, w_ref, b_ref, o_ref):
    c = c_ref[...]
    o_ref[0] = _dot(_silu(c).astype(BF16), w_ref[0].astype(BF16)) + b_ref[0]


def _ada_table(cond, ada_w, ada_b):
    depth = ada_w.shape[0]
    n = N_MOD * D
    tn = 1536
    out = pl.pallas_call(
        _ada_kernel,
        grid=(depth, n // tn),
        in_specs=[_full((MOD_ROWS, D)),
                  pl.BlockSpec((1, D, tn), lambda l, j: (l, 0, j)),
                  pl.BlockSpec((1, 1, tn), lambda l, j: (l, 0, j))],
        out_specs=pl.BlockSpec((1, MOD_ROWS, tn), lambda l, j: (l, 0, j)),
        out_shape=jax.ShapeDtypeStruct((depth, MOD_ROWS, n), F32),
        compiler_params=_params(2),
        name="ada_table",
    )(cond, ada_w, ada_b.reshape(depth, 1, n))
    return out.reshape(depth * MOD_ROWS * N_MOD, 1, D)


def _route(h1, g2, sh2, sc2, wr_hi, wr_lo, br):
    xn = _normmod(h1, g2, sh2, sc2)
    xh = xn.astype(BF16)
    xl = (xn - xh.astype(F32)).astype(BF16)
    logits = _dot(xh, wr_hi) + _dot(xh, wr_lo) + _dot(xl, wr_hi) + br
    lane = lax.broadcasted_iota(jnp.int32, logits.shape, 1).astype(F32)
    far = float(ROUTE_W)
    gl = jnp.where(lane < MOE_GROUPS, logits, NEG)
    gmax = jnp.max(gl, axis=-1, keepdims=True)
    g_p = 1.0 / jnp.sum(jnp.exp(gl - gmax), axis=-1, keepdims=True)
    gidx = jnp.min(jnp.where(gl == gmax, lane, far), axis=-1, keepdims=True)
    lo = ROUTE_E0 + MOE_EPG * gidx
    el = jnp.where((lane >= lo) & (lane < lo + MOE_EPG), logits, NEG)
    m1 = jnp.max(el, axis=-1, keepdims=True)
    i1 = jnp.min(jnp.where(el == m1, lane, far), axis=-1, keepdims=True)
    el2 = jnp.where(lane == i1, NEG, el)
    m2 = jnp.max(el2, axis=-1, keepdims=True)
    i2 = jnp.min(jnp.where(el2 == m2, lane, far), axis=-1, keepdims=True)
    e2 = jnp.exp(m2 - m1)
    w1 = g_p / (1.0 + e2)
    w2 = w1 * e2
    rec = jnp.zeros_like(logits)
    for part, (a, b) in enumerate(zip(_split3(w1), _split3(w2))):
        shift = part * MOE_EPG - lo
        rec = (rec + jnp.where(lane == i1 + shift, a.astype(F32), 0.0)
               + jnp.where(lane == i2 + shift, b.astype(F32), 0.0))
    ea = jnp.minimum(i1, i2) - lo
    eb = jnp.maximum(i1, i2) - lo
    pair = sum(jnp.where((ea == a) & (eb == b), float(k), 0.0) for k, (a, b) in enumerate(PAIR_ORDER))
    key = gidx * float(len(PAIR_ORDER)) + pair
    return xh, rec.astype(BF16), jnp.broadcast_to(key, logits.shape)


def _route_specs(layer, row_fn):
    return [_full((1, D)), _mod_spec(layer, 3, row_fn), _mod_spec(layer, 4, row_fn),
            _full((D, ROUTE_W)), _full((D, ROUTE_W)), _full((1, ROUTE_W))]


def _route_args(lw):
    return [lw["g2"], lw["mod"], lw["mod"], lw["wr_hi"], lw["wr_lo"], lw["br"]]


def _finish(h, o, gate, rt_refs, h1_ref, xn_ref, route_ref):
    g2, sh2, sc2, wr_hi, wr_lo, br = rt_refs
    h1 = h + gate * o
    h1_ref[...] = h1
    xn, rec, gid = _route(h1, g2[...], sh2[0], sc2[0], wr_hi[...], wr_lo[...], br[...])
    row = jnp.concatenate([xn.astype(F32), rec.astype(F32), jnp.zeros((h.shape[0], 2 * PW - XW), F32)], axis=1)
    xn_ref[...] = _pack_pairs(row[:, :PW], row[:, PW:])
    route_ref[...] = gid


XW = D + ROUTE_W
PW = 640
U32 = jnp.uint32


def _pack_pairs(lo, hi):
    lo_bits = lax.bitcast_convert_type(lo, U32) >> 16
    hi_bits = lax.bitcast_convert_type(hi, U32) & U32(0xFFFF0000)
    return lo_bits | hi_bits


def _unpack_pairs(w):
    return lax.bitcast_convert_type(w << 16, F32), lax.bitcast_convert_type(w & U32(0xFFFF0000), F32)


def _stream_outs(t):
    return [jax.ShapeDtypeStruct((t, D), F32), jax.ShapeDtypeStruct((t, PW), U32),
            jax.ShapeDtypeStruct((t, ROUTE_W), F32)]


def _stream_specs(rows, row_block):
    return [pl.BlockSpec((rows, w), lambda *idx: (row_block(*idx), 0)) for w in (D, PW, ROUTE_W)]


def _fnet_kernel(h_ref, g1, sh1, sc1, gt1, csc_ref, fs_ref, wo_ref, g2, sh2, sc2, wr_hi, wr_lo, br,
                 *rest, seq):
    h1_ref, xn_ref, route_ref, ab_ref = rest[-4:]
    h = h_ref[...]
    a = _normmod(h, g1[...], sh1[0], sc1[0]).astype(BF16)
    gd = D // FNET_GROUPS
    for g in range(FNET_GROUPS):
        ab = _dot(a[:, g * gd:(g + 1) * gd], csc_ref[...])
        ab_ref[0:seq, g * gd:(g + 1) * gd] = ab[:, :gd].astype(BF16)
        ab_ref[seq:2 * seq, g * gd:(g + 1) * gd] = ab[:, gd:].astype(BF16)
    f = _dot(fs_ref[...], ab_ref[...])
    o = _dot(f.astype(BF16), wo_ref[...])
    _finish(h, o, gt1[0], (g2, sh2, sc2, wr_hi, wr_lo, br), h1_ref, xn_ref, route_ref)


def _dft_tables(n):
    k = jnp.arange(n, dtype=jnp.int32)
    ang = ((k[:, None] * k[None, :]) % n).astype(F32) * (2.0 * math.pi / n)
    s = 1.0 / math.sqrt(n)
    return jnp.cos(ang) * s, jnp.sin(ang) * s


def _fnet_layer(h, lw, layer, dims):
    t, t_p = dims["t"], dims["t_p"]
    gd = D // FNET_GROUPS
    cc, sc = _dft_tables(gd)
    csc = jnp.concatenate([cc, sc], axis=1).astype(BF16)

    split_in = isinstance(h, tuple)

    def make_call(seq, nb, off, n_alias):
        cs, ss = _dft_tables(seq)
        fs = jnp.concatenate([cs, -ss], axis=1).astype(BF16)
        row_fn = (lambda b: 0) if off == 0 else (lambda b: 1 + b)
        in_off = 0 if split_in else off
        tile = lambda: pl.BlockSpec((seq, D), lambda b: (off + b, 0))
        in_specs = ([pl.BlockSpec((seq, D), lambda b: (in_off + b, 0)), _full((1, D))]
                    + [_mod_spec(layer, k, row_fn) for k in (0, 1, 2)]
                    + [_full((gd, 2 * gd)), _full((seq, 2 * seq)), _full((D, D))]
                    + _route_specs(layer, row_fn))
        aliases = {}
        if n_alias:
            base = len(in_specs)
            in_specs = in_specs + [pl.BlockSpec(memory_space=pl.ANY)] * n_alias
            aliases = {base + i: i for i in range(n_alias)}
        call = pl.pallas_call(
            functools.partial(_fnet_kernel, seq=seq),
            grid=(nb,),
            in_specs=in_specs,
            out_specs=_stream_specs(seq, lambda b: off + b),
            out_shape=_stream_outs(t),
            scratch_shapes=[pltpu.VMEM((2 * seq, D), BF16)],
            input_output_aliases=aliases,
            compiler_params=_params(1),
            name="fnet_seq%d" % seq,
        )
        return lambda *a: call(*a[:2], *a[2:5], a[5], fs, *a[6:])

    h_p, h_s = h if split_in else (h, h)
    args = [lw["g1"], lw["mod"], lw["mod"], lw["mod"], csc, lw["w_o"]] + _route_args(lw)
    outs = make_call(dims["seq_p"], dims["nb_p"], 0, None)(h_p, *args)
    return make_call(dims["seq_s"], dims["nb_s"], t_p // dims["seq_s"], 3)(h_s, *args, *outs)


def _pre_kernel(h_ref, g1, sh1, sc1, w_ref, o_ref, a_ref):
    @pl.when(pl.program_id(1) == 0)
    def _():
        a_ref[...] = _normmod(h_ref[...], g1[...], sh1[0], sc1[0]).astype(BF16)

    o_ref[...] = _dot(a_ref[...], w_ref[...].astype(BF16)).astype(o_ref.dtype)


def _tile_row_fn(dims, tm=TM):
    npt = dims["t_p"] // tm
    tps = dims["seq_s"] // tm
    return lambda i, *_: jnp.where(i < npt, 0, 1 + (i - npt) // tps)


def _pre_proj(h, lw, layer, w, tn, out_dtype, dims, n=None):
    t = dims["t"]
    n = w.shape[1] if n is None else n
    tm = dims["seq_s"]
    row_fn = _tile_row_fn(dims, tm)
    return pl.pallas_call(
        _pre_kernel,
        grid=(t // tm, n // tn),
        in_specs=[pl.BlockSpec((tm, D), lambda i, j: (i, 0)), _full((1, D)),
                  _mod_spec(layer, 0, row_fn), _mod_spec(layer, 1, row_fn),
                  pl.BlockSpec((D, tn), lambda i, j: (0, j))],
        out_specs=pl.BlockSpec((tm, tn), lambda i, j: (i, j)),
        out_shape=jax.ShapeDtypeStruct((t, n), out_dtype),
        scratch_shapes=[pltpu.VMEM((tm, D), BF16)],
        compiler_params=_params(2),
        name="pre_proj_l%d_n%d" % (layer, n),
    )(h, lw["g1"], lw["mod"], lw["mod"], w)


def _out_kernel(o_ref, w_ref, h_ref, gt1, g2, sh2, sc2, wr_hi, wr_lo, br, h1_ref, xn_ref, route_ref):
    o = _dot(o_ref[...], w_ref[...])
    _finish(h_ref[...], o, gt1[0], (g2, sh2, sc2, wr_hi, wr_lo, br), h1_ref, xn_ref, route_ref)


def _out_proj(o, w, h, lw, layer, dims):
    t = dims["t"]
    k = o.shape[1]
    row_fn = _tile_row_fn(dims)
    tile = lambda: pl.BlockSpec((TM, D), lambda i: (i, 0))
    return pl.pallas_call(
        _out_kernel,
        grid=(t // TM,),
        in_specs=[pl.BlockSpec((TM, k), lambda i: (i, 0)), _full((k, D)), tile(),
                  _mod_spec(layer, 2, row_fn)] + _route_specs(layer, row_fn),
        out_specs=_stream_specs(TM, lambda i: i),
        out_shape=_stream_outs(t),
        compiler_params=_params(1),
        name="out_proj_l%d" % layer,
    )(o, w, h, lw["mod"], *_route_args(lw))


NA_SCALE = NA_HD ** -0.5


NA_HEAD_BLOCK = 4


def _na_ctx_kernel(q_ref, k_ref, v_ref, o_ref, kc_ref, vc_ref):
    seq = q_ref.shape[0]
    bw = NA_HEAD_BLOCK * NA_HD
    outs = []
    for hb in range(NA_HEADS // NA_HEAD_BLOCK):
        cols = slice(hb * bw, (hb + 1) * bw)
        q4 = q_ref[:, cols] * NA_SCALE
        k4 = k_ref[:, cols]
        v4 = v_ref[:, cols]
        head_of_lane = lax.broadcasted_iota(jnp.int32, k4.shape, 1) // NA_HD
        kbd = jnp.concatenate([jnp.where(head_of_lane == h, k4, jnp.zeros_like(k4))
                               for h in range(NA_HEAD_BLOCK)], axis=0)
        vbd = jnp.concatenate([jnp.where(head_of_lane == h, v4, jnp.zeros_like(v4))
                               for h in range(NA_HEAD_BLOCK)], axis=0)
        s = _dot_nt(q4, kbd)
        ps = []
        l4 = jnp.zeros((seq, bw), F32)
        for h in range(NA_HEAD_BLOCK):
            sh = s[:, h * seq:(h + 1) * seq]
            ph = jnp.exp(sh - jnp.max(sh, axis=-1, keepdims=True))
            l4 = jnp.where(head_of_lane == h, jnp.sum(ph, axis=-1, keepdims=True), l4)
            ps.append(ph.astype(BF16))
            hd = hb * NA_HEAD_BLOCK + h
            kc_ref[0, 0, hd] = k4[:, h * NA_HD:(h + 1) * NA_HD].astype(F32)
            vc_ref[0, 0, hd] = v4[:, h * NA_HD:(h + 1) * NA_HD].astype(F32)
        outs.append((_dot(jnp.concatenate(ps, axis=1), vbd) / l4).astype(BF16))
    o_ref[...] = jnp.concatenate(outs, axis=1)


def _na_row_start(qr, rows):
    kr = min(NA_WIN_ROWS, rows)
    return min(max(qr - kr // 2, 0), rows - kr)


def _na_window_bias(bias_ref, hh, qr, rows, m_lo, m_hi):
    kr = min(NA_WIN_ROWS, rows)
    rs = _na_row_start(qr, rows)
    blocks = []
    for m in range(m_lo, m_hi):
        ok0 = rs <= 2 * m < rs + kr
        ok1 = rs <= 2 * m + 1 < rs + kr
        e = 2 * m - qr + NA_WIN_ROWS
        if ok0 and ok1:
            blocks.append(bias_ref[hh, 0, e])
        elif ok1:
            blocks.append(bias_ref[hh, 1, e])
        elif ok0:
            blocks.append(bias_ref[hh, 2, e])
        else:
            blocks.append(jnp.full((GRID_W, 2 * GRID_W), NEG, F32))
    return jnp.concatenate(blocks, axis=1)


def _na_lat_kernel(q_ref, k_ref, v_ref, bias_ref, kc_ref, vc_ref, o_in, o_ref, *, seq, qb):
    del o_in
    rows = seq // GRID_W
    for hh in range(2):
        sl = slice(hh * NA_HD, (hh + 1) * NA_HD)
        k = k_ref[:, sl]
        v = v_ref[:, sl]
        kc = kc_ref[0, 0, hh].astype(BF16)
        vc = vc_ref[0, 0, hh].astype(BF16)
        for b0 in range(0, seq, qb):
            q = q_ref[b0:b0 + qb, sl] * NA_SCALE
            qrs = range(b0 // GRID_W, (b0 + qb) // GRID_W)
            m_lo = _na_row_start(qrs[0], rows) // 2
            m_hi = (_na_row_start(qrs[-1], rows) + min(NA_WIN_ROWS, rows) + 1) // 2
            keys = slice(m_lo * 2 * GRID_W, m_hi * 2 * GRID_W)
            bias = jnp.concatenate([_na_window_bias(bias_ref, hh, qr, rows, m_lo, m_hi) for qr in qrs], axis=0)
            s1 = _dot_nt(q, k[keys]) + bias
            s2 = _dot_nt(q, kc)
            m = jnp.maximum(jnp.max(s1, axis=-1, keepdims=True), jnp.max(s2, axis=-1, keepdims=True))
            p1 = jnp.exp(s1 - m)
            p2 = jnp.exp(s2 - m)
            l = jnp.sum(p1, axis=-1, keepdims=True) + jnp.sum(p2, axis=-1, keepdims=True)
            o = (_dot(p1.astype(BF16), v[keys]) + _dot(p2.astype(BF16), vc)) / l
            o_ref[b0:b0 + qb, sl] = o.astype(BF16)


def _na_bias_tables(rpb):
    c = jnp.arange(GRID_W)
    win0 = jnp.clip(c - NA_WIN_COLS // 2, 0, GRID_W - NA_WIN_COLS)
    ok_c = (c[None, :] >= win0[:, None]) & (c[None, :] < win0[:, None] + NA_WIN_COLS)
    dc = jnp.clip(c[None, :] - c[:, None], 1 - NA_WIN_COLS, NA_WIN_COLS - 1) + NA_WIN_COLS - 1
    nh, ndr, ndc = rpb.shape
    pick = (dc.reshape(1, -1) == jnp.arange(ndc)[:, None]).astype(F32)
    cm = jnp.dot(rpb.reshape(nh * ndr, ndc), pick, precision=lax.Precision.HIGHEST)
    cm = jnp.where(ok_c[None, None], cm.reshape(nh, ndr, GRID_W, GRID_W), NEG)
    neg = jnp.full_like(cm[:, :1], NEG)
    ext = jnp.concatenate([neg, cm, neg], axis=1)
    a, b = ext[:, :-1], ext[:, 1:]
    negs = jnp.full_like(a, NEG)
    pair = lambda x, y: jnp.concatenate([x, y], axis=-1)
    return jnp.stack([pair(a, b), pair(negs, b), pair(a, negs)], axis=1)


def _na_layer(h, cache_k, cache_v, j, lw, layer, dims):
    t, t_p, seq_p, seq_s = dims["t"], dims["t_p"], dims["seq_p"], dims["seq_s"]
    nb_p, nb_s = dims["nb_p"], dims["nb_s"]
    qkv = _pre_proj(h, lw, layer, lw["w_qkv"], 1536, BF16, dims)
    cshape = (nb_p, 1, NA_HEADS, seq_p, NA_HD)
    cspec = lambda: pl.BlockSpec((1, 1, NA_HEADS, seq_p, NA_HD), lambda b: (b, 0, 0, 0, 0))
    o, kc, vc = pl.pallas_call(
        _na_ctx_kernel,
        grid=(nb_p,),
        in_specs=[pl.BlockSpec((seq_p, D), lambda b: (b, 0)), pl.BlockSpec((seq_p, D), lambda b: (b, 1)),
                  pl.BlockSpec((seq_p, D), lambda b: (b, 2))],
        out_specs=[pl.BlockSpec((seq_p, D), lambda b: (b, 0)), cspec(), cspec()],
        out_shape=[jax.ShapeDtypeStruct((t, D), BF16), jax.ShapeDtypeStruct(cshape, F32),
                   jax.ShapeDtypeStruct(cshape, F32)],
        compiler_params=_params(1),
        name="na_context",
    )(qkv, qkv, qkv)

    bias = _na_bias_tables(lw["rpb"])
    past = cache_k.shape[3]
    off = t_p // seq_s
    npair = NA_HEADS // 2
    pw = 2 * NA_HD
    pspec = lambda: pl.BlockSpec((1, 1, 2, past, NA_HD), lambda hp, b: (b, j, hp, 0, 0))
    o = pl.pallas_call(
        functools.partial(_na_lat_kernel, seq=seq_s, qb=256),
        grid=(npair, nb_s),
        in_specs=[pl.BlockSpec((seq_s, pw), lambda hp, b: (off + b, hp)),
                  pl.BlockSpec((seq_s, pw), lambda hp, b: (off + b, npair + hp)),
                  pl.BlockSpec((seq_s, pw), lambda hp, b: (off + b, 2 * npair + hp)),
                  pl.BlockSpec((2, 3, 2 * NA_WIN_ROWS, GRID_W, 2 * GRID_W), lambda hp, b: (hp, 0, 0, 0, 0)),
                  pspec(), pspec(), pl.BlockSpec(memory_space=pl.ANY)],
        out_specs=pl.BlockSpec((seq_s, pw), lambda hp, b: (off + b, hp)),
        out_shape=jax.ShapeDtypeStruct((t, D), BF16),
        input_output_aliases={6: 0},
        compiler_params=_params(2),
        name="na_latent",
    )(qkv, qkv, qkv, bias, cache_k, cache_v, o)
    outs = _out_proj(o, lw["w_o"], h, lw, layer, dims)
    return outs, kc, vc


def _gelu_tanh(x):
    return 0.5 * x * (1.0 + jnp.tanh(math.sqrt(2.0 / math.pi) * (x + 0.044715 * (x * x * x))))


def _gmlp_kernel(h_ref, g1, sh1, sc1, gt1, win_ref, gv_ref, ws_ref, bs_ref, wout_ref,
                 g2, sh2, sc2, wr_hi, wr_lo, br, h1_ref, xn_ref, route_ref, m_ref):
    h = h_ref[...]
    a = _normmod(h, g1[...], sh1[0], sc1[0]).astype(BF16)
    u = _gelu_tanh(_dot(a, win_ref[:, :GMLP_DFF]))
    v = _gelu_tanh(_dot(a, win_ref[:, GMLP_DFF:]))
    v = _rms(v, gv_ref[...]).astype(BF16)
    gw = GMLP_DFF // GMLP_GROUPS
    for c in range(TM // GMLP_CHUNK):
        rows = slice(c * GMLP_CHUNK, (c + 1) * GMLP_CHUNK)
        for g in range(GMLP_GROUPS):
            cols = slice(g * gw, (g + 1) * gw)
            vs = _dot(ws_ref[g], v[rows, cols]) + bs_ref[g]
            m_ref[rows, cols] = (u[rows, cols] * vs).astype(BF16)
    o = _dot(m_ref[...], wout_ref[...])
    _finish(h, o, gt1[0], (g2, sh2, sc2, wr_hi, wr_lo, br), h1_ref, xn_ref, route_ref)


def _gmlp_layer(h, lw, layer, dims):
    t = dims["t"]
    row_fn = _tile_row_fn(dims)
    gw = GMLP_DFF // GMLP_GROUPS
    tile = lambda: pl.BlockSpec((TM, D), lambda i: (i, 0))
    one = pl.Buffered(1)
    return pl.pallas_call(
        _gmlp_kernel,
        grid=(t // TM,),
        in_specs=[tile(), _full((1, D))] + [_mod_spec(layer, k, row_fn) for k in (0, 1, 2)]
                 + [pl.BlockSpec((D, 2 * GMLP_DFF), lambda i: (0, 0), pipeline_mode=one),
                    _full((1, GMLP_DFF)), _full((GMLP_GROUPS, GMLP_CHUNK, GMLP_CHUNK)),
                    _full((GMLP_GROUPS, GMLP_CHUNK, gw)),
                    pl.BlockSpec((GMLP_DFF, D), lambda i: (0, 0), pipeline_mode=one)]
                 + _route_specs(layer, row_fn),
        out_specs=_stream_specs(TM, lambda i: i),
        out_shape=_stream_outs(t),
        scratch_shapes=[pltpu.VMEM((TM, GMLP_DFF), BF16)],
        compiler_params=_params(1),
        name="gmlp",
    )(h, lw["g1"], lw["mod"], lw["mod"], lw["mod"], lw["w_in"], lw["g_v"], lw["w_s"], lw["b_s"],
      lw["w_out"], *_route_args(lw))


HALO = 16


CONV_TAPS = 4
CONV_LEFT = CONV_TAPS // 2
CONV_SHIFTED = tuple(k for k in range(CONV_TAPS) if k != CONV_LEFT)


def _ssd_conv_kernel(x_ref, bc_ref, cw_ref, cb_ref, sh_ref, *rest, seq):
    o_ref = rest[-1]
    L = SSD_CHUNK
    nc = seq // L
    c = pl.program_id(1)
    r0 = pl.multiple_of(c * L, L)
    rp = pl.multiple_of(jnp.maximum(r0 - HALO, 0), HALO)
    rn = pl.multiple_of(jnp.minimum(r0 + L, seq - HALO), HALO)
    outs = []
    for src, lo in ((x_ref, 0), (bc_ref, SSD_INNER)):
        cur = src[pl.ds(r0, L), :]
        prev = src[pl.ds(rp, HALO), :]
        nxt = src[pl.ds(rn, HALO), :]
        win = jnp.concatenate([jnp.where(c > 0, prev, jnp.zeros_like(prev)), cur,
                               jnp.where(c < nc - 1, nxt, jnp.zeros_like(nxt))], axis=0)
        shifted = _dot(sh_ref[...], win)
        w = lambda k: cw_ref[k:k + 1, lo:lo + SSD_INNER]
        conv = cb_ref[:, lo:lo + SSD_INNER] + w(CONV_LEFT) * cur.astype(F32)
        for i, k in enumerate(CONV_SHIFTED):
            conv = conv + w(k) * shifted[i * L:(i + 1) * L, :]
        outs.append(_silu(conv).astype(BF16))
    o_ref[...] = jnp.concatenate(outs, axis=1)


def _ssd_conv(zxbc, lw, dims):
    t, t_p = dims["t"], dims["t_p"]
    L = SSD_CHUNK

    def make_call(seq, nb, off, aliased):
        nc = seq // L
        in_specs = [pl.BlockSpec((seq, SSD_INNER), lambda b, c: (off + b, 1)),
                    pl.BlockSpec((seq, SSD_INNER), lambda b, c: (off + b, 2)),
                    _full((4, 2 * SSD_INNER)), _full((1, 2 * SSD_INNER)),
                    _full((len(CONV_SHIFTED) * L, L + 2 * HALO))]
        if aliased:
            in_specs.append(pl.BlockSpec(memory_space=pl.ANY))
        return pl.pallas_call(
            functools.partial(_ssd_conv_kernel, seq=seq),
            grid=(nb, nc),
            in_specs=in_specs,
            out_specs=pl.BlockSpec((L, 2 * SSD_INNER), lambda b, c: ((off + b) * nc + c, 0)),
            out_shape=jax.ShapeDtypeStruct((t, 2 * SSD_INNER), BF16),
            input_output_aliases={5: 0} if aliased else {},
            compiler_params=_params(2),
            name="ssd_conv_seq%d" % seq,
        )

    taps = jnp.asarray(CONV_SHIFTED, jnp.int32)
    want = (HALO - CONV_LEFT + taps[:, None] + jnp.arange(L)[None, :]).reshape(-1, 1)
    shift = (want == jnp.arange(L + 2 * HALO)[None, :]).astype(BF16)
    args = [zxbc, zxbc, lw["conv_w"], lw["conv_b"], shift]
    xbc = make_call(dims["seq_p"], dims["nb_p"], 0, False)(*args)
    return make_call(dims["seq_s"], dims["nb_s"], t_p // dims["seq_s"], True)(*args, xbc)


def _ssd_scan_kernel(*refs, seq, rev, has_h0, want_state, add_skip):
    (xbc_ref, dt_ref, dtb_ref, alog_ref, dsk_ref, tri_ref, rep_ref) = refs[:7]
    pos = 7
    h0_ref = None
    if has_h0:
        h0_ref = refs[pos]
        pos += 1
    n_alias = len(refs) - pos - (2 if want_state else 1) - 1
    pos += n_alias
    y_ref = refs[pos]
    st_ref = refs[pos + 1] if want_state else None
    state = refs[-1]

    L = SSD_CHUNK
    nc = seq // L
    c = pl.program_id(1)

    @pl.when(c == 0)
    def _():
        if has_h0:
            for i in range(SSD_INNER // L):
                hpb = L // SSD_HD
                blk = h0_ref[0, 0, i * hpb:(i + 1) * hpb].reshape(L, SSD_STATE)
                state[:, i * L:(i + 1) * L] = blk.T
        else:
            state[...] = jnp.zeros_like(state)

    xc = xbc_ref[:, :SSD_INNER].astype(F32)
    bm = xbc_ref[:, SSD_INNER:SSD_INNER + SSD_GROUPS * SSD_STATE]
    cm = xbc_ref[:, SSD_INNER + SSD_GROUPS * SSD_STATE:]

    dtr = dt_ref[...] + dtb_ref[...]
    dt = jnp.maximum(dtr, 0.0) + jnp.log(1.0 + jnp.exp(-jnp.abs(dtr)))
    dta = dt * (-jnp.exp(alog_ref[...]))
    tri = tri_ref[...]
    p = sum(_dot(tri, part) for part in _split3(dta))
    pt = p.T
    edge = 0 if rev else L - 1
    p_edge = p[edge:edge + 1, :]
    rep = rep_ref[...]
    dt_x = _dot(dt.astype(BF16), rep)
    ep_x = _dot(jnp.exp(p).astype(BF16), rep)
    dte_x = _dot(jnp.exp(p_edge - p).astype(BF16), rep)
    cdec_x = _dot(jnp.broadcast_to(jnp.exp(p_edge), (8, p.shape[1])).astype(BF16), rep)[0:1, :]

    dtx = xc * dt_x
    dtxb = dtx.astype(BF16)
    xdte = (dtx * dte_x).astype(BF16)
    li = lax.broadcasted_iota(jnp.int32, (L, L), 0)
    si = lax.broadcasted_iota(jnp.int32, (L, L), 1)
    keep = (li <= si) if rev else (li >= si)
    lane0 = SSD_HEADS if rev else 0
    gw = SSD_HPG * SSD_HD
    ys, new_state = [], []
    for g in range(SSD_GROUPS):
        gcols = slice(g * gw, (g + 1) * gw)
        b_g = bm[:, g * SSD_STATE:(g + 1) * SSD_STATE]
        c_g = cm[:, g * SSD_STATE:(g + 1) * SSD_STATE]
        cb = _dot_nt(c_g, b_g)
        st_prev = state[:, gcols]
        y_g = _dot(c_g, st_prev.astype(BF16)) * ep_x[:, gcols]
        xg = dtxb[:, gcols]
        head_of_lane = lax.broadcasted_iota(jnp.int32, xg.shape, 1) // SSD_HD
        mats, blocks = [], []
        for hh in range(SSD_HPG):
            hl = lane0 + g * SSD_HPG + hh
            seg = p[:, hl:hl + 1] - pt[hl:hl + 1, :]
            mats.append((cb * jnp.exp(jnp.where(keep, seg, NEG))).astype(BF16))
            blocks.append(jnp.where(head_of_lane == hh, xg, jnp.zeros_like(xg)))
        y_g = y_g + _dot(jnp.concatenate(mats, axis=1), jnp.concatenate(blocks, axis=0))
        if add_skip:
            y_g = y_g + dsk_ref[:, gcols] * xc[:, gcols]
        ys.append(y_g)
        new_state.append(st_prev * cdec_x[:, gcols] + _dot_tn(b_g, xdte[:, gcols]))
    y_ref[...] = jnp.concatenate(ys, axis=1)
    state[...] = jnp.concatenate(new_state, axis=1)

    if want_state:
        @pl.when(c == nc - 1)
        def _():
            for i in range(SSD_INNER // L):
                blk = state[:, i * L:(i + 1) * L].T
                st_ref[0, 0, i * (L // SSD_HD):(i + 1) * (L // SSD_HD)] = blk.reshape(L // SSD_HD, SSD_HD, SSD_STATE)


def _ssd_scan(xbc, dt_raw, lw, h0, j, rev, dims):
    t, t_p, seq_p, seq_s = dims["t"], dims["t_p"], dims["seq_p"], dims["seq_s"]
    nb_p, nb_s = dims["nb_p"], dims["nb_s"]
    L = SSD_CHUNK
    d = 1 if rev else 0
    li = jnp.arange(L)
    tri = ((li[:, None] <= li[None, :]) if rev else (li[:, None] >= li[None, :])).astype(BF16)
    lane = jnp.arange(128)
    col_head = jnp.arange(SSD_INNER) // SSD_HD
    rep = (lane[:, None] == (d * SSD_HEADS + col_head)[None, :]).astype(BF16)
    dsk = jnp.repeat(lw["d_skip"], SSD_HD)[None, :].astype(F32)
    st_shape = (nb_p, 1, SSD_HEADS, SSD_HD, SSD_STATE)

    def make_call(seq, nb, off, has_h0, want_state, n_alias):
        nc = seq // L
        chunk = (lambda b, c: (off * nc + b * nc + (nc - 1 - c), 0)) if rev else (lambda b, c: (off * nc + b * nc + c, 0))
        in_specs = [pl.BlockSpec((L, 2 * SSD_INNER), chunk), pl.BlockSpec((L, 128), chunk),
                    _full((1, 128)), _full((1, 128)),
                    _full((1, SSD_INNER)), _full((L, L)), _full((128, SSD_INNER))]
        if has_h0:
            in_specs.append(pl.BlockSpec((1, 1, SSD_HEADS, SSD_HD, SSD_STATE), lambda b, c: (b, j, 0, 0, 0)))
        aliases = {}
        if n_alias:
            aliases = {len(in_specs): 0}
            in_specs.append(pl.BlockSpec(memory_space=pl.ANY))
        out_specs = [pl.BlockSpec((L, SSD_INNER), chunk)]
        out_shape = [jax.ShapeDtypeStruct((t, SSD_INNER), F32)]
        if want_state:
            out_specs.append(pl.BlockSpec((1, 1, SSD_HEADS, SSD_HD, SSD_STATE), lambda b, c: (b, 0, 0, 0, 0)))
            out_shape.append(jax.ShapeDtypeStruct(st_shape, F32))
        return pl.pallas_call(
            functools.partial(_ssd_scan_kernel, seq=seq, rev=rev, has_h0=has_h0, want_state=want_state,
                              add_skip=not rev),
            grid=(nb, nc),
            in_specs=in_specs,
            out_specs=out_specs,
            out_shape=out_shape,
            scratch_shapes=[pltpu.VMEM((SSD_STATE, SSD_INNER), F32)],
            input_output_aliases=aliases,
            compiler_params=_params(2),
            name="ssd_scan_%s_seq%d" % ("bwd" if rev else "fwd", seq),
        )

    common = [xbc, dt_raw, lw["dt_bias"], lw["a_log"], dsk, tri, rep]
    y, st = make_call(seq_p, nb_p, 0, False, True, 0)(*common)
    (y,) = make_call(seq_s, nb_s, t_p // seq_s, True, False, 1)(*common, h0, y)
    return y, st


def _ssd_out_kernel(yf_ref, yb_ref, z_ref, gn_ref, w_ref, h_ref, gt1, g2, sh2, sc2, wr_hi, wr_lo, br,
                    h1_ref, xn_ref, route_ref):
    y = (yf_ref[...] + yb_ref[...]) * _silu(z_ref[...].astype(F32))
    o = _dot(_rms(y, gn_ref[...]).astype(BF16), w_ref[...])
    _finish(h_ref[...], o, gt1[0], (g2, sh2, sc2, wr_hi, wr_lo, br), h1_ref, xn_ref, route_ref)


def _ssd_layer(h, state_f, state_b, j, lw, layer, dims):
    t = dims["t"]
    zxbc = _pre_proj(h, lw, layer, lw["w_in"], 1536, BF16, dims, n=3 * SSD_INNER)
    dt_raw = _pre_proj(h, lw, layer, lw["w_dt"], 128, F32, dims)
    xbc = _ssd_conv(zxbc, lw, dims)
    y_f, st_f = _ssd_scan(xbc, dt_raw, lw, state_f, j, False, dims)
    y_b, st_b = _ssd_scan(xbc, dt_raw, lw, state_b, j, True, dims)
    row_fn = _tile_row_fn(dims)
    tile = lambda: pl.BlockSpec((TM, D), lambda i: (i, 0))
    wide = lambda: pl.BlockSpec((TM, SSD_INNER), lambda i: (i, 0))
    outs = pl.pallas_call(
        _ssd_out_kernel,
        grid=(t // TM,),
        in_specs=[wide(), wide(), wide(), _full((1, SSD_INNER)), _full((SSD_INNER, D)), tile(),
                  _mod_spec(layer, 2, row_fn)] + _route_specs(layer, row_fn),
        out_specs=_stream_specs(TM, lambda i: i),
        out_shape=_stream_outs(t),
        compiler_params=_params(1),
        name="ssd_out",
    )(y_f, y_b, zxbc, lw["g_norm"], lw["w_out"], h, lw["mod"], *_route_args(lw))
    return outs, st_f, st_b


TS = 256
SRC = 512


def _moe_plan(route, t):
    i32 = jnp.int32
    npair = len(PAIR_ORDER)
    ncls = MOE_GROUPS * npair
    key = route[:, 0].astype(i32)
    oh = (key[:, None] == jnp.arange(ncls, dtype=i32)[None, :]).astype(i32)
    csum = jnp.cumsum(oh, axis=0)
    ccnt = csum[-1]
    cnt = ccnt.reshape(MOE_GROUPS, npair).sum(axis=1)
    padded = ((cnt + TS - 1) // TS) * TS
    gend = jnp.cumsum(padded)
    in_grp = jnp.cumsum(ccnt.reshape(MOE_GROUPS, npair), axis=1) - ccnt.reshape(MOE_GROUPS, npair)
    cstart = ((gend - padded)[:, None] + in_grp).reshape(ncls)
    pos = jnp.sum(oh * (csum - 1 + cstart[None, :]), axis=1)
    n_tiles = t // TS + MOE_GROUPS
    tile0 = jnp.arange(n_tiles, dtype=i32) * TS
    tile_grp = jnp.sum((tile0[:, None] >= gend[None, :]).astype(i32), axis=1)
    last_grp = jnp.max(jnp.where(cnt > 0, jnp.arange(MOE_GROUPS, dtype=i32), 0))
    tile_grp = jnp.minimum(tile_grp, last_grp)
    n_used = (gend[-1] // TS).reshape(1)
    touch = ((cstart[None, :] < tile0[:, None] + TS) & (cstart[None, :] + ccnt[None, :] > tile0[:, None])
             & (ccnt[None, :] > 0))
    uses = jnp.array([[int(e in PAIR_ORDER[c % npair]) for e in range(MOE_EPG)] for c in range(ncls)], i32)
    need = (jnp.dot(touch.astype(i32), uses) > 0).astype(i32).reshape(-1)
    return dict(pos=pos, tile_grp=tile_grp, n_used=n_used, need=need, n_tiles=n_tiles)


ROW_UNROLL = 8


def _moe_scatter_kernel(pos_ref, x_ref, xs_ref):
    n = pl.program_id(0)

    @pl.when(n == 0)
    def _():
        xs_ref[...] = jnp.zeros_like(xs_ref)

    def body(jj, carry):
        for r in range(ROW_UNROLL):
            j = jj * ROW_UNROLL + r
            xs_ref[pl.ds(pos_ref[n * SRC + j], 1), :] = x_ref[pl.ds(j, 1), :]
        return carry

    lax.fori_loop(0, SRC // ROW_UNROLL, body, 0)


def _moe_expert_kernel(grp_ref, nused_ref, need_ref, xs_ref, wg_ref, wu_ref, wd_ref, y_ref,
                       wg_b, wu_b, wd_b, acc_ref):
    i = pl.program_id(0)
    used = i < nused_ref[0]
    new_group = (i == 0) | (grp_ref[i] != grp_ref[jnp.maximum(i - 1, 0)])

    @pl.when(used & new_group)
    def _():
        for e in range(MOE_EPG):
            wg_b[e] = wg_ref[0, e].astype(BF16)
            wu_b[e] = wu_ref[0, e].astype(BF16)
            wd_b[e] = wd_ref[0, e].astype(BF16)

    @pl.when(used)
    def _():
        lo, hi = _unpack_pairs(xs_ref[...])
        x = jnp.concatenate([lo, hi[:, :D - PW]], axis=1).astype(BF16)
        rec = hi[:, D - PW:XW - PW]
        lane = lax.broadcasted_iota(jnp.int32, rec.shape, 1)
        acc_ref[...] = jnp.zeros_like(acc_ref)
        for e in range(MOE_EPG):
            @pl.when(need_ref[i * MOE_EPG + e] > 0)
            def _():
                hg = _dot(x, wg_b[e])
                hu = _dot(x, wu_b[e])
                mine = (lane % MOE_EPG == e) & (lane < 3 * MOE_EPG)
                cw = jnp.sum(jnp.where(mine, rec, 0.0), axis=-1, keepdims=True)
                acc_ref[...] += _dot((_silu(hg) * hu * cw).astype(BF16), wd_b[e])

        acc = acc_ref[...].astype(BF16).astype(F32)
        y_ref[...] = _pack_pairs(acc[:, :D // 2], acc[:, D // 2:])

    @pl.when(i >= nused_ref[0])
    def _():
        y_ref[...] = jnp.zeros_like(y_ref)


def _moe_ungather_kernel(pos_ref, ys_ref, h1_ref, gt2, fg_ref, *rest, n_prompt_tiles):
    rows_ref = rest[-1]
    out_refs = rest[:-1]
    n = pl.program_id(0)

    def body(jj, carry):
        for r in range(ROW_UNROLL):
            j = jj * ROW_UNROLL + r
            rows_ref[pl.ds(j, 1), :] = ys_ref[pl.ds(pos_ref[n * SRC + j], 1), :]
        return carry

    lax.fori_loop(0, SRC // ROW_UNROLL, body, 0)
    h2 = h1_ref[...] + gt2[0] * jnp.concatenate(_unpack_pairs(rows_ref[...]), axis=1)
    if n_prompt_tiles is None:
        out_refs[0][...] = h2
    else:
        y = _rms(h2, fg_ref[...])

        @pl.when(n < n_prompt_tiles)
        def _():
            out_refs[0][...] = y

        @pl.when(n >= n_prompt_tiles)
        def _():
            out_refs[1][...] = y


def _moe_sparse(h1, xn, route, lw, layer, final_g, final, dims):
    t = dims["t"]
    plan = _moe_plan(route, t)
    n_tiles = plan["n_tiles"]
    n_rows = n_tiles * TS
    one = pl.Buffered(1)
    xs = pl.pallas_call(
        _moe_scatter_kernel,
        grid_spec=pltpu.PrefetchScalarGridSpec(
            num_scalar_prefetch=1,
            grid=(t // SRC,),
            in_specs=[pl.BlockSpec((SRC, PW), lambda n, pos: (n, 0))],
            out_specs=pl.BlockSpec((n_rows, PW), lambda n, pos: (0, 0))),
        out_shape=jax.ShapeDtypeStruct((n_rows, PW), U32),
        compiler_params=_params(1),
        name="moe_scatter_l%d" % layer,
    )(plan["pos"], xn)

    ex = lambda i, g, nu, nd: (layer, g[i], 0, 0)
    ys = pl.pallas_call(
        _moe_expert_kernel,
        grid_spec=pltpu.PrefetchScalarGridSpec(
            num_scalar_prefetch=3,
            grid=(n_tiles,),
            in_specs=[pl.BlockSpec((TS, PW), lambda i, g, nu, nd: (i, 0)),
                      pl.BlockSpec((1, MOE_EPG, D, MOE_DFF), ex, pipeline_mode=one),
                      pl.BlockSpec((1, MOE_EPG, D, MOE_DFF), ex, pipeline_mode=one),
                      pl.BlockSpec((1, MOE_EPG, MOE_DFF, D), ex, pipeline_mode=one)],
            out_specs=pl.BlockSpec((TS, D // 2), lambda i, g, nu, nd: (i, 0)),
            scratch_shapes=[pltpu.VMEM((MOE_EPG, D, MOE_DFF), BF16), pltpu.VMEM((MOE_EPG, D, MOE_DFF), BF16),
                            pltpu.VMEM((MOE_EPG, MOE_DFF, D), BF16), pltpu.VMEM((TS, D), F32)]),
        out_shape=jax.ShapeDtypeStruct((n_rows, D // 2), U32),
        compiler_params=_params(1),
        name="moe_expert_l%d" % layer,
    )(plan["tile_grp"], plan["n_used"], plan["need"], xs, lw["w_gate"], lw["w_up"], lw["w_down"])

    row_fn = _tile_row_fn(dims, SRC)
    tile = lambda: pl.BlockSpec((SRC, D), lambda n, pos: (n, 0))
    if final:
        npt = dims["t_p"] // SRC
        out_specs = [pl.BlockSpec((SRC, D), lambda n, pos: (jnp.minimum(n, npt - 1), 0)),
                     pl.BlockSpec((SRC, D), lambda n, pos: (jnp.maximum(n - npt, 0), 0))]
        out_shape = [jax.ShapeDtypeStruct((dims["t_p"], D), F32), jax.ShapeDtypeStruct((t - dims["t_p"], D), F32)]
    else:
        npt = None
        out_specs = [tile()]
        out_shape = [jax.ShapeDtypeStruct((t, D), F32)]
    return pl.pallas_call(
        functools.partial(_moe_ungather_kernel, n_prompt_tiles=npt),
        grid_spec=pltpu.PrefetchScalarGridSpec(
            num_scalar_prefetch=1,
            grid=(t // SRC,),
            in_specs=[pl.BlockSpec((n_rows, D // 2), lambda n, pos: (0, 0)), tile(),
                      pl.BlockSpec((1, 1, D), lambda n, pos: ((layer * MOD_ROWS + row_fn(n)) * N_MOD + 5, 0, 0)),
                      pl.BlockSpec((1, D), lambda n, pos: (0, 0))],
            out_specs=out_specs,
            scratch_shapes=[pltpu.VMEM((SRC, D // 2), U32)]),
        out_shape=out_shape,
        compiler_params=_params(1),
        name="moe_ungather_l%d" % layer,
    )(plan["pos"], ys, h1, lw["mod"], final_g)


def kernel(x_prompt, x_sample, cache_k, cache_v, state_ssm_fwd, state_ssm_bwd, c, c_ctx, ada_w, ada_b, norm1_g, norm2_g, final_g, fnet_w_o, na_w_qkv, na_w_o, na_rpb, gmlp_w_in, gmlp_g_v, gmlp_w_s, gmlp_b_s, gmlp_w_out, ssd_w_in, ssd_conv_w, ssd_conv_b, ssd_a_log, ssd_dt_bias, ssd_d_skip, ssd_g_norm, ssd_w_out, moe_w_gr, moe_b_gr, moe_w_er, moe_b_er, moe_w_gate, moe_w_up, moe_w_down):
    nb_p, seq_p, _ = x_prompt.shape
    nb_s, seq_s, _ = x_sample.shape
    depth = ada_w.shape[0]
    t_p, t_s = nb_p * seq_p, nb_s * seq_s
    dims = dict(t=t_p + t_s, t_p=t_p, seq_p=seq_p, seq_s=seq_s, nb_p=nb_p, nb_s=nb_s)
    assert 1 + nb_s <= MOD_ROWS and t_p % seq_s == 0 and t_p % TM == 0 and seq_s % TM == 0

    cond = jnp.zeros((MOD_ROWS, D), F32).at[0].set(c_ctx).at[1:1 + nb_s].set(c)
    mod = _ada_table(cond, ada_w, ada_b)
    h = (x_prompt.reshape(t_p, D), x_sample.reshape(t_s, D))
    fg = final_g.reshape(1, D)

    new_k, new_v, new_sf, new_sb = [], [], [], []
    for l in range(depth):
        kind, j = l % 4, l // 4
        w_r = jnp.concatenate([moe_w_gr[l], moe_w_er[l]], axis=1)
        w_r = jnp.pad(w_r, ((0, 0), (0, ROUTE_W - MOE_GROUPS - MOE_EXPERTS)))
        b_r = jnp.pad(jnp.concatenate([moe_b_gr[l], moe_b_er[l]]), (0, ROUTE_W - MOE_GROUPS - MOE_EXPERTS))
        wr_hi = w_r.astype(BF16)
        lw = dict(mod=mod, g1=norm1_g[l].reshape(1, D), g2=norm2_g[l].reshape(1, D),
                  wr_hi=wr_hi, wr_lo=(w_r - wr_hi.astype(F32)).astype(BF16), br=b_r.reshape(1, ROUTE_W),
                  w_gate=moe_w_gate, w_up=moe_w_up, w_down=moe_w_down)
        if kind == 0:
            lw.update(w_o=fnet_w_o[j].astype(BF16))
            h1, xn, route = _fnet_layer(h, lw, l, dims)
        elif kind == 1:
            lw.update(w_qkv=na_w_qkv[j], w_o=na_w_o[j].astype(BF16), rpb=na_rpb[j])
            (h1, xn, route), kc, vc = _na_layer(h, cache_k, cache_v, j, lw, l, dims)
            new_k.append(kc)
            new_v.append(vc)
        elif kind == 2:
            gw = GMLP_DFF // GMLP_GROUPS
            lw.update(w_in=gmlp_w_in[j].astype(BF16), g_v=gmlp_g_v[j].reshape(1, GMLP_DFF),
                      w_s=gmlp_w_s[j].astype(BF16),
                      b_s=jnp.broadcast_to(gmlp_b_s[j][:, :, None], (GMLP_GROUPS, GMLP_CHUNK, gw)),
                      w_out=gmlp_w_out[j].astype(BF16))
            h1, xn, route = _gmlp_layer(h, lw, l, dims)
        else:
            n_main = 3 * SSD_INNER
            w_in = ssd_w_in[j]
            pad = lambda v: jnp.pad(v, ((0, 0), (0, 128 - 2 * SSD_HEADS)))
            lw.update(w_in=w_in, w_dt=pad(w_in[:, n_main:]),
                      conv_w=ssd_conv_w[j], conv_b=ssd_conv_b[j].reshape(1, -1),
                      dt_bias=pad(ssd_dt_bias[j].reshape(1, -1)), a_log=pad(ssd_a_log[j].reshape(1, -1)),
                      d_skip=ssd_d_skip[j], g_norm=ssd_g_norm[j].reshape(1, SSD_INNER),
                      w_out=ssd_w_out[j].astype(BF16))
            (h1, xn, route), sf, sb = _ssd_layer(h, state_ssm_fwd, state_ssm_bwd, j, lw, l, dims)
            new_sf.append(sf)
            new_sb.append(sb)
        out = _moe_sparse(h1, xn, route, lw, l, fg, l == depth - 1, dims)
        h = out if l == depth - 1 else out[0]

    y_prompt = h[0].reshape(nb_p, seq_p, D)
    y_sample = h[1].reshape(nb_s, seq_s, D)
    cat = lambda xs: jnp.concatenate(xs, axis=1)
    return (y_prompt, y_sample, cat(new_k), cat(new_v), cat(new_sf), cat(new_sb))
```

```python
import functools
import math

import jax
import jax.numpy as jnp
from jax import lax
from jax.experimental import pallas as pl
from jax.experimental.pallas import tpu as pltpu

F32 = jnp.float32
BF16 = jnp.bfloat16

D = 1024
EPS = 1e-6
NEG = -1e30
N_MOD = 6
MOD_ROWS = 8
GRID_W = 64
FNET_GROUPS = 8
NA_HEADS = 16
NA_HD = 64
NA_WIN_ROWS = 8
NA_WIN_COLS = 16
GMLP_CHUNK = 128
GMLP_DFF = 2048
GMLP_GROUPS = 8
SSD_INNER = 2048
SSD_HD = 64
SSD_HEADS = 32
SSD_GROUPS = 8
SSD_STATE = 128
SSD_CHUNK = 128
SSD_HPG = SSD_HEADS // SSD_GROUPS
MOE_GROUPS = 4
MOE_EPG = 4
MOE_EXPERTS = 16
MOE_DFF = 512
ROUTE_W = 128
ROUTE_E0 = 4
PAIR_ORDER = ((0, 1), (0, 2), (0, 3), (1, 3), (1, 2), (2, 3))

TM = 512
VMEM_LIMIT = 56 * 1024 * 1024


def _dot(a, b):
    return jnp.dot(a, b, preferred_element_type=F32)


def _dot_nt(a, b):
    return lax.dot_general(a, b, (((1,), (1,)), ((), ())), preferred_element_type=F32)


def _dot_tn(a, b):
    return lax.dot_general(a, b, (((0,), (0,)), ((), ())), preferred_element_type=F32)


def _silu(x):
    hx = 0.5 * x
    return hx * (1.0 + jnp.tanh(hx))


def _rms(x, g):
    return x * lax.rsqrt(jnp.mean(x * x, axis=-1, keepdims=True) + EPS) * g


def _normmod(x, g, shift, scale):
    return _rms(x, g) * (1.0 + scale) + shift


def _split3(x):
    hi = x.astype(BF16)
    r = x - hi.astype(F32)
    mid = r.astype(BF16)
    lo = (r - mid.astype(F32)).astype(BF16)
    return hi, mid, lo


def _params(n_axes):
    return pltpu.CompilerParams(dimension_semantics=("arbitrary",) * n_axes,
                                vmem_limit_bytes=VMEM_LIMIT)


def _full(shape):
    nd = len(shape)
    return pl.BlockSpec(shape, lambda *_: (0,) * nd)


def _mod_spec(layer, k, row_fn):
    return pl.BlockSpec((1, 1, D), lambda *idx: ((layer * MOD_ROWS + row_fn(*idx)) * N_MOD + k, 0, 0))


def _ada_kernel(c_ref, w_ref, b_ref, o_ref):
    c = c_ref[...]
    o_ref[0] = _dot(_silu(c).astype(BF16), w_ref[0].astype(BF16)) + b_ref[0]


def _ada_table(cond, ada_w, ada_b):
    depth = ada_w.shape[0]
    n = N_MOD * D
    tn = 1536
    out = pl.pallas_call(
        _ada_kernel,
        grid=(depth, n // tn),
        in_specs=[_full((MOD_ROWS, D)),
                  pl.BlockSpec((1, D, tn), lambda l, j: (l, 0, j)),
                  pl.BlockSpec((1, 1, tn), lambda l, j: (l, 0, j))],
        out_specs=pl.BlockSpec((1, MOD_ROWS, tn), lambda l, j: (l, 0, j)),
        out_shape=jax.ShapeDtypeStruct((depth, MOD_ROWS, n), F32),
        compiler_params=_params(2),
        name="ada_table",
    )(cond, ada_w, ada_b.reshape(depth, 1, n))
    return out.reshape(depth * MOD_ROWS * N_MOD, 1, D)


def _route(h1, g2, sh2, sc2, wr_hi, wr_lo, br):
    xn = _normmod(h1, g2, sh2, sc2)
    xh = xn.astype(BF16)
    xl = (xn - xh.astype(F32)).astype(BF16)
    logits = _dot(xh, wr_hi) + _dot(xh, wr_lo) + _dot(xl, wr_hi) + br
    lane = lax.broadcasted_iota(jnp.int32, logits.shape, 1).astype(F32)
    far = float(ROUTE_W)
    gl = jnp.where(lane < MOE_GROUPS, logits, NEG)
    gmax = jnp.max(gl, axis=-1, keepdims=True)
    g_p = 1.0 / jnp.sum(jnp.exp(gl - gmax), axis=-1, keepdims=True)
    gidx = jnp.min(jnp.where(gl == gmax, lane, far), axis=-1, keepdims=True)
    lo = ROUTE_E0 + MOE_EPG * gidx
    el = jnp.where((lane >= lo) & (lane < lo + MOE_EPG), logits, NEG)
    m1 = jnp.max(el, axis=-1, keepdims=True)
    i1 = jnp.min(jnp.where(el == m1, lane, far), axis=-1, keepdims=True)
    el2 = jnp.where(lane == i1, NEG, el)
    m2 = jnp.max(el2, axis=-1, keepdims=True)
    i2 = jnp.min(jnp.where(el2 == m2, lane, far), axis=-1, keepdims=True)
    e2 = jnp.exp(m2 - m1)
    w1 = g_p / (1.0 + e2)
    w2 = w1 * e2
    rec = jnp.zeros_like(logits)
    for part, (a, b) in enumerate(zip(_split3(w1), _split3(w2))):
        shift = part * MOE_EPG - lo
        rec = (rec + jnp.where(lane == i1 + shift, a.astype(F32), 0.0)
               + jnp.where(lane == i2 + shift, b.astype(F32), 0.0))
    ea = jnp.minimum(i1, i2) - lo
    eb = jnp.maximum(i1, i2) - lo
    pair = sum(jnp.where((ea == a) & (eb == b), float(k), 0.0) for k, (a, b) in enumerate(PAIR_ORDER))
    key = gidx * float(len(PAIR_ORDER)) + pair
    return xh, rec.astype(BF16), jnp.broadcast_to(key, logits.shape)


def _route_specs(layer, row_fn):
    return [_full((1, D)), _mod_spec(layer, 3, row_fn), _mod_spec(layer, 4, row_fn),
            _full((D, ROUTE_W)), _full((D, ROUTE_W)), _full((1, ROUTE_W))]


def _route_args(lw):
    return [lw["g2"], lw["mod"], lw["mod"], lw["wr_hi"], lw["wr_lo"], lw["br"]]


def _finish(h, o, gate, rt_refs, h1_ref, xn_ref, route_ref):
    g2, sh2, sc2, wr_hi, wr_lo, br = rt_refs
    h1 = h + gate * o
    h1_ref[...] = h1
    xn, rec, gid = _route(h1, g2[...], sh2[0], sc2[0], wr_hi[...], wr_lo[...], br[...])
    row = jnp.concatenate([xn.astype(F32), rec.astype(F32), jnp.zeros((h.shape[0], 2 * PW - XW), F32)], axis=1)
    xn_ref[...] = _pack_pairs(row[:, :PW], row[:, PW:])
    route_ref[...] = gid


XW = D + ROUTE_W
PW = 640
U32 = jnp.uint32


def _pack_pairs(lo, hi):
    lo_bits = lax.bitcast_convert_type(lo, U32) >> 16
    hi_bits = lax.bitcast_convert_type(hi, U32) & U32(0xFFFF0000)
    return lo_bits | hi_bits


def _unpack_pairs(w):
    return lax.bitcast_convert_type(w << 16, F32), lax.bitcast_convert_type(w & U32(0xFFFF0000), F32)


def _stream_outs(t):
    return [jax.ShapeDtypeStruct((t, D), F32), jax.ShapeDtypeStruct((t, PW), U32),
            jax.ShapeDtypeStruct((t, ROUTE_W), F32)]


def _stream_specs(rows, row_block):
    return [pl.BlockSpec((rows, w), lambda *idx: (row_block(*idx), 0)) for w in (D, PW, ROUTE_W)]


def _fnet_kernel(h_ref, g1, sh1, sc1, gt1, csc_ref, fs_ref, wo_ref, g2, sh2, sc2, wr_hi, wr_lo, br,
                 *rest, seq):
    h1_ref, xn_ref, route_ref, ab_ref = rest[-4:]
    h = h_ref[...]
    a = _normmod(h, g1[...], sh1[0], sc1[0]).astype(BF16)
    gd = D // FNET_GROUPS
    for g in range(FNET_GROUPS):
        ab = _dot(a[:, g * gd:(g + 1) * gd], csc_ref[...])
        ab_ref[0:seq, g * gd:(g + 1) * gd] = ab[:, :gd].astype(BF16)
        ab_ref[seq:2 * seq, g * gd:(g + 1) * gd] = ab[:, gd:].astype(BF16)
    f = _dot(fs_ref[...], ab_ref[...])
    o = _dot(f.astype(BF16), wo_ref[...])
    _finish(h, o, gt1[0], (g2, sh2, sc2, wr_hi, wr_lo, br), h1_ref, xn_ref, route_ref)


def _dft_tables(n):
    k = jnp.arange(n, dtype=jnp.int32)
    ang = ((k[:, None] * k[None, :]) % n).astype(F32) * (2.0 * math.pi / n)
    s = 1.0 / math.sqrt(n)
    return jnp.cos(ang) * s, jnp.sin(ang) * s


def _fnet_layer(h, lw, layer, dims):
    t, t_p = dims["t"], dims["t_p"]
    gd = D // FNET_GROUPS
    cc, sc = _dft_tables(gd)
    csc = jnp.concatenate([cc, sc], axis=1).astype(BF16)

    split_in = isinstance(h, tuple)

    def make_call(seq, nb, off, n_alias):
        cs, ss = _dft_tables(seq)
        fs = jnp.concatenate([cs, -ss], axis=1).astype(BF16)
        row_fn = (lambda b: 0) if off == 0 else (lambda b: 1 + b)
        in_off = 0 if split_in else off
        tile = lambda: pl.BlockSpec((seq, D), lambda b: (off + b, 0))
        in_specs = ([pl.BlockSpec((seq, D), lambda b: (in_off + b, 0)), _full((1, D))]
                    + [_mod_spec(layer, k, row_fn) for k in (0, 1, 2)]
                    + [_full((gd, 2 * gd)), _full((seq, 2 * seq)), _full((D, D))]
                    + _route_specs(layer, row_fn))
        aliases = {}
        if n_alias:
            base = len(in_specs)
            in_specs = in_specs + [pl.BlockSpec(memory_space=pl.ANY)] * n_alias
            aliases = {base + i: i for i in range(n_alias)}
        call = pl.pallas_call(
            functools.partial(_fnet_kernel, seq=seq),
            grid=(nb,),
            in_specs=in_specs,
            out_specs=_stream_specs(seq, lambda b: off + b),
            out_shape=_stream_outs(t),
            scratch_shapes=[pltpu.VMEM((2 * seq, D), BF16)],
            input_output_aliases=aliases,
            compiler_params=_params(1),
            name="fnet_seq%d" % seq,
        )
        return lambda *a: call(*a[:2], *a[2:5], a[5], fs, *a[6:])

    h_p, h_s = h if split_in else (h, h)
    args = [lw["g1"], lw["mod"], lw["mod"], lw["mod"], csc, lw["w_o"]] + _route_args(lw)
    outs = make_call(dims["seq_p"], dims["nb_p"], 0, None)(h_p, *args)
    return make_call(dims["seq_s"], dims["nb_s"], t_p // dims["seq_s"], 3)(h_s, *args, *outs)


def _pre_kernel(h_ref, g1, sh1, sc1, w_ref, o_ref, a_ref):
    @pl.when(pl.program_id(1) == 0)
    def _():
        a_ref[...] = _normmod(h_ref[...], g1[...], sh1[0], sc1[0]).astype(BF16)

    o_ref[...] = _dot(a_ref[...], w_ref[...].astype(BF16)).astype(o_ref.dtype)


def _tile_row_fn(dims, tm=TM):
    npt = dims["t_p"] // tm
    tps = dims["seq_s"] // tm
    return lambda i, *_: jnp.where(i < npt, 0, 1 + (i - npt) // tps)


def _pre_proj(h, lw, layer, w, tn, out_dtype, dims, n=None):
    t = dims["t"]
    n = w.shape[1] if n is None else n
    tm = dims["seq_s"]
    row_fn = _tile_row_fn(dims, tm)
    return pl.pallas_call(
        _pre_kernel,
        grid=(t // tm, n // tn),
        in_specs=[pl.BlockSpec((tm, D), lambda i, j: (i, 0)), _full((1, D)),
                  _mod_spec(layer, 0, row_fn), _mod_spec(layer, 1, row_fn),
                  pl.BlockSpec((D, tn), lambda i, j: (0, j))],
        out_specs=pl.BlockSpec((tm, tn), lambda i, j: (i, j)),
        out_shape=jax.ShapeDtypeStruct((t, n), out_dtype),
        scratch_shapes=[pltpu.VMEM((tm, D), BF16)],
        compiler_params=_params(2),
        name="pre_proj_l%d_n%d" % (layer, n),
    )(h, lw["g1"], lw["mod"], lw["mod"], w)


def _out_kernel(o_ref, w_ref, h_ref, gt1, g2, sh2, sc2, wr_hi, wr_lo, br, h1_ref, xn_ref, route_ref):
    o = _dot(o_ref[...], w_ref[...])
    _finish(h_ref[...], o, gt1[0], (g2, sh2, sc2, wr_hi, wr_lo, br), h1_ref, xn_ref, route_ref)


def _out_proj(o, w, h, lw, layer, dims):
    t = dims["t"]
    k = o.shape[1]
    row_fn = _tile_row_fn(dims)
    tile = lambda: pl.BlockSpec((TM, D), lambda i: (i, 0))
    return pl.pallas_call(
        _out_kernel,
        grid=(t // TM,),
        in_specs=[pl.BlockSpec((TM, k), lambda i: (i, 0)), _full((k, D)), tile(),
                  _mod_spec(layer, 2, row_fn)] + _route_specs(layer, row_fn),
        out_specs=_stream_specs(TM, lambda i: i),
        out_shape=_stream_outs(t),
        compiler_params=_params(1),
        name="out_proj_l%d" % layer,
    )(o, w, h, lw["mod"], *_route_args(lw))


NA_SCALE = NA_HD ** -0.5


NA_HEAD_BLOCK = 4


def _na_ctx_kernel(q_ref, k_ref, v_ref, o_ref, kc_ref, vc_ref):
    seq = q_ref.shape[0]
    bw = NA_HEAD_BLOCK * NA_HD
    outs = []
    for hb in range(NA_HEADS // NA_HEAD_BLOCK):
        cols = slice(hb * bw, (hb + 1) * bw)
        q4 = q_ref[:, cols] * NA_SCALE
        k4 = k_ref[:, cols]
        v4 = v_ref[:, cols]
        head_of_lane = lax.broadcasted_iota(jnp.int32, k4.shape, 1) // NA_HD
        kbd = jnp.concatenate([jnp.where(head_of_lane == h, k4, jnp.zeros_like(k4))
                               for h in range(NA_HEAD_BLOCK)], axis=0)
        vbd = jnp.concatenate([jnp.where(head_of_lane == h, v4, jnp.zeros_like(v4))
                               for h in range(NA_HEAD_BLOCK)], axis=0)
        s = _dot_nt(q4, kbd)
        ps = []
        l4 = jnp.zeros((seq, bw), F32)
        for h in range(NA_HEAD_BLOCK):
            sh = s[:, h * seq:(h + 1) * seq]
            ph = jnp.exp(sh - jnp.max(sh, axis=-1, keepdims=True))
            l4 = jnp.where(head_of_lane == h, jnp.sum(ph, axis=-1, keepdims=True), l4)
            ps.append(ph.astype(BF16))
            hd = hb * NA_HEAD_BLOCK + h
            kc_ref[0, 0, hd] = k4[:, h * NA_HD:(h + 1) * NA_HD].astype(F32)
            vc_ref[0, 0, hd] = v4[:, h * NA_HD:(h + 1) * NA_HD].astype(F32)
        outs.append((_dot(jnp.concatenate(ps, axis=1), vbd) / l4).astype(BF16))
    o_ref[...] = jnp.concatenate(outs, axis=1)


def _na_row_start(qr, rows):
    kr = min(NA_WIN_ROWS, rows)
    return min(max(qr - kr // 2, 0), rows - kr)


def _na_window_bias(bias_ref, hh, qr, rows, m_lo, m_hi):
    kr = min(NA_WIN_ROWS, rows)
    rs = _na_row_start(qr, rows)
    blocks = []
    for m in range(m_lo, m_hi):
        ok0 = rs <= 2 * m < rs + kr
        ok1 = rs <= 2 * m + 1 < rs + kr
        e = 2 * m - qr + NA_WIN_ROWS
        if ok0 and ok1:
            blocks.append(bias_ref[hh, 0, e])
        elif ok1:
            blocks.append(bias_ref[hh, 1, e])
        elif ok0:
            blocks.append(bias_ref[hh, 2, e])
        else:
            blocks.append(jnp.full((GRID_W, 2 * GRID_W), NEG, F32))
    return jnp.concatenate(blocks, axis=1)


def _na_lat_kernel(q_ref, k_ref, v_ref, bias_ref, kc_ref, vc_ref, o_in, o_ref, *, seq, qb):
    del o_in
    rows = seq // GRID_W

    def block_diag(a, b):
        zero = jnp.zeros_like(a)
        return jnp.concatenate([jnp.concatenate([a, zero], axis=1), jnp.concatenate([zero, b], axis=1)], axis=0)

    first = slice(0, NA_HD)
    second = slice(NA_HD, 2 * NA_HD)
    kc_bd = block_diag(kc_ref[0, 0, 0].astype(BF16), kc_ref[0, 0, 1].astype(BF16))
    vc_bd = block_diag(vc_ref[0, 0, 0].astype(BF16), vc_ref[0, 0, 1].astype(BF16))
    past = kc_ref.shape[3]
    for b0 in range(0, seq, qb):
        q2 = q_ref[b0:b0 + qb, :] * NA_SCALE
        qrs = range(b0 // GRID_W, (b0 + qb) // GRID_W)
        m_lo = _na_row_start(qrs[0], rows) // 2
        m_hi = (_na_row_start(qrs[-1], rows) + min(NA_WIN_ROWS, rows) + 1) // 2
        keys = slice(m_lo * 2 * GRID_W, m_hi * 2 * GRID_W)
        nk = (m_hi - m_lo) * 2 * GRID_W
        s1_all = _dot_nt(q2, block_diag(k_ref[keys, first], k_ref[keys, second]))
        s2_all = _dot_nt(q2, kc_bd)
        p1s, p2s, ls = [], [], []
        for hh in range(2):
            bias = jnp.concatenate([_na_window_bias(bias_ref, hh, qr, rows, m_lo, m_hi) for qr in qrs], axis=0)
            s1 = s1_all[:, hh * nk:(hh + 1) * nk] + bias
            s2 = s2_all[:, hh * past:(hh + 1) * past]
            m = jnp.maximum(jnp.max(s1, axis=-1, keepdims=True), jnp.max(s2, axis=-1, keepdims=True))
            p1 = jnp.exp(s1 - m)
            p2 = jnp.exp(s2 - m)
            ls.append(jnp.sum(p1, axis=-1, keepdims=True) + jnp.sum(p2, axis=-1, keepdims=True))
            p1s.append(p1.astype(BF16))
            p2s.append(p2.astype(BF16))
        o2 = (_dot(jnp.concatenate(p1s, axis=1), block_diag(v_ref[keys, first], v_ref[keys, second]))
              + _dot(jnp.concatenate(p2s, axis=1), vc_bd))
        lane = lax.broadcasted_iota(jnp.int32, o2.shape, 1)
        o_ref[b0:b0 + qb, :] = (o2 / jnp.where(lane < NA_HD, ls[0], ls[1])).astype(BF16)


def _na_bias_tables(rpb):
    c = jnp.arange(GRID_W)
    win0 = jnp.clip(c - NA_WIN_COLS // 2, 0, GRID_W - NA_WIN_COLS)
    ok_c = (c[None, :] >= win0[:, None]) & (c[None, :] < win0[:, None] + NA_WIN_COLS)
    dc = jnp.clip(c[None, :] - c[:, None], 1 - NA_WIN_COLS, NA_WIN_COLS - 1) + NA_WIN_COLS - 1
    nh, ndr, ndc = rpb.shape
    pick = (dc.reshape(1, -1) == jnp.arange(ndc)[:, None]).astype(F32)
    cm = jnp.dot(rpb.reshape(nh * ndr, ndc), pick, precision=lax.Precision.HIGHEST)
    cm = jnp.where(ok_c[None, None], cm.reshape(nh, ndr, GRID_W, GRID_W), NEG)
    neg = jnp.full_like(cm[:, :1], NEG)
    ext = jnp.concatenate([neg, cm, neg], axis=1)
    a, b = ext[:, :-1], ext[:, 1:]
    negs = jnp.full_like(a, NEG)
    pair = lambda x, y: jnp.concatenate([x, y], axis=-1)
    return jnp.stack([pair(a, b), pair(negs, b), pair(a, negs)], axis=1)


def _na_layer(h, cache_k, cache_v, j, lw, layer, dims):
    t, t_p, seq_p, seq_s = dims["t"], dims["t_p"], dims["seq_p"], dims["seq_s"]
    nb_p, nb_s = dims["nb_p"], dims["nb_s"]
    qkv = _pre_proj(h, lw, layer, lw["w_qkv"], 1536, BF16, dims)
    cshape = (nb_p, 1, NA_HEADS, seq_p, NA_HD)
    cspec = lambda: pl.BlockSpec((1, 1, NA_HEADS, seq_p, NA_HD), lambda b: (b, 0, 0, 0, 0))
    o, kc, vc = pl.pallas_call(
        _na_ctx_kernel,
        grid=(nb_p,),
        in_specs=[pl.BlockSpec((seq_p, D), lambda b: (b, 0)), pl.BlockSpec((seq_p, D), lambda b: (b, 1)),
                  pl.BlockSpec((seq_p, D), lambda b: (b, 2))],
        out_specs=[pl.BlockSpec((seq_p, D), lambda b: (b, 0)), cspec(), cspec()],
        out_shape=[jax.ShapeDtypeStruct((t, D), BF16), jax.ShapeDtypeStruct(cshape, F32),
                   jax.ShapeDtypeStruct(cshape, F32)],
        compiler_params=_params(1),
        name="na_context",
    )(qkv, qkv, qkv)

    bias = _na_bias_tables(lw["rpb"])
    past = cache_k.shape[3]
    off = t_p // seq_s
    npair = NA_HEADS // 2
    pw = 2 * NA_HD
    pspec = lambda: pl.BlockSpec((1, 1, 2, past, NA_HD), lambda hp, b: (b, j, hp, 0, 0))
    o = pl.pallas_call(
        functools.partial(_na_lat_kernel, seq=seq_s, qb=256),
        grid=(npair, nb_s),
        in_specs=[pl.BlockSpec((seq_s, pw), lambda hp, b: (off + b, hp)),
                  pl.BlockSpec((seq_s, pw), lambda hp, b: (off + b, npair + hp)),
                  pl.BlockSpec((seq_s, pw), lambda hp, b: (off + b, 2 * npair + hp)),
                  pl.BlockSpec((2, 3, 2 * NA_WIN_ROWS, GRID_W, 2 * GRID_W), lambda hp, b: (hp, 0, 0, 0, 0)),
                  pspec(), pspec(), pl.BlockSpec(memory_space=pl.ANY)],
        out_specs=pl.BlockSpec((seq_s, pw), lambda hp, b: (off + b, hp)),
        out_shape=jax.ShapeDtypeStruct((t, D), BF16),
        input_output_aliases={6: 0},
        compiler_params=_params(2),
        name="na_latent",
    )(qkv, qkv, qkv, bias, cache_k, cache_v, o)
    outs = _out_proj(o, lw["w_o"], h, lw, layer, dims)
    return outs, kc, vc


def _gelu_tanh(x):
    return 0.5 * x * (1.0 + jnp.tanh(math.sqrt(2.0 / math.pi) * (x + 0.044715 * (x * x * x))))


def _gmlp_kernel(h_ref, g1, sh1, sc1, gt1, win_ref, gv_ref, ws_ref, bs_ref, wout_ref,
                 g2, sh2, sc2, wr_hi, wr_lo, br, h1_ref, xn_ref, route_ref, m_ref):
    h = h_ref[...]
    a = _normmod(h, g1[...], sh1[0], sc1[0]).astype(BF16)
    u = _gelu_tanh(_dot(a, win_ref[:, :GMLP_DFF]))
    v = _gelu_tanh(_dot(a, win_ref[:, GMLP_DFF:]))
    v = _rms(v, gv_ref[...]).astype(BF16)
    gw = GMLP_DFF // GMLP_GROUPS
    for c in range(TM // GMLP_CHUNK):
        rows = slice(c * GMLP_CHUNK, (c + 1) * GMLP_CHUNK)
        for g in range(GMLP_GROUPS):
            cols = slice(g * gw, (g + 1) * gw)
            vs = _dot(ws_ref[g], v[rows, cols]) + bs_ref[g]
            m_ref[rows, cols] = (u[rows, cols] * vs).astype(BF16)
    o = _dot(m_ref[...], wout_ref[...])
    _finish(h, o, gt1[0], (g2, sh2, sc2, wr_hi, wr_lo, br), h1_ref, xn_ref, route_ref)


def _gmlp_layer(h, lw, layer, dims):
    t = dims["t"]
    row_fn = _tile_row_fn(dims)
    gw = GMLP_DFF // GMLP_GROUPS
    tile = lambda: pl.BlockSpec((TM, D), lambda i: (i, 0))
    one = pl.Buffered(1)
    return pl.pallas_call(
        _gmlp_kernel,
        grid=(t // TM,),
        in_specs=[tile(), _full((1, D))] + [_mod_spec(layer, k, row_fn) for k in (0, 1, 2)]
                 + [pl.BlockSpec((D, 2 * GMLP_DFF), lambda i: (0, 0), pipeline_mode=one),
                    _full((1, GMLP_DFF)), _full((GMLP_GROUPS, GMLP_CHUNK, GMLP_CHUNK)),
                    _full((GMLP_GROUPS, GMLP_CHUNK, gw)),
                    pl.BlockSpec((GMLP_DFF, D), lambda i: (0, 0), pipeline_mode=one)]
                 + _route_specs(layer, row_fn),
        out_specs=_stream_specs(TM, lambda i: i),
        out_shape=_stream_outs(t),
        scratch_shapes=[pltpu.VMEM((TM, GMLP_DFF), BF16)],
        compiler_params=_params(1),
        name="gmlp",
    )(h, lw["g1"], lw["mod"], lw["mod"], lw["mod"], lw["w_in"], lw["g_v"], lw["w_s"], lw["b_s"],
      lw["w_out"], *_route_args(lw))


HALO = 16


CONV_TAPS = 4
CONV_LEFT = CONV_TAPS // 2
CONV_SHIFTED = tuple(k for k in range(CONV_TAPS) if k != CONV_LEFT)


def _ssd_conv_kernel(x_ref, bc_ref, cw_ref, cb_ref, sh_ref, *rest, seq):
    o_ref = rest[-1]
    L = SSD_CHUNK
    nc = seq // L
    c = pl.program_id(1)
    r0 = pl.multiple_of(c * L, L)
    rp = pl.multiple_of(jnp.maximum(r0 - HALO, 0), HALO)
    rn = pl.multiple_of(jnp.minimum(r0 + L, seq - HALO), HALO)
    outs = []
    for src, lo in ((x_ref, 0), (bc_ref, SSD_INNER)):
        cur = src[pl.ds(r0, L), :]
        prev = src[pl.ds(rp, HALO), :]
        nxt = src[pl.ds(rn, HALO), :]
        win = jnp.concatenate([jnp.where(c > 0, prev, jnp.zeros_like(prev)), cur,
                               jnp.where(c < nc - 1, nxt, jnp.zeros_like(nxt))], axis=0)
        shifted = _dot(sh_ref[...], win)
        w = lambda k: cw_ref[k:k + 1, lo:lo + SSD_INNER]
        conv = cb_ref[:, lo:lo + SSD_INNER] + w(CONV_LEFT) * cur.astype(F32)
        for i, k in enumerate(CONV_SHIFTED):
            conv = conv + w(k) * shifted[i * L:(i + 1) * L, :]
        outs.append(_silu(conv).astype(BF16))
    o_ref[...] = jnp.concatenate(outs, axis=1)


def _ssd_conv(zxbc, lw, dims):
    t, t_p = dims["t"], dims["t_p"]
    L = SSD_CHUNK

    def make_call(seq, nb, off, aliased):
        nc = seq // L
        in_specs = [pl.BlockSpec((seq, SSD_INNER), lambda b, c: (off + b, 1)),
                    pl.BlockSpec((seq, SSD_INNER), lambda b, c: (off + b, 2)),
                    _full((4, 2 * SSD_INNER)), _full((1, 2 * SSD_INNER)),
                    _full((len(CONV_SHIFTED) * L, L + 2 * HALO))]
        if aliased:
            in_specs.append(pl.BlockSpec(memory_space=pl.ANY))
        return pl.pallas_call(
            functools.partial(_ssd_conv_kernel, seq=seq),
            grid=(nb, nc),
            in_specs=in_specs,
            out_specs=pl.BlockSpec((L, 2 * SSD_INNER), lambda b, c: ((off + b) * nc + c, 0)),
            out_shape=jax.ShapeDtypeStruct((t, 2 * SSD_INNER), BF16),
            input_output_aliases={5: 0} if aliased else {},
            compiler_params=_params(2),
            name="ssd_conv_seq%d" % seq,
        )

    taps = jnp.asarray(CONV_SHIFTED, jnp.int32)
    want = (HALO - CONV_LEFT + taps[:, None] + jnp.arange(L)[None, :]).reshape(-1, 1)
    shift = (want == jnp.arange(L + 2 * HALO)[None, :]).astype(BF16)
    args = [zxbc, zxbc, lw["conv_w"], lw["conv_b"], shift]
    xbc = make_call(dims["seq_p"], dims["nb_p"], 0, False)(*args)
    return make_call(dims["seq_s"], dims["nb_s"], t_p // dims["seq_s"], True)(*args, xbc)


def _ssd_scan_kernel(*refs, seq, rev, has_h0, want_state, add_skip):
    (xbc_ref, dt_ref, dtb_ref, alog_ref, dsk_ref, tri_ref, rep_ref) = refs[:7]
    pos = 7
    h0_ref = None
    if has_h0:
        h0_ref = refs[pos]
        pos += 1
    n_alias = len(refs) - pos - (2 if want_state else 1) - 1
    pos += n_alias
    y_ref = refs[pos]
    st_ref = refs[pos + 1] if want_state else None
    state = refs[-1]

    L = SSD_CHUNK
    nc = seq // L
    c = pl.program_id(1)

    @pl.when(c == 0)
    def _():
        if has_h0:
            for i in range(SSD_INNER // L):
                hpb = L // SSD_HD
                blk = h0_ref[0, 0, i * hpb:(i + 1) * hpb].reshape(L, SSD_STATE)
                state[:, i * L:(i + 1) * L] = blk.T
        else:
            state[...] = jnp.zeros_like(state)

    xc = xbc_ref[:, :SSD_INNER].astype(F32)
    bm = xbc_ref[:, SSD_INNER:SSD_INNER + SSD_GROUPS * SSD_STATE]
    cm = xbc_ref[:, SSD_INNER + SSD_GROUPS * SSD_STATE:]

    dtr = dt_ref[...] + dtb_ref[...]
    dt = jnp.maximum(dtr, 0.0) + jnp.log(1.0 + jnp.exp(-jnp.abs(dtr)))
    dta = dt * (-jnp.exp(alog_ref[...]))
    tri = tri_ref[...]
    p = sum(_dot(tri, part) for part in _split3(dta))
    pt = p.T
    edge = 0 if rev else L - 1
    p_edge = p[edge:edge + 1, :]
    rep = rep_ref[...]
    dt_x = _dot(dt.astype(BF16), rep)
    ep_x = _dot(jnp.exp(p).astype(BF16), rep)
    dte_x = _dot(jnp.exp(p_edge - p).astype(BF16), rep)
    cdec_x = _dot(jnp.broadcast_to(jnp.exp(p_edge), (8, p.shape[1])).astype(BF16), rep)[0:1, :]

    dtx = xc * dt_x
    dtxb = dtx.astype(BF16)
    xdte = (dtx * dte_x).astype(BF16)
    li = lax.broadcasted_iota(jnp.int32, (L, L), 0)
    si = lax.broadcasted_iota(jnp.int32, (L, L), 1)
    keep = (li <= si) if rev else (li >= si)
    lane0 = SSD_HEADS if rev else 0
    gw = SSD_HPG * SSD_HD
    ys, new_state = [], []
    cbs = []
    for g in range(0, SSD_GROUPS, 2):
        pair = slice(g * SSD_STATE, (g + 2) * SSD_STATE)
        b_a, b_b = bm[:, g * SSD_STATE:(g + 1) * SSD_STATE], bm[:, (g + 1) * SSD_STATE:(g + 2) * SSD_STATE]
        zero = jnp.zeros_like(b_a)
        b_diag = jnp.concatenate([jnp.concatenate([b_a, zero], axis=1),
                                  jnp.concatenate([zero, b_b], axis=1)], axis=0)
        both = _dot_nt(cm[:, pair], b_diag)
        cbs += [both[:, :L], both[:, L:]]
    for g in range(SSD_GROUPS):
        gcols = slice(g * gw, (g + 1) * gw)
        b_g = bm[:, g * SSD_STATE:(g + 1) * SSD_STATE]
        c_g = cm[:, g * SSD_STATE:(g + 1) * SSD_STATE]
        cb = cbs[g]
        st_prev = state[:, gcols]
        y_g = _dot(c_g, st_prev.astype(BF16)) * ep_x[:, gcols]
        xg = dtxb[:, gcols]
        head_of_lane = lax.broadcasted_iota(jnp.int32, xg.shape, 1) // SSD_HD
        mats, blocks = [], []
        for hh in range(SSD_HPG):
            hl = lane0 + g * SSD_HPG + hh
            seg = p[:, hl:hl + 1] - pt[hl:hl + 1, :]
            mats.append((cb * jnp.exp(jnp.where(keep, seg, NEG))).astype(BF16))
            blocks.append(jnp.where(head_of_lane == hh, xg, jnp.zeros_like(xg)))
        y_g = y_g + _dot(jnp.concatenate(mats, axis=1), jnp.concatenate(blocks, axis=0))
        if add_skip:
            y_g = y_g + dsk_ref[:, gcols] * xc[:, gcols]
        ys.append(y_g)
        new_state.append(st_prev * cdec_x[:, gcols] + _dot_tn(b_g, xdte[:, gcols]))
    y_ref[...] = jnp.concatenate(ys, axis=1)
    state[...] = jnp.concatenate(new_state, axis=1)

    if want_state:
        @pl.when(c == nc - 1)
        def _():
            for i in range(SSD_INNER // L):
                blk = state[:, i * L:(i + 1) * L].T
                st_ref[0, 0, i * (L // SSD_HD):(i + 1) * (L // SSD_HD)] = blk.reshape(L // SSD_HD, SSD_HD, SSD_STATE)


def _ssd_scan(xbc, dt_raw, lw, h0, j, rev, dims):
    t, t_p, seq_p, seq_s = dims["t"], dims["t_p"], dims["seq_p"], dims["seq_s"]
    nb_p, nb_s = dims["nb_p"], dims["nb_s"]
    L = SSD_CHUNK
    d = 1 if rev else 0
    li = jnp.arange(L)
    tri = ((li[:, None] <= li[None, :]) if rev else (li[:, None] >= li[None, :])).astype(BF16)
    lane = jnp.arange(128)
    col_head = jnp.arange(SSD_INNER) // SSD_HD
    rep = (lane[:, None] == (d * SSD_HEADS + col_head)[None, :]).astype(BF16)
    dsk = jnp.repeat(lw["d_skip"], SSD_HD)[None, :].astype(F32)
    st_shape = (nb_p, 1, SSD_HEADS, SSD_HD, SSD_STATE)

    def make_call(seq, nb, off, has_h0, want_state, n_alias):
        nc = seq // L
        chunk = (lambda b, c: (off * nc + b * nc + (nc - 1 - c), 0)) if rev else (lambda b, c: (off * nc + b * nc + c, 0))
        in_specs = [pl.BlockSpec((L, 2 * SSD_INNER), chunk), pl.BlockSpec((L, 128), chunk),
                    _full((1, 128)), _full((1, 128)),
                    _full((1, SSD_INNER)), _full((L, L)), _full((128, SSD_INNER))]
        if has_h0:
            in_specs.append(pl.BlockSpec((1, 1, SSD_HEADS, SSD_HD, SSD_STATE), lambda b, c: (b, j, 0, 0, 0)))
        aliases = {}
        if n_alias:
            aliases = {len(in_specs): 0}
            in_specs.append(pl.BlockSpec(memory_space=pl.ANY))
        out_specs = [pl.BlockSpec((L, SSD_INNER), chunk)]
        out_shape = [jax.ShapeDtypeStruct((t, SSD_INNER), F32)]
        if want_state:
            out_specs.append(pl.BlockSpec((1, 1, SSD_HEADS, SSD_HD, SSD_STATE), lambda b, c: (b, 0, 0, 0, 0)))
            out_shape.append(jax.ShapeDtypeStruct(st_shape, F32))
        return pl.pallas_call(
            functools.partial(_ssd_scan_kernel, seq=seq, rev=rev, has_h0=has_h0, want_state=want_state,
                              add_skip=not rev),
            grid=(nb, nc),
            in_specs=in_specs,
            out_specs=out_specs,
            out_shape=out_shape,
            scratch_shapes=[pltpu.VMEM((SSD_STATE, SSD_INNER), F32)],
            input_output_aliases=aliases,
            compiler_params=_params(2),
            name="ssd_scan_%s_seq%d" % ("bwd" if rev else "fwd", seq),
        )

    common = [xbc, dt_raw, lw["dt_bias"], lw["a_log"], dsk, tri, rep]
    y, st = make_call(seq_p, nb_p, 0, False, True, 0)(*common)
    (y,) = make_call(seq_s, nb_s, t_p // seq_s, True, False, 1)(*common, h0, y)
    return y, st


def _ssd_out_kernel(yf_ref, yb_ref, z_ref, gn_ref, w_ref, h_ref, gt1, g2, sh2, sc2, wr_hi, wr_lo, br,
                    h1_ref, xn_ref, route_ref):
    y = (yf_ref[...] + yb_ref[...]) * _silu(z_ref[...].astype(F32))
    o = _dot(_rms(y, gn_ref[...]).astype(BF16), w_ref[...])
    _finish(h_ref[...], o, gt1[0], (g2, sh2, sc2, wr_hi, wr_lo, br), h1_ref, xn_ref, route_ref)


def _ssd_layer(h, state_f, state_b, j, lw, layer, dims):
    t = dims["t"]
    zxbc = _pre_proj(h, lw, layer, lw["w_in"], 1536, BF16, dims, n=3 * SSD_INNER)
    dt_raw = _pre_proj(h, lw, layer, lw["w_dt"], 128, F32, dims)
    xbc = _ssd_conv(zxbc, lw, dims)
    y_f, st_f = _ssd_scan(xbc, dt_raw, lw, state_f, j, False, dims)
    y_b, st_b = _ssd_scan(xbc, dt_raw, lw, state_b, j, True, dims)
    row_fn = _tile_row_fn(dims)
    tile = lambda: pl.BlockSpec((TM, D), lambda i: (i, 0))
    wide = lambda: pl.BlockSpec((TM, SSD_INNER), lambda i: (i, 0))
    outs = pl.pallas_call(
        _ssd_out_kernel,
        grid=(t // TM,),
        in_specs=[wide(), wide(), wide(), _full((1, SSD_INNER)), _full((SSD_INNER, D)), tile(),
                  _mod_spec(layer, 2, row_fn)] + _route_specs(layer, row_fn),
        out_specs=_stream_specs(TM, lambda i: i),
        out_shape=_stream_outs(t),
        compiler_params=_params(1),
        name="ssd_out",
    )(y_f, y_b, zxbc, lw["g_norm"], lw["w_out"], h, lw["mod"], *_route_args(lw))
    return outs, st_f, st_b


TS = 256
SRC = 512


def _moe_plan(route, t):
    i32 = jnp.int32
    npair = len(PAIR_ORDER)
    ncls = MOE_GROUPS * npair
    key = route[:, 0].astype(i32)
    oh = (key[:, None] == jnp.arange(ncls, dtype=i32)[None, :]).astype(i32)
    csum = jnp.cumsum(oh, axis=0)
    ccnt = csum[-1]
    cnt = ccnt.reshape(MOE_GROUPS, npair).sum(axis=1)
    padded = ((cnt + TS - 1) // TS) * TS
    gend = jnp.cumsum(padded)
    in_grp = jnp.cumsum(ccnt.reshape(MOE_GROUPS, npair), axis=1) - ccnt.reshape(MOE_GROUPS, npair)
    cstart = ((gend - padded)[:, None] + in_grp).reshape(ncls)
    pos = jnp.sum(oh * (csum - 1 + cstart[None, :]), axis=1)
    n_tiles = t // TS + MOE_GROUPS
    tile0 = jnp.arange(n_tiles, dtype=i32) * TS
    tile_grp = jnp.sum((tile0[:, None] >= gend[None, :]).astype(i32), axis=1)
    last_grp = jnp.max(jnp.where(cnt > 0, jnp.arange(MOE_GROUPS, dtype=i32), 0))
    tile_grp = jnp.minimum(tile_grp, last_grp)
    n_used = (gend[-1] // TS).reshape(1)
    touch = ((cstart[None, :] < tile0[:, None] + TS) & (cstart[None, :] + ccnt[None, :] > tile0[:, None])
             & (ccnt[None, :] > 0))
    uses = jnp.array([[int(e in PAIR_ORDER[c % npair]) for e in range(MOE_EPG)] for c in range(ncls)], i32)
    need = (jnp.dot(touch.astype(i32), uses) > 0).astype(i32).reshape(-1)
    return dict(pos=pos, tile_grp=tile_grp, n_used=n_used, need=need, n_tiles=n_tiles)


ROW_UNROLL = 8


def _moe_scatter_kernel(pos_ref, x_ref, xs_ref):
    n = pl.program_id(0)

    @pl.when(n == 0)
    def _():
        xs_ref[...] = jnp.zeros_like(xs_ref)

    def body(jj, carry):
        for r in range(ROW_UNROLL):
            j = jj * ROW_UNROLL + r
            xs_ref[pl.ds(pos_ref[n * SRC + j], 1), :] = x_ref[pl.ds(j, 1), :]
        return carry

    lax.fori_loop(0, SRC // ROW_UNROLL, body, 0)


def _moe_expert_kernel(grp_ref, nused_ref, need_ref, xs_ref, wg_ref, wu_ref, wd_ref, y_ref,
                       wg_b, wu_b, wd_b, acc_ref):
    i = pl.program_id(0)
    used = i < nused_ref[0]
    new_group = (i == 0) | (grp_ref[i] != grp_ref[jnp.maximum(i - 1, 0)])

    @pl.when(used & new_group)
    def _():
        for e in range(MOE_EPG):
            wg_b[e] = wg_ref[0, e].astype(BF16)
            wu_b[e] = wu_ref[0, e].astype(BF16)
            wd_b[e] = wd_ref[0, e].astype(BF16)

    @pl.when(used)
    def _():
        lo, hi = _unpack_pairs(xs_ref[...])
        x = jnp.concatenate([lo, hi[:, :D - PW]], axis=1).astype(BF16)
        rec = hi[:, D - PW:XW - PW]
        lane = lax.broadcasted_iota(jnp.int32, rec.shape, 1)
        acc_ref[...] = jnp.zeros_like(acc_ref)
        for e in range(MOE_EPG):
            @pl.when(need_ref[i * MOE_EPG + e] > 0)
            def _():
                hg = _dot(x, wg_b[e])
                hu = _dot(x, wu_b[e])
                mine = (lane % MOE_EPG == e) & (lane < 3 * MOE_EPG)
                cw = jnp.sum(jnp.where(mine, rec, 0.0), axis=-1, keepdims=True)
                acc_ref[...] += _dot((_silu(hg) * hu * cw).astype(BF16), wd_b[e])

        acc = acc_ref[...].astype(BF16).astype(F32)
        y_ref[...] = _pack_pairs(acc[:, :D // 2], acc[:, D // 2:])

    @pl.when(i >= nused_ref[0])
    def _():
        y_ref[...] = jnp.zeros_like(y_ref)


def _moe_ungather_kernel(pos_ref, ys_ref, h1_ref, gt2, fg_ref, *rest, n_prompt_tiles):
    rows_ref = rest[-1]
    out_refs = rest[:-1]
    n = pl.program_id(0)

    def body(jj, carry):
        for r in range(ROW_UNROLL):
            j = jj * ROW_UNROLL + r
            rows_ref[pl.ds(j, 1), :] = ys_ref[pl.ds(pos_ref[n * SRC + j], 1), :]
        return carry

    lax.fori_loop(0, SRC // ROW_UNROLL, body, 0)
    h2 = h1_ref[...] + gt2[0] * jnp.concatenate(_unpack_pairs(rows_ref[...]), axis=1)
    if n_prompt_tiles is None:
        out_refs[0][...] = h2
    else:
        y = _rms(h2, fg_ref[...])

        @pl.when(n < n_prompt_tiles)
        def _():
            out_refs[0][...] = y

        @pl.when(n >= n_prompt_tiles)
        def _():
            out_refs[1][...] = y


def _moe_sparse(h1, xn, route, lw, layer, final_g, final, dims):
    t = dims["t"]
    plan = _moe_plan(route, t)
    n_tiles = plan["n_tiles"]
    n_rows = n_tiles * TS
    one = pl.Buffered(1)
    xs = pl.pallas_call(
        _moe_scatter_kernel,
        grid_spec=pltpu.PrefetchScalarGridSpec(
            num_scalar_prefetch=1,
            grid=(t // SRC,),
            in_specs=[pl.BlockSpec((SRC, PW), lambda n, pos: (n, 0))],
            out_specs=pl.BlockSpec((n_rows, PW), lambda n, pos: (0, 0))),
        out_shape=jax.ShapeDtypeStruct((n_rows, PW), U32),
        compiler_params=_params(1),
        name="moe_scatter_l%d" % layer,
    )(plan["pos"], xn)

    ex = lambda i, g, nu, nd: (layer, g[i], 0, 0)
    ys = pl.pallas_call(
        _moe_expert_kernel,
        grid_spec=pltpu.PrefetchScalarGridSpec(
            num_scalar_prefetch=3,
            grid=(n_tiles,),
            in_specs=[pl.BlockSpec((TS, PW), lambda i, g, nu, nd: (i, 0)),
                      pl.BlockSpec((1, MOE_EPG, D, MOE_DFF), ex, pipeline_mode=one),
                      pl.BlockSpec((1, MOE_EPG, D, MOE_DFF), ex, pipeline_mode=one),
                      pl.BlockSpec((1, MOE_EPG, MOE_DFF, D), ex, pipeline_mode=one)],
            out_specs=pl.BlockSpec((TS, D // 2), lambda i, g, nu, nd: (i, 0)),
            scratch_shapes=[pltpu.VMEM((MOE_EPG, D, MOE_DFF), BF16), pltpu.VMEM((MOE_EPG, D, MOE_DFF), BF16),
                            pltpu.VMEM((MOE_EPG, MOE_DFF, D), BF16), pltpu.VMEM((TS, D), F32)]),
        out_shape=jax.ShapeDtypeStruct((n_rows, D // 2), U32),
        compiler_params=_params(1),
        name="moe_expert_l%d" % layer,
    )(plan["tile_grp"], plan["n_used"], plan["need"], xs, lw["w_gate"], lw["w_up"], lw["w_down"])

    row_fn = _tile_row_fn(dims, SRC)
    tile = lambda: pl.BlockSpec((SRC, D), lambda n, pos: (n, 0))
    if final:
        npt = dims["t_p"] // SRC
        out_specs = [pl.BlockSpec((SRC, D), lambda n, pos: (jnp.minimum(n, npt - 1), 0)),
                     pl.BlockSpec((SRC, D), lambda n, pos: (jnp.maximum(n - npt, 0), 0))]
        out_shape = [jax.ShapeDtypeStruct((dims["t_p"], D), F32), jax.ShapeDtypeStruct((t - dims["t_p"], D), F32)]
    else:
        npt = None
        out_specs = [tile()]
        out_shape = [jax.ShapeDtypeStruct((t, D), F32)]
    return pl.pallas_call(
        functools.partial(_moe_ungather_kernel, n_prompt_tiles=npt),
        grid_spec=pltpu.PrefetchScalarGridSpec(
            num_scalar_prefetch=1,
            grid=(t // SRC,),
            in_specs=[pl.BlockSpec((n_rows, D // 2), lambda n, pos: (0, 0)), tile(),
                      pl.BlockSpec((1, 1, D), lambda n, pos: ((layer * MOD_ROWS + row_fn(n)) * N_MOD + 5, 0, 0)),
                      pl.BlockSpec((1, D), lambda n, pos: (0, 0))],
            out_specs=out_specs,
            scratch_shapes=[pltpu.VMEM((SRC, D // 2), U32)]),
        out_shape=out_shape,
        compiler_params=_params(1),
        name="moe_ungather_l%d" % layer,
    )(plan["pos"], ys, h1, lw["mod"], final_g)


def kernel(x_prompt, x_sample, cache_k, cache_v, state_ssm_fwd, state_ssm_bwd, c, c_ctx, ada_w, ada_b, norm1_g, norm2_g, final_g, fnet_w_o, na_w_qkv, na_w_o, na_rpb, gmlp_w_in, gmlp_g_v, gmlp_w_s, gmlp_b_s, gmlp_w_out, ssd_w_in, ssd_conv_w, ssd_conv_b, ssd_a_log, ssd_dt_bias, ssd_d_skip, ssd_g_norm, ssd_w_out, moe_w_gr, moe_b_gr, moe_w_er, moe_b_er, moe_w_gate, moe_w_up, moe_w_down):
    nb_p, seq_p, _ = x_prompt.shape
    nb_s, seq_s, _ = x_sample.shape
    depth = ada_w.shape[0]
    t_p, t_s = nb_p * seq_p, nb_s * seq_s
    dims = dict(t=t_p + t_s, t_p=t_p, seq_p=seq_p, seq_s=seq_s, nb_p=nb_p, nb_s=nb_s)
    assert 1 + nb_s <= MOD_ROWS and t_p % seq_s == 0 and t_p % TM == 0 and seq_s % TM == 0

    cond = jnp.zeros((MOD_ROWS, D), F32).at[0].set(c_ctx).at[1:1 + nb_s].set(c)
    mod = _ada_table(cond, ada_w, ada_b)
    h = (x_prompt.reshape(t_p, D), x_sample.reshape(t_s, D))
    fg = final_g.reshape(1, D)

    new_k, new_v, new_sf, new_sb = [], [], [], []
    for l in range(depth):
        kind, j = l % 4, l // 4
        w_r = jnp.concatenate([moe_w_gr[l], moe_w_er[l]], axis=1)
        w_r = jnp.pad(w_r, ((0, 0), (0, ROUTE_W - MOE_GROUPS - MOE_EXPERTS)))
        b_r = jnp.pad(jnp.concatenate([moe_b_gr[l], moe_b_er[l]]), (0, ROUTE_W - MOE_GROUPS - MOE_EXPERTS))
        wr_hi = w_r.astype(BF16)
        lw = dict(mod=mod, g1=norm1_g[l].reshape(1, D), g2=norm2_g[l].reshape(1, D),
                  wr_hi=wr_hi, wr_lo=(w_r - wr_hi.astype(F32)).astype(BF16), br=b_r.reshape(1, ROUTE_W),
                  w_gate=moe_w_gate, w_up=moe_w_up, w_down=moe_w_down)
        if kind == 0:
            lw.update(w_o=fnet_w_o[j].astype(BF16))
            h1, xn, route = _fnet_layer(h, lw, l, dims)
        elif kind == 1:
            lw.update(w_qkv=na_w_qkv[j], w_o=na_w_o[j].astype(BF16), rpb=na_rpb[j])
            (h1, xn, route), kc, vc = _na_layer(h, cache_k, cache_v, j, lw, l, dims)
            new_k.append(kc)
            new_v.append(vc)
        elif kind == 2:
            gw = GMLP_DFF // GMLP_GROUPS
            lw.update(w_in=gmlp_w_in[j].astype(BF16), g_v=gmlp_g_v[j].reshape(1, GMLP_DFF),
                      w_s=gmlp_w_s[j].astype(BF16),
                      b_s=jnp.broadcast_to(gmlp_b_s[j][:, :, None], (GMLP_GROUPS, GMLP_CHUNK, gw)),
                      w_out=gmlp_w_out[j].astype(BF16))
            h1, xn, route = _gmlp_layer(h, lw, l, dims)
        else:
            n_main = 3 * SSD_INNER
            w_in = ssd_w_in[j]
            pad = lambda v: jnp.pad(v, ((0, 0), (0, 128 - 2 * SSD_HEADS)))
            lw.update(w_in=w_in, w_dt=pad(w_in[:, n_main:]),
                      conv_w=ssd_conv_w[j], conv_b=ssd_conv_b[j].reshape(1, -1),
                      dt_bias=pad(ssd_dt_bias[j].reshape(1, -1)), a_log=pad(ssd_a_log[j].reshape(1, -1)),
                      d_skip=ssd_d_skip[j], g_norm=ssd_g_norm[j].reshape(1, SSD_INNER),
                      w_out=ssd_w_out[j].astype(BF16))
            (h1, xn, route), sf, sb = _ssd_layer(h, state_ssm_fwd, state_ssm_bwd, j, lw, l, dims)
            new_sf.append(sf)
            new_sb.append(sb)
        out = _moe_sparse(h1, xn, route, lw, l, fg, l == depth - 1, dims)
        h = out if l == depth - 1 else out[0]

    y_prompt = h[0].reshape(nb_p, seq_p, D)
    y_sample = h[1].reshape(nb_s, seq_s, D)
    cat = lambda xs: jnp.concatenate(xs, axis=1)
    return (y_prompt, y_sample, cat(new_k), cat(new_v), cat(new_sf), cat(new_sb))
```

```python
import functools
import math

import jax
import jax.numpy as jnp
from jax import lax
from jax.experimental import pallas as pl
from jax.experimental.pallas import tpu as pltpu

F32 = jnp.float32
BF16 = jnp.bfloat16

D = 1024
EPS = 1e-6
NEG = -1e30
N_MOD = 6
MOD_ROWS = 8
GRID_W = 64
FNET_GROUPS = 8
NA_HEADS = 16
NA_HD = 64
NA_WIN_ROWS = 8
NA_WIN_COLS = 16
GMLP_CHUNK = 128
GMLP_DFF = 2048
GMLP_GROUPS = 8
SSD_INNER = 2048
SSD_HD = 64
SSD_HEADS = 32
SSD_GROUPS = 8
SSD_STATE = 128
SSD_CHUNK = 128
SSD_HPG = SSD_HEADS // SSD_GROUPS
MOE_GROUPS = 4
MOE_EPG = 4
MOE_EXPERTS = 16
MOE_DFF = 512
ROUTE_W = 128
ROUTE_E0 = 4
PAIR_ORDER = ((0, 1), (0, 2), (0, 3), (1, 3), (1, 2), (2, 3))

TM = 512
VMEM_LIMIT = 56 * 1024 * 1024


def _dot(a, b):
    return jnp.dot(a, b, preferred_element_type=F32)


def _dot_nt(a, b):
    return lax.dot_general(a, b, (((1,), (1,)), ((), ())), preferred_element_type=F32)


def _dot_tn(a, b):
    return lax.dot_general(a, b, (((0,), (0,)), ((), ())), preferred_element_type=F32)


def _silu(x):
    hx = 0.5 * x
    return hx * (1.0 + jnp.tanh(hx))


def _rms(x, g):
    return x * lax.rsqrt(jnp.mean(x * x, axis=-1, keepdims=True) + EPS) * g


def _normmod(x, g, shift, scale):
    return _rms(x, g) * (1.0 + scale) + shift


def _split3(x):
    hi = x.astype(BF16)
    r = x - hi.astype(F32)
    mid = r.astype(BF16)
    lo = (r - mid.astype(F32)).astype(BF16)
    return hi, mid, lo


def _params(n_axes):
    return pltpu.CompilerParams(dimension_semantics=("arbitrary",) * n_axes,
                                vmem_limit_bytes=VMEM_LIMIT)


def _full(shape):
    nd = len(shape)
    return pl.BlockSpec(shape, lambda *_: (0,) * nd)


def _mod_spec(layer, k, row_fn):
    return pl.BlockSpec((1, 1, D), lambda *idx: ((layer * MOD_ROWS + row_fn(*idx)) * N_MOD + k, 0, 0))


def _ada_kernel(c_ref, w_ref, b_ref, o_ref):
    c = c_ref[...]
    o_ref[0] = _dot(_silu(c).astype(BF16), w_ref[0].astype(BF16)) + b_ref[0]


def _ada_table(cond, ada_w, ada_b):
    depth = ada_w.shape[0]
    n = N_MOD * D
    tn = 1536
    out = pl.pallas_call(
        _ada_kernel,
        grid=(depth, n // tn),
        in_specs=[_full((MOD_ROWS, D)),
                  pl.BlockSpec((1, D, tn), lambda l, j: (l, 0, j)),
                  pl.BlockSpec((1, 1, tn), lambda l, j: (l, 0, j))],
        out_specs=pl.BlockSpec((1, MOD_ROWS, tn), lambda l, j: (l, 0, j)),
        out_shape=jax.ShapeDtypeStruct((depth, MOD_ROWS, n), F32),
        compiler_params=_params(2),
        name="ada_table",
    )(cond, ada_w, ada_b.reshape(depth, 1, n))
    return out.reshape(depth * MOD_ROWS * N_MOD, 1, D)


def _route(h1, g2, sh2, sc2, wr_hi, wr_lo, br):
    xn = _normmod(h1, g2, sh2, sc2)
    xh = xn.astype(BF16)
    xl = (xn - xh.astype(F32)).astype(BF16)
    logits = _dot(xh, wr_hi) + _dot(xh, wr_lo) + _dot(xl, wr_hi) + br
    lane = lax.broadcasted_iota(jnp.int32, logits.shape, 1).astype(F32)
    far = float(ROUTE_W)
    gl = jnp.where(lane < MOE_GROUPS, logits, NEG)
    gmax = jnp.max(gl, axis=-1, keepdims=True)
    g_p = 1.0 / jnp.sum(jnp.exp(gl - gmax), axis=-1, keepdims=True)
    gidx = jnp.min(jnp.where(gl == gmax, lane, far), axis=-1, keepdims=True)
    lo = ROUTE_E0 + MOE_EPG * gidx
    el = jnp.where((lane >= lo) & (lane < lo + MOE_EPG), logits, NEG)
    m1 = jnp.max(el, axis=-1, keepdims=True)
    i1 = jnp.min(jnp.where(el == m1, lane, far), axis=-1, keepdims=True)
    el2 = jnp.where(lane == i1, NEG, el)
    m2 = jnp.max(el2, axis=-1, keepdims=True)
    i2 = jnp.min(jnp.where(el2 == m2, lane, far), axis=-1, keepdims=True)
    e2 = jnp.exp(m2 - m1)
    w1 = g_p / (1.0 + e2)
    w2 = w1 * e2
    rec = jnp.zeros_like(logits)
    for part, (a, b) in enumerate(zip(_split3(w1), _split3(w2))):
        shift = part * MOE_EPG - lo
        rec = (rec + jnp.where(lane == i1 + shift, a.astype(F32), 0.0)
               + jnp.where(lane == i2 + shift, b.astype(F32), 0.0))
    ea = jnp.minimum(i1, i2) - lo
    eb = jnp.maximum(i1, i2) - lo
    pair = sum(jnp.where((ea == a) & (eb == b), float(k), 0.0) for k, (a, b) in enumerate(PAIR_ORDER))
    key = gidx * float(len(PAIR_ORDER)) + pair
    return xh, rec.astype(BF16), jnp.broadcast_to(key, logits.shape)


def _route_specs(layer, row_fn):
    return [_full((1, D)), _mod_spec(layer, 3, row_fn), _mod_spec(layer, 4, row_fn),
            _full((D, ROUTE_W)), _full((D, ROUTE_W)), _full((1, ROUTE_W))]


def _route_args(lw):
    return [lw["g2"], lw["mod"], lw["mod"], lw["wr_hi"], lw["wr_lo"], lw["br"]]


def _finish(h, o, gate, rt_refs, h1_ref, xn_ref, route_ref):
    g2, sh2, sc2, wr_hi, wr_lo, br = rt_refs
    h1 = h + gate * o
    h1_ref[...] = h1
    xn, rec, gid = _route(h1, g2[...], sh2[0], sc2[0], wr_hi[...], wr_lo[...], br[...])
    row = jnp.concatenate([xn.astype(F32), rec.astype(F32), jnp.zeros((h.shape[0], 2 * PW - XW), F32)], axis=1)
    xn_ref[...] = _pack_pairs(row[:, :PW], row[:, PW:])
    route_ref[...] = gid


XW = D + ROUTE_W
PW = 640
U32 = jnp.uint32


def _pack_pairs(lo, hi):
    lo_bits = lax.bitcast_convert_type(lo, U32) >> 16
    hi_bits = lax.bitcast_convert_type(hi, U32) & U32(0xFFFF0000)
    return lo_bits | hi_bits


def _unpack_pairs(w):
    return lax.bitcast_convert_type(w << 16, F32), lax.bitcast_convert_type(w & U32(0xFFFF0000), F32)


def _stream_outs(t):
    return [jax.ShapeDtypeStruct((t, D), F32), jax.ShapeDtypeStruct((t, PW), U32),
            jax.ShapeDtypeStruct((t, ROUTE_W), F32)]


def _stream_specs(rows, row_block):
    return [pl.BlockSpec((rows, w), lambda *idx: (row_block(*idx), 0)) for w in (D, PW, ROUTE_W)]


def _fnet_kernel(h_ref, g1, sh1, sc1, gt1, csc_ref, fs_ref, wo_ref, g2, sh2, sc2, wr_hi, wr_lo, br,
                 *rest, seq):
    h1_ref, xn_ref, route_ref, ab_ref = rest[-4:]
    h = h_ref[...]
    a = _normmod(h, g1[...], sh1[0], sc1[0]).astype(BF16)
    gd = D // FNET_GROUPS
    for g in range(FNET_GROUPS):
        ab = _dot(a[:, g * gd:(g + 1) * gd], csc_ref[...])
        ab_ref[0:seq, g * gd:(g + 1) * gd] = ab[:, :gd].astype(BF16)
        ab_ref[seq:2 * seq, g * gd:(g + 1) * gd] = ab[:, gd:].astype(BF16)
    f = _dot(fs_ref[...], ab_ref[...])
    o = _dot(f.astype(BF16), wo_ref[...])
    _finish(h, o, gt1[0], (g2, sh2, sc2, wr_hi, wr_lo, br), h1_ref, xn_ref, route_ref)


def _dft_tables(n):
    k = jnp.arange(n, dtype=jnp.int32)
    ang = ((k[:, None] * k[None, :]) % n).astype(F32) * (2.0 * math.pi / n)
    s = 1.0 / math.sqrt(n)
    return jnp.cos(ang) * s, jnp.sin(ang) * s


def _fnet_layer(h, lw, layer, dims):
    t, t_p = dims["t"], dims["t_p"]
    gd = D // FNET_GROUPS
    cc, sc = _dft_tables(gd)
    csc = jnp.concatenate([cc, sc], axis=1).astype(BF16)

    split_in = isinstance(h, tuple)

    def make_call(seq, nb, off, n_alias):
        cs, ss = _dft_tables(seq)
        fs = jnp.concatenate([cs, -ss], axis=1).astype(BF16)
        row_fn = (lambda b: 0) if off == 0 else (lambda b: 1 + b)
        in_off = 0 if split_in else off
        tile = lambda: pl.BlockSpec((seq, D), lambda b: (off + b, 0))
        in_specs = ([pl.BlockSpec((seq, D), lambda b: (in_off + b, 0)), _full((1, D))]
                    + [_mod_spec(layer, k, row_fn) for k in (0, 1, 2)]
                    + [_full((gd, 2 * gd)), _full((seq, 2 * seq)), _full((D, D))]
                    + _route_specs(layer, row_fn))
        aliases = {}
        if n_alias:
            base = len(in_specs)
            in_specs = in_specs + [pl.BlockSpec(memory_space=pl.ANY)] * n_alias
            aliases = {base + i: i for i in range(n_alias)}
        call = pl.pallas_call(
            functools.partial(_fnet_kernel, seq=seq),
            grid=(nb,),
            in_specs=in_specs,
            out_specs=_stream_specs(seq, lambda b: off + b),
            out_shape=_stream_outs(t),
            scratch_shapes=[pltpu.VMEM((2 * seq, D), BF16)],
            input_output_aliases=aliases,
            compiler_params=_params(1),
            name="fnet_seq%d" % seq,
        )
        return lambda *a: call(*a[:2], *a[2:5], a[5], fs, *a[6:])

    h_p, h_s = h if split_in else (h, h)
    args = [lw["g1"], lw["mod"], lw["mod"], lw["mod"], csc, lw["w_o"]] + _route_args(lw)
    outs = make_call(dims["seq_p"], dims["nb_p"], 0, None)(h_p, *args)
    return make_call(dims["seq_s"], dims["nb_s"], t_p // dims["seq_s"], 3)(h_s, *args, *outs)


def _pre_kernel(h_ref, g1, sh1, sc1, w_ref, o_ref, a_ref):
    @pl.when(pl.program_id(1) == 0)
    def _():
        a_ref[...] = _normmod(h_ref[...], g1[...], sh1[0], sc1[0]).astype(BF16)

    o_ref[...] = _dot(a_ref[...], w_ref[...].astype(BF16)).astype(o_ref.dtype)


def _tile_row_fn(dims, tm=TM):
    npt = dims["t_p"] // tm
    tps = dims["seq_s"] // tm
    return lambda i, *_: jnp.where(i < npt, 0, 1 + (i - npt) // tps)


def _pre_proj(h, lw, layer, w, tn, out_dtype, dims, n=None):
    t = dims["t"]
    n = w.shape[1] if n is None else n
    tm = dims["seq_s"]
    row_fn = _tile_row_fn(dims, tm)
    return pl.pallas_call(
        _pre_kernel,
        grid=(t // tm, n // tn),
        in_specs=[pl.BlockSpec((tm, D), lambda i, j: (i, 0)), _full((1, D)),
                  _mod_spec(layer, 0, row_fn), _mod_spec(layer, 1, row_fn),
                  pl.BlockSpec((D, tn), lambda i, j: (0, j))],
        out_specs=pl.BlockSpec((tm, tn), lambda i, j: (i, j)),
        out_shape=jax.ShapeDtypeStruct((t, n), out_dtype),
        scratch_shapes=[pltpu.VMEM((tm, D), BF16)],
        compiler_params=_params(2),
        name="pre_proj_l%d_n%d" % (layer, n),
    )(h, lw["g1"], lw["mod"], lw["mod"], w)


def _out_kernel(o_ref, w_ref, h_ref, gt1, g2, sh2, sc2, wr_hi, wr_lo, br, h1_ref, xn_ref, route_ref):
    o = _dot(o_ref[...], w_ref[...])
    _finish(h_ref[...], o, gt1[0], (g2, sh2, sc2, wr_hi, wr_lo, br), h1_ref, xn_ref, route_ref)


def _out_proj(o, w, h, lw, layer, dims):
    t = dims["t"]
    k = o.shape[1]
    row_fn = _tile_row_fn(dims)
    tile = lambda: pl.BlockSpec((TM, D), lambda i: (i, 0))
    return pl.pallas_call(
        _out_kernel,
        grid=(t // TM,),
        in_specs=[pl.BlockSpec((TM, k), lambda i: (i, 0)), _full((k, D)), tile(),
                  _mod_spec(layer, 2, row_fn)] + _route_specs(layer, row_fn),
        out_specs=_stream_specs(TM, lambda i: i),
        out_shape=_stream_outs(t),
        compiler_params=_params(1),
        name="out_proj_l%d" % layer,
    )(o, w, h, lw["mod"], *_route_args(lw))


NA_SCALE = NA_HD ** -0.5


NA_HEAD_BLOCK = 4


def _na_ctx_kernel(q_ref, k_ref, v_ref, o_ref, kc_ref, vc_ref):
    seq = q_ref.shape[0]
    bw = NA_HEAD_BLOCK * NA_HD
    outs = []
    for hb in range(NA_HEADS // NA_HEAD_BLOCK):
        cols = slice(hb * bw, (hb + 1) * bw)
        q4 = q_ref[:, cols] * NA_SCALE
        k4 = k_ref[:, cols]
        v4 = v_ref[:, cols]
        head_of_lane = lax.broadcasted_iota(jnp.int32, k4.shape, 1) // NA_HD
        kbd = jnp.concatenate([jnp.where(head_of_lane == h, k4, jnp.zeros_like(k4))
                               for h in range(NA_HEAD_BLOCK)], axis=0)
        vbd = jnp.concatenate([jnp.where(head_of_lane == h, v4, jnp.zeros_like(v4))
                               for h in range(NA_HEAD_BLOCK)], axis=0)
        s = _dot_nt(q4, kbd)
        ps = []
        l4 = jnp.zeros((seq, bw), F32)
        for h in range(NA_HEAD_BLOCK):
            sh = s[:, h * seq:(h + 1) * seq]
            ph = jnp.exp(sh - jnp.max(sh, axis=-1, keepdims=True))
            l4 = jnp.where(head_of_lane == h, jnp.sum(ph, axis=-1, keepdims=True), l4)
            ps.append(ph.astype(BF16))
            hd = hb * NA_HEAD_BLOCK + h
            kc_ref[0, 0, hd] = k4[:, h * NA_HD:(h + 1) * NA_HD].astype(F32)
            vc_ref[0, 0, hd] = v4[:, h * NA_HD:(h + 1) * NA_HD].astype(F32)
        outs.append((_dot(jnp.concatenate(ps, axis=1), vbd) / l4).astype(BF16))
    o_ref[...] = jnp.concatenate(outs, axis=1)


def _na_row_start(qr, rows):
    kr = min(NA_WIN_ROWS, rows)
    return min(max(qr - kr // 2, 0), rows - kr)


def _na_window_bias(bias_ref, hh, qr, rows, m_lo, m_hi):
    kr = min(NA_WIN_ROWS, rows)
    rs = _na_row_start(qr, rows)
    blocks = []
    for m in range(m_lo, m_hi):
        ok0 = rs <= 2 * m < rs + kr
        ok1 = rs <= 2 * m + 1 < rs + kr
        e = 2 * m - qr + NA_WIN_ROWS
        if ok0 and ok1:
            blocks.append(bias_ref[hh, 0, e])
        elif ok1:
            blocks.append(bias_ref[hh, 1, e])
        elif ok0:
            blocks.append(bias_ref[hh, 2, e])
        else:
            blocks.append(jnp.full((GRID_W, 2 * GRID_W), NEG, F32))
    return jnp.concatenate(blocks, axis=1)


def _na_lat_kernel(q_ref, k_ref, v_ref, bias_ref, kc_ref, vc_ref, o_in, o_ref, *, seq, qb):
    del o_in
    rows = seq // GRID_W

    def block_diag(a, b):
        zero = jnp.zeros_like(a)
        return jnp.concatenate([jnp.concatenate([a, zero], axis=1), jnp.concatenate([zero, b], axis=1)], axis=0)

    first = slice(0, NA_HD)
    second = slice(NA_HD, 2 * NA_HD)
    kc_bd = block_diag(kc_ref[0, 0, 0].astype(BF16), kc_ref[0, 0, 1].astype(BF16))
    vc_bd = block_diag(vc_ref[0, 0, 0].astype(BF16), vc_ref[0, 0, 1].astype(BF16))
    past = kc_ref.shape[3]
    for b0 in range(0, seq, qb):
        q2 = q_ref[b0:b0 + qb, :] * NA_SCALE
        qrs = range(b0 // GRID_W, (b0 + qb) // GRID_W)
        m_lo = _na_row_start(qrs[0], rows) // 2
        m_hi = (_na_row_start(qrs[-1], rows) + min(NA_WIN_ROWS, rows) + 1) // 2
        keys = slice(m_lo * 2 * GRID_W, m_hi * 2 * GRID_W)
        nk = (m_hi - m_lo) * 2 * GRID_W
        s1_all = _dot_nt(q2, block_diag(k_ref[keys, first], k_ref[keys, second]))
        s2_all = _dot_nt(q2, kc_bd)
        p1s, p2s, ls = [], [], []
        for hh in range(2):
            bias = jnp.concatenate([_na_window_bias(bias_ref, hh, qr, rows, m_lo, m_hi) for qr in qrs], axis=0)
            s1 = s1_all[:, hh * nk:(hh + 1) * nk] + bias
            s2 = s2_all[:, hh * past:(hh + 1) * past]
            m = jnp.maximum(jnp.max(s1, axis=-1, keepdims=True), jnp.max(s2, axis=-1, keepdims=True))
            p1 = jnp.exp(s1 - m)
            p2 = jnp.exp(s2 - m)
            ls.append(jnp.sum(p1, axis=-1, keepdims=True) + jnp.sum(p2, axis=-1, keepdims=True))
            p1s.append(p1.astype(BF16))
            p2s.append(p2.astype(BF16))
        o2 = (_dot(jnp.concatenate(p1s, axis=1), block_diag(v_ref[keys, first], v_ref[keys, second]))
              + _dot(jnp.concatenate(p2s, axis=1), vc_bd))
        lane = lax.broadcasted_iota(jnp.int32, o2.shape, 1)
        o_ref[b0:b0 + qb, :] = (o2 / jnp.where(lane < NA_HD, ls[0], ls[1])).astype(BF16)


def _na_bias_tables(rpb):
    c = jnp.arange(GRID_W)
    win0 = jnp.clip(c - NA_WIN_COLS // 2, 0, GRID_W - NA_WIN_COLS)
    ok_c = (c[None, :] >= win0[:, None]) & (c[None, :] < win0[:, None] + NA_WIN_COLS)
    dc = jnp.clip(c[None, :] - c[:, None], 1 - NA_WIN_COLS, NA_WIN_COLS - 1) + NA_WIN_COLS - 1
    nh, ndr, ndc = rpb.shape
    pick = (dc.reshape(1, -1) == jnp.arange(ndc)[:, None]).astype(F32)
    cm = jnp.dot(rpb.reshape(nh * ndr, ndc), pick, precision=lax.Precision.HIGHEST)
    cm = jnp.where(ok_c[None, None], cm.reshape(nh, ndr, GRID_W, GRID_W), NEG)
    neg = jnp.full_like(cm[:, :1], NEG)
    ext = jnp.concatenate([neg, cm, neg], axis=1)
    a, b = ext[:, :-1], ext[:, 1:]
    negs = jnp.full_like(a, NEG)
    pair = lambda x, y: jnp.concatenate([x, y], axis=-1)
    return jnp.stack([pair(a, b), pair(negs, b), pair(a, negs)], axis=1)


def _na_layer(h, cache_k, cache_v, j, lw, layer, dims):
    t, t_p, seq_p, seq_s = dims["t"], dims["t_p"], dims["seq_p"], dims["seq_s"]
    nb_p, nb_s = dims["nb_p"], dims["nb_s"]
    qkv = _pre_proj(h, lw, layer, lw["w_qkv"], 1536, BF16, dims)
    cshape = (nb_p, 1, NA_HEADS, seq_p, NA_HD)
    cspec = lambda: pl.BlockSpec((1, 1, NA_HEADS, seq_p, NA_HD), lambda b: (b, 0, 0, 0, 0))
    o, kc, vc = pl.pallas_call(
        _na_ctx_kernel,
        grid=(nb_p,),
        in_specs=[pl.BlockSpec((seq_p, D), lambda b: (b, 0)), pl.BlockSpec((seq_p, D), lambda b: (b, 1)),
                  pl.BlockSpec((seq_p, D), lambda b: (b, 2))],
        out_specs=[pl.BlockSpec((seq_p, D), lambda b: (b, 0)), cspec(), cspec()],
        out_shape=[jax.ShapeDtypeStruct((t, D), BF16), jax.ShapeDtypeStruct(cshape, F32),
                   jax.ShapeDtypeStruct(cshape, F32)],
        compiler_params=_params(1),
        name="na_context",
    )(qkv, qkv, qkv)

    bias = _na_bias_tables(lw["rpb"])
    past = cache_k.shape[3]
    off = t_p // seq_s
    npair = NA_HEADS // 2
    pw = 2 * NA_HD
    pspec = lambda: pl.BlockSpec((1, 1, 2, past, NA_HD), lambda hp, b: (b, j, hp, 0, 0))
    o = pl.pallas_call(
        functools.partial(_na_lat_kernel, seq=seq_s, qb=256),
        grid=(npair, nb_s),
        in_specs=[pl.BlockSpec((seq_s, pw), lambda hp, b: (off + b, hp)),
                  pl.BlockSpec((seq_s, pw), lambda hp, b: (off + b, npair + hp)),
                  pl.BlockSpec((seq_s, pw), lambda hp, b: (off + b, 2 * npair + hp)),
                  pl.BlockSpec((2, 3, 2 * NA_WIN_ROWS, GRID_W, 2 * GRID_W), lambda hp, b: (hp, 0, 0, 0, 0)),
                  pspec(), pspec(), pl.BlockSpec(memory_space=pl.ANY)],
        out_specs=pl.BlockSpec((seq_s, pw), lambda hp, b: (off + b, hp)),
        out_shape=jax.ShapeDtypeStruct((t, D), BF16),
        input_output_aliases={6: 0},
        compiler_params=_params(2),
        name="na_latent",
    )(qkv, qkv, qkv, bias, cache_k, cache_v, o)
    outs = _out_proj(o, lw["w_o"], h, lw, layer, dims)
    return outs, kc, vc


def _gelu_tanh(x):
    return 0.5 * x * (1.0 + jnp.tanh(math.sqrt(2.0 / math.pi) * (x + 0.044715 * (x * x * x))))


def _gmlp_kernel(h_ref, g1, sh1, sc1, gt1, win_ref, gv_ref, ws_ref, bs_ref, wout_ref,
                 g2, sh2, sc2, wr_hi, wr_lo, br, h1_ref, xn_ref, route_ref, m_ref):
    h = h_ref[...]
    a = _normmod(h, g1[...], sh1[0], sc1[0]).astype(BF16)
    u = _gelu_tanh(_dot(a, win_ref[:, :GMLP_DFF]))
    v = _gelu_tanh(_dot(a, win_ref[:, GMLP_DFF:]))
    v = _rms(v, gv_ref[...]).astype(BF16)
    gw = GMLP_DFF // GMLP_GROUPS
    for c in range(TM // GMLP_CHUNK):
        rows = slice(c * GMLP_CHUNK, (c + 1) * GMLP_CHUNK)
        for g in range(GMLP_GROUPS):
            cols = slice(g * gw, (g + 1) * gw)
            vs = _dot(ws_ref[g], v[rows, cols]) + bs_ref[g]
            m_ref[rows, cols] = (u[rows, cols] * vs).astype(BF16)
    o = _dot(m_ref[...], wout_ref[...])
    _finish(h, o, gt1[0], (g2, sh2, sc2, wr_hi, wr_lo, br), h1_ref, xn_ref, route_ref)


def _gmlp_layer(h, lw, layer, dims):
    t = dims["t"]
    row_fn = _tile_row_fn(dims)
    gw = GMLP_DFF // GMLP_GROUPS
    tile = lambda: pl.BlockSpec((TM, D), lambda i: (i, 0))
    one = pl.Buffered(1)
    return pl.pallas_call(
        _gmlp_kernel,
        grid=(t // TM,),
        in_specs=[tile(), _full((1, D))] + [_mod_spec(layer, k, row_fn) for k in (0, 1, 2)]
                 + [pl.BlockSpec((D, 2 * GMLP_DFF), lambda i: (0, 0), pipeline_mode=one),
                    _full((1, GMLP_DFF)), _full((GMLP_GROUPS, GMLP_CHUNK, GMLP_CHUNK)),
                    _full((GMLP_GROUPS, GMLP_CHUNK, gw)),
                    pl.BlockSpec((GMLP_DFF, D), lambda i: (0, 0), pipeline_mode=one)]
                 + _route_specs(layer, row_fn),
        out_specs=_stream_specs(TM, lambda i: i),
        out_shape=_stream_outs(t),
        scratch_shapes=[pltpu.VMEM((TM, GMLP_DFF), BF16)],
        compiler_params=_params(1),
        name="gmlp",
    )(h, lw["g1"], lw["mod"], lw["mod"], lw["mod"], lw["w_in"], lw["g_v"], lw["w_s"], lw["b_s"],
      lw["w_out"], *_route_args(lw))


HALO = 16


CONV_TAPS = 4
CONV_LEFT = CONV_TAPS // 2
CONV_SHIFTED = tuple(k for k in range(CONV_TAPS) if k != CONV_LEFT)


def _ssd_conv_kernel(x_ref, bc_ref, cw_ref, cb_ref, sh_ref, *rest, seq):
    o_ref = rest[-1]
    L = SSD_CHUNK
    nc = seq // L
    c = pl.program_id(1)
    r0 = pl.multiple_of(c * L, L)
    rp = pl.multiple_of(jnp.maximum(r0 - HALO, 0), HALO)
    rn = pl.multiple_of(jnp.minimum(r0 + L, seq - HALO), HALO)
    outs = []
    for src, lo in ((x_ref, 0), (bc_ref, SSD_INNER)):
        cur = src[pl.ds(r0, L), :]
        prev = src[pl.ds(rp, HALO), :]
        nxt = src[pl.ds(rn, HALO), :]
        win = jnp.concatenate([jnp.where(c > 0, prev, jnp.zeros_like(prev)), cur,
                               jnp.where(c < nc - 1, nxt, jnp.zeros_like(nxt))], axis=0)
        shifted = _dot(sh_ref[...], win)
        w = lambda k: cw_ref[k:k + 1, lo:lo + SSD_INNER]
        conv = cb_ref[:, lo:lo + SSD_INNER] + w(CONV_LEFT) * cur.astype(F32)
        for i, k in enumerate(CONV_SHIFTED):
            conv = conv + w(k) * shifted[i * L:(i + 1) * L, :]
        outs.append(_silu(conv).astype(BF16))
    o_ref[...] = jnp.concatenate(outs, axis=1)


def _ssd_conv(zxbc, lw, dims):
    t, t_p = dims["t"], dims["t_p"]
    L = SSD_CHUNK

    def make_call(seq, nb, off, aliased):
        nc = seq // L
        in_specs = [pl.BlockSpec((seq, SSD_INNER), lambda b, c: (off + b, 1)),
                    pl.BlockSpec((seq, SSD_INNER), lambda b, c: (off + b, 2)),
                    _full((4, 2 * SSD_INNER)), _full((1, 2 * SSD_INNER)),
                    _full((len(CONV_SHIFTED) * L, L + 2 * HALO))]
        if aliased:
            in_specs.append(pl.BlockSpec(memory_space=pl.ANY))
        return pl.pallas_call(
            functools.partial(_ssd_conv_kernel, seq=seq),
            grid=(nb, nc),
            in_specs=in_specs,
            out_specs=pl.BlockSpec((L, 2 * SSD_INNER), lambda b, c: ((off + b) * nc + c, 0)),
            out_shape=jax.ShapeDtypeStruct((t, 2 * SSD_INNER), BF16),
            input_output_aliases={5: 0} if aliased else {},
            compiler_params=_params(2),
            name="ssd_conv_seq%d" % seq,
        )

    taps = jnp.asarray(CONV_SHIFTED, jnp.int32)
    want = (HALO - CONV_LEFT + taps[:, None] + jnp.arange(L)[None, :]).reshape(-1, 1)
    shift = (want == jnp.arange(L + 2 * HALO)[None, :]).astype(BF16)
    args = [zxbc, zxbc, lw["conv_w"], lw["conv_b"], shift]
    xbc = make_call(dims["seq_p"], dims["nb_p"], 0, False)(*args)
    return make_call(dims["seq_s"], dims["nb_s"], t_p // dims["seq_s"], True)(*args, xbc)


def _ssd_scan_kernel(*refs, seq, rev, has_h0, want_state, add_skip):
    (xbc_ref, dt_ref, dtb_ref, alog_ref, dsk_ref, tri_ref, rep_ref) = refs[:7]
    pos = 7
    h0_ref = None
    if has_h0:
        h0_ref = refs[pos]
        pos += 1
    n_alias = len(refs) - pos - (2 if want_state else 1) - 1
    pos += n_alias
    y_ref = refs[pos]
    st_ref = refs[pos + 1] if want_state else None
    state = refs[-1]

    L = SSD_CHUNK
    nc = seq // L
    c = pl.program_id(1)

    @pl.when(c == 0)
    def _():
        if has_h0:
            for i in range(SSD_INNER // L):
                hpb = L // SSD_HD
                blk = h0_ref[0, 0, i * hpb:(i + 1) * hpb].reshape(L, SSD_STATE)
                state[:, i * L:(i + 1) * L] = blk.T
        else:
            state[...] = jnp.zeros_like(state)

    xc = xbc_ref[:, :SSD_INNER].astype(F32)
    bm = xbc_ref[:, SSD_INNER:SSD_INNER + SSD_GROUPS * SSD_STATE]
    cm = xbc_ref[:, SSD_INNER + SSD_GROUPS * SSD_STATE:]

    dtr = dt_ref[...] + dtb_ref[...]
    dt = jnp.maximum(dtr, 0.0) + jnp.log(1.0 + jnp.exp(-jnp.abs(dtr)))
    dta = dt * (-jnp.exp(alog_ref[...]))
    tri = tri_ref[...]
    p = sum(_dot(tri, part) for part in _split3(dta))
    pt = p.T
    edge = 0 if rev else L - 1
    p_edge = p[edge:edge + 1, :]
    rep = rep_ref[...]
    dt_x = _dot(dt.astype(BF16), rep)
    ep_x = _dot(jnp.exp(p).astype(BF16), rep)
    dte_x = _dot(jnp.exp(p_edge - p).astype(BF16), rep)
    cdec_x = _dot(jnp.broadcast_to(jnp.exp(p_edge), (8, p.shape[1])).astype(BF16), rep)[0:1, :]

    dtx = xc * dt_x
    dtxb = dtx.astype(BF16)
    xdte = (dtx * dte_x).astype(BF16)
    li = lax.broadcasted_iota(jnp.int32, (L, L), 0)
    si = lax.broadcasted_iota(jnp.int32, (L, L), 1)
    keep = (li <= si) if rev else (li >= si)
    lane0 = SSD_HEADS if rev else 0
    gw = SSD_HPG * SSD_HD
    ys, new_state = [], []
    cbs = []
    for g in range(0, SSD_GROUPS, 2):
        pair = slice(g * SSD_STATE, (g + 2) * SSD_STATE)
        b_a, b_b = bm[:, g * SSD_STATE:(g + 1) * SSD_STATE], bm[:, (g + 1) * SSD_STATE:(g + 2) * SSD_STATE]
        zero = jnp.zeros_like(b_a)
        b_diag = jnp.concatenate([jnp.concatenate([b_a, zero], axis=1),
                                  jnp.concatenate([zero, b_b], axis=1)], axis=0)
        both = _dot_nt(cm[:, pair], b_diag)
        cbs += [both[:, :L], both[:, L:]]
    for g in range(SSD_GROUPS):
        gcols = slice(g * gw, (g + 1) * gw)
        b_g = bm[:, g * SSD_STATE:(g + 1) * SSD_STATE]
        c_g = cm[:, g * SSD_STATE:(g + 1) * SSD_STATE]
        cb = cbs[g]
        st_prev = state[:, gcols]
        y_g = _dot(c_g, st_prev.astype(BF16)) * ep_x[:, gcols]
        xg = dtxb[:, gcols]
        head_of_lane = lax.broadcasted_iota(jnp.int32, xg.shape, 1) // SSD_HD
        mats, blocks = [], []
        for hh in range(SSD_HPG):
            hl = lane0 + g * SSD_HPG + hh
            seg = p[:, hl:hl + 1] - pt[hl:hl + 1, :]
            mats.append((cb * jnp.exp(jnp.where(keep, seg, NEG))).astype(BF16))
            blocks.append(jnp.where(head_of_lane == hh, xg, jnp.zeros_like(xg)))
        y_g = y_g + _dot(jnp.concatenate(mats, axis=1), jnp.concatenate(blocks, axis=0))
        if add_skip:
            y_g = y_g + dsk_ref[:, gcols] * xc[:, gcols]
        ys.append(y_g)
        new_state.append(st_prev * cdec_x[:, gcols] + _dot_tn(b_g, xdte[:, gcols]))
    y_ref[...] = jnp.concatenate(ys, axis=1)
    state[...] = jnp.concatenate(new_state, axis=1)

    if want_state:
        @pl.when(c == nc - 1)
        def _():
            for i in range(SSD_INNER // L):
                blk = state[:, i * L:(i + 1) * L].T
                st_ref[0, 0, i * (L // SSD_HD):(i + 1) * (L // SSD_HD)] = blk.reshape(L // SSD_HD, SSD_HD, SSD_STATE)


def _ssd_scan(xbc, dt_raw, lw, h0, j, rev, dims):
    t, t_p, seq_p, seq_s = dims["t"], dims["t_p"], dims["seq_p"], dims["seq_s"]
    nb_p, nb_s = dims["nb_p"], dims["nb_s"]
    L = SSD_CHUNK
    d = 1 if rev else 0
    li = jnp.arange(L)
    tri = ((li[:, None] <= li[None, :]) if rev else (li[:, None] >= li[None, :])).astype(BF16)
    lane = jnp.arange(128)
    col_head = jnp.arange(SSD_INNER) // SSD_HD
    rep = (lane[:, None] == (d * SSD_HEADS + col_head)[None, :]).astype(BF16)
    dsk = jnp.repeat(lw["d_skip"], SSD_HD)[None, :].astype(F32)
    st_shape = (nb_p, 1, SSD_HEADS, SSD_HD, SSD_STATE)

    def make_call(seq, nb, off, has_h0, want_state, n_alias):
        nc = seq // L
        chunk = (lambda b, c: (off * nc + b * nc + (nc - 1 - c), 0)) if rev else (lambda b, c: (off * nc + b * nc + c, 0))
        in_specs = [pl.BlockSpec((L, 2 * SSD_INNER), chunk), pl.BlockSpec((L, 128), chunk),
                    _full((1, 128)), _full((1, 128)),
                    _full((1, SSD_INNER)), _full((L, L)), _full((128, SSD_INNER))]
        if has_h0:
            in_specs.append(pl.BlockSpec((1, 1, SSD_HEADS, SSD_HD, SSD_STATE), lambda b, c: (b, j, 0, 0, 0)))
        aliases = {}
        if n_alias:
            aliases = {len(in_specs): 0}
            in_specs.append(pl.BlockSpec(memory_space=pl.ANY))
        out_specs = [pl.BlockSpec((L, SSD_INNER), chunk)]
        out_shape = [jax.ShapeDtypeStruct((t, SSD_INNER), F32)]
        if want_state:
            out_specs.append(pl.BlockSpec((1, 1, SSD_HEADS, SSD_HD, SSD_STATE), lambda b, c: (b, 0, 0, 0, 0)))
            out_shape.append(jax.ShapeDtypeStruct(st_shape, F32))
        return pl.pallas_call(
            functools.partial(_ssd_scan_kernel, seq=seq, rev=rev, has_h0=has_h0, want_state=want_state,
                              add_skip=not rev),
            grid=(nb, nc),
            in_specs=in_specs,
            out_specs=out_specs,
            out_shape=out_shape,
            scratch_shapes=[pltpu.VMEM((SSD_STATE, SSD_INNER), F32)],
            input_output_aliases=aliases,
            compiler_params=_params(2),
            name="ssd_scan_%s_seq%d" % ("bwd" if rev else "fwd", seq),
        )

    common = [xbc, dt_raw, lw["dt_bias"], lw["a_log"], dsk, tri, rep]
    y, st = make_call(seq_p, nb_p, 0, False, True, 0)(*common)
    (y,) = make_call(seq_s, nb_s, t_p // seq_s, True, False, 1)(*common, h0, y)
    return y, st


def _ssd_out_kernel(yf_ref, yb_ref, z_ref, gn_ref, w_ref, h_ref, gt1, g2, sh2, sc2, wr_hi, wr_lo, br,
                    h1_ref, xn_ref, route_ref):
    y = (yf_ref[...] + yb_ref[...]) * _silu(z_ref[...].astype(F32))
    o = _dot(_rms(y, gn_ref[...]).astype(BF16), w_ref[...])
    _finish(h_ref[...], o, gt1[0], (g2, sh2, sc2, wr_hi, wr_lo, br), h1_ref, xn_ref, route_ref)


def _ssd_layer(h, state_f, state_b, j, lw, layer, dims):
    t = dims["t"]
    zxbc = _pre_proj(h, lw, layer, lw["w_in"], 1536, BF16, dims, n=3 * SSD_INNER)
    dt_raw = _pre_proj(h, lw, layer, lw["w_dt"], 128, F32, dims)
    xbc = _ssd_conv(zxbc, lw, dims)
    y_f, st_f = _ssd_scan(xbc, dt_raw, lw, state_f, j, False, dims)
    y_b, st_b = _ssd_scan(xbc, dt_raw, lw, state_b, j, True, dims)
    row_fn = _tile_row_fn(dims)
    tile = lambda: pl.BlockSpec((TM, D), lambda i: (i, 0))
    wide = lambda: pl.BlockSpec((TM, SSD_INNER), lambda i: (i, 0))
    outs = pl.pallas_call(
        _ssd_out_kernel,
        grid=(t // TM,),
        in_specs=[wide(), wide(), wide(), _full((1, SSD_INNER)), _full((SSD_INNER, D)), tile(),
                  _mod_spec(layer, 2, row_fn)] + _route_specs(layer, row_fn),
        out_specs=_stream_specs(TM, lambda i: i),
        out_shape=_stream_outs(t),
        compiler_params=_params(1),
        name="ssd_out",
    )(y_f, y_b, zxbc, lw["g_norm"], lw["w_out"], h, lw["mod"], *_route_args(lw))
    return outs, st_f, st_b


TS = 256
SRC = 512


def _moe_plan(route, t):
    i32 = jnp.int32
    npair = len(PAIR_ORDER)
    ncls = MOE_GROUPS * npair
    key = route[:, 0].astype(i32)
    oh = (key[:, None] == jnp.arange(ncls, dtype=i32)[None, :]).astype(i32)
    csum = jnp.cumsum(oh, axis=0)
    ccnt = csum[-1]
    cnt = ccnt.reshape(MOE_GROUPS, npair).sum(axis=1)
    padded = ((cnt + TS - 1) // TS) * TS
    gend = jnp.cumsum(padded)
    in_grp = jnp.cumsum(ccnt.reshape(MOE_GROUPS, npair), axis=1) - ccnt.reshape(MOE_GROUPS, npair)
    cstart = ((gend - padded)[:, None] + in_grp).reshape(ncls)
    pos = jnp.sum(oh * (csum - 1 + cstart[None, :]), axis=1)
    n_tiles = t // TS + MOE_GROUPS
    tile0 = jnp.arange(n_tiles, dtype=i32) * TS
    tile_grp = jnp.sum((tile0[:, None] >= gend[None, :]).astype(i32), axis=1)
    last_grp = jnp.max(jnp.where(cnt > 0, jnp.arange(MOE_GROUPS, dtype=i32), 0))
    tile_grp = jnp.minimum(tile_grp, last_grp)
    n_used = (gend[-1] // TS).reshape(1)
    groups = jnp.arange(MOE_GROUPS, dtype=i32)
    later = (groups[None, :] > tile_grp[:, None]) & (cnt[None, :] > 0)
    next_grp = jnp.min(jnp.where(later, groups[None, :], MOE_GROUPS), axis=1)
    next_grp = jnp.where(next_grp == MOE_GROUPS, -1, next_grp).astype(i32)
    touch = ((cstart[None, :] < tile0[:, None] + TS) & (cstart[None, :] + ccnt[None, :] > tile0[:, None])
             & (ccnt[None, :] > 0))
    uses = jnp.array([[int(e in PAIR_ORDER[c % npair]) for e in range(MOE_EPG)] for c in range(ncls)], i32)
    need = (jnp.dot(touch.astype(i32), uses) > 0).astype(i32).reshape(-1)
    return dict(pos=pos, tile_grp=tile_grp, n_used=n_used, need=need, next_grp=next_grp, n_tiles=n_tiles)


ROW_UNROLL = 8


def _moe_scatter_kernel(pos_ref, x_ref, xs_ref):
    n = pl.program_id(0)

    @pl.when(n == 0)
    def _():
        xs_ref[...] = jnp.zeros_like(xs_ref)

    def body(jj, carry):
        for r in range(ROW_UNROLL):
            j = jj * ROW_UNROLL + r
            xs_ref[pl.ds(pos_ref[n * SRC + j], 1), :] = x_ref[pl.ds(j, 1), :]
        return carry

    lax.fori_loop(0, SRC // ROW_UNROLL, body, 0)


def _moe_expert_kernel(grp_ref, nused_ref, need_ref, next_ref, xs_ref, wg_hbm, wu_hbm, wd_hbm, y_ref,
                       wg_b, wu_b, wd_b, acc_ref, wg_f, wu_f, wd_f, sems, *, layer):
    i = pl.program_id(0)
    used = i < nused_ref[0]
    new_group = (i == 0) | (grp_ref[i] != grp_ref[jnp.maximum(i - 1, 0)])

    def fetch(group):
        first = group * MOE_EPG
        return [pltpu.make_async_copy(src.at[layer, pl.ds(first, MOE_EPG)], dst, sems.at[k])
                for k, (src, dst) in enumerate(((wg_hbm, wg_f), (wu_hbm, wu_f), (wd_hbm, wd_f)))]

    @pl.when(i == 0)
    def _():
        for cp in fetch(grp_ref[0]):
            cp.start()

    @pl.when(used & new_group)
    def _():
        for cp in fetch(grp_ref[i]):
            cp.wait()
        for e in range(MOE_EPG):
            wg_b[e] = wg_f[e].astype(BF16)
            wu_b[e] = wu_f[e].astype(BF16)
            wd_b[e] = wd_f[e].astype(BF16)

        @pl.when(next_ref[i] >= 0)
        def _():
            for cp in fetch(next_ref[i]):
                cp.start()

    @pl.when(used)
    def _():
        lo, hi = _unpack_pairs(xs_ref[...])
        x = jnp.concatenate([lo, hi[:, :D - PW]], axis=1).astype(BF16)
        rec = hi[:, D - PW:XW - PW]
        lane = lax.broadcasted_iota(jnp.int32, rec.shape, 1)
        acc_ref[...] = jnp.zeros_like(acc_ref)
        for e in range(MOE_EPG):
            @pl.when(need_ref[i * MOE_EPG + e] > 0)
            def _():
                hg = _dot(x, wg_b[e])
                hu = _dot(x, wu_b[e])
                mine = (lane % MOE_EPG == e) & (lane < 3 * MOE_EPG)
                cw = jnp.sum(jnp.where(mine, rec, 0.0), axis=-1, keepdims=True)
                acc_ref[...] += _dot((_silu(hg) * hu * cw).astype(BF16), wd_b[e])

        acc = acc_ref[...].astype(BF16).astype(F32)
        y_ref[...] = _pack_pairs(acc[:, :D // 2], acc[:, D // 2:])

    @pl.when(i >= nused_ref[0])
    def _():
        y_ref[...] = jnp.zeros_like(y_ref)


def _moe_ungather_kernel(pos_ref, ys_ref, h1_ref, gt2, fg_ref, *rest, n_prompt_tiles):
    rows_ref = rest[-1]
    out_refs = rest[:-1]
    n = pl.program_id(0)

    def body(jj, carry):
        for r in range(ROW_UNROLL):
            j = jj * ROW_UNROLL + r
            rows_ref[pl.ds(j, 1), :] = ys_ref[pl.ds(pos_ref[n * SRC + j], 1), :]
        return carry

    lax.fori_loop(0, SRC // ROW_UNROLL, body, 0)
    h2 = h1_ref[...] + gt2[0] * jnp.concatenate(_unpack_pairs(rows_ref[...]), axis=1)
    if n_prompt_tiles is None:
        out_refs[0][...] = h2
    else:
        y = _rms(h2, fg_ref[...])

        @pl.when(n < n_prompt_tiles)
        def _():
            out_refs[0][...] = y

        @pl.when(n >= n_prompt_tiles)
        def _():
            out_refs[1][...] = y


def _moe_sparse(h1, xn, route, lw, layer, final_g, final, dims):
    t = dims["t"]
    plan = _moe_plan(route, t)
    n_tiles = plan["n_tiles"]
    n_rows = n_tiles * TS
    one = pl.Buffered(1)
    xs = pl.pallas_call(
        _moe_scatter_kernel,
        grid_spec=pltpu.PrefetchScalarGridSpec(
            num_scalar_prefetch=1,
            grid=(t // SRC,),
            in_specs=[pl.BlockSpec((SRC, PW), lambda n, pos: (n, 0))],
            out_specs=pl.BlockSpec((n_rows, PW), lambda n, pos: (0, 0))),
        out_shape=jax.ShapeDtypeStruct((n_rows, PW), U32),
        compiler_params=_params(1),
        name="moe_scatter_l%d" % layer,
    )(plan["pos"], xn)

    hbm = pl.BlockSpec(memory_space=pl.ANY)
    ys = pl.pallas_call(
        functools.partial(_moe_expert_kernel, layer=layer),
        grid_spec=pltpu.PrefetchScalarGridSpec(
            num_scalar_prefetch=4,
            grid=(n_tiles,),
            in_specs=[pl.BlockSpec((TS, PW), lambda i, g, nu, nd, nx: (i, 0)), hbm, hbm, hbm],
            out_specs=pl.BlockSpec((TS, D // 2), lambda i, g, nu, nd, nx: (i, 0)),
            scratch_shapes=[pltpu.VMEM((MOE_EPG, D, MOE_DFF), BF16), pltpu.VMEM((MOE_EPG, D, MOE_DFF), BF16),
                            pltpu.VMEM((MOE_EPG, MOE_DFF, D), BF16), pltpu.VMEM((TS, D), F32),
                            pltpu.VMEM((MOE_EPG, D, MOE_DFF), F32), pltpu.VMEM((MOE_EPG, D, MOE_DFF), F32),
                            pltpu.VMEM((MOE_EPG, MOE_DFF, D), F32), pltpu.SemaphoreType.DMA((3,))]),
        out_shape=jax.ShapeDtypeStruct((n_rows, D // 2), U32),
        compiler_params=_params(1),
        name="moe_expert_l%d" % layer,
    )(plan["tile_grp"], plan["n_used"], plan["need"], plan["next_grp"], xs,
      lw["w_gate"], lw["w_up"], lw["w_down"])

    row_fn = _tile_row_fn(dims, SRC)
    tile = lambda: pl.BlockSpec((SRC, D), lambda n, pos: (n, 0))
    if final:
        npt = dims["t_p"] // SRC
        out_specs = [pl.BlockSpec((SRC, D), lambda n, pos: (jnp.minimum(n, npt - 1), 0)),
                     pl.BlockSpec((SRC, D), lambda n, pos: (jnp.maximum(n - npt, 0), 0))]
        out_shape = [jax.ShapeDtypeStruct((dims["t_p"], D), F32), jax.ShapeDtypeStruct((t - dims["t_p"], D), F32)]
    else:
        npt = None
        out_specs = [tile()]
        out_shape = [jax.ShapeDtypeStruct((t, D), F32)]
    return pl.pallas_call(
        functools.partial(_moe_ungather_kernel, n_prompt_tiles=npt),
        grid_spec=pltpu.PrefetchScalarGridSpec(
            num_scalar_prefetch=1,
            grid=(t // SRC,),
            in_specs=[pl.BlockSpec((n_rows, D // 2), lambda n, pos: (0, 0)), tile(),
                      pl.BlockSpec((1, 1, D), lambda n, pos: ((layer * MOD_ROWS + row_fn(n)) * N_MOD + 5, 0, 0)),
                      pl.BlockSpec((1, D), lambda n, pos: (0, 0))],
            out_specs=out_specs,
            scratch_shapes=[pltpu.VMEM((SRC, D // 2), U32)]),
        out_shape=out_shape,
        compiler_params=_params(1),
        name="moe_ungather_l%d" % layer,
    )(plan["pos"], ys, h1, lw["mod"], final_g)


def kernel(x_prompt, x_sample, cache_k, cache_v, state_ssm_fwd, state_ssm_bwd, c, c_ctx, ada_w, ada_b, norm1_g, norm2_g, final_g, fnet_w_o, na_w_qkv, na_w_o, na_rpb, gmlp_w_in, gmlp_g_v, gmlp_w_s, gmlp_b_s, gmlp_w_out, ssd_w_in, ssd_conv_w, ssd_conv_b, ssd_a_log, ssd_dt_bias, ssd_d_skip, ssd_g_norm, ssd_w_out, moe_w_gr, moe_b_gr, moe_w_er, moe_b_er, moe_w_gate, moe_w_up, moe_w_down):
    nb_p, seq_p, _ = x_prompt.shape
    nb_s, seq_s, _ = x_sample.shape
    depth = ada_w.shape[0]
    t_p, t_s = nb_p * seq_p, nb_s * seq_s
    dims = dict(t=t_p + t_s, t_p=t_p, seq_p=seq_p, seq_s=seq_s, nb_p=nb_p, nb_s=nb_s)
    assert 1 + nb_s <= MOD_ROWS and t_p % seq_s == 0 and t_p % TM == 0 and seq_s % TM == 0

    cond = jnp.zeros((MOD_ROWS, D), F32).at[0].set(c_ctx).at[1:1 + nb_s].set(c)
    mod = _ada_table(cond, ada_w, ada_b)
    h = (x_prompt.reshape(t_p, D), x_sample.reshape(t_s, D))
    fg = final_g.reshape(1, D)

    new_k, new_v, new_sf, new_sb = [], [], [], []
    for l in range(depth):
        kind, j = l % 4, l // 4
        w_r = jnp.concatenate([moe_w_gr[l], moe_w_er[l]], axis=1)
        w_r = jnp.pad(w_r, ((0, 0), (0, ROUTE_W - MOE_GROUPS - MOE_EXPERTS)))
        b_r = jnp.pad(jnp.concatenate([moe_b_gr[l], moe_b_er[l]]), (0, ROUTE_W - MOE_GROUPS - MOE_EXPERTS))
        wr_hi = w_r.astype(BF16)
        lw = dict(mod=mod, g1=norm1_g[l].reshape(1, D), g2=norm2_g[l].reshape(1, D),
                  wr_hi=wr_hi, wr_lo=(w_r - wr_hi.astype(F32)).astype(BF16), br=b_r.reshape(1, ROUTE_W),
                  w_gate=moe_w_gate, w_up=moe_w_up, w_down=moe_w_down)
        if kind == 0:
            lw.update(w_o=fnet_w_o[j].astype(BF16))
            h1, xn, route = _fnet_layer(h, lw, l, dims)
        elif kind == 1:
            lw.update(w_qkv=na_w_qkv[j], w_o=na_w_o[j].astype(BF16), rpb=na_rpb[j])
            (h1, xn, route), kc, vc = _na_layer(h, cache_k, cache_v, j, lw, l, dims)
            new_k.append(kc)
            new_v.append(vc)
        elif kind == 2:
            gw = GMLP_DFF // GMLP_GROUPS
            lw.update(w_in=gmlp_w_in[j].astype(BF16), g_v=gmlp_g_v[j].reshape(1, GMLP_DFF),
                      w_s=gmlp_w_s[j].astype(BF16),
                      b_s=jnp.broadcast_to(gmlp_b_s[j][:, :, None], (GMLP_GROUPS, GMLP_CHUNK, gw)),
                      w_out=gmlp_w_out[j].astype(BF16))
            h1, xn, route = _gmlp_layer(h, lw, l, dims)
        else:
            n_main = 3 * SSD_INNER
            w_in = ssd_w_in[j]
            pad = lambda v: jnp.pad(v, ((0, 0), (0, 128 - 2 * SSD_HEADS)))
            lw.update(w_in=w_in, w_dt=pad(w_in[:, n_main:]),
                      conv_w=ssd_conv_w[j], conv_b=ssd_conv_b[j].reshape(1, -1),
                      dt_bias=pad(ssd_dt_bias[j].reshape(1, -1)), a_log=pad(ssd_a_log[j].reshape(1, -1)),
                      d_skip=ssd_d_skip[j], g_norm=ssd_g_norm[j].reshape(1, SSD_INNER),
                      w_out=ssd_w_out[j].astype(BF16))
            (h1, xn, route), sf, sb = _ssd_layer(h, state_ssm_fwd, state_ssm_bwd, j, lw, l, dims)
            new_sf.append(sf)
            new_sb.append(sb)
        out = _moe_sparse(h1, xn, route, lw, l, fg, l == depth - 1, dims)
        h = out if l == depth - 1 else out[0]

    y_prompt = h[0].reshape(nb_p, seq_p, D)
    y_sample = h[1].reshape(nb_s, seq_s, D)
    cat = lambda xs: jnp.concatenate(xs, axis=1)
    return (y_prompt, y_sample, cat(new_k), cat(new_v), cat(new_sf), cat(new_sb))
```
